```python
import jax, jax.numpy as jnp
from jax import lax
import numpy as np

D_MODEL = 1024
BATCH = 8
SEQ = 16384
DEPTH = 4

CHUNK = 64
SGU_BLOCK = 128
D_A = 1024
A_GROUPS = 8
A_GROUP_DIM = D_A // A_GROUPS
D_B = 1024
POOL_WINDOWS = (2, 4, 8, 16)
B_GROUPS = len(POOL_WINDOWS)
B_GROUP_DIM = D_B // B_GROUPS
D_C = 1024
CONV_WIDTH = 3
N_BRANCH = 3
SPLIT_SIZES = (D_A, D_A, D_A, D_B, D_B, D_C, D_C, D_C, D_C, N_BRANCH * D_MODEL)
N_IN = sum(SPLIT_SIZES)
SPLIT_OFFSETS = tuple(int(o) for o in np.cumsum(SPLIT_SIZES)[:-1])
RMS_EPS = 1e-6
LN_EPS = 1e-5

kernel_name = "hybrid_gated_parallel_mixers"


def rmsnorm(x, g):
    xf = x.astype(jnp.float32)
    y = xf * lax.rsqrt(jnp.mean(xf * xf, axis=-1, keepdims=True) + RMS_EPS)
    return (y * g.astype(jnp.float32)).astype(x.dtype)


def layernorm(x, g, b):
    xf = x.astype(jnp.float32)
    mu = jnp.mean(xf, axis=-1, keepdims=True)
    var = jnp.mean(jnp.square(xf - mu), axis=-1, keepdims=True)
    y = (xf - mu) * lax.rsqrt(var + LN_EPS)
    return (y * g.astype(jnp.float32) + b.astype(jnp.float32)).astype(x.dtype)


def chunk_causal_mask():
    c = jnp.arange(SGU_BLOCK) // CHUNK
    return c[None, :] <= c[:, None]


def spatial_gating(u, v, w_s, b_s, ln_g, ln_b):
    bsz, s, _ = v.shape
    v = layernorm(v, ln_g, ln_b)
    vb = v.reshape(bsz, s // SGU_BLOCK, SGU_BLOCK, A_GROUPS, A_GROUP_DIM)
    w = jnp.where(chunk_causal_mask()[None], w_s, jnp.zeros_like(w_s))
    mixed = jnp.einsum('gij,bnjgc->bnigc', w, vb) + b_s.T[:, :, None]
    return u * mixed.reshape(bsz, s, D_A)


def multiscale_pool(p, w_g, b_g, scale):
    bsz, s, _ = p.shape
    pf = p.astype(jnp.float32)
    csum = jnp.cumsum(pf, axis=1)
    pos1 = jnp.arange(1, s + 1, dtype=jnp.int32)
    outs = []
    for k, win in enumerate(POOL_WINDOWS):
        sl = slice(k * B_GROUP_DIM, (k + 1) * B_GROUP_DIM)
        cg = csum[..., sl]
        shifted = jnp.pad(cg, ((0, 0), (win, 0), (0, 0)))[:, :s]
        count = jnp.minimum(pos1, win).astype(jnp.float32)[:, None]
        outs.append((cg - shifted) / count - pf[..., sl])
    d = jnp.concatenate(outs, axis=-1).astype(p.dtype)
    d = d.reshape(bsz, s, B_GROUPS, B_GROUP_DIM)
    y = jnp.einsum('bsgc,gcd->bsgd', d, w_g).reshape(bsz, s, D_B) + b_g
    return y * scale


def causal_depthwise_conv(h, w, b):
    s = h.shape[1]
    hp = jnp.pad(h, ((0, 0), (CONV_WIDTH - 1, 0), (0, 0)))
    y = sum(w[k] * hp[:, k:k + s] for k in range(CONV_WIDTH))
    return y + b


def _fwd_setup_inputs(seed: int = 0) -> dict:
    key = jax.random.key(seed)
    ks = jax.random.split(key, 20)

    def nrm(k, shape, scale):
        return jax.random.normal(k, shape, jnp.float32) * scale

    L = DEPTH
    return {
        "x": nrm(ks[0], (BATCH, SEQ, D_MODEL), 1.0),
        "norm_g": 1.0 + nrm(ks[1], (L, D_MODEL), 0.05),
        "w_in": nrm(ks[2], (L, D_MODEL, N_IN), D_MODEL ** -0.5),
        "sgu_ln_g": 1.0 + nrm(ks[3], (L, D_A), 0.05),
        "sgu_ln_b": nrm(ks[4], (L, D_A), 0.02),
        "sgu_w": nrm(ks[5], (L, A_GROUPS, SGU_BLOCK, SGU_BLOCK), SGU_BLOCK ** -0.5),
        "sgu_b": 1.0 + nrm(ks[6], (L, A_GROUPS, SGU_BLOCK), 0.05),
        "pool_w": nrm(ks[7], (L, B_GROUPS, B_GROUP_DIM, B_GROUP_DIM), B_GROUP_DIM ** -0.5),
        "pool_b": nrm(ks[8], (L, D_B), 0.01),
        "pool_scale": 1.0 + nrm(ks[9], (L, D_B), 0.1),
        "conv_w": nrm(ks[10], (L, CONV_WIDTH, D_C), CONV_WIDTH ** -0.5),
        "conv_b": nrm(ks[11], (L, D_C), 0.01),
        "w_branch_a": nrm(ks[12], (L, D_A, D_MODEL), D_A ** -0.5),
        "w_branch_b": nrm(ks[13], (L, D_B, D_MODEL), D_B ** -0.5),
        "w_branch_c": nrm(ks[14], (L, D_C, D_MODEL), D_C ** -0.5),
        "w_out": nrm(ks[15], (L, D_MODEL, D_MODEL), D_MODEL ** -0.5),
        "final_g": 1.0 + nrm(ks[16], (D_MODEL,), 0.05),
    }


def _fwd_reference(x, norm_g, w_in, sgu_ln_g, sgu_ln_b, sgu_w, sgu_b, pool_w, pool_b,
              pool_scale, conv_w, conv_b, w_branch_a, w_branch_b, w_branch_c,
              w_out, final_g):
    bsz, s, _ = x.shape
    for l in range(DEPTH):
        h = rmsnorm(x, norm_g[l])
        proj = jnp.einsum('bsd,dn->bsn', h, w_in[l])
        a_u, a_v, a_z, b_p, b_z, c_h, c_b, c_c, c_z, gate_logits = jnp.split(
            proj, SPLIT_OFFSETS, axis=-1)

        ya = spatial_gating(jax.nn.gelu(a_u), jax.nn.gelu(a_v), sgu_w[l], sgu_b[l],
                            sgu_ln_g[l], sgu_ln_b[l]) * jax.nn.silu(a_z)
        yb = multiscale_pool(b_p, pool_w[l], pool_b[l], pool_scale[l]) * jax.nn.silu(b_z)
        yc = c_b * causal_depthwise_conv(c_c * c_h, conv_w[l], conv_b[l]) * jax.nn.silu(c_z)

        gates = jax.nn.sigmoid(gate_logits.reshape(bsz, s, N_BRANCH, D_MODEL))
        merged = (gates[:, :, 0] * jnp.einsum('bsc,cd->bsd', ya, w_branch_a[l])
                  + gates[:, :, 1] * jnp.einsum('bsc,cd->bsd', yb, w_branch_b[l])
                  + gates[:, :, 2] * jnp.einsum('bsc,cd->bsd', yc, w_branch_c[l]))
        x = x + jnp.einsum('bsd,de->bse', merged, w_out[l])
    return rmsnorm(x, final_g)


import jax as _jax
import jax.numpy as _jnp

TWIN_FORMAT = 'train_step'
FWD_PARAMS = ['x', 'norm_g', 'w_in', 'sgu_ln_g', 'sgu_ln_b', 'sgu_w', 'sgu_b', 'pool_w', 'pool_b', 'pool_scale', 'conv_w', 'conv_b', 'w_branch_a', 'w_branch_b', 'w_branch_c', 'w_out', 'final_g']
TWIN_WEIGHTS = ['norm_g', 'w_in', 'sgu_ln_g', 'sgu_ln_b', 'sgu_w', 'sgu_b', 'pool_w', 'pool_b', 'pool_scale', 'conv_w', 'conv_b', 'w_branch_a', 'w_branch_b', 'w_branch_c', 'w_out', 'final_g']
TWIN_DIFF_INPUT = 'x'
TWIN_INPUTS = ['x', 'norm_g', 'w_in', 'sgu_ln_g', 'sgu_ln_b', 'sgu_w', 'sgu_b', 'pool_w', 'pool_b', 'pool_scale', 'conv_w', 'conv_b', 'w_branch_a', 'w_branch_b', 'w_branch_c', 'w_out', 'final_g', 'loss_target', 'm_norm_g', 'm_w_in', 'm_sgu_ln_g', 'm_sgu_ln_b', 'm_sgu_w', 'm_sgu_b', 'm_pool_w', 'm_pool_b', 'm_pool_scale', 'm_conv_w', 'm_conv_b', 'm_w_branch_a', 'm_w_branch_b', 'm_w_branch_c', 'm_w_out', 'm_final_g', 'v_norm_g', 'v_w_in', 'v_sgu_ln_g', 'v_sgu_ln_b', 'v_sgu_w', 'v_sgu_b', 'v_pool_w', 'v_pool_b', 'v_pool_scale', 'v_conv_w', 'v_conv_b', 'v_w_branch_a', 'v_w_branch_b', 'v_w_branch_c', 'v_w_out', 'v_final_g']
TWIN_OUTPUTS = ['loss', 'grad_x', 'grad_norm_g', 'grad_w_in', 'grad_sgu_ln_g', 'grad_sgu_ln_b', 'grad_sgu_w', 'grad_sgu_b', 'grad_pool_w', 'grad_pool_b', 'grad_pool_scale', 'grad_conv_w', 'grad_conv_b', 'grad_w_branch_a', 'grad_w_branch_b', 'grad_w_branch_c', 'grad_w_out', 'grad_final_g', 'delta_norm_g', 'delta_w_in', 'delta_sgu_ln_g', 'delta_sgu_ln_b', 'delta_sgu_w', 'delta_sgu_b', 'delta_pool_w', 'delta_pool_b', 'delta_pool_scale', 'delta_conv_w', 'delta_conv_b', 'delta_w_branch_a', 'delta_w_branch_b', 'delta_w_branch_c', 'delta_w_out', 'delta_final_g', 'new_m_norm_g', 'new_m_w_in', 'new_m_sgu_ln_g', 'new_m_sgu_ln_b', 'new_m_sgu_w', 'new_m_sgu_b', 'new_m_pool_w', 'new_m_pool_b', 'new_m_pool_scale', 'new_m_conv_w', 'new_m_conv_b', 'new_m_w_branch_a', 'new_m_w_branch_b', 'new_m_w_branch_c', 'new_m_w_out', 'new_m_final_g', 'new_v_norm_g', 'new_v_w_in', 'new_v_sgu_ln_g', 'new_v_sgu_ln_b', 'new_v_sgu_w', 'new_v_sgu_b', 'new_v_pool_w', 'new_v_pool_b', 'new_v_pool_scale', 'new_v_conv_w', 'new_v_conv_b', 'new_v_w_branch_a', 'new_v_w_branch_b', 'new_v_w_branch_c', 'new_v_w_out', 'new_v_final_g']
TWIN_LEAF_KINDS = {'loss': 'loss', 'grad_x': 'grad_x', 'grad_norm_g': 'grad_w', 'grad_w_in': 'grad_w', 'grad_sgu_ln_g': 'grad_w', 'grad_sgu_ln_b': 'grad_w', 'grad_sgu_w': 'grad_w', 'grad_sgu_b': 'grad_w', 'grad_pool_w': 'grad_w', 'grad_pool_b': 'grad_w', 'grad_pool_scale': 'grad_w', 'grad_conv_w': 'grad_w', 'grad_conv_b': 'grad_w', 'grad_w_branch_a': 'grad_w', 'grad_w_branch_b': 'grad_w', 'grad_w_branch_c': 'grad_w', 'grad_w_out': 'grad_w', 'grad_final_g': 'grad_w', 'delta_norm_g': 'delta_w', 'delta_w_in': 'delta_w', 'delta_sgu_ln_g': 'delta_w', 'delta_sgu_ln_b': 'delta_w', 'delta_sgu_w': 'delta_w', 'delta_sgu_b': 'delta_w', 'delta_pool_w': 'delta_w', 'delta_pool_b': 'delta_w', 'delta_pool_scale': 'delta_w', 'delta_conv_w': 'delta_w', 'delta_conv_b': 'delta_w', 'delta_w_branch_a': 'delta_w', 'delta_w_branch_b': 'delta_w', 'delta_w_branch_c': 'delta_w', 'delta_w_out': 'delta_w', 'delta_final_g': 'delta_w', 'new_m_norm_g': 'new_m', 'new_m_w_in': 'new_m', 'new_m_sgu_ln_g': 'new_m', 'new_m_sgu_ln_b': 'new_m', 'new_m_sgu_w': 'new_m', 'new_m_sgu_b': 'new_m', 'new_m_pool_w': 'new_m', 'new_m_pool_b': 'new_m', 'new_m_pool_scale': 'new_m', 'new_m_conv_w': 'new_m', 'new_m_conv_b': 'new_m', 'new_m_w_branch_a': 'new_m', 'new_m_w_branch_b': 'new_m', 'new_m_w_branch_c': 'new_m', 'new_m_w_out': 'new_m', 'new_m_final_g': 'new_m', 'new_v_norm_g': 'new_v', 'new_v_w_in': 'new_v', 'new_v_sgu_ln_g': 'new_v', 'new_v_sgu_ln_b': 'new_v', 'new_v_sgu_w': 'new_v', 'new_v_sgu_b': 'new_v', 'new_v_pool_w': 'new_v', 'new_v_pool_b': 'new_v', 'new_v_pool_scale': 'new_v', 'new_v_conv_w': 'new_v', 'new_v_conv_b': 'new_v', 'new_v_w_branch_a': 'new_v', 'new_v_w_branch_b': 'new_v', 'new_v_w_branch_c': 'new_v', 'new_v_w_out': 'new_v', 'new_v_final_g': 'new_v'}


def _forward(args):
    return _fwd_reference(*[args[k] for k in FWD_PARAMS])


def _output_shape():
    def fwd():
        inp = _fwd_setup_inputs(0)
        return _fwd_reference(*[inp[k] for k in FWD_PARAMS])
    out = _jax.eval_shape(fwd)
    return out.shape, out.dtype

N_MICROBATCH = 1
ADAM_LR = 0.001
ADAM_B1 = 0.9
ADAM_B2 = 0.999
ADAM_EPS = 1e-08
ADAM_WD = 0.01
ADAM_STEP = 10
PER_EXAMPLE_BATCH_AXIS = {'x': 0, 'loss_target': 0}
SHARED_INPUTS = []
_WEIGHT_DTYPES = {'norm_g': _jnp.float32, 'w_in': _jnp.float32, 'sgu_ln_g': _jnp.float32, 'sgu_ln_b': _jnp.float32, 'sgu_w': _jnp.float32, 'sgu_b': _jnp.float32, 'pool_w': _jnp.float32, 'pool_b': _jnp.float32, 'pool_scale': _jnp.float32, 'conv_w': _jnp.float32, 'conv_b': _jnp.float32, 'w_branch_a': _jnp.float32, 'w_branch_b': _jnp.float32, 'w_branch_c': _jnp.float32, 'w_out': _jnp.float32, 'final_g': _jnp.float32}
MOMENT_SCALE = {'norm_g': 3.391658e-01, 'w_in': 9.695521e-02, 'sgu_ln_g': 6.809796e-02, 'sgu_ln_b': 6.540155e-02, 'sgu_w': 6.652775e-02, 'sgu_b': 7.811751e-02, 'pool_w': 1.056861e-01, 'pool_b': 1.192323e-01, 'pool_scale': 1.042594e-01, 'conv_w': 1.184101e-01, 'conv_b': 1.176097e-01, 'w_branch_a': 1.023314e-01, 'w_branch_b': 1.052261e-01, 'w_branch_c': 1.177865e-01, 'w_out': 1.886476e-01, 'final_g': 1.284493e+02}


def _to_microbatches(a, axis):
    t = _jnp.moveaxis(a, axis, 0)
    t = t.reshape((N_MICROBATCH, t.shape[0] // N_MICROBATCH) + t.shape[1:])
    return _jnp.moveaxis(t, 1, axis + 1)


def setup_inputs(seed: int = 0) -> dict:
    inp = _fwd_setup_inputs(seed)
    key = _jax.random.fold_in(_jax.random.key(seed), 7919)
    shape, _ = _output_shape()
    out = dict(inp)
    out["loss_target"] = _jax.random.normal(_jax.random.fold_in(key, 0), shape, _jnp.float32)
    for i, name in enumerate(TWIN_WEIGHTS):
        w = inp[name].astype(_jnp.float32)
        if MOMENT_SCALE is None:
            s = _jnp.sqrt(_jnp.mean(_jnp.square(w)) + 1e-30)
        else:
            s = MOMENT_SCALE[name]
        km, kv = _jax.random.split(_jax.random.fold_in(key, i + 1))
        out[name] = w
        out["m_" + name] = s * _jax.random.normal(km, w.shape, _jnp.float32)
        out["v_" + name] = (s * s) * _jax.random.uniform(kv, w.shape, _jnp.float32, 0.5, 1.5)
    if N_MICROBATCH > 1:
        for name, axis in PER_EXAMPLE_BATCH_AXIS.items():
            out[name] = _to_microbatches(out[name], axis)
    return {'x': out['x'], 'norm_g': out['norm_g'], 'w_in': out['w_in'], 'sgu_ln_g': out['sgu_ln_g'], 'sgu_ln_b': out['sgu_ln_b'], 'sgu_w': out['sgu_w'], 'sgu_b': out['sgu_b'], 'pool_w': out['pool_w'], 'pool_b': out['pool_b'], 'pool_scale': out['pool_scale'], 'conv_w': out['conv_w'], 'conv_b': out['conv_b'], 'w_branch_a': out['w_branch_a'], 'w_branch_b': out['w_branch_b'], 'w_branch_c': out['w_branch_c'], 'w_out': out['w_out'], 'final_g': out['final_g'], 'loss_target': out['loss_target'], 'm_norm_g': out['m_norm_g'], 'm_w_in': out['m_w_in'], 'm_sgu_ln_g': out['m_sgu_ln_g'], 'm_sgu_ln_b': out['m_sgu_ln_b'], 'm_sgu_w': out['m_sgu_w'], 'm_sgu_b': out['m_sgu_b'], 'm_pool_w': out['m_pool_w'], 'm_pool_b': out['m_pool_b'], 'm_pool_scale': out['m_pool_scale'], 'm_conv_w': out['m_conv_w'], 'm_conv_b': out['m_conv_b'], 'm_w_branch_a': out['m_w_branch_a'], 'm_w_branch_b': out['m_w_branch_b'], 'm_w_branch_c': out['m_w_branch_c'], 'm_w_out': out['m_w_out'], 'm_final_g': out['m_final_g'], 'v_norm_g': out['v_norm_g'], 'v_w_in': out['v_w_in'], 'v_sgu_ln_g': out['v_sgu_ln_g'], 'v_sgu_ln_b': out['v_sgu_ln_b'], 'v_sgu_w': out['v_sgu_w'], 'v_sgu_b': out['v_sgu_b'], 'v_pool_w': out['v_pool_w'], 'v_pool_b': out['v_pool_b'], 'v_pool_scale': out['v_pool_scale'], 'v_conv_w': out['v_conv_w'], 'v_conv_b': out['v_conv_b'], 'v_w_branch_a': out['v_w_branch_a'], 'v_w_branch_b': out['v_w_branch_b'], 'v_w_branch_c': out['v_w_branch_c'], 'v_w_out': out['v_w_out'], 'v_final_g': out['v_final_g']}


def _loss(weights, diff, rest, loss_target):
    with _jax.named_scope("forward"):
        args = {**rest, TWIN_DIFF_INPUT: diff, **{k: w.astype(_WEIGHT_DTYPES[k]) for k, w in weights.items()}}
        y = _forward(args)
    with _jax.named_scope("loss_head"):
        err = _jnp.square(y.astype(_jnp.float32) - loss_target)
        return 0.5 * _jnp.sum(_jnp.mean(err, axis=-1)) if err.ndim else 0.5 * err


def _adamw(w, g, m, v):
    m = ADAM_B1 * m + (1.0 - ADAM_B1) * g
    v = ADAM_B2 * v + (1.0 - ADAM_B2) * _jnp.square(g)
    m_hat = m / (1.0 - ADAM_B1 ** ADAM_STEP)
    v_hat = v / (1.0 - ADAM_B2 ** ADAM_STEP)
    delta = -ADAM_LR * (m_hat / (_jnp.sqrt(v_hat) + ADAM_EPS) + ADAM_WD * w)
    return delta, m, v


def reference(x, norm_g, w_in, sgu_ln_g, sgu_ln_b, sgu_w, sgu_b, pool_w, pool_b, pool_scale, conv_w, conv_b, w_branch_a, w_branch_b, w_branch_c, w_out, final_g, loss_target, m_norm_g, m_w_in, m_sgu_ln_g, m_sgu_ln_b, m_sgu_w, m_sgu_b, m_pool_w, m_pool_b, m_pool_scale, m_conv_w, m_conv_b, m_w_branch_a, m_w_branch_b, m_w_branch_c, m_w_out, m_final_g, v_norm_g, v_w_in, v_sgu_ln_g, v_sgu_ln_b, v_sgu_w, v_sgu_b, v_pool_w, v_pool_b, v_pool_scale, v_conv_w, v_conv_b, v_w_branch_a, v_w_branch_b, v_w_branch_c, v_w_out, v_final_g):
    given = dict(x=x, norm_g=norm_g, w_in=w_in, sgu_ln_g=sgu_ln_g, sgu_ln_b=sgu_ln_b, sgu_w=sgu_w, sgu_b=sgu_b, pool_w=pool_w, pool_b=pool_b, pool_scale=pool_scale, conv_w=conv_w, conv_b=conv_b, w_branch_a=w_branch_a, w_branch_b=w_branch_b, w_branch_c=w_branch_c, w_out=w_out, final_g=final_g, loss_target=loss_target, m_norm_g=m_norm_g, m_w_in=m_w_in, m_sgu_ln_g=m_sgu_ln_g, m_sgu_ln_b=m_sgu_ln_b, m_sgu_w=m_sgu_w, m_sgu_b=m_sgu_b, m_pool_w=m_pool_w, m_pool_b=m_pool_b, m_pool_scale=m_pool_scale, m_conv_w=m_conv_w, m_conv_b=m_conv_b, m_w_branch_a=m_w_branch_a, m_w_branch_b=m_w_branch_b, m_w_branch_c=m_w_branch_c, m_w_out=m_w_out, m_final_g=m_final_g, v_norm_g=v_norm_g, v_w_in=v_w_in, v_sgu_ln_g=v_sgu_ln_g, v_sgu_ln_b=v_sgu_ln_b, v_sgu_w=v_sgu_w, v_sgu_b=v_sgu_b, v_pool_w=v_pool_w, v_pool_b=v_pool_b, v_pool_scale=v_pool_scale, v_conv_w=v_conv_w, v_conv_b=v_conv_b, v_w_branch_a=v_w_branch_a, v_w_branch_b=v_w_branch_b, v_w_branch_c=v_w_branch_c, v_w_out=v_w_out, v_final_g=v_final_g)
    weights = {n: given[n] for n in TWIN_WEIGHTS}
    shared = {n: given[n] for n in SHARED_INPUTS}
    per_example = {n: given[n] for n in ['x']}
    grad_fn = _jax.value_and_grad(_loss, argnums=(0, 1))

    def one_microbatch(ex, loss_target):
        ex = dict(ex)
        diff = ex.pop(TWIN_DIFF_INPUT)
        return grad_fn(weights, diff, {**shared, **ex}, loss_target)

    if N_MICROBATCH == 1:
        loss, (grad_w, grad_x) = one_microbatch(per_example, given["loss_target"])
    else:
        def body(carry, xs):
            loss_sum, grad_sum = carry
            l_k, (gw_k, gx_k) = one_microbatch(xs[0], xs[1])
            with _jax.named_scope("update"):
                return (loss_sum + l_k, _jax.tree.map(_jnp.add, grad_sum, gw_k)), gx_k

        init = (_jnp.zeros((), _jnp.float32), _jax.tree.map(_jnp.zeros_like, weights))
        (loss, grad_w), grad_x = _jax.lax.scan(body, init, (per_example, given["loss_target"]))
    with _jax.named_scope("update"):
        delta_w, new_m, new_v = {}, {}, {}
        for n in TWIN_WEIGHTS:
            delta_w[n], new_m[n], new_v[n] = _adamw(weights[n], grad_w[n], given["m_" + n], given["v_" + n])
    return (loss, grad_x, *[grad_w[n] for n in TWIN_WEIGHTS], *[delta_w[n] for n in TWIN_WEIGHTS],
            *[new_m[n] for n in TWIN_WEIGHTS], *[new_v[n] for n in TWIN_WEIGHTS])
```

```python
import functools

import jax
import jax.numpy as jnp
from jax import lax
from jax.experimental import pallas as pl
from jax.experimental.pallas import tpu as pltpu

F32 = jnp.float32
BF = jnp.bfloat16
MESH = pl.DeviceIdType.MESH

D = 1024
L = 4
NSH = 4
SHW = 3 * D
NIN = NSH * SHW
GRP = 8
BLK = 128
NPG = 4
PGW = D // NPG
POOL_WINDOWS = (2, 4, 8, 16)
HALO = 16
RMS_EPS = 1e-6
LN_EPS = 1e-5
ADAM_LR, ADAM_B1, ADAM_B2, ADAM_EPS, ADAM_WD, ADAM_STEP = 0.001, 0.9, 0.999, 1e-8, 0.01, 10

RC = 16
CW = 512
COLS = tuple(range(0, D, CW))
MIB = 1 << 20

_C0 = 0.7978845608028654
_C1 = 0.044715


def _pcall(body, **kw):
    return pl.pallas_call(body, **kw)


def _params(sem, vmem_mib):
    return pltpu.CompilerParams(dimension_semantics=sem, vmem_limit_bytes=vmem_mib * MIB)


def _mm(a, b):
    return jnp.dot(a.astype(BF), b.astype(BF), preferred_element_type=F32)


def _mm_nt(a, b):
    return lax.dot_general(a.astype(BF), b.astype(BF), (((1,), (1,)), ((), ())), preferred_element_type=F32)


def _mm_tn(a, b):
    return lax.dot_general(a.astype(BF), b.astype(BF), (((0,), (0,)), ((), ())), preferred_element_type=F32)


def _gelu(x):
    return 0.5 * x * (1.0 + jnp.tanh(_C0 * x * (1.0 + _C1 * x * x)))


def _gelu_fg(x):
    x2 = x * x
    t = jnp.tanh(_C0 * x * (1.0 + _C1 * x2))
    f = 0.5 * x * (1.0 + t)
    g = 0.5 * (1.0 + t) + 0.5 * x * (1.0 - t * t) * (_C0 * (1.0 + 3.0 * _C1 * x2))
    return f, g


def _silu(x):
    return x * jax.nn.sigmoid(x)


def _silu_fg(x):
    s = jax.nn.sigmoid(x)
    return x * s, s * (1.0 + x * (1.0 - s))


def _rowsum(v):
    return jnp.sum(v, axis=1, keepdims=True)


def _chunks(n_rows, fn):
    def body(c, carry):
        fn(pl.multiple_of(c * RC, RC))
        return carry
    lax.fori_loop(0, n_rows // RC, body, 0)


def _trail(e, win):
    s, sh = e, 1
    while sh < win:
        s = s + pltpu.roll(s, sh, 0)
        sh *= 2
    return s


def _lead(e, win):
    n = e.shape[0]
    s, sh = e, 1
    while sh < win:
        s = s + pltpu.roll(s, n - sh, 0)
        sh *= 2
    return s


def _inv_count(pos0, win):
    pos = pos0 + lax.broadcasted_iota(jnp.int32, (RC, 1), 0)
    return 1.0 / jnp.minimum(pos + 1, win).astype(F32)


def _masked_sgu(sgw_ref, g):
    r = lax.broadcasted_iota(jnp.int32, (BLK, BLK), 0) // 64
    c = lax.broadcasted_iota(jnp.int32, (BLK, BLK), 1) // 64
    return jnp.where(c <= r, sgw_ref[g], 0.0)


def _tile(t, want):
    return min(t, want)


def _fwd_in(x, ng, w4, l, lng, lnb, sgw, sgbT, pwb, pb, psc, cw, cb):
    T = x.shape[0]
    tm = _tile(T, 512)
    nT = T // tm

    def body(x_ref, ng_ref, w_ref, lng_ref, lnb_ref, sgw_ref, sgbT_ref, pw_ref, pb_ref, psc_ref, cw_ref, cb_ref,
             proj_ref, h_ref, y_ref, h_s, ch_s, vn_s, mix_s, d_s, wm_s, extp_s, extc_s):
        i = pl.program_id(0)
        p = pl.program_id(1)

        @pl.when((i == 0) & (p == 0))
        def _():
            extp_s[0:HALO, :] = jnp.zeros((HALO, D), F32)
            extc_s[0:HALO, :] = jnp.zeros((HALO, D), F32)
            for g in range(GRP):
                wm_s[g] = _masked_sgu(sgw_ref, g).astype(BF)

        @pl.when(p == 0)
        def _():
            def f(r0):
                rows = pl.ds(r0, RC)
                xs = [x_ref[rows, c0:c0 + CW] for c0 in COLS]
                ms = sum(_rowsum(v * v) for v in xs) * (1.0 / D)
                r = lax.rsqrt(ms + RMS_EPS)
                for k, c0 in enumerate(COLS):
                    hb = (xs[k] * r * ng_ref[:, c0:c0 + CW]).astype(BF)
                    h_s[rows, c0:c0 + CW] = hb
                    h_ref[rows, c0:c0 + CW] = hb
            _chunks(tm, f)

        for b in range(3):
            proj_ref[:, b * D:(b + 1) * D] = jnp.dot(h_s[...], w_ref[:, b * D:(b + 1) * D], preferred_element_type=F32)

        @pl.when(p == 0)
        def _():
            def f1(r0):
                rows = pl.ds(r0, RC)
                gv = [_gelu(proj_ref[rows, D + c0:D + c0 + CW]) for c0 in COLS]
                mu = sum(_rowsum(v) for v in gv) * (1.0 / D)
                dv = [v - mu for v in gv]
                var = sum(_rowsum(v * v) for v in dv) * (1.0 / D)
                rstd = lax.rsqrt(var + LN_EPS)
                for k, c0 in enumerate(COLS):
                    vn_s[rows, c0:c0 + CW] = (dv[k] * rstd * lng_ref[:, c0:c0 + CW] + lnb_ref[:, c0:c0 + CW]).astype(BF)
            _chunks(tm, f1)
            for g in range(GRP):
                w = wm_s[g]
                bcol = sgbT_ref[:, g:g + 1]
                gc = slice(g * BLK, (g + 1) * BLK)
                for n in range(tm // BLK):
                    rr = slice(n * BLK, (n + 1) * BLK)
                    mix_s[rr, gc] = jnp.dot(w, vn_s[rr, gc], preferred_element_type=F32) + bcol

            def f2(r0):
                rows = pl.ds(r0, RC)
                for c0 in COLS:
                    au = proj_ref[rows, c0:c0 + CW]
                    az = proj_ref[rows, 2 * D + c0:2 * D + c0 + CW]
                    y_ref[rows, c0:c0 + CW] = (_gelu(au) * mix_s[rows, c0:c0 + CW] * _silu(az)).astype(BF)
            _chunks(tm, f2)

        @pl.when(p == 1)
        def _():
            extp_s[HALO:HALO + tm, :] = proj_ref[:, 0:D]
            ch_s[...] = proj_ref[:, 2 * D:3 * D]

            def f1(r0):
                rows = pl.ds(r0, RC)
                e = extp_s[pl.ds(r0, 2 * RC), :]
                for k, win in enumerate(POOL_WINDOWS):
                    gc = slice(k * PGW, (k + 1) * PGW)
                    s = _trail(e[:, gc], win)[RC:2 * RC]
                    d_s[rows, gc] = (s * _inv_count(i * tm + r0, win) - e[RC:2 * RC, gc]).astype(BF)
            _chunks(tm, f1)
            extp_s[0:HALO, :] = extp_s[tm:tm + HALO, :]
            for k in range(NPG):
                gc = slice(k * PGW, (k + 1) * PGW)
                mix_s[:, gc] = jnp.dot(d_s[:, gc], pw_ref[k], preferred_element_type=F32)

            def f2(r0):
                rows = pl.ds(r0, RC)
                for c0 in COLS:
                    cs = slice(c0, c0 + CW)
                    bz = proj_ref[rows, D + c0:D + c0 + CW]
                    y_ref[rows, cs] = ((mix_s[rows, cs] + pb_ref[:, cs]) * psc_ref[:, cs] * _silu(bz)).astype(BF)
            _chunks(tm, f2)

        @pl.when(p == 2)
        def _():
            def f1(r0):
                rows = pl.ds(r0, RC)
                extc_s[pl.ds(HALO + r0, RC), :] = proj_ref[rows, D:2 * D] * ch_s[rows, :]
            _chunks(tm, f1)

            def f2(r0):
                rows = pl.ds(r0, RC)
                for c0 in COLS:
                    cs = slice(c0, c0 + CW)
                    e = extc_s[pl.ds(r0, 2 * RC), cs]
                    conv = (cw_ref[2:3, cs] * e + cw_ref[1:2, cs] * pltpu.roll(e, 1, 0)
                            + cw_ref[0:1, cs] * pltpu.roll(e, 2, 0))[RC:2 * RC] + cb_ref[:, cs]
                    cbv = proj_ref[rows, cs]
                    cz = proj_ref[rows, 2 * D + c0:2 * D + c0 + CW]
                    y_ref[rows, cs] = (cbv * conv * _silu(cz)).astype(BF)
            _chunks(tm, f2)
            extc_s[0:HALO, :] = extc_s[tm:tm + HALO, :]

    vec = pl.BlockSpec((1, D), lambda i, p: (0, 0))
    return _pcall(
        body, name="fwd_in", grid=(nT, NSH),
        in_specs=[
            pl.BlockSpec((tm, D), lambda i, p: (i, 0)), vec,
            pl.BlockSpec((None, None, D, SHW), lambda i, p: (p, l, 0, 0)),
            vec, vec,
            pl.BlockSpec((GRP, BLK, BLK), lambda i, p: (0, 0, 0)),
            pl.BlockSpec((BLK, GRP), lambda i, p: (0, 0)),
            pl.BlockSpec((NPG, PGW, PGW), lambda i, p: (0, 0, 0)),
            vec, vec,
            pl.BlockSpec((3, D), lambda i, p: (0, 0)), vec,
        ],
        out_specs=[
            pl.BlockSpec((tm, SHW), lambda i, p: (i, p)),
            pl.BlockSpec((tm, D), lambda i, p: (i, 0)),
            pl.BlockSpec((None, tm, D), lambda i, p: (jnp.minimum(p, 2), i, 0)),
        ],
        out_shape=[jax.ShapeDtypeStruct((T, NIN), F32), jax.ShapeDtypeStruct((T, D), BF),
                   jax.ShapeDtypeStruct((3, T, D), BF)],
        scratch_shapes=[
            pltpu.VMEM((tm, D), BF), pltpu.VMEM((tm, D), F32), pltpu.VMEM((tm, D), BF), pltpu.VMEM((tm, D), F32),
            pltpu.VMEM((tm, D), BF), pltpu.VMEM((GRP, BLK, BLK), BF),
            pltpu.VMEM((tm + HALO, D), F32), pltpu.VMEM((tm + HALO, D), F32),
        ],
        compiler_params=_params(("arbitrary", "arbitrary"), 56),
    )(x, ng, w4, lng, lnb, sgw, sgbT, pwb, pb, psc, cw, cb)


def _fwd_out(y3, proj, x, wbr):
    T = x.shape[0]
    tm = _tile(T, 256)

    def body(y_ref, gl_ref, x_ref, w_ref, p_ref, mg_ref, xo_ref):
        for k in range(3):
            p_ref[k] = jnp.dot(y_ref[k], w_ref[k], preferred_element_type=F32)

        def f(r0):
            rows = pl.ds(r0, RC)
            for c0 in COLS:
                cs = slice(c0, c0 + CW)
                m = sum(jax.nn.sigmoid(gl_ref[rows, k * D + c0:k * D + c0 + CW]) * p_ref[k, rows, cs] for k in range(3))
                mg_ref[rows, cs] = m.astype(BF)
        _chunks(tm, f)
        xo_ref[...] = x_ref[...] + jnp.dot(mg_ref[...], w_ref[3], preferred_element_type=F32)

    return _pcall(
        body, name="fwd_out", grid=(T // tm,),
        in_specs=[
            pl.BlockSpec((3, tm, D), lambda i: (0, i, 0)),
            pl.BlockSpec((tm, SHW), lambda i: (i, 3)),
            pl.BlockSpec((tm, D), lambda i: (i, 0)),
            pl.BlockSpec((4, D, D), lambda i: (0, 0, 0)),
        ],
        out_specs=[
            pl.BlockSpec((3, tm, D), lambda i: (0, i, 0)),
            pl.BlockSpec((tm, D), lambda i: (i, 0)),
            pl.BlockSpec((tm, D), lambda i: (i, 0)),
        ],
        out_shape=[jax.ShapeDtypeStruct((3, T, D), F32), jax.ShapeDtypeStruct((T, D), BF),
                   jax.ShapeDtypeStruct((T, D), F32)],
        compiler_params=_params(("arbitrary",), 48),
    )(y3, proj, x, wbr)


def _loss_head(x, fg, tgt):
    T = x.shape[0]
    tm = _tile(T, 512)

    def body(x_ref, g_ref, t_ref, dx_ref, sq_ref, dg_ref, acc_s):
        i = pl.program_id(0)

        @pl.when(i == 0)
        def _():
            acc_s[...] = jnp.zeros_like(acc_s)

        def f(r0):
            rows = pl.ds(r0, RC)
            xs = [x_ref[rows, c0:c0 + CW] for c0 in COLS]
            r = lax.rsqrt(sum(_rowsum(v * v) for v in xs) * (1.0 / D) + RMS_EPS)
            xh = [v * r for v in xs]
            dyg, m = [], 0.0
            for k, c0 in enumerate(COLS):
                cs = slice(c0, c0 + CW)
                err = xh[k] * g_ref[:, cs] - t_ref[rows, cs]
                acc_s[0, :, cs] += err * err
                dy = err * (1.0 / D)
                acc_s[1, :, cs] += dy * xh[k]
                dyg.append(dy * g_ref[:, cs])
                m = m + _rowsum(dyg[k] * xh[k])
            m = m * (1.0 / D)
            for k, c0 in enumerate(COLS):
                dx_ref[rows, c0:c0 + CW] = r * (dyg[k] - xh[k] * m)
        _chunks(tm, f)

        @pl.when(i == pl.num_programs(0) - 1)
        def _():
            sq_ref[...] = jnp.sum(acc_s[0], axis=0, keepdims=True)
            dg_ref[...] = jnp.sum(acc_s[1], axis=0, keepdims=True)

    vec = pl.BlockSpec((1, D), lambda i: (0, 0))
    tile = pl.BlockSpec((tm, D), lambda i: (i, 0))
    return _pcall(
        body, name="loss_head", grid=(T // tm,),
        in_specs=[tile, vec, tile], out_specs=[tile, vec, vec],
        out_shape=[jax.ShapeDtypeStruct((T, D), F32), jax.ShapeDtypeStruct((1, D), F32), jax.ShapeDtypeStruct((1, D), F32)],
        scratch_shapes=[pltpu.VMEM((2, RC, D), F32)],
        compiler_params=_params(("arbitrary",), 32),
    )(x, fg, tgt)


def _bwd_mix(dxo, p3, proj, wbr, lng, lnb, sgw, sgbT, pwb, pb, psc, cw, cb):
    T = dxo.shape[0]
    tm = _tile(T, 256)
    nT = T // tm
    hb = tm // HALO

    def body(dxo_ref, p_ref, pj_ref, ch_ref, bpp_ref, ccp_ref, chp_ref, w_ref, lng_ref, lnb_ref, sgw_ref, sgbT_ref,
             pw_ref, pb_ref, psc_ref, cw_ref, cb_ref,
             dpj_ref, dp_ref, dsw_ref, dsbT_ref, vec_ref, dpw_ref,
             dy_s, ext_s, nxt_s, halo_s, dch_s, d_s, t_s, dy0_s, vn_s, dmix_s, xh_s, rstd_s, acc_s, accb_s, wm_s, wmT_s):
        i = pl.program_id(0)
        p = pl.program_id(1)
        ti = nT - 1 - i

        @pl.when((i == 0) & (p == 0))
        def _():
            dsw_ref[...] = jnp.zeros_like(dsw_ref)
            dsbT_ref[...] = jnp.zeros_like(dsbT_ref)
            vec_ref[...] = jnp.zeros_like(vec_ref)
            dpw_ref[...] = jnp.zeros_like(dpw_ref)
            halo_s[...] = jnp.zeros_like(halo_s)
            for g in range(GRP):
                wm = _masked_sgu(sgw_ref, g)
                wm_s[g] = wm.astype(BF)
                wmT_s[g] = wm.T.astype(BF)

        def flush(n_acc, rows_of):
            for a in range(n_acc):
                vec_ref[rows_of[a]:rows_of[a] + 1, :] += jnp.sum(acc_s[a], axis=0, keepdims=True)

        @pl.when(p == 0)
        def _():
            dy_s[...] = _mm_nt(dxo_ref[...], w_ref[...])

            def f(r0):
                rows = pl.ds(r0, RC)
                for c0 in COLS:
                    cs = slice(c0, c0 + CW)
                    dm = dy_s[rows, cs]
                    for k in range(3):
                        s = jax.nn.sigmoid(pj_ref[rows, k * D + c0:k * D + c0 + CW])
                        dp_ref[k, rows, cs] = (s * dm).astype(BF)
                        dpj_ref[k, rows, cs] = (dm * p_ref[k, rows, cs] * s * (1.0 - s)).astype(BF)
            _chunks(tm, f)

        @pl.when(p == 1)
        def _():
            dy_s[...] = _mm_nt(dp_ref[2], w_ref[...])
            acc_s[...] = jnp.zeros_like(acc_s)
            ext_s[0:HALO, :] = jnp.where(ti > 0, ccp_ref[...] * chp_ref[...], 0.0)
            nxt_s[tm:tm + HALO, :] = halo_s[0]

            def f1(r0):
                rows = pl.ds(r0, RC)
                ext_s[pl.ds(HALO + r0, RC), :] = pj_ref[rows, D:2 * D] * ch_ref[rows, :]
            _chunks(tm, f1)

            def f2(r0):
                rows = pl.ds(r0, RC)
                for c0 in COLS:
                    cs = slice(c0, c0 + CW)
                    e = ext_s[pl.ds(r0, 2 * RC), cs]
                    e0, e1, e2 = e[RC:2 * RC], pltpu.roll(e, 1, 0)[RC:2 * RC], pltpu.roll(e, 2, 0)[RC:2 * RC]
                    conv = cw_ref[2:3, cs] * e0 + cw_ref[1:2, cs] * e1 + cw_ref[0:1, cs] * e2 + cb_ref[:, cs]
                    cbv = pj_ref[rows, cs]
                    sz, sg = _silu_fg(pj_ref[rows, 2 * D + c0:2 * D + c0 + CW])
                    dyc = dy_s[rows, cs]
                    dconv = dyc * cbv * sz
                    dpj_ref[0, rows, cs] = (dyc * conv * sz).astype(BF)
                    dpj_ref[2, rows, cs] = (dyc * cbv * conv * sg).astype(BF)
                    nxt_s[rows, cs] = dconv
                    acc_s[0, :, cs] += dconv
                    acc_s[1, :, cs] += dconv * e2
                    acc_s[2, :, cs] += dconv * e1
                    acc_s[3, :, cs] += dconv * e0
            _chunks(tm, f2)

            def f3(r0):
                rows = pl.ds(r0, RC)
                for c0 in COLS:
                    cs = slice(c0, c0 + CW)
                    e = nxt_s[pl.ds(r0, 2 * RC), cs]
                    dcc = (cw_ref[2:3, cs] * e + cw_ref[1:2, cs] * pltpu.roll(e, 2 * RC - 1, 0)
                           + cw_ref[0:1, cs] * pltpu.roll(e, 2 * RC - 2, 0))[0:RC]
                    dpj_ref[1, rows, cs] = (dcc * ch_ref[rows, cs]).astype(BF)
                    dch_s[rows, cs] = dcc * pj_ref[rows, D + c0:D + c0 + CW]
            _chunks(tm, f3)
            halo_s[0] = nxt_s[0:HALO, :]
            flush(4, (4, 5, 6, 7))

        @pl.when(p == 2)
        def _():
            dy_s[...] = _mm_nt(dp_ref[1], w_ref[...])
            acc_s[...] = jnp.zeros_like(acc_s)
            ext_s[0:HALO, :] = jnp.where(ti > 0, bpp_ref[...], 0.0)
            ext_s[HALO:HALO + tm, :] = pj_ref[:, 0:D]
            nxt_s[tm:tm + HALO, :] = halo_s[1]

            def f1(r0):
                rows = pl.ds(r0, RC)
                e = ext_s[pl.ds(r0, 2 * RC), :]
                for k, win in enumerate(POOL_WINDOWS):
                    gc = slice(k * PGW, (k + 1) * PGW)
                    s = _trail(e[:, gc], win)[RC:2 * RC]
                    d_s[rows, gc] = (s * _inv_count(ti * tm + r0, win) - e[RC:2 * RC, gc]).astype(BF)
            _chunks(tm, f1)
            for k in range(NPG):
                gc = slice(k * PGW, (k + 1) * PGW)
                t_s[:, gc] = jnp.dot(d_s[:, gc], pw_ref[k], preferred_element_type=F32)

            def f2(r0):
                rows = pl.ds(r0, RC)
                for c0 in COLS:
                    cs = slice(c0, c0 + CW)
                    y0 = t_s[rows, cs] + pb_ref[:, cs]
                    sz, sg = _silu_fg(pj_ref[rows, D + c0:D + c0 + CW])
                    dyb = dy_s[rows, cs]
                    dy0 = dyb * psc_ref[:, cs] * sz
                    acc_s[0, :, cs] += dy0
                    acc_s[1, :, cs] += dyb * y0 * sz
                    dy0_s[rows, cs] = dy0.astype(BF)
                    dpj_ref[1, rows, cs] = (dyb * y0 * psc_ref[:, cs] * sg).astype(BF)
            _chunks(tm, f2)
            for k in range(NPG):
                gc = slice(k * PGW, (k + 1) * PGW)
                dpw_ref[k] += _mm_tn(d_s[:, gc], dy0_s[:, gc])
                t_s[:, gc] = _mm_nt(dy0_s[:, gc], pw_ref[k])

            def f3(r0):
                rows = pl.ds(r0, RC)
                for k, win in enumerate(POOL_WINDOWS):
                    gc = slice(k * PGW, (k + 1) * PGW)
                    nxt_s[rows, gc] = t_s[rows, gc] * _inv_count(ti * tm + r0, win)
            _chunks(tm, f3)

            def f4(r0):
                rows = pl.ds(r0, RC)
                e = nxt_s[pl.ds(r0, 2 * RC), :]
                for k, win in enumerate(POOL_WINDOWS):
                    gc = slice(k * PGW, (k + 1) * PGW)
                    dpj_ref[0, rows, gc] = (_lead(e[:, gc], win)[0:RC] - t_s[rows, gc]).astype(BF)
                dpj_ref[2, rows, :] = dch_s[rows, :].astype(BF)
            _chunks(tm, f4)
            halo_s[1] = nxt_s[0:HALO, :]
            flush(2, (2, 3))

        @pl.when(p == 3)
        def _():
            dy_s[...] = _mm_nt(dp_ref[0], w_ref[...])
            acc_s[...] = jnp.zeros_like(acc_s)
            accb_s[...] = jnp.zeros_like(accb_s)

            def f1(r0):
                rows = pl.ds(r0, RC)
                gv = [_gelu(pj_ref[rows, D + c0:D + c0 + CW]) for c0 in COLS]
                mu = sum(_rowsum(v) for v in gv) * (1.0 / D)
                dv = [v - mu for v in gv]
                var = sum(_rowsum(v * v) for v in dv) * (1.0 / D)
                rstd = lax.rsqrt(var + LN_EPS)
                rstd_s[rows, :] = jnp.broadcast_to(rstd, (RC, BLK))
                for k, c0 in enumerate(COLS):
                    cs = slice(c0, c0 + CW)
                    xh = dv[k] * rstd
                    xh_s[rows, cs] = xh
                    vn_s[rows, cs] = (xh * lng_ref[:, cs] + lnb_ref[:, cs]).astype(BF)
            _chunks(tm, f1)
            for g in range(GRP):
                bcol = sgbT_ref[:, g:g + 1]
                gc = slice(g * BLK, (g + 1) * BLK)
                for n in range(tm // BLK):
                    rr = slice(n * BLK, (n + 1) * BLK)
                    t_s[rr, gc] = jnp.dot(wm_s[g], vn_s[rr, gc], preferred_element_type=F32) + bcol

            def f2(r0):
                rows = pl.ds(r0, RC)
                brow = pl.ds(pl.multiple_of(r0 % BLK, RC), RC)
                for c0 in COLS:
                    cs = slice(c0, c0 + CW)
                    gu, ggu = _gelu_fg(pj_ref[rows, cs])
                    sz, sg = _silu_fg(pj_ref[rows, 2 * D + c0:2 * D + c0 + CW])
                    dya = dy_s[rows, cs]
                    mix = t_s[rows, cs]
                    dmix = dya * gu * sz
                    dpj_ref[0, rows, cs] = (dya * mix * sz * ggu).astype(BF)
                    dpj_ref[2, rows, cs] = (dya * gu * mix * sg).astype(BF)
                    dmix_s[rows, cs] = dmix.astype(BF)
                    accb_s[brow, cs] += dmix
            _chunks(tm, f2)
            for g in range(GRP):
                gc = slice(g * BLK, (g + 1) * BLK)
                dsbT_ref[:, g:g + 1] += _rowsum(accb_s[:, gc])
                for n in range(tm // BLK):
                    rr = slice(n * BLK, (n + 1) * BLK)
                    t_s[rr, gc] = jnp.dot(wmT_s[g], dmix_s[rr, gc], preferred_element_type=F32)
                    dsw_ref[g] += _mm_nt(dmix_s[rr, gc], vn_s[rr, gc])

            def f3(r0):
                rows = pl.ds(r0, RC)
                rstd = rstd_s[rows, 0:1]
                dxh, m1, m2 = [], 0.0, 0.0
                for k, c0 in enumerate(COLS):
                    cs = slice(c0, c0 + CW)
                    dvn = t_s[rows, cs]
                    xh = xh_s[rows, cs]
                    acc_s[0, :, cs] += dvn * xh
                    acc_s[1, :, cs] += dvn
                    dxh.append(dvn * lng_ref[:, cs])
                    m1 = m1 + _rowsum(dxh[k])
                    m2 = m2 + _rowsum(dxh[k] * xh)
                m1 = m1 * (1.0 / D)
                m2 = m2 * (1.0 / D)
                for k, c0 in enumerate(COLS):
                    cs = slice(c0, c0 + CW)
                    _, ggv = _gelu_fg(pj_ref[rows, D + c0:D + c0 + CW])
                    dpj_ref[1, rows, cs] = (rstd * (dxh[k] - m1 - xh_s[rows, cs] * m2) * ggv).astype(BF)
            _chunks(tm, f3)
            flush(2, (0, 1))

            @pl.when(i == nT - 1)
            def _():
                for g in range(GRP):
                    r = lax.broadcasted_iota(jnp.int32, (BLK, BLK), 0) // 64
                    c = lax.broadcasted_iota(jnp.int32, (BLK, BLK), 1) // 64
                    dsw_ref[g] = jnp.where(c <= r, dsw_ref[g], 0.0)

    def prev(col):
        return pl.BlockSpec((HALO, D), lambda i, p: (jnp.maximum((nT - 1 - i) * hb - 1, 0), col))

    vec = pl.BlockSpec((1, D), lambda i, p: (0, 0))
    const3 = lambda i, p: (0, 0, 0)
    return _pcall(
        body, name="bwd_mix", grid=(nT, NSH),
        in_specs=[
            pl.BlockSpec((tm, D), lambda i, p: (nT - 1 - i, 0)),
            pl.BlockSpec((3, tm, D), lambda i, p: (0, nT - 1 - i, 0)),
            pl.BlockSpec((tm, SHW), lambda i, p: (nT - 1 - i, 3 - p)),
            pl.BlockSpec((tm, D), lambda i, p: (nT - 1 - i, 5)),
            prev(3), prev(7), prev(5),
            pl.BlockSpec((None, D, D), lambda i, p: (3 - p, 0, 0)),
            vec, vec,
            pl.BlockSpec((GRP, BLK, BLK), const3),
            pl.BlockSpec((BLK, GRP), lambda i, p: (0, 0)),
            pl.BlockSpec((NPG, PGW, PGW), const3),
            vec, vec,
            pl.BlockSpec((3, D), lambda i, p: (0, 0)), vec,
        ],
        out_specs=[
            pl.BlockSpec((3, tm, D), lambda i, p: (3 - p, nT - 1 - i, 0)),
            pl.BlockSpec((3, tm, D), lambda i, p: (0, nT - 1 - i, 0)),
            pl.BlockSpec((GRP, BLK, BLK), const3),
            pl.BlockSpec((BLK, GRP), lambda i, p: (0, 0)),
            pl.BlockSpec((8, D), lambda i, p: (0, 0)),
            pl.BlockSpec((NPG, PGW, PGW), const3),
        ],
        out_shape=[
            jax.ShapeDtypeStruct((12, T, D), BF), jax.ShapeDtypeStruct((3, T, D), BF),
            jax.ShapeDtypeStruct((GRP, BLK, BLK), F32), jax.ShapeDtypeStruct((BLK, GRP), F32),
            jax.ShapeDtypeStruct((8, D), F32), jax.ShapeDtypeStruct((NPG, PGW, PGW), F32),
        ],
        scratch_shapes=[
            pltpu.VMEM((tm, D), F32),
            pltpu.VMEM((tm + HALO, D), F32),
            pltpu.VMEM((tm + HALO, D), F32),
            pltpu.VMEM((2, HALO, D), F32),
            pltpu.VMEM((tm, D), F32),
            pltpu.VMEM((tm, D), BF),
            pltpu.VMEM((tm, D), F32),
            pltpu.VMEM((tm, D), BF),
            pltpu.VMEM((tm, D), BF),
            pltpu.VMEM((tm, D), BF),
            pltpu.VMEM((tm, D), F32),
            pltpu.VMEM((tm, BLK), F32),
            pltpu.VMEM((4, RC, D), F32),
            pltpu.VMEM((BLK, D), F32),
            pltpu.VMEM((GRP, BLK, BLK), BF), pltpu.VMEM((GRP, BLK, BLK), BF),
        ],
        compiler_params=_params(("arbitrary", "arbitrary"), 56),
    )(dxo, p3, proj, proj, proj, proj, proj, wbr, lng, lnb, sgw, sgbT, pwb, pb, psc, cw, cb)


def _bwd_in(dproj, w4, l, x, dxo, ng):
    T = x.shape[0]
    tm = _tile(T, 512)

    def body(dpj_ref, w_ref, x_ref, dxo_ref, ng_ref, dx_ref, dng_ref, acc_s, g_s):
        i = pl.program_id(0)
        j = pl.program_id(1)

        @pl.when((i == 0) & (j == 0))
        def _():
            g_s[...] = jnp.zeros_like(g_s)

        part = sum(_mm_nt(dpj_ref[b], w_ref[:, b * D:(b + 1) * D]) for b in range(3))

        @pl.when(j == 0)
        def _():
            acc_s[...] = part

        @pl.when(j > 0)
        def _():
            acc_s[...] += part

        @pl.when(j == NSH - 1)
        def _():
            def f(r0):
                rows = pl.ds(r0, RC)
                xs = [x_ref[rows, c0:c0 + CW] for c0 in COLS]
                r = lax.rsqrt(sum(_rowsum(v * v) for v in xs) * (1.0 / D) + RMS_EPS)
                xh = [v * r for v in xs]
                dhg, m = [], 0.0
                for k, c0 in enumerate(COLS):
                    cs = slice(c0, c0 + CW)
                    dh = acc_s[rows, cs]
                    g_s[:, cs] += dh * xh[k]
                    dhg.append(dh * ng_ref[:, cs])
                    m = m + _rowsum(dhg[k] * xh[k])
                m = m * (1.0 / D)
                for k, c0 in enumerate(COLS):
                    cs = slice(c0, c0 + CW)
                    dx_ref[rows, cs] = dxo_ref[rows, cs] + r * (dhg[k] - xh[k] * m)
            _chunks(tm, f)

        @pl.when((i == pl.num_programs(0) - 1) & (j == NSH - 1))
        def _():
            dng_ref[...] = jnp.sum(g_s[...], axis=0, keepdims=True)

    vec = pl.BlockSpec((1, D), lambda i, j: (0, 0))
    tile = pl.BlockSpec((tm, D), lambda i, j: (i, 0))
    return _pcall(
        body, name="bwd_in", grid=(T // tm, NSH),
        in_specs=[
            pl.BlockSpec((3, tm, D), lambda i, j: (j, i, 0)),
            pl.BlockSpec((None, None, D, SHW), lambda i, j: (j, l, 0, 0)),
            tile, tile, vec,
        ],
        out_specs=[tile, vec],
        out_shape=[jax.ShapeDtypeStruct((T, D), F32), jax.ShapeDtypeStruct((1, D), F32)],
        scratch_shapes=[pltpu.VMEM((tm, D), F32), pltpu.VMEM((RC, D), F32)],
        compiler_params=_params(("arbitrary", "arbitrary"), 48),
    )(dproj, w4, x, dxo, ng)


def _tn_grad(a3, b3, out_shape, out_map, name):
    nb, T, _ = b3.shape
    tk = _tile(T, 2048)
    nk = T // tk
    a_batched = a3.shape[0] > 1

    def body(a_ref, b_ref, o_ref, acc_s):
        k = pl.program_id(1)
        part = _mm_tn(a_ref[...], b_ref[...])

        @pl.when(k == 0)
        def _():
            acc_s[...] = part

        @pl.when(k > 0)
        def _():
            acc_s[...] += part

        @pl.when(k == nk - 1)
        def _():
            o_ref[...] = acc_s[...].astype(o_ref.dtype)

    return _pcall(
        body, name=name, grid=(nb, nk),
        in_specs=[
            pl.BlockSpec((None, tk, D), (lambda n, k: (n, k, 0)) if a_batched else (lambda n, k: (0, k, 0))),
            pl.BlockSpec((None, tk, D), lambda n, k: (n, k, 0)),
        ],
        out_specs=pl.BlockSpec((None, D, D), out_map),
        out_shape=jax.ShapeDtypeStruct(out_shape, BF),
        scratch_shapes=[pltpu.VMEM((D, D), F32)],
        compiler_params=_params(("arbitrary", "arbitrary"), 56),
    )(a3, b3)


def _place():
    x, y, c = lax.axis_index("x"), lax.axis_index("y"), lax.axis_index("c")
    chips = [(1 - x, y), (x, 1 - y), (1 - x, 1 - y)]
    return x, y, c, chips


ANY = pl.BlockSpec(memory_space=pl.ANY)


def _allgather_weights(srcs):
    n = len(srcs)

    def body(*refs):
        src, dst = refs[:n], refs[n:2 * n]
        ssem, rsem, lsem = refs[2 * n:]
        x, y, c, chips = _place()
        me = 2 * x + y
        local = [pltpu.make_async_copy(src[a], dst[a].at[me], lsem.at[a]) for a in range(n)]
        for cp in local:
            cp.start()
        sends = []
        for j, (px, py) in enumerate(chips):
            for a in range(n):
                sends.append(pltpu.make_async_remote_copy(
                    src_ref=src[a], dst_ref=dst[a].at[me], send_sem=ssem.at[j * n + a], recv_sem=rsem.at[j * n + a],
                    device_id=(px, py, c), device_id_type=MESH))
        for cp in sends:
            cp.start()
        for j, (px, py) in enumerate(chips):
            for a in range(n):
                pltpu.make_async_remote_copy(
                    src_ref=src[a], dst_ref=dst[a].at[2 * px + py], send_sem=ssem.at[j * n + a],
                    recv_sem=rsem.at[j * n + a], device_id=(px, py, c), device_id_type=MESH).wait_recv()
        for cp in sends:
            cp.wait_send()
        for cp in local:
            cp.wait()

    return _pcall(
        body, name="allgather_weights",
        in_specs=[ANY] * n, out_specs=[ANY] * n,
        out_shape=[jax.ShapeDtypeStruct((NSH,) + s.shape, s.dtype) for s in srcs],
        scratch_shapes=[pltpu.SemaphoreType.DMA((3 * n,)), pltpu.SemaphoreType.DMA((3 * n,)),
                        pltpu.SemaphoreType.DMA((n,))],
    )(*srcs)


def _grad_slices(refs, k):
    gw, gabc, go, gp = refs
    rows = pl.ds(pl.multiple_of(k * (D // NSH), D // NSH), D // NSH)
    prow = pl.ds(pl.multiple_of(k * (PGW // NSH), PGW // NSH), PGW // NSH)
    return [gw.at[k], gabc.at[:, rows, :], go.at[0, rows, :], gp.at[:, prow, :]]


def _scatter_grads(gw, gabc, go, gp):
    n = 4

    def body(*refs):
        src, dst = refs[:n], refs[n:2 * n]
        ssem, rsem, lsem = refs[2 * n:]
        x, y, c, chips = _place()
        me = 2 * x + y
        mine = _grad_slices(src, me)
        local = [pltpu.make_async_copy(mine[a], dst[a].at[me], lsem.at[a]) for a in range(n)]
        for cp in local:
            cp.start()
        sends = []
        for j, (px, py) in enumerate(chips):
            theirs = _grad_slices(src, 2 * px + py)
            for a in range(n):
                sends.append(pltpu.make_async_remote_copy(
                    src_ref=theirs[a], dst_ref=dst[a].at[me], send_sem=ssem.at[j * n + a], recv_sem=rsem.at[j * n + a],
                    device_id=(px, py, c), device_id_type=MESH))
        for cp in sends:
            cp.start()
        for j, (px, py) in enumerate(chips):
            for a in range(n):
                pltpu.make_async_remote_copy(
                    src_ref=mine[a], dst_ref=dst[a].at[2 * px + py], send_sem=ssem.at[j * n + a],
                    recv_sem=rsem.at[j * n + a], device_id=(px, py, c), device_id_type=MESH).wait_recv()
        for cp in sends:
            cp.wait_send()
        for cp in local:
            cp.wait()

    q = D // NSH
    shapes = [((NSH, D, SHW), gw.dtype), ((NSH, 3, q, D), gabc.dtype), ((NSH, q, D), go.dtype),
              ((NSH, NPG, PGW // NSH, PGW), gp.dtype)]
    return _pcall(
        body, name="scatter_grads",
        in_specs=[ANY] * n, out_specs=[ANY] * n,
        out_shape=[jax.ShapeDtypeStruct(s, d) for s, d in shapes],
        scratch_shapes=[pltpu.SemaphoreType.DMA((3 * n,)), pltpu.SemaphoreType.DMA((3 * n,)),
                        pltpu.SemaphoreType.DMA((n,))],
    )(gw, gabc, go, gp)


def _swap_sibling(arrs):
    n = len(arrs)

    def body(*refs):
        src, dst = refs[:n], refs[n:2 * n]
        ssem, rsem = refs[2 * n:]
        x, y, c, _ = _place()
        cps = [pltpu.make_async_remote_copy(src_ref=src[a], dst_ref=dst[a], send_sem=ssem.at[a], recv_sem=rsem.at[a],
                                            device_id=(x, y, 1 - c), device_id_type=MESH) for a in range(n)]
        for cp in cps:
            cp.start()
        for cp in cps:
            cp.wait()

    return _pcall(
        body, name="swap_sibling",
        in_specs=[ANY] * n, out_specs=[ANY] * n,
        out_shape=[jax.ShapeDtypeStruct(a.shape, a.dtype) for a in arrs],
        scratch_shapes=[pltpu.SemaphoreType.DMA((n,)), pltpu.SemaphoreType.DMA((n,))],
    )(*arrs)


def _gather_all(v):
    def body(src, dst, ssem, rsem, lsem):
        x, y, c, _ = _place()
        me = 4 * x + 2 * y + c
        peers = [(x, y, 1 - c), (1 - x, y, c), (1 - x, y, 1 - c), (x, 1 - y, c), (x, 1 - y, 1 - c),
                 (1 - x, 1 - y, c), (1 - x, 1 - y, 1 - c)]
        local = pltpu.make_async_copy(src, dst.at[me], lsem)
        local.start()
        sends = [pltpu.make_async_remote_copy(src_ref=src, dst_ref=dst.at[me], send_sem=ssem.at[j], recv_sem=rsem.at[j],
                                              device_id=pr, device_id_type=MESH) for j, pr in enumerate(peers)]
        for cp in sends:
            cp.start()
        for j, (px, py, pc) in enumerate(peers):
            pltpu.make_async_remote_copy(src_ref=src, dst_ref=dst.at[4 * px + 2 * py + pc], send_sem=ssem.at[j],
                                         recv_sem=rsem.at[j], device_id=(px, py, pc), device_id_type=MESH).wait_recv()
        for cp in sends:
            cp.wait_send()
        local.wait()

    return _pcall(
        body, name="gather_all", in_specs=[ANY], out_specs=ANY,
        out_shape=jax.ShapeDtypeStruct((8,) + v.shape, v.dtype),
        scratch_shapes=[pltpu.SemaphoreType.DMA((7,)), pltpu.SemaphoreType.DMA((7,)), pltpu.SemaphoreType.DMA(())],
    )(v)


def _sum_slots(r, rb):
    S = r.shape[0]

    def body(r_ref, o_ref):
        acc = r_ref[0].astype(F32)
        for s in range(1, S):
            acc = acc + r_ref[s].astype(F32)
        o_ref[...] = acc

    if r.ndim == 3:
        _, R, C = r.shape
        grid, blk, imap = (R // rb,), (S, rb, C), (lambda i: (0, i, 0))
        oblk, omap = (rb, C), (lambda i: (i, 0))
    else:
        _, K, R, C = r.shape
        grid, blk, imap = (K,), (S, None, R, C), (lambda i: (0, i, 0, 0))
        oblk, omap = (None, R, C), (lambda i: (i, 0, 0))
    return _pcall(
        body, name="sum_slots", grid=grid, in_specs=[pl.BlockSpec(blk, imap)], out_specs=pl.BlockSpec(oblk, omap),
        out_shape=jax.ShapeDtypeStruct(r.shape[1:], F32), compiler_params=_params(("arbitrary",), 48),
    )(r)


def _adamw(gs, w, m, v, lead, prev, rb):
    rest = w.shape[len(lead):]
    ng = len(gs)
    bc1 = 1.0 - ADAM_B1 ** ADAM_STEP
    bc2 = 1.0 - ADAM_B2 ** ADAM_STEP

    def body(*refs):
        g = refs[0][...]
        for a in range(1, ng):
            g = g + refs[a][...]
        w_ref, m_ref, v_ref = refs[ng:ng + 3]
        go, do, mo, vo = refs[ng + 3 + 4:]
        mn = ADAM_B1 * m_ref[...] + (1.0 - ADAM_B1) * g
        vn = ADAM_B2 * v_ref[...] + (1.0 - ADAM_B2) * (g * g)
        go[...] = g
        mo[...] = mn
        vo[...] = vn
        do[...] = -ADAM_LR * ((mn / bc1) / (jnp.sqrt(vn / bc2) + ADAM_EPS) + ADAM_WD * w_ref[...])

    def spec(ld):
        nl = len(ld)
        if len(rest) == 2:
            return pl.BlockSpec((None,) * nl + (rb, rest[1]), lambda i: tuple(ld) + (i, 0))
        return pl.BlockSpec((None,) * nl + (None, rest[1], rest[2]), lambda i: tuple(ld) + (i, 0, 0))

    grid = (rest[0] // rb,) if len(rest) == 2 else (rest[0],)
    out = jax.ShapeDtypeStruct(w.shape, F32)
    k0 = ng + 3
    return _pcall(
        body, name="adamw", grid=grid,
        in_specs=[spec(ld) for _, ld in gs] + [spec(lead)] * 3 + [ANY] * 4,
        out_specs=[spec(lead)] * 4, out_shape=[out] * 4,
        input_output_aliases={k0: 0, k0 + 1: 1, k0 + 2: 2, k0 + 3: 3},
        compiler_params=_params(("arbitrary",), 48),
    )(*[a for a, _ in gs], w, m, v, *prev)


def _empty4(w):
    return tuple(lax.empty(w.shape, F32) for _ in range(4))


N_SGW = L * GRP * BLK * BLK // D
O_NG, O_VEC, O_SGB, O_FG, O_SGW = 0, 8, 32, 40, 48
O_CW = O_SGW + N_SGW
N_PACK = O_CW + 16
PACK_RB = N_PACK // 3


def _pad_to(a, rows):
    return jnp.pad(a, ((0, rows - a.shape[0]), (0, 0)))


def _pack_small(ng, vecs, sgb, fg, sgw, cw):
    parts = [_pad_to(ng, 8), _pad_to(vecs.reshape(L * 5, D), 24), _pad_to(sgb.reshape(L, D), 8),
             _pad_to(fg.reshape(1, D), 8), sgw.reshape(N_SGW, D), _pad_to(cw, 16)]
    return jnp.concatenate(parts, axis=0)


def kernel(x, norm_g, w_in, sgu_ln_g, sgu_ln_b, sgu_w, sgu_b, pool_w, pool_b, pool_scale, conv_w, conv_b, w_branch_a, w_branch_b, w_branch_c, w_out, final_g, loss_target, m_norm_g, m_w_in, m_sgu_ln_g, m_sgu_ln_b, m_sgu_w, m_sgu_b, m_pool_w, m_pool_b, m_pool_scale, m_conv_w, m_conv_b, m_w_branch_a, m_w_branch_b, m_w_branch_c, m_w_out, m_final_g, v_norm_g, v_w_in, v_sgu_ln_g, v_sgu_ln_b, v_sgu_w, v_sgu_b, v_pool_w, v_pool_b, v_pool_scale, v_conv_w, v_conv_b, v_w_branch_a, v_w_branch_b, v_w_branch_c, v_w_out, v_final_g):
    cx, cy = lax.axis_index("x"), lax.axis_index("y")
    me = 2 * cx + cy
    xl, tgt = x[0], loss_target[0]
    q = D // NSH

    br = jnp.stack([w_branch_a, w_branch_b, w_branch_c, w_out]).astype(BF)
    W4, BR, PW, CWG = _allgather_weights([w_in.astype(BF), br, pool_w, conv_w])
    wbr = BR.transpose(2, 1, 0, 3, 4).reshape(L, 4, D, D)
    pwb = PW.transpose(1, 2, 0, 3, 4).reshape(L, NPG, PGW, PGW).astype(BF)
    cwf = CWG.transpose(1, 2, 0, 3).reshape(L, 3, D)
    sgbT = sgu_b.transpose(0, 2, 1)

    def lw(l):
        return (sgu_ln_g[l:l + 1], sgu_ln_b[l:l + 1], sgu_w[l], sgbT[l], pwb[l], pool_b[l:l + 1], pool_scale[l:l + 1],
                cwf[l], conv_b[l:l + 1])

    xs, saved = [xl], []
    for l in range(L):
        proj, h, y3 = _fwd_in(xs[l], norm_g[l:l + 1], W4, l, *lw(l))
        p3, mg, xo = _fwd_out(y3, proj, xs[l], wbr[l])
        saved.append((proj, h, y3, p3, mg))
        xs.append(xo)

    dx, sq, dfg = _loss_head(xs[L], final_g[None], tgt)
    loss = lax.psum(jnp.sum(sq) * (0.5 / D), ("x", "y", "c"))

    g_w_in = _empty4(w_in)
    g_br = [_empty4(w_out) for _ in range(4)]
    g_pw = _empty4(pool_w)
    dng, dvec, dsgw, dsgb = [None] * L, [None] * L, [None] * L, [None] * L
    for l in reversed(range(L)):
        proj, h, y3, p3, mg = saved[l]
        dproj, dp3, dsgw[l], dsbT, dvec[l], dpw = _bwd_mix(dx, p3, proj, wbr[l], *lw(l))
        dsgb[l] = dsbT.T
        gw = _tn_grad(h[None], dproj, (NSH, D, SHW), lambda n, k: (n // 3, 0, n % 3), "grad_w_in")
        gabc = _tn_grad(y3, dp3, (3, D, D), lambda n, k: (n, 0, 0), "grad_w_branch")
        go = _tn_grad(mg[None], dx[None], (1, D, D), lambda n, k: (n, 0, 0), "grad_w_out")
        dx, dng[l] = _bwd_in(dproj, W4, l, xs[l], dx, norm_g[l:l + 1])

        rW, rABC, rO, rP = _scatter_grads(gw, gabc, go, dpw)
        sums = [_sum_slots(rW, 256), _sum_slots(rABC, None), _sum_slots(rO, q), _sum_slots(rP, None)]
        got = _swap_sibling(sums)
        g_w_in = _adamw([(sums[0], ()), (got[0], ())], w_in, m_w_in, v_w_in, (l,), g_w_in, 128)
        for k, (w, m, v) in enumerate([(w_branch_a, m_w_branch_a, v_w_branch_a), (w_branch_b, m_w_branch_b, v_w_branch_b),
                                       (w_branch_c, m_w_branch_c, v_w_branch_c)]):
            g_br[k] = _adamw([(sums[1], (k,)), (got[1], (k,))], w, m, v, (l,), g_br[k], q)
        g_br[3] = _adamw([(sums[2], ()), (got[2], ())], w_out, m_w_out, v_w_out, (l,), g_br[3], q)
        g_pw = _adamw([(sums[3], ()), (got[3], ())], pool_w, m_pool_w, v_pool_w, (l,), g_pw, None)

    dvec = jnp.stack(dvec)
    small = _pack_small(jnp.concatenate(dng), dvec[:, 0:5], jnp.stack(dsgb), dfg[0], jnp.stack(dsgw),
                        dvec[:, 5:8].reshape(L * 3, D))
    gsmall = _sum_slots(_gather_all(small), PACK_RB)
    gcw = lax.dynamic_slice_in_dim(gsmall[O_CW:O_CW + L * 3], me * q, q, axis=1)
    gpack = jnp.concatenate([gsmall[:O_CW], _pad_to(gcw.reshape(L * 3 * q // D, D), 16)])

    def pack(ng, lg, lb, sw, sb, pb_, ps, cwv, cb_, fg):
        return _pack_small(ng, jnp.stack([lg, lb, pb_, ps, cb_], axis=1), sb, fg, sw, cwv.reshape(L * 3 * q // D, D))

    wp = pack(norm_g, sgu_ln_g, sgu_ln_b, sgu_w, sgu_b, pool_b, pool_scale, conv_w, conv_b, final_g)
    mp = pack(m_norm_g, m_sgu_ln_g, m_sgu_ln_b, m_sgu_w, m_sgu_b, m_pool_b, m_pool_scale, m_conv_w, m_conv_b, m_final_g)
    vp = pack(v_norm_g, v_sgu_ln_g, v_sgu_ln_b, v_sgu_w, v_sgu_b, v_pool_b, v_pool_scale, v_conv_w, v_conv_b, v_final_g)
    sm = _adamw([(gpack, ())], wp, mp, vp, (), _empty4(wp), PACK_RB)

    def unpack(a):
        vv = a[O_VEC:O_VEC + L * 5].reshape(L, 5, D)
        sb = a[O_SGB:O_SGB + L].reshape(L, GRP, BLK)
        fg = a[O_FG]
        sw = a[O_SGW:O_SGW + N_SGW].reshape(L, GRP, BLK, BLK)
        cwv = a[O_CW:O_CW + L * 3 * q // D].reshape(L, 3, q)
        return dict(norm_g=a[O_NG:O_NG + L], w_in=None, sgu_ln_g=vv[:, 0], sgu_ln_b=vv[:, 1], sgu_w=sw, sgu_b=sb, pool_w=None,
                    pool_b=vv[:, 2], pool_scale=vv[:, 3], conv_w=cwv, conv_b=vv[:, 4], w_branch_a=None,
                    w_branch_b=None, w_branch_c=None, w_out=None, final_g=fg)

    outs = [loss, dx[None]]
    for kind in range(4):
        d = unpack(sm[kind])
        d.update(w_in=g_w_in[kind], pool_w=g_pw[kind], w_branch_a=g_br[0][kind], w_branch_b=g_br[1][kind],
                 w_branch_c=g_br[2][kind], w_out=g_br[3][kind])
        outs.extend(d[n] for n in ("norm_g", "w_in", "sgu_ln_g", "sgu_ln_b", "sgu_w", "sgu_b", "pool_w", "pool_b",
                                   "pool_scale", "conv_w", "conv_b", "w_branch_a", "w_branch_b", "w_branch_c", "w_out",
                                   "final_g"))
    return tuple(outs)
```

```python
import functools

import jax
import jax.numpy as jnp
from jax import lax
from jax.experimental import pallas as pl
from jax.experimental.pallas import tpu as pltpu

F32 = jnp.float32
BF = jnp.bfloat16
MESH = pl.DeviceIdType.MESH

D = 1024
L = 4
NSH = 4
SHW = 3 * D
NIN = NSH * SHW
GRP = 8
BLK = 128
NPG = 4
PGW = D // NPG
POOL_WINDOWS = (2, 4, 8, 16)
HALO = 16
RMS_EPS = 1e-6
LN_EPS = 1e-5
ADAM_LR, ADAM_B1, ADAM_B2, ADAM_EPS, ADAM_WD, ADAM_STEP = 0.001, 0.9, 0.999, 1e-8, 0.01, 10

RC = 16
CW = 512
COLS = tuple(range(0, D, CW))
MIB = 1 << 20

PK_BR = SHW
PK_PW = PK_BR + D
PK_CW = PK_PW + NPG * (PGW // NSH) * PGW // D
PK_ROWS = PK_CW + 16

_C0 = 0.7978845608028654
_C1 = 0.044715


def _pcall(body, **kw):
    return pl.pallas_call(body, **kw)


def _params(sem, vmem_mib):
    return pltpu.CompilerParams(dimension_semantics=sem, vmem_limit_bytes=vmem_mib * MIB)


def _mm(a, b):
    return jnp.dot(a.astype(BF), b.astype(BF), preferred_element_type=F32)


def _mm_nt(a, b):
    return lax.dot_general(a.astype(BF), b.astype(BF), (((1,), (1,)), ((), ())), preferred_element_type=F32)


def _mm_tn(a, b):
    return lax.dot_general(a.astype(BF), b.astype(BF), (((0,), (0,)), ((), ())), preferred_element_type=F32)


def _gelu(x):
    return 0.5 * x * (1.0 + jnp.tanh(_C0 * x * (1.0 + _C1 * x * x)))


def _gelu_fg(x):
    x2 = x * x
    t = jnp.tanh(_C0 * x * (1.0 + _C1 * x2))
    f = 0.5 * x * (1.0 + t)
    g = 0.5 * (1.0 + t) + 0.5 * x * (1.0 - t * t) * (_C0 * (1.0 + 3.0 * _C1 * x2))
    return f, g


def _sigmoid(x):
    return 0.5 * jnp.tanh(0.5 * x) + 0.5


def _silu(x):
    return x * _sigmoid(x)


def _silu_fg(x):
    s = _sigmoid(x)
    return x * s, s * (1.0 + x * (1.0 - s))


def _rowsum(v):
    return jnp.sum(v, axis=1, keepdims=True)


def _chunks(n_rows, fn, unroll=2):
    def body(c, carry):
        fn(pl.multiple_of(c * RC, RC))
        return carry
    lax.fori_loop(0, n_rows // RC, body, 0, unroll=unroll)


def _trail(e, win):
    s, sh = e, 1
    while sh < win:
        s = s + pltpu.roll(s, sh, 0)
        sh *= 2
    return s


def _lead(e, win):
    n = e.shape[0]
    s, sh = e, 1
    while sh < win:
        s = s + pltpu.roll(s, n - sh, 0)
        sh *= 2
    return s


def _inv_count(pos0, win):
    pos = pos0 + lax.broadcasted_iota(jnp.int32, (RC, 1), 0)
    return 1.0 / jnp.minimum(pos + 1, win).astype(F32)


def _masked_sgu(sgw_ref, g):
    r = lax.broadcasted_iota(jnp.int32, (BLK, BLK), 0) // 64
    c = lax.broadcasted_iota(jnp.int32, (BLK, BLK), 1) // 64
    return jnp.where(c <= r, sgw_ref[g], 0.0)


def _tile(t, want):
    return min(t, want)


def _fwd_in(x, ng, wl, lng, lnb, sgw, sgbT, pwb, pb, psc, cw, cb):
    T = x.shape[0]
    tm = _tile(T, 512)
    nT = T // tm

    def body(x_ref, ng_ref, w_ref, lng_ref, lnb_ref, sgw_ref, sgbT_ref, pw_ref, pb_ref, psc_ref, cw_ref, cb_ref,
             proj_ref, h_ref, y_ref, h_s, ch_s, vn_s, mix_s, d_s, wm_s, extp_s, extc_s):
        i = pl.program_id(0)
        p = pl.program_id(1)

        @pl.when((i == 0) & (p == 0))
        def _():
            extp_s[0:HALO, :] = jnp.zeros((HALO, D), F32)
            extc_s[0:HALO, :] = jnp.zeros((HALO, D), F32)
            for g in range(GRP):
                wm_s[g] = _masked_sgu(sgw_ref, g).astype(BF)

        @pl.when(p == 0)
        def _():
            def f(r0):
                rows = pl.ds(r0, RC)
                xs = [x_ref[rows, c0:c0 + CW] for c0 in COLS]
                ms = sum(_rowsum(v * v) for v in xs) * (1.0 / D)
                r = lax.rsqrt(ms + RMS_EPS)
                for k, c0 in enumerate(COLS):
                    hb = (xs[k] * r * ng_ref[:, c0:c0 + CW]).astype(BF)
                    h_s[rows, c0:c0 + CW] = hb
                    h_ref[rows, c0:c0 + CW] = hb
            _chunks(tm, f, unroll=4)

        for b in range(3):
            proj_ref[:, b * D:(b + 1) * D] = jnp.dot(h_s[...], w_ref[b * D:(b + 1) * D, :], preferred_element_type=F32)

        @pl.when(p == 0)
        def _():
            def f1(r0):
                rows = pl.ds(r0, RC)
                gv = [_gelu(proj_ref[rows, D + c0:D + c0 + CW]) for c0 in COLS]
                mu = sum(_rowsum(v) for v in gv) * (1.0 / D)
                dv = [v - mu for v in gv]
                var = sum(_rowsum(v * v) for v in dv) * (1.0 / D)
                rstd = lax.rsqrt(var + LN_EPS)
                for k, c0 in enumerate(COLS):
                    vn_s[rows, c0:c0 + CW] = (dv[k] * rstd * lng_ref[:, c0:c0 + CW] + lnb_ref[:, c0:c0 + CW]).astype(BF)
            _chunks(tm, f1, unroll=4)
            for g in range(GRP):
                w = wm_s[g]
                bcol = sgbT_ref[:, g:g + 1]
                gc = slice(g * BLK, (g + 1) * BLK)
                for n in range(tm // BLK):
                    rr = slice(n * BLK, (n + 1) * BLK)
                    mix_s[rr, gc] = jnp.dot(w, vn_s[rr, gc], preferred_element_type=F32) + bcol

            def f2(r0):
                rows = pl.ds(r0, RC)
                for c0 in COLS:
                    au = proj_ref[rows, c0:c0 + CW]
                    az = proj_ref[rows, 2 * D + c0:2 * D + c0 + CW]
                    y_ref[rows, c0:c0 + CW] = (_gelu(au) * mix_s[rows, c0:c0 + CW] * _silu(az)).astype(BF)
            _chunks(tm, f2)

        @pl.when(p == 1)
        def _():
            extp_s[HALO:HALO + tm, :] = proj_ref[:, 0:D]
            ch_s[...] = proj_ref[:, 2 * D:3 * D]

            def f1(r0):
                rows = pl.ds(r0, RC)
                e = extp_s[pl.ds(r0, 2 * RC), :]
                for k, win in enumerate(POOL_WINDOWS):
                    gc = slice(k * PGW, (k + 1) * PGW)
                    s = _trail(e[:, gc], win)[RC:2 * RC]
                    d_s[rows, gc] = (s * _inv_count(i * tm + r0, win) - e[RC:2 * RC, gc]).astype(BF)
            _chunks(tm, f1)
            extp_s[0:HALO, :] = extp_s[tm:tm + HALO, :]
            for k in range(NPG):
                gc = slice(k * PGW, (k + 1) * PGW)
                mix_s[:, gc] = jnp.dot(d_s[:, gc], pw_ref[k], preferred_element_type=F32)

            def f2(r0):
                rows = pl.ds(r0, RC)
                for c0 in COLS:
                    cs = slice(c0, c0 + CW)
                    bz = proj_ref[rows, D + c0:D + c0 + CW]
                    y_ref[rows, cs] = ((mix_s[rows, cs] + pb_ref[:, cs]) * psc_ref[:, cs] * _silu(bz)).astype(BF)
            _chunks(tm, f2)

        @pl.when(p == 2)
        def _():
            def f1(r0):
                rows = pl.ds(r0, RC)
                extc_s[pl.ds(HALO + r0, RC), :] = proj_ref[rows, D:2 * D] * ch_s[rows, :]
            _chunks(tm, f1, unroll=4)

            def f2(r0):
                rows = pl.ds(r0, RC)
                for c0 in COLS:
                    cs = slice(c0, c0 + CW)
                    e = extc_s[pl.ds(r0, 2 * RC), cs]
                    conv = (cw_ref[2:3, cs] * e + cw_ref[1:2, cs] * pltpu.roll(e, 1, 0)
                            + cw_ref[0:1, cs] * pltpu.roll(e, 2, 0))[RC:2 * RC] + cb_ref[:, cs]
                    cbv = proj_ref[rows, cs]
                    cz = proj_ref[rows, 2 * D + c0:2 * D + c0 + CW]
                    y_ref[rows, cs] = (cbv * conv * _silu(cz)).astype(BF)
            _chunks(tm, f2)
            extc_s[0:HALO, :] = extc_s[tm:tm + HALO, :]

    vec = pl.BlockSpec((1, D), lambda i, p: (0, 0))
    return _pcall(
        body, name="fwd_in", grid=(nT, NSH),
        in_specs=[
            pl.BlockSpec((tm, D), lambda i, p: (i, 0)), vec,
            pl.BlockSpec((None, SHW, D), lambda i, p: (p, 0, 0)),
            vec, vec,
            pl.BlockSpec((GRP, BLK, BLK), lambda i, p: (0, 0, 0)),
            pl.BlockSpec((BLK, GRP), lambda i, p: (0, 0)),
            pl.BlockSpec((NPG, PGW, PGW), lambda i, p: (0, 0, 0)),
            vec, vec,
            pl.BlockSpec((3, D), lambda i, p: (0, 0)), vec,
        ],
        out_specs=[
            pl.BlockSpec((tm, SHW), lambda i, p: (i, p)),
            pl.BlockSpec((tm, D), lambda i, p: (i, 0)),
            pl.BlockSpec((None, tm, D), lambda i, p: (jnp.minimum(p, 2), i, 0)),
        ],
        out_shape=[jax.ShapeDtypeStruct((T, NIN), F32), jax.ShapeDtypeStruct((T, D), BF),
                   jax.ShapeDtypeStruct((3, T, D), BF)],
        scratch_shapes=[
            pltpu.VMEM((tm, D), BF), pltpu.VMEM((tm, D), F32), pltpu.VMEM((tm, D), BF), pltpu.VMEM((tm, D), F32),
            pltpu.VMEM((tm, D), BF), pltpu.VMEM((GRP, BLK, BLK), BF),
            pltpu.VMEM((tm + HALO, D), F32), pltpu.VMEM((tm + HALO, D), F32),
        ],
        compiler_params=_params(("arbitrary", "arbitrary"), 56),
    )(x, ng, wl, lng, lnb, sgw, sgbT, pwb, pb, psc, cw, cb)


def _branch_spec(which):
    q = D // NSH
    return lambda *g: (0, PK_BR // q + (which(*g) if callable(which) else which), 0)


def _fwd_out(y3, proj, x, wl):
    T = x.shape[0]
    tm = _tile(T, 256)
    q = D // NSH

    def body(y_ref, gl_ref, x_ref, wa_ref, wb_ref, wc_ref, wo_ref, p_ref, mg_ref, xo_ref):
        for k, w_ref in enumerate((wa_ref, wb_ref, wc_ref)):
            p_ref[k] = jnp.dot(y_ref[k], w_ref[...].reshape(D, D), preferred_element_type=F32)

        def f(r0):
            rows = pl.ds(r0, RC)
            for c0 in COLS:
                cs = slice(c0, c0 + CW)
                m = sum(_sigmoid(gl_ref[rows, k * D + c0:k * D + c0 + CW]) * p_ref[k, rows, cs] for k in range(3))
                mg_ref[rows, cs] = m.astype(BF)
        _chunks(tm, f)
        xo_ref[...] = x_ref[...] + jnp.dot(mg_ref[...], wo_ref[...].reshape(D, D), preferred_element_type=F32)

    return _pcall(
        body, name="fwd_out", grid=(T // tm,),
        in_specs=[
            pl.BlockSpec((3, tm, D), lambda i: (0, i, 0)),
            pl.BlockSpec((tm, SHW), lambda i: (i, 3)),
            pl.BlockSpec((tm, D), lambda i: (i, 0)),
        ] + [pl.BlockSpec((NSH, q, D), _branch_spec(k)) for k in range(4)],
        out_specs=[
            pl.BlockSpec((3, tm, D), lambda i: (0, i, 0)),
            pl.BlockSpec((tm, D), lambda i: (i, 0)),
            pl.BlockSpec((tm, D), lambda i: (i, 0)),
        ],
        out_shape=[jax.ShapeDtypeStruct((3, T, D), F32), jax.ShapeDtypeStruct((T, D), BF),
                   jax.ShapeDtypeStruct((T, D), F32)],
        compiler_params=_params(("arbitrary",), 48),
    )(y3, proj, x, wl, wl, wl, wl)


def _loss_head(x, fg, tgt):
    T = x.shape[0]
    tm = _tile(T, 512)

    def body(x_ref, g_ref, t_ref, dx_ref, sq_ref, dg_ref, acc_s):
        i = pl.program_id(0)

        @pl.when(i == 0)
        def _():
            acc_s[...] = jnp.zeros_like(acc_s)

        def f(r0):
            rows = pl.ds(r0, RC)
            xs = [x_ref[rows, c0:c0 + CW] for c0 in COLS]
            r = lax.rsqrt(sum(_rowsum(v * v) for v in xs) * (1.0 / D) + RMS_EPS)
            xh = [v * r for v in xs]
            dyg, m = [], 0.0
            for k, c0 in enumerate(COLS):
                cs = slice(c0, c0 + CW)
                err = xh[k] * g_ref[:, cs] - t_ref[rows, cs]
                acc_s[0, :, cs] += err * err
                dy = err * (1.0 / D)
                acc_s[1, :, cs] += dy * xh[k]
                dyg.append(dy * g_ref[:, cs])
                m = m + _rowsum(dyg[k] * xh[k])
            m = m * (1.0 / D)
            for k, c0 in enumerate(COLS):
                dx_ref[rows, c0:c0 + CW] = r * (dyg[k] - xh[k] * m)
        _chunks(tm, f)

        @pl.when(i == pl.num_programs(0) - 1)
        def _():
            sq_ref[...] = jnp.sum(acc_s[0], axis=0, keepdims=True)
            dg_ref[...] = jnp.sum(acc_s[1], axis=0, keepdims=True)

    vec = pl.BlockSpec((1, D), lambda i: (0, 0))
    tile = pl.BlockSpec((tm, D), lambda i: (i, 0))
    return _pcall(
        body, name="loss_head", grid=(T // tm,),
        in_specs=[tile, vec, tile], out_specs=[tile, vec, vec],
        out_shape=[jax.ShapeDtypeStruct((T, D), F32), jax.ShapeDtypeStruct((1, D), F32), jax.ShapeDtypeStruct((1, D), F32)],
        scratch_shapes=[pltpu.VMEM((2, RC, D), F32)],
        compiler_params=_params(("arbitrary",), 32),
    )(x, fg, tgt)


def _bwd_mix(dxo, p3, proj, wl, lng, lnb, sgw, sgbT, pwb, pb, psc, cw, cb):
    T = dxo.shape[0]
    tm = _tile(T, 256)
    nT = T // tm
    hb = tm // HALO

    def body(dxo_ref, p_ref, pj_ref, ch_ref, bpp_ref, ccp_ref, chp_ref, w_ref, lng_ref, lnb_ref, sgw_ref, sgbT_ref,
             pw_ref, pb_ref, psc_ref, cw_ref, cb_ref,
             dpj_ref, dp_ref, dsw_ref, dsbT_ref, vec_ref, dpw_ref,
             dy_s, ext_s, nxt_s, halo_s, dch_s, d_s, t_s, dy0_s, vn_s, dmix_s, xh_s, rstd_s, acc_s, accb_s, wm_s, wmT_s):
        i = pl.program_id(0)
        p = pl.program_id(1)
        ti = nT - 1 - i

        @pl.when((i == 0) & (p == 0))
        def _():
            dsw_ref[...] = jnp.zeros_like(dsw_ref)
            dsbT_ref[...] = jnp.zeros_like(dsbT_ref)
            vec_ref[...] = jnp.zeros_like(vec_ref)
            dpw_ref[...] = jnp.zeros_like(dpw_ref)
            halo_s[...] = jnp.zeros_like(halo_s)
            for g in range(GRP):
                wm = _masked_sgu(sgw_ref, g)
                wm_s[g] = wm.astype(BF)
                wmT_s[g] = wm.T.astype(BF)

        def flush(n_acc, rows_of):
            for a in range(n_acc):
                vec_ref[rows_of[a]:rows_of[a] + 1, :] += jnp.sum(acc_s[a], axis=0, keepdims=True)

        @pl.when(p == 0)
        def _():
            dy_s[...] = _mm_nt(dxo_ref[...], w_ref[...].reshape(D, D))

            def f(r0):
                rows = pl.ds(r0, RC)
                for c0 in COLS:
                    cs = slice(c0, c0 + CW)
                    dm = dy_s[rows, cs]
                    for k in range(3):
                        s = _sigmoid(pj_ref[rows, k * D + c0:k * D + c0 + CW])
                        dp_ref[k, rows, cs] = (s * dm).astype(BF)
                        dpj_ref[k, rows, cs] = (dm * p_ref[k, rows, cs] * s * (1.0 - s)).astype(BF)
            _chunks(tm, f)

        @pl.when(p == 1)
        def _():
            dy_s[...] = _mm_nt(dp_ref[2], w_ref[...].reshape(D, D))
            acc_s[...] = jnp.zeros_like(acc_s)
            ext_s[0:HALO, :] = jnp.where(ti > 0, ccp_ref[...] * chp_ref[...], 0.0)
            nxt_s[tm:tm + HALO, :] = halo_s[0]

            def f1(r0):
                rows = pl.ds(r0, RC)
                ext_s[pl.ds(HALO + r0, RC), :] = pj_ref[rows, D:2 * D] * ch_ref[rows, :]
            _chunks(tm, f1, unroll=4)

            def f2(r0):
                rows = pl.ds(r0, RC)
                for c0 in COLS:
                    cs = slice(c0, c0 + CW)
                    e = ext_s[pl.ds(r0, 2 * RC), cs]
                    e0, e1, e2 = e[RC:2 * RC], pltpu.roll(e, 1, 0)[RC:2 * RC], pltpu.roll(e, 2, 0)[RC:2 * RC]
                    conv = cw_ref[2:3, cs] * e0 + cw_ref[1:2, cs] * e1 + cw_ref[0:1, cs] * e2 + cb_ref[:, cs]
                    cbv = pj_ref[rows, cs]
                    sz, sg = _silu_fg(pj_ref[rows, 2 * D + c0:2 * D + c0 + CW])
                    dyc = dy_s[rows, cs]
                    dconv = dyc * cbv * sz
                    dpj_ref[0, rows, cs] = (dyc * conv * sz).astype(BF)
                    dpj_ref[2, rows, cs] = (dyc * cbv * conv * sg).astype(BF)
                    nxt_s[rows, cs] = dconv
                    acc_s[0, :, cs] += dconv
                    acc_s[1, :, cs] += dconv * e2
                    acc_s[2, :, cs] += dconv * e1
                    acc_s[3, :, cs] += dconv * e0
            _chunks(tm, f2)

            def f3(r0):
                rows = pl.ds(r0, RC)
                for c0 in COLS:
                    cs = slice(c0, c0 + CW)
                    e = nxt_s[pl.ds(r0, 2 * RC), cs]
                    dcc = (cw_ref[2:3, cs] * e + cw_ref[1:2, cs] * pltpu.roll(e, 2 * RC - 1, 0)
                           + cw_ref[0:1, cs] * pltpu.roll(e, 2 * RC - 2, 0))[0:RC]
                    dpj_ref[1, rows, cs] = (dcc * ch_ref[rows, cs]).astype(BF)
                    dch_s[rows, cs] = dcc * pj_ref[rows, D + c0:D + c0 + CW]
            _chunks(tm, f3)
            halo_s[0] = nxt_s[0:HALO, :]
            flush(4, (4, 5, 6, 7))

        @pl.when(p == 2)
        def _():
            dy_s[...] = _mm_nt(dp_ref[1], w_ref[...].reshape(D, D))
            acc_s[...] = jnp.zeros_like(acc_s)
            ext_s[0:HALO, :] = jnp.where(ti > 0, bpp_ref[...], 0.0)
            ext_s[HALO:HALO + tm, :] = pj_ref[:, 0:D]
            nxt_s[tm:tm + HALO, :] = halo_s[1]

            def f1(r0):
                rows = pl.ds(r0, RC)
                e = ext_s[pl.ds(r0, 2 * RC), :]
                for k, win in enumerate(POOL_WINDOWS):
                    gc = slice(k * PGW, (k + 1) * PGW)
                    s = _trail(e[:, gc], win)[RC:2 * RC]
                    d_s[rows, gc] = (s * _inv_count(ti * tm + r0, win) - e[RC:2 * RC, gc]).astype(BF)
            _chunks(tm, f1)
            for k in range(NPG):
                gc = slice(k * PGW, (k + 1) * PGW)
                t_s[:, gc] = jnp.dot(d_s[:, gc], pw_ref[k], preferred_element_type=F32)

            def f2(r0):
                rows = pl.ds(r0, RC)
                for c0 in COLS:
                    cs = slice(c0, c0 + CW)
                    y0 = t_s[rows, cs] + pb_ref[:, cs]
                    sz, sg = _silu_fg(pj_ref[rows, D + c0:D + c0 + CW])
                    dyb = dy_s[rows, cs]
                    dy0 = dyb * psc_ref[:, cs] * sz
                    acc_s[0, :, cs] += dy0
                    acc_s[1, :, cs] += dyb * y0 * sz
                    dy0_s[rows, cs] = dy0.astype(BF)
                    dpj_ref[1, rows, cs] = (dyb * y0 * psc_ref[:, cs] * sg).astype(BF)
            _chunks(tm, f2)
            for k in range(NPG):
                gc = slice(k * PGW, (k + 1) * PGW)
                dpw_ref[k] += _mm_tn(d_s[:, gc], dy0_s[:, gc])
                t_s[:, gc] = _mm_nt(dy0_s[:, gc], pw_ref[k])

            def f3(r0):
                rows = pl.ds(r0, RC)
                for k, win in enumerate(POOL_WINDOWS):
                    gc = slice(k * PGW, (k + 1) * PGW)
                    nxt_s[rows, gc] = t_s[rows, gc] * _inv_count(ti * tm + r0, win)
            _chunks(tm, f3, unroll=4)

            def f4(r0):
                rows = pl.ds(r0, RC)
                e = nxt_s[pl.ds(r0, 2 * RC), :]
                for k, win in enumerate(POOL_WINDOWS):
                    gc = slice(k * PGW, (k + 1) * PGW)
                    dpj_ref[0, rows, gc] = (_lead(e[:, gc], win)[0:RC] - t_s[rows, gc]).astype(BF)
                dpj_ref[2, rows, :] = dch_s[rows, :].astype(BF)
            _chunks(tm, f4)
            halo_s[1] = nxt_s[0:HALO, :]
            flush(2, (2, 3))

        @pl.when(p == 3)
        def _():
            dy_s[...] = _mm_nt(dp_ref[0], w_ref[...].reshape(D, D))
            acc_s[...] = jnp.zeros_like(acc_s)
            accb_s[...] = jnp.zeros_like(accb_s)

            def f1(r0):
                rows = pl.ds(r0, RC)
                gv = [_gelu(pj_ref[rows, D + c0:D + c0 + CW]) for c0 in COLS]
                mu = sum(_rowsum(v) for v in gv) * (1.0 / D)
                dv = [v - mu for v in gv]
                var = sum(_rowsum(v * v) for v in dv) * (1.0 / D)
                rstd = lax.rsqrt(var + LN_EPS)
                rstd_s[rows, :] = jnp.broadcast_to(rstd, (RC, BLK))
                for k, c0 in enumerate(COLS):
                    cs = slice(c0, c0 + CW)
                    xh = dv[k] * rstd
                    xh_s[rows, cs] = xh
                    vn_s[rows, cs] = (xh * lng_ref[:, cs] + lnb_ref[:, cs]).astype(BF)
            _chunks(tm, f1, unroll=4)
            for g in range(GRP):
                bcol = sgbT_ref[:, g:g + 1]
                gc = slice(g * BLK, (g + 1) * BLK)
                for n in range(tm // BLK):
                    rr = slice(n * BLK, (n + 1) * BLK)
                    t_s[rr, gc] = jnp.dot(wm_s[g], vn_s[rr, gc], preferred_element_type=F32) + bcol

            def f2(r0):
                rows = pl.ds(r0, RC)
                brow = pl.ds(pl.multiple_of(r0 % BLK, RC), RC)
                for c0 in COLS:
                    cs = slice(c0, c0 + CW)
                    gu, ggu = _gelu_fg(pj_ref[rows, cs])
                    sz, sg = _silu_fg(pj_ref[rows, 2 * D + c0:2 * D + c0 + CW])
                    dya = dy_s[rows, cs]
                    mix = t_s[rows, cs]
                    dmix = dya * gu * sz
                    dpj_ref[0, rows, cs] = (dya * mix * sz * ggu).astype(BF)
                    dpj_ref[2, rows, cs] = (dya * gu * mix * sg).astype(BF)
                    dmix_s[rows, cs] = dmix.astype(BF)
                    accb_s[brow, cs] += dmix
            _chunks(tm, f2)
            for g in range(GRP):
                gc = slice(g * BLK, (g + 1) * BLK)
                dsbT_ref[:, g:g + 1] += _rowsum(accb_s[:, gc])
                for n in range(tm // BLK):
                    rr = slice(n * BLK, (n + 1) * BLK)
                    t_s[rr, gc] = jnp.dot(wmT_s[g], dmix_s[rr, gc], preferred_element_type=F32)
                    dsw_ref[g] += _mm_nt(dmix_s[rr, gc], vn_s[rr, gc])

            def f3(r0):
                rows = pl.ds(r0, RC)
                rstd = rstd_s[rows, 0:1]
                dxh, m1, m2 = [], 0.0, 0.0
                for k, c0 in enumerate(COLS):
                    cs = slice(c0, c0 + CW)
                    dvn = t_s[rows, cs]
                    xh = xh_s[rows, cs]
                    acc_s[0, :, cs] += dvn * xh
                    acc_s[1, :, cs] += dvn
                    dxh.append(dvn * lng_ref[:, cs])
                    m1 = m1 + _rowsum(dxh[k])
                    m2 = m2 + _rowsum(dxh[k] * xh)
                m1 = m1 * (1.0 / D)
                m2 = m2 * (1.0 / D)
                for k, c0 in enumerate(COLS):
                    cs = slice(c0, c0 + CW)
                    _, ggv = _gelu_fg(pj_ref[rows, D + c0:D + c0 + CW])
                    dpj_ref[1, rows, cs] = (rstd * (dxh[k] - m1 - xh_s[rows, cs] * m2) * ggv).astype(BF)
            _chunks(tm, f3)
            flush(2, (0, 1))

            @pl.when(i == nT - 1)
            def _():
                for g in range(GRP):
                    r = lax.broadcasted_iota(jnp.int32, (BLK, BLK), 0) // 64
                    c = lax.broadcasted_iota(jnp.int32, (BLK, BLK), 1) // 64
                    dsw_ref[g] = jnp.where(c <= r, dsw_ref[g], 0.0)

    def prev(col):
        return pl.BlockSpec((HALO, D), lambda i, p: (jnp.maximum((nT - 1 - i) * hb - 1, 0), col))

    vec = pl.BlockSpec((1, D), lambda i, p: (0, 0))
    const3 = lambda i, p: (0, 0, 0)
    return _pcall(
        body, name="bwd_mix", grid=(nT, NSH),
        in_specs=[
            pl.BlockSpec((tm, D), lambda i, p: (nT - 1 - i, 0)),
            pl.BlockSpec((3, tm, D), lambda i, p: (0, nT - 1 - i, 0)),
            pl.BlockSpec((tm, SHW), lambda i, p: (nT - 1 - i, 3 - p)),
            pl.BlockSpec((tm, D), lambda i, p: (nT - 1 - i, 5)),
            prev(3), prev(7), prev(5),
            pl.BlockSpec((NSH, D // NSH, D), _branch_spec(lambda i, p: 3 - p)),
            vec, vec,
            pl.BlockSpec((GRP, BLK, BLK), const3),
            pl.BlockSpec((BLK, GRP), lambda i, p: (0, 0)),
            pl.BlockSpec((NPG, PGW, PGW), const3),
            vec, vec,
            pl.BlockSpec((3, D), lambda i, p: (0, 0)), vec,
        ],
        out_specs=[
            pl.BlockSpec((3, tm, D), lambda i, p: (3 - p, nT - 1 - i, 0)),
            pl.BlockSpec((3, tm, D), lambda i, p: (0, nT - 1 - i, 0)),
            pl.BlockSpec((GRP, BLK, BLK), const3),
            pl.BlockSpec((BLK, GRP), lambda i, p: (0, 0)),
            pl.BlockSpec((8, D), lambda i, p: (0, 0)),
            pl.BlockSpec((NPG, PGW, PGW), const3),
        ],
        out_shape=[
            jax.ShapeDtypeStruct((12, T, D), BF), jax.ShapeDtypeStruct((3, T, D), BF),
            jax.ShapeDtypeStruct((GRP, BLK, BLK), F32), jax.ShapeDtypeStruct((BLK, GRP), F32),
            jax.ShapeDtypeStruct((8, D), F32), jax.ShapeDtypeStruct((NPG, PGW, PGW), F32),
        ],
        scratch_shapes=[
            pltpu.VMEM((tm, D), F32),
            pltpu.VMEM((tm + HALO, D), F32),
            pltpu.VMEM((tm + HALO, D), F32),
            pltpu.VMEM((2, HALO, D), F32),
            pltpu.VMEM((tm, D), F32),
            pltpu.VMEM((tm, D), BF),
            pltpu.VMEM((tm, D), F32),
            pltpu.VMEM((tm, D), BF),
            pltpu.VMEM((tm, D), BF),
            pltpu.VMEM((tm, D), BF),
            pltpu.VMEM((tm, D), F32),
            pltpu.VMEM((tm, BLK), F32),
            pltpu.VMEM((4, RC, D), F32),
            pltpu.VMEM((BLK, D), F32),
            pltpu.VMEM((GRP, BLK, BLK), BF), pltpu.VMEM((GRP, BLK, BLK), BF),
        ],
        compiler_params=_params(("arbitrary", "arbitrary"), 56),
    )(dxo, p3, proj, proj, proj, proj, proj, wl, lng, lnb, sgw, sgbT, pwb, pb, psc, cw, cb)


def _bwd_in(dproj, wl, x, dxo, ng):
    T = x.shape[0]
    tm = _tile(T, 512)

    def body(dpj_ref, w_ref, x_ref, dxo_ref, ng_ref, dx_ref, dng_ref, acc_s, g_s):
        i = pl.program_id(0)
        j = pl.program_id(1)

        @pl.when((i == 0) & (j == 0))
        def _():
            g_s[...] = jnp.zeros_like(g_s)

        part = sum(_mm_nt(dpj_ref[b], w_ref[b * D:(b + 1) * D, :]) for b in range(3))

        @pl.when(j == 0)
        def _():
            acc_s[...] = part

        @pl.when(j > 0)
        def _():
            acc_s[...] += part

        @pl.when(j == NSH - 1)
        def _():
            def f(r0):
                rows = pl.ds(r0, RC)
                xs = [x_ref[rows, c0:c0 + CW] for c0 in COLS]
                r = lax.rsqrt(sum(_rowsum(v * v) for v in xs) * (1.0 / D) + RMS_EPS)
                xh = [v * r for v in xs]
                dhg, m = [], 0.0
                for k, c0 in enumerate(COLS):
                    cs = slice(c0, c0 + CW)
                    dh = acc_s[rows, cs]
                    g_s[:, cs] += dh * xh[k]
                    dhg.append(dh * ng_ref[:, cs])
                    m = m + _rowsum(dhg[k] * xh[k])
                m = m * (1.0 / D)
                for k, c0 in enumerate(COLS):
                    cs = slice(c0, c0 + CW)
                    dx_ref[rows, cs] = dxo_ref[rows, cs] + r * (dhg[k] - xh[k] * m)
            _chunks(tm, f, unroll=4)

        @pl.when((i == pl.num_programs(0) - 1) & (j == NSH - 1))
        def _():
            dng_ref[...] = jnp.sum(g_s[...], axis=0, keepdims=True)

    vec = pl.BlockSpec((1, D), lambda i, j: (0, 0))
    tile = pl.BlockSpec((tm, D), lambda i, j: (i, 0))
    return _pcall(
        body, name="bwd_in", grid=(T // tm, NSH),
        in_specs=[
            pl.BlockSpec((3, tm, D), lambda i, j: (j, i, 0)),
            pl.BlockSpec((None, SHW, D), lambda i, j: (j, 0, 0)),
            tile, tile, vec,
        ],
        out_specs=[tile, vec],
        out_shape=[jax.ShapeDtypeStruct((T, D), F32), jax.ShapeDtypeStruct((1, D), F32)],
        scratch_shapes=[pltpu.VMEM((tm, D), F32), pltpu.VMEM((RC, D), F32)],
        compiler_params=_params(("arbitrary", "arbitrary"), 48),
    )(dproj, wl, x, dxo, ng)


def _tn_grad(a3, b3, mblk, out_map, name, into=None):
    nb, T, _ = b3.shape
    tk = _tile(T, 2048)
    nk = T // tk
    a_batched = a3.shape[0] > 1

    def body(a_ref, b_ref, *rest):
        o_ref, acc_s = rest[-2:]
        k = pl.program_id(2)
        part = _mm_tn(a_ref[...], b_ref[...])

        @pl.when(k == 0)
        def _():
            acc_s[...] = part

        @pl.when(k > 0)
        def _():
            acc_s[...] += part

        @pl.when(k == nk - 1)
        def _():
            o_ref[...] = acc_s[...].astype(o_ref.dtype)

    extra = [] if into is None else [into]
    return _pcall(
        body, name=name, grid=(nb, D // mblk, nk),
        in_specs=[
            pl.BlockSpec((None, tk, mblk), (lambda n, m, k: (n, k, m)) if a_batched else (lambda n, m, k: (0, k, m))),
            pl.BlockSpec((None, tk, D), lambda n, m, k: (n, k, 0)),
        ] + [ANY] * len(extra),
        out_specs=pl.BlockSpec((None, mblk, D), out_map),
        out_shape=jax.ShapeDtypeStruct((NSH, PK_ROWS, D), BF),
        input_output_aliases={2: 0} if extra else {},
        scratch_shapes=[pltpu.VMEM((mblk, D), F32)],
        compiler_params=_params(("arbitrary", "arbitrary", "arbitrary"), 56),
    )(a3, b3, *extra)


def _place():
    x, y, c = lax.axis_index("x"), lax.axis_index("y"), lax.axis_index("c")
    chips = [(1 - x, y), (x, 1 - y), (1 - x, 1 - y)]
    return x, y, c, chips


ANY = pl.BlockSpec(memory_space=pl.ANY)


HBM = pl.BlockSpec(memory_space=pltpu.HBM)
SEM = pl.BlockSpec(memory_space=pltpu.SEMAPHORE)
EFFECT = pltpu.SideEffectType.DATAFLOW_SIDE_EFFECTING


def _own_slot(src, from_slot, name):
    shape = (NSH,) + src.shape[-2:]

    def body(src_ref, land_ref, sem):
        x, y, _, _ = _place()
        me = 2 * x + y
        cp = pltpu.make_async_copy(src_ref.at[me] if from_slot else src_ref, land_ref.at[me], sem)
        cp.start()
        cp.wait()

    return _pcall(body, name=name, in_specs=[ANY], out_specs=ANY, out_shape=jax.ShapeDtypeStruct(shape, src.dtype),
                  scratch_shapes=[pltpu.SemaphoreType.DMA(())])(src)


def _push_start(name, srcs, lands, per_peer):
    n = len(srcs)

    def body(*refs):
        src, land = refs[:n], refs[n:2 * n]
        ssem, rsem = refs[2 * n:5 * n], refs[5 * n:8 * n]
        token = refs[-1]
        x, y, c, chips = _place()
        me = 2 * x + y
        for i in range(n):
            for j, (px, py) in enumerate(chips):
                pltpu.make_async_remote_copy(
                    src_ref=src[i].at[2 * px + py] if per_peer else src[i], dst_ref=land[i].at[me],
                    send_sem=ssem[3 * i + j], recv_sem=rsem[3 * i + j], device_id=(px, py, c), device_id_type=MESH).start()
        token[...] = jnp.zeros_like(token)

    ops = list(srcs) + list(lands)
    out = _pcall(
        body, name=name,
        out_shape=tuple([pltpu.SemaphoreType.DMA(())] * (6 * n) + [pltpu.HBM(a.shape, a.dtype) for a in ops]
                        + [jax.ShapeDtypeStruct((8, 128), F32)]),
        in_specs=[HBM] * (2 * n),
        out_specs=tuple([SEM] * (6 * n) + [HBM] * (2 * n) + [pl.BlockSpec(memory_space=pltpu.VMEM)]),
        input_output_aliases={i: 6 * n + i for i in range(2 * n)},
        compiler_params=pltpu.CompilerParams(has_side_effects=EFFECT),
    )(*[pltpu.with_memory_space_constraint(a, pltpu.HBM) for a in ops])
    return out[:3 * n], out[3 * n:6 * n], out[6 * n:7 * n], out[7 * n:8 * n], out[-1]


def _push_wait(name, src, land, ssem, rsem, after, per_peer):
    def body(src_ref, land_ref, s0, s1, s2, r0, r1, r2, after_ref, src_out, land_out):
        x, y, c, chips = _place()
        for (px, py), s, r in zip(chips, (s0, s1, s2), (r0, r1, r2)):
            k = 2 * px + py
            cp = pltpu.make_async_remote_copy(
                src_ref=src_ref.at[k] if per_peer else src_ref, dst_ref=land_ref.at[k], send_sem=s, recv_sem=r,
                device_id=(px, py, c), device_id_type=MESH)
            cp.wait_send()
            cp.wait_recv()

    return _pcall(
        body, name=name,
        out_shape=(pltpu.HBM(src.shape, src.dtype), pltpu.HBM(land.shape, land.dtype)),
        in_specs=[HBM, HBM] + [SEM] * 6 + [ANY], out_specs=(HBM, HBM),
        input_output_aliases={0: 0, 1: 1},
        compiler_params=pltpu.CompilerParams(has_side_effects=EFFECT),
    )(src, land, *ssem, *rsem, after)[1]


def _swap_sibling(arrs):
    n = len(arrs)

    def body(*refs):
        src, dst = refs[:n], refs[n:2 * n]
        ssem, rsem = refs[2 * n:]
        x, y, c, _ = _place()
        cps = [pltpu.make_async_remote_copy(src_ref=src[a], dst_ref=dst[a], send_sem=ssem.at[a], recv_sem=rsem.at[a],
                                            device_id=(x, y, 1 - c), device_id_type=MESH) for a in range(n)]
        for cp in cps:
            cp.start()
        for cp in cps:
            cp.wait()

    return _pcall(
        body, name="swap_sibling",
        in_specs=[ANY] * n, out_specs=[ANY] * n,
        out_shape=[jax.ShapeDtypeStruct(a.shape, a.dtype) for a in arrs],
        scratch_shapes=[pltpu.SemaphoreType.DMA((n,)), pltpu.SemaphoreType.DMA((n,))],
    )(*arrs)


def _gather_all(v):
    def body(src, dst, ssem, rsem, lsem):
        x, y, c, _ = _place()
        me = 4 * x + 2 * y + c
        peers = [(x, y, 1 - c), (1 - x, y, c), (1 - x, y, 1 - c), (x, 1 - y, c), (x, 1 - y, 1 - c),
                 (1 - x, 1 - y, c), (1 - x, 1 - y, 1 - c)]
        local = pltpu.make_async_copy(src, dst.at[me], lsem)
        local.start()
        sends = [pltpu.make_async_remote_copy(src_ref=src, dst_ref=dst.at[me], send_sem=ssem.at[j], recv_sem=rsem.at[j],
                                              device_id=pr, device_id_type=MESH) for j, pr in enumerate(peers)]
        for cp in sends:
            cp.start()
        for j, (px, py, pc) in enumerate(peers):
            pltpu.make_async_remote_copy(src_ref=src, dst_ref=dst.at[4 * px + 2 * py + pc], send_sem=ssem.at[j],
                                         recv_sem=rsem.at[j], device_id=(px, py, pc), device_id_type=MESH).wait_recv()
        for cp in sends:
            cp.wait_send()
        local.wait()

    return _pcall(
        body, name="gather_all", in_specs=[ANY], out_specs=ANY,
        out_shape=jax.ShapeDtypeStruct((8,) + v.shape, v.dtype),
        scratch_shapes=[pltpu.SemaphoreType.DMA((7,)), pltpu.SemaphoreType.DMA((7,)), pltpu.SemaphoreType.DMA(())],
    )(v)


def _sum_slots(r, rb):
    S = r.shape[0]

    def body(r_ref, o_ref):
        acc = r_ref[0].astype(F32)
        for s in range(1, S):
            acc = acc + r_ref[s].astype(F32)
        o_ref[...] = acc

    if r.ndim == 3:
        _, R, C = r.shape
        grid, blk, imap = (R // rb,), (S, rb, C), (lambda i: (0, i, 0))
        oblk, omap = (rb, C), (lambda i: (i, 0))
    else:
        _, K, R, C = r.shape
        grid, blk, imap = (K,), (S, None, R, C), (lambda i: (0, i, 0, 0))
        oblk, omap = (None, R, C), (lambda i: (i, 0, 0))
    return _pcall(
        body, name="sum_slots", grid=grid, in_specs=[pl.BlockSpec(blk, imap)], out_specs=pl.BlockSpec(oblk, omap),
        out_shape=jax.ShapeDtypeStruct(r.shape[1:], F32), compiler_params=_params(("arbitrary",), 48),
    )(r)


def _adamw(gs, g_spec, w, m, v, p_spec, prev, grid):
    ng = len(gs)
    bc1 = 1.0 - ADAM_B1 ** ADAM_STEP
    bc2 = 1.0 - ADAM_B2 ** ADAM_STEP

    def body(*refs):
        g = refs[0][...]
        for a in range(1, ng):
            g = g + refs[a][...]
        w_ref, m_ref, v_ref = refs[ng:ng + 3]
        go, do, mo, vo = refs[ng + 3 + 4:]
        mn = ADAM_B1 * m_ref[...] + (1.0 - ADAM_B1) * g
        vn = ADAM_B2 * v_ref[...] + (1.0 - ADAM_B2) * (g * g)
        go[...] = g
        mo[...] = mn
        vo[...] = vn
        do[...] = -ADAM_LR * ((mn / bc1) / (jnp.sqrt(vn / bc2) + ADAM_EPS) + ADAM_WD * w_ref[...])

    out = jax.ShapeDtypeStruct(w.shape, F32)
    k0 = ng + 3
    return _pcall(
        body, name="adamw", grid=grid,
        in_specs=[g_spec] * ng + [p_spec] * 3 + [ANY] * 4,
        out_specs=[p_spec] * 4, out_shape=[out] * 4,
        input_output_aliases={k0: 0, k0 + 1: 1, k0 + 2: 2, k0 + 3: 3},
        compiler_params=_params(("arbitrary",) * len(grid), 48),
    )(*gs, w, m, v, *prev)


def _empty4(w):
    return tuple(lax.empty(w.shape, F32) for _ in range(4))


N_SGW = L * GRP * BLK * BLK // D
O_NG, O_VEC, O_SGB, O_FG, O_SGW = 0, 8, 32, 40, 48
O_CW = O_SGW + N_SGW
N_PACK = O_CW + 16
PACK_RB = N_PACK // 3


def _pad_to(a, rows):
    return jnp.pad(a, ((0, rows - a.shape[0]), (0, 0)))


def _pack_small(ng, vecs, sgb, fg, sgw, cw):
    parts = [_pad_to(ng, 8), _pad_to(vecs.reshape(L * 5, D), 24), _pad_to(sgb.reshape(L, D), 8),
             _pad_to(fg.reshape(1, D), 8), sgw.reshape(N_SGW, D), _pad_to(cw, 16)]
    return jnp.concatenate(parts, axis=0)


def kernel(x, norm_g, w_in, sgu_ln_g, sgu_ln_b, sgu_w, sgu_b, pool_w, pool_b, pool_scale, conv_w, conv_b, w_branch_a, w_branch_b, w_branch_c, w_out, final_g, loss_target, m_norm_g, m_w_in, m_sgu_ln_g, m_sgu_ln_b, m_sgu_w, m_sgu_b, m_pool_w, m_pool_b, m_pool_scale, m_conv_w, m_conv_b, m_w_branch_a, m_w_branch_b, m_w_branch_c, m_w_out, m_final_g, v_norm_g, v_w_in, v_sgu_ln_g, v_sgu_ln_b, v_sgu_w, v_sgu_b, v_pool_w, v_pool_b, v_pool_scale, v_conv_w, v_conv_b, v_w_branch_a, v_w_branch_b, v_w_branch_c, v_w_out, v_final_g):
    cx, cy = lax.axis_index("x"), lax.axis_index("y")
    me = 2 * cx + cy
    xl, tgt = x[0], loss_target[0]
    q = D // NSH

    wq = w_in.astype(BF).reshape(L, D, 3, D).transpose(0, 2, 1, 3).reshape(L, SHW, D)
    brq = jnp.stack([w_branch_a, w_branch_b, w_branch_c, w_out], axis=1).astype(BF).reshape(L, D, D)
    pwq = pool_w.astype(BF).reshape(L, PK_CW - PK_PW, D)
    cwq = lax.bitcast_convert_type(conv_w, BF).reshape(L, 3 * q * 2)
    cwq = jnp.pad(cwq, ((0, 0), (0, 16 * D - 3 * q * 2))).reshape(L, 16, D)
    packs = [jnp.concatenate([wq[l], brq[l], pwq[l], cwq[l]], axis=0) for l in range(L)]
    lands = [_own_slot(packs[l], False, f"ag_own_{l}") for l in range(L)]
    ag_s, ag_r, packs, lands, tok = _push_start("ag_start", packs, lands, False)
    sgbT = sgu_b.transpose(0, 2, 1)

    def layer_weights(l, after):
        wl = _push_wait(f"ag_wait_{l}", packs[l], lands[l], ag_s[3 * l:3 * l + 3], ag_r[3 * l:3 * l + 3], after, False)
        pwb = wl[:, PK_PW:PK_CW].reshape(NSH, NPG, PGW // NSH, PGW).transpose(1, 0, 2, 3).reshape(NPG, PGW, PGW)
        cwb = wl[:, PK_CW:].reshape(NSH, 16 * D)[:, :3 * q * 2].reshape(NSH, 3, q, 2)
        cwf = lax.bitcast_convert_type(cwb, F32).transpose(1, 0, 2).reshape(3, D)
        small = (sgu_ln_g[l:l + 1], sgu_ln_b[l:l + 1], sgu_w[l], sgbT[l], pwb, pool_b[l:l + 1], pool_scale[l:l + 1],
                 cwf, conv_b[l:l + 1])
        return wl, small

    xs, saved, wts = [xl], [], []
    for l in range(L):
        wl, small = layer_weights(l, tok if l == 0 else xs[l])
        wts.append((wl, small))
        proj, h, y3 = _fwd_in(xs[l], norm_g[l:l + 1], wl, *small)
        p3, mg, xo = _fwd_out(y3, proj, xs[l], wl)
        saved.append((proj, h, y3, p3, mg))
        xs.append(xo)

    dx, sq, dfg = _loss_head(xs[L], final_g[None], tgt)
    loss = lax.psum(jnp.sum(sq) * (0.5 / D), ("x", "y", "c"))

    g_w_in = _empty4(w_in)
    g_br = [_empty4(w_out) for _ in range(4)]
    g_pw = _empty4(pool_w)
    dng, dvec, dsgw, dsgb = [None] * L, [None] * L, [None] * L, [None] * L
    branches = [(w_branch_a, m_w_branch_a, v_w_branch_a), (w_branch_b, m_w_branch_b, v_w_branch_b),
                (w_branch_c, m_w_branch_c, v_w_branch_c), (w_out, m_w_out, v_w_out)]
    nb = D // 128

    def finish(l, landed):
        nonlocal g_w_in, g_pw
        mine = _sum_slots(landed, PK_ROWS // 9)
        sums = [mine, _swap_sibling([mine])[0]]
        g_w_in = _adamw(sums, pl.BlockSpec((128, D), lambda b, i: (b * nb + i, 0)), w_in, m_w_in, v_w_in,
                        pl.BlockSpec((None, 128, D), lambda b, i: (l, i, b)), g_w_in, (3, nb))
        for k, (w, m, v) in enumerate(branches):
            g_br[k] = _adamw(sums, pl.BlockSpec((q, D), lambda i, k=k: (PK_BR // q + k, 0)), w, m, v,
                             pl.BlockSpec((None, q, D), lambda i: (l, 0, 0)), g_br[k], (1,))
        pools = [a[PK_PW:PK_CW].reshape(NPG, PGW // NSH, PGW) for a in sums]
        g_pw = _adamw(pools, pl.BlockSpec((None, PGW // NSH, PGW), lambda g: (g, 0, 0)), pool_w, m_pool_w, v_pool_w,
                      pl.BlockSpec((None, None, PGW // NSH, PGW), lambda g: (l, g, 0, 0)), g_pw, (NPG,))

    pend = None
    for l in reversed(range(L)):
        proj, h, y3, p3, mg = saved[l]
        wl, small = wts[l]
        dproj, dp3, dsgw[l], dsbT, dvec[l], dpw = _bwd_mix(dx, p3, proj, wl, *small)
        dsgb[l] = dsbT.T
        grads = _tn_grad(h[None], dproj, D, lambda n, m, k: (n // 3, n % 3, 0), "grad_w_in")
        grads = _tn_grad(y3, dp3, q, lambda n, m, k: (m, PK_BR // q + n, 0), "grad_w_branch", into=grads)
        grads = _tn_grad(mg[None], dx[None], q, lambda n, m, k: (m, PK_BR // q + 3, 0), "grad_w_out", into=grads)
        dpq = dpw.astype(BF).reshape(NPG, NSH, PGW // NSH, PGW).transpose(1, 0, 2, 3).reshape(NSH, PK_CW - PK_PW, D)
        grads = lax.dynamic_update_slice(grads, jnp.pad(dpq, ((0, 0), (0, PK_ROWS - PK_CW), (0, 0))), (0, PK_PW, 0))
        if pend is not None:
            landed = _push_wait(f"rs_wait_{pend[0]}", *pend[1:], dproj, True)
        land = _own_slot(grads, True, f"rs_own_{l}")
        ss, rs, (grads,), (land,), tok = _push_start(f"rs_start_{l}", [grads], [land], True)
        if pend is not None:
            finish(pend[0], landed)
        dx, dng[l] = _bwd_in(dproj, wl, xs[l], dx, norm_g[l:l + 1] + tok[0, 0])
        pend = (l, grads, land, ss, rs)
    finish(pend[0], _push_wait(f"rs_wait_{pend[0]}", *pend[1:], dx, True))

    dvec = jnp.stack(dvec)
    small = _pack_small(jnp.concatenate(dng), dvec[:, 0:5], jnp.stack(dsgb), dfg[0], jnp.stack(dsgw),
                        dvec[:, 5:8].reshape(L * 3, D))
    gsmall = _sum_slots(_gather_all(small), PACK_RB)
    gcw = lax.dynamic_slice_in_dim(gsmall[O_CW:O_CW + L * 3], me * q, q, axis=1)
    gpack = jnp.concatenate([gsmall[:O_CW], _pad_to(gcw.reshape(L * 3 * q // D, D), 16)])

    def pack(ng, lg, lb, sw, sb, pb_, ps, cwv, cb_, fg):
        return _pack_small(ng, jnp.stack([lg, lb, pb_, ps, cb_], axis=1), sb, fg, sw, cwv.reshape(L * 3 * q // D, D))

    wp = pack(norm_g, sgu_ln_g, sgu_ln_b, sgu_w, sgu_b, pool_b, pool_scale, conv_w, conv_b, final_g)
    mp = pack(m_norm_g, m_sgu_ln_g, m_sgu_ln_b, m_sgu_w, m_sgu_b, m_pool_b, m_pool_scale, m_conv_w, m_conv_b, m_final_g)
    vp = pack(v_norm_g, v_sgu_ln_g, v_sgu_ln_b, v_sgu_w, v_sgu_b, v_pool_b, v_pool_scale, v_conv_w, v_conv_b, v_final_g)
    rows = pl.BlockSpec((PACK_RB, D), lambda i: (i, 0))
    sm = _adamw([gpack], rows, wp, mp, vp, rows, _empty4(wp), (N_PACK // PACK_RB,))

    def unpack(a):
        vv = a[O_VEC:O_VEC + L * 5].reshape(L, 5, D)
        sb = a[O_SGB:O_SGB + L].reshape(L, GRP, BLK)
        fg = a[O_FG]
        sw = a[O_SGW:O_SGW + N_SGW].reshape(L, GRP, BLK, BLK)
        cwv = a[O_CW:O_CW + L * 3 * q // D].reshape(L, 3, q)
        return dict(norm_g=a[O_NG:O_NG + L], w_in=None, sgu_ln_g=vv[:, 0], sgu_ln_b=vv[:, 1], sgu_w=sw, sgu_b=sb, pool_w=None,
                    pool_b=vv[:, 2], pool_scale=vv[:, 3], conv_w=cwv, conv_b=vv[:, 4], w_branch_a=None,
                    w_branch_b=None, w_branch_c=None, w_out=None, final_g=fg)

    outs = [loss, dx[None]]
    for kind in range(4):
        d = unpack(sm[kind])
        d.update(w_in=g_w_in[kind], pool_w=g_pw[kind], w_branch_a=g_br[0][kind], w_branch_b=g_br[1][kind],
                 w_branch_c=g_br[2][kind], w_out=g_br[3][kind])
        outs.extend(d[n] for n in ("norm_g", "w_in", "sgu_ln_g", "sgu_ln_b", "sgu_w", "sgu_b", "pool_w", "pool_b",
                                   "pool_scale", "conv_w", "conv_b", "w_branch_a", "w_branch_b", "w_branch_c", "w_out",
                                   "final_g"))
    return tuple(outs)
```

```python
import functools

import jax
import jax.numpy as jnp
from jax import lax
from jax.experimental import pallas as pl
from jax.experimental.pallas import tpu as pltpu

F32 = jnp.float32
BF = jnp.bfloat16
MESH = pl.DeviceIdType.MESH

D = 1024
L = 4
NSH = 4
SHW = 3 * D
NIN = NSH * SHW
GRP = 8
BLK = 128
NPG = 4
PGW = D // NPG
POOL_WINDOWS = (2, 4, 8, 16)
HALO = 16
RMS_EPS = 1e-6
LN_EPS = 1e-5
ADAM_LR, ADAM_B1, ADAM_B2, ADAM_EPS, ADAM_WD, ADAM_STEP = 0.001, 0.9, 0.999, 1e-8, 0.01, 10

RC = 16
CW = 512
COLS = tuple(range(0, D, CW))
MIB = 1 << 20

PK_BR = SHW
PK_PW = PK_BR + D
PK_CW = PK_PW + NPG * (PGW // NSH) * PGW // D
PK_ROWS = PK_CW + 16

_C0 = 0.7978845608028654
_C1 = 0.044715


def _pcall(body, **kw):
    return pl.pallas_call(body, **kw)


def _params(sem, vmem_mib):
    return pltpu.CompilerParams(dimension_semantics=sem, vmem_limit_bytes=vmem_mib * MIB)


def _mm(a, b):
    return jnp.dot(a.astype(BF), b.astype(BF), preferred_element_type=F32)


def _mm_nt(a, b):
    return lax.dot_general(a.astype(BF), b.astype(BF), (((1,), (1,)), ((), ())), preferred_element_type=F32)


def _mm_tn(a, b):
    return lax.dot_general(a.astype(BF), b.astype(BF), (((0,), (0,)), ((), ())), preferred_element_type=F32)


def _gelu(x):
    return 0.5 * x * (1.0 + jnp.tanh(_C0 * x * (1.0 + _C1 * x * x)))


def _gelu_fg(x):
    x2 = x * x
    t = jnp.tanh(_C0 * x * (1.0 + _C1 * x2))
    f = 0.5 * x * (1.0 + t)
    g = 0.5 * (1.0 + t) + 0.5 * x * (1.0 - t * t) * (_C0 * (1.0 + 3.0 * _C1 * x2))
    return f, g


def _sigmoid(x):
    return 0.5 * jnp.tanh(0.5 * x) + 0.5


def _silu(x):
    return x * _sigmoid(x)


def _silu_fg(x):
    s = _sigmoid(x)
    return x * s, s * (1.0 + x * (1.0 - s))


def _rowsum(v):
    return jnp.sum(v, axis=1, keepdims=True)


def _chunks(n_rows, fn, unroll=2):
    def body(c, carry):
        fn(pl.multiple_of(c * RC, RC))
        return carry
    lax.fori_loop(0, n_rows // RC, body, 0, unroll=unroll)


def _trail(e, win):
    s, sh = e, 1
    while sh < win:
        s = s + pltpu.roll(s, sh, 0)
        sh *= 2
    return s


def _lead(e, win):
    n = e.shape[0]
    s, sh = e, 1
    while sh < win:
        s = s + pltpu.roll(s, n - sh, 0)
        sh *= 2
    return s


def _inv_count(pos0, win):
    pos = pos0 + lax.broadcasted_iota(jnp.int32, (RC, 1), 0)
    return 1.0 / jnp.minimum(pos + 1, win).astype(F32)


def _masked_sgu(sgw_ref, g):
    r = lax.broadcasted_iota(jnp.int32, (BLK, BLK), 0) // 64
    c = lax.broadcasted_iota(jnp.int32, (BLK, BLK), 1) // 64
    return jnp.where(c <= r, sgw_ref[g], 0.0)


def _tile(t, want):
    return min(t, want)


def _fwd_in(x, ng, wl, lng, lnb, sgw, sgbT, pwb, pb, psc, cw, cb):
    T = x.shape[0]
    tm = _tile(T, 512)
    nT = T // tm

    def body(x_ref, ng_ref, w_ref, lng_ref, lnb_ref, sgw_ref, sgbT_ref, pw_ref, pb_ref, psc_ref, cw_ref, cb_ref,
             proj_ref, h_ref, y_ref, h_s, ch_s, vn_s, mix_s, d_s, wm_s, extp_s, extc_s):
        i = pl.program_id(0)
        p = pl.program_id(1)

        @pl.when((i == 0) & (p == 0))
        def _():
            extp_s[0:HALO, :] = jnp.zeros((HALO, D), F32)
            extc_s[0:HALO, :] = jnp.zeros((HALO, D), F32)
            for g in range(GRP):
                wm_s[g] = _masked_sgu(sgw_ref, g).astype(BF)

        @pl.when(p == 0)
        def _():
            def f(r0):
                rows = pl.ds(r0, RC)
                xs = [x_ref[rows, c0:c0 + CW] for c0 in COLS]
                ms = sum(_rowsum(v * v) for v in xs) * (1.0 / D)
                r = lax.rsqrt(ms + RMS_EPS)
                for k, c0 in enumerate(COLS):
                    hb = (xs[k] * r * ng_ref[:, c0:c0 + CW]).astype(BF)
                    h_s[rows, c0:c0 + CW] = hb
                    h_ref[rows, c0:c0 + CW] = hb
            _chunks(tm, f, unroll=4)

        for b in range(3):
            proj_ref[:, b * D:(b + 1) * D] = jnp.dot(h_s[...], w_ref[b * D:(b + 1) * D, :], preferred_element_type=F32)

        @pl.when(p == 0)
        def _():
            def f1(r0):
                rows = pl.ds(r0, RC)
                gv = [_gelu(proj_ref[rows, D + c0:D + c0 + CW]) for c0 in COLS]
                mu = sum(_rowsum(v) for v in gv) * (1.0 / D)
                dv = [v - mu for v in gv]
                var = sum(_rowsum(v * v) for v in dv) * (1.0 / D)
                rstd = lax.rsqrt(var + LN_EPS)
                for k, c0 in enumerate(COLS):
                    vn_s[rows, c0:c0 + CW] = (dv[k] * rstd * lng_ref[:, c0:c0 + CW] + lnb_ref[:, c0:c0 + CW]).astype(BF)
            _chunks(tm, f1, unroll=4)
            for g in range(GRP):
                w = wm_s[g]
                bcol = sgbT_ref[:, g:g + 1]
                gc = slice(g * BLK, (g + 1) * BLK)
                for n in range(tm // BLK):
                    rr = slice(n * BLK, (n + 1) * BLK)
                    mix_s[rr, gc] = jnp.dot(w, vn_s[rr, gc], preferred_element_type=F32) + bcol

            def f2(r0):
                rows = pl.ds(r0, RC)
                for c0 in COLS:
                    au = proj_ref[rows, c0:c0 + CW]
                    az = proj_ref[rows, 2 * D + c0:2 * D + c0 + CW]
                    y_ref[rows, c0:c0 + CW] = (_gelu(au) * mix_s[rows, c0:c0 + CW] * _silu(az)).astype(BF)
            _chunks(tm, f2)

        @pl.when(p == 1)
        def _():
            extp_s[HALO:HALO + tm, :] = proj_ref[:, 0:D]
            ch_s[...] = proj_ref[:, 2 * D:3 * D]

            def f1(r0):
                rows = pl.ds(r0, RC)
                e = extp_s[pl.ds(r0, 2 * RC), :]
                for k, win in enumerate(POOL_WINDOWS):
                    gc = slice(k * PGW, (k + 1) * PGW)
                    s = _trail(e[:, gc], win)[RC:2 * RC]
                    d_s[rows, gc] = (s * _inv_count(i * tm + r0, win) - e[RC:2 * RC, gc]).astype(BF)
            _chunks(tm, f1)
            extp_s[0:HALO, :] = extp_s[tm:tm + HALO, :]
            for k in range(NPG):
                gc = slice(k * PGW, (k + 1) * PGW)
                mix_s[:, gc] = jnp.dot(d_s[:, gc], pw_ref[k], preferred_element_type=F32)

            def f2(r0):
                rows = pl.ds(r0, RC)
                for c0 in COLS:
                    cs = slice(c0, c0 + CW)
                    bz = proj_ref[rows, D + c0:D + c0 + CW]
                    y_ref[rows, cs] = ((mix_s[rows, cs] + pb_ref[:, cs]) * psc_ref[:, cs] * _silu(bz)).astype(BF)
            _chunks(tm, f2)

        @pl.when(p == 2)
        def _():
            def f1(r0):
                rows = pl.ds(r0, RC)
                extc_s[pl.ds(HALO + r0, RC), :] = proj_ref[rows, D:2 * D] * ch_s[rows, :]
            _chunks(tm, f1, unroll=4)

            def f2(r0):
                rows = pl.ds(r0, RC)
                for c0 in COLS:
                    cs = slice(c0, c0 + CW)
                    e = extc_s[pl.ds(r0, 2 * RC), cs]
                    conv = (cw_ref[2:3, cs] * e + cw_ref[1:2, cs] * pltpu.roll(e, 1, 0)
                            + cw_ref[0:1, cs] * pltpu.roll(e, 2, 0))[RC:2 * RC] + cb_ref[:, cs]
                    cbv = proj_ref[rows, cs]
                    cz = proj_ref[rows, 2 * D + c0:2 * D + c0 + CW]
                    y_ref[rows, cs] = (cbv * conv * _silu(cz)).astype(BF)
            _chunks(tm, f2)
            extc_s[0:HALO, :] = extc_s[tm:tm + HALO, :]

    vec = pl.BlockSpec((1, D), lambda i, p: (0, 0))
    return _pcall(
        body, name="fwd_in", grid=(nT, NSH),
        in_specs=[
            pl.BlockSpec((tm, D), lambda i, p: (i, 0)), vec,
            pl.BlockSpec((None, SHW, D), lambda i, p: (p, 0, 0)),
            vec, vec,
            pl.BlockSpec((GRP, BLK, BLK), lambda i, p: (0, 0, 0)),
            pl.BlockSpec((BLK, GRP), lambda i, p: (0, 0)),
            pl.BlockSpec((NPG, PGW, PGW), lambda i, p: (0, 0, 0)),
            vec, vec,
            pl.BlockSpec((3, D), lambda i, p: (0, 0)), vec,
        ],
        out_specs=[
            pl.BlockSpec((tm, SHW), lambda i, p: (i, p)),
            pl.BlockSpec((tm, D), lambda i, p: (i, 0)),
            pl.BlockSpec((None, tm, D), lambda i, p: (jnp.minimum(p, 2), i, 0)),
        ],
        out_shape=[jax.ShapeDtypeStruct((T, NIN), F32), jax.ShapeDtypeStruct((T, D), BF),
                   jax.ShapeDtypeStruct((3, T, D), BF)],
        scratch_shapes=[
            pltpu.VMEM((tm, D), BF), pltpu.VMEM((tm, D), F32), pltpu.VMEM((tm, D), BF), pltpu.VMEM((tm, D), F32),
            pltpu.VMEM((tm, D), BF), pltpu.VMEM((GRP, BLK, BLK), BF),
            pltpu.VMEM((tm + HALO, D), F32), pltpu.VMEM((tm + HALO, D), F32),
        ],
        compiler_params=_params(("arbitrary", "arbitrary"), 56),
    )(x, ng, wl, lng, lnb, sgw, sgbT, pwb, pb, psc, cw, cb)


def _branch_spec(which):
    q = D // NSH
    return lambda *g: (0, PK_BR // q + (which(*g) if callable(which) else which), 0)


def _fwd_out(y3, proj, x, wl):
    T = x.shape[0]
    tm = _tile(T, 256)
    q = D // NSH

    def body(y_ref, gl_ref, x_ref, wa_ref, wb_ref, wc_ref, wo_ref, p_ref, mg_ref, xo_ref):
        for k, w_ref in enumerate((wa_ref, wb_ref, wc_ref)):
            p_ref[k] = jnp.dot(y_ref[k], w_ref[...].reshape(D, D), preferred_element_type=F32)

        def f(r0):
            rows = pl.ds(r0, RC)
            for c0 in COLS:
                cs = slice(c0, c0 + CW)
                m = sum(_sigmoid(gl_ref[rows, k * D + c0:k * D + c0 + CW]) * p_ref[k, rows, cs] for k in range(3))
                mg_ref[rows, cs] = m.astype(BF)
        _chunks(tm, f)
        xo_ref[...] = x_ref[...] + jnp.dot(mg_ref[...], wo_ref[...].reshape(D, D), preferred_element_type=F32)

    return _pcall(
        body, name="fwd_out", grid=(T // tm,),
        in_specs=[
            pl.BlockSpec((3, tm, D), lambda i: (0, i, 0)),
            pl.BlockSpec((tm, SHW), lambda i: (i, 3)),
            pl.BlockSpec((tm, D), lambda i: (i, 0)),
        ] + [pl.BlockSpec((NSH, q, D), _branch_spec(k)) for k in range(4)],
        out_specs=[
            pl.BlockSpec((3, tm, D), lambda i: (0, i, 0)),
            pl.BlockSpec((tm, D), lambda i: (i, 0)),
            pl.BlockSpec((tm, D), lambda i: (i, 0)),
        ],
        out_shape=[jax.ShapeDtypeStruct((3, T, D), F32), jax.ShapeDtypeStruct((T, D), BF),
                   jax.ShapeDtypeStruct((T, D), F32)],
        compiler_params=_params(("arbitrary",), 48),
    )(y3, proj, x, wl, wl, wl, wl)


def _loss_head(x, fg, tgt):
    T = x.shape[0]
    tm = _tile(T, 512)

    def body(x_ref, g_ref, t_ref, dx_ref, sq_ref, dg_ref, acc_s):
        i = pl.program_id(0)

        @pl.when(i == 0)
        def _():
            acc_s[...] = jnp.zeros_like(acc_s)

        def f(r0):
            rows = pl.ds(r0, RC)
            xs = [x_ref[rows, c0:c0 + CW] for c0 in COLS]
            r = lax.rsqrt(sum(_rowsum(v * v) for v in xs) * (1.0 / D) + RMS_EPS)
            xh = [v * r for v in xs]
            dyg, m = [], 0.0
            for k, c0 in enumerate(COLS):
                cs = slice(c0, c0 + CW)
                err = xh[k] * g_ref[:, cs] - t_ref[rows, cs]
                acc_s[0, :, cs] += err * err
                dy = err * (1.0 / D)
                acc_s[1, :, cs] += dy * xh[k]
                dyg.append(dy * g_ref[:, cs])
                m = m + _rowsum(dyg[k] * xh[k])
            m = m * (1.0 / D)
            for k, c0 in enumerate(COLS):
                dx_ref[rows, c0:c0 + CW] = r * (dyg[k] - xh[k] * m)
        _chunks(tm, f)

        @pl.when(i == pl.num_programs(0) - 1)
        def _():
            sq_ref[...] = jnp.sum(acc_s[0], axis=0, keepdims=True)
            dg_ref[...] = jnp.sum(acc_s[1], axis=0, keepdims=True)

    vec = pl.BlockSpec((1, D), lambda i: (0, 0))
    tile = pl.BlockSpec((tm, D), lambda i: (i, 0))
    return _pcall(
        body, name="loss_head", grid=(T // tm,),
        in_specs=[tile, vec, tile], out_specs=[tile, vec, vec],
        out_shape=[jax.ShapeDtypeStruct((T, D), F32), jax.ShapeDtypeStruct((1, D), F32), jax.ShapeDtypeStruct((1, D), F32)],
        scratch_shapes=[pltpu.VMEM((2, RC, D), F32)],
        compiler_params=_params(("arbitrary",), 32),
    )(x, fg, tgt)


def _bwd_mix(dxo, p3, proj, wl, lng, lnb, sgw, sgbT, pwb, pb, psc, cw, cb):
    T = dxo.shape[0]
    tm = _tile(T, 256)
    nT = T // tm
    hb = tm // HALO

    def body(dxo_ref, p_ref, pj_ref, ch_ref, bpp_ref, ccp_ref, chp_ref, w_ref, lng_ref, lnb_ref, sgw_ref, sgbT_ref,
             pw_ref, pb_ref, psc_ref, cw_ref, cb_ref,
             dpj_ref, dp_ref, dsw_ref, dsbT_ref, vec_ref, dpw_ref,
             dy_s, ext_s, nxt_s, halo_s, dch_s, d_s, t_s, dy0_s, vn_s, dmix_s, xh_s, rstd_s, acc_s, accb_s, wm_s, wmT_s):
        i = pl.program_id(0)
        p = pl.program_id(1)
        ti = nT - 1 - i

        @pl.when((i == 0) & (p == 0))
        def _():
            dsw_ref[...] = jnp.zeros_like(dsw_ref)
            dsbT_ref[...] = jnp.zeros_like(dsbT_ref)
            vec_ref[...] = jnp.zeros_like(vec_ref)
            dpw_ref[...] = jnp.zeros_like(dpw_ref)
            halo_s[...] = jnp.zeros_like(halo_s)
            for g in range(GRP):
                wm = _masked_sgu(sgw_ref, g)
                wm_s[g] = wm.astype(BF)
                wmT_s[g] = wm.T.astype(BF)

        def flush(n_acc, rows_of):
            for a in range(n_acc):
                vec_ref[rows_of[a]:rows_of[a] + 1, :] += jnp.sum(acc_s[a], axis=0, keepdims=True)

        @pl.when(p == 0)
        def _():
            dy_s[...] = _mm_nt(dxo_ref[...], w_ref[...].reshape(D, D))

            def f(r0):
                rows = pl.ds(r0, RC)
                for c0 in COLS:
                    cs = slice(c0, c0 + CW)
                    dm = dy_s[rows, cs]
                    for k in range(3):
                        s = _sigmoid(pj_ref[rows, k * D + c0:k * D + c0 + CW])
                        dp_ref[k, rows, cs] = (s * dm).astype(BF)
                        dpj_ref[k, rows, cs] = (dm * p_ref[k, rows, cs] * s * (1.0 - s)).astype(BF)
            _chunks(tm, f)

        @pl.when(p == 1)
        def _():
            dy_s[...] = _mm_nt(dp_ref[2], w_ref[...].reshape(D, D))
            acc_s[...] = jnp.zeros_like(acc_s)
            ext_s[0:HALO, :] = jnp.where(ti > 0, ccp_ref[...] * chp_ref[...], 0.0)
            nxt_s[tm:tm + HALO, :] = halo_s[0]

            def f1(r0):
                rows = pl.ds(r0, RC)
                ext_s[pl.ds(HALO + r0, RC), :] = pj_ref[rows, D:2 * D] * ch_ref[rows, :]
            _chunks(tm, f1, unroll=4)

            def f2(r0):
                rows = pl.ds(r0, RC)
                for c0 in COLS:
                    cs = slice(c0, c0 + CW)
                    e = ext_s[pl.ds(r0, 2 * RC), cs]
                    e0, e1, e2 = e[RC:2 * RC], pltpu.roll(e, 1, 0)[RC:2 * RC], pltpu.roll(e, 2, 0)[RC:2 * RC]
                    conv = cw_ref[2:3, cs] * e0 + cw_ref[1:2, cs] * e1 + cw_ref[0:1, cs] * e2 + cb_ref[:, cs]
                    cbv = pj_ref[rows, cs]
                    sz, sg = _silu_fg(pj_ref[rows, 2 * D + c0:2 * D + c0 + CW])
                    dyc = dy_s[rows, cs]
                    dconv = dyc * cbv * sz
                    dpj_ref[0, rows, cs] = (dyc * conv * sz).astype(BF)
                    dpj_ref[2, rows, cs] = (dyc * cbv * conv * sg).astype(BF)
                    nxt_s[rows, cs] = dconv
                    acc_s[0, :, cs] += dconv
                    acc_s[1, :, cs] += dconv * e2
                    acc_s[2, :, cs] += dconv * e1
                    acc_s[3, :, cs] += dconv * e0
            _chunks(tm, f2)

            def f3(r0):
                rows = pl.ds(r0, RC)
                for c0 in COLS:
                    cs = slice(c0, c0 + CW)
                    e = nxt_s[pl.ds(r0, 2 * RC), cs]
                    dcc = (cw_ref[2:3, cs] * e + cw_ref[1:2, cs] * pltpu.roll(e, 2 * RC - 1, 0)
                           + cw_ref[0:1, cs] * pltpu.roll(e, 2 * RC - 2, 0))[0:RC]
                    dpj_ref[1, rows, cs] = (dcc * ch_ref[rows, cs]).astype(BF)
                    dch_s[rows, cs] = dcc * pj_ref[rows, D + c0:D + c0 + CW]
            _chunks(tm, f3)
            halo_s[0] = nxt_s[0:HALO, :]
            flush(4, (4, 5, 6, 7))

        @pl.when(p == 2)
        def _():
            dy_s[...] = _mm_nt(dp_ref[1], w_ref[...].reshape(D, D))
            acc_s[...] = jnp.zeros_like(acc_s)
            ext_s[0:HALO, :] = jnp.where(ti > 0, bpp_ref[...], 0.0)
            ext_s[HALO:HALO + tm, :] = pj_ref[:, 0:D]
            nxt_s[tm:tm + HALO, :] = halo_s[1]

            def f1(r0):
                rows = pl.ds(r0, RC)
                e = ext_s[pl.ds(r0, 2 * RC), :]
                for k, win in enumerate(POOL_WINDOWS):
                    gc = slice(k * PGW, (k + 1) * PGW)
                    s = _trail(e[:, gc], win)[RC:2 * RC]
                    d_s[rows, gc] = (s * _inv_count(ti * tm + r0, win) - e[RC:2 * RC, gc]).astype(BF)
            _chunks(tm, f1)
            for k in range(NPG):
                gc = slice(k * PGW, (k + 1) * PGW)
                t_s[:, gc] = jnp.dot(d_s[:, gc], pw_ref[k], preferred_element_type=F32)

            def f2(r0):
                rows = pl.ds(r0, RC)
                for c0 in COLS:
                    cs = slice(c0, c0 + CW)
                    y0 = t_s[rows, cs] + pb_ref[:, cs]
                    sz, sg = _silu_fg(pj_ref[rows, D + c0:D + c0 + CW])
                    dyb = dy_s[rows, cs]
                    dy0 = dyb * psc_ref[:, cs] * sz
                    acc_s[0, :, cs] += dy0
                    acc_s[1, :, cs] += dyb * y0 * sz
                    dy0_s[rows, cs] = dy0.astype(BF)
                    dpj_ref[1, rows, cs] = (dyb * y0 * psc_ref[:, cs] * sg).astype(BF)
            _chunks(tm, f2)
            for k in range(NPG):
                gc = slice(k * PGW, (k + 1) * PGW)
                dpw_ref[k] += _mm_tn(d_s[:, gc], dy0_s[:, gc])
                t_s[:, gc] = _mm_nt(dy0_s[:, gc], pw_ref[k])

            def f3(r0):
                rows = pl.ds(r0, RC)
                for k, win in enumerate(POOL_WINDOWS):
                    gc = slice(k * PGW, (k + 1) * PGW)
                    nxt_s[rows, gc] = t_s[rows, gc] * _inv_count(ti * tm + r0, win)
            _chunks(tm, f3, unroll=4)

            def f4(r0):
                rows = pl.ds(r0, RC)
                e = nxt_s[pl.ds(r0, 2 * RC), :]
                for k, win in enumerate(POOL_WINDOWS):
                    gc = slice(k * PGW, (k + 1) * PGW)
                    dpj_ref[0, rows, gc] = (_lead(e[:, gc], win)[0:RC] - t_s[rows, gc]).astype(BF)
                dpj_ref[2, rows, :] = dch_s[rows, :].astype(BF)
            _chunks(tm, f4)
            halo_s[1] = nxt_s[0:HALO, :]
            flush(2, (2, 3))

        @pl.when(p == 3)
        def _():
            dy_s[...] = _mm_nt(dp_ref[0], w_ref[...].reshape(D, D))
            acc_s[...] = jnp.zeros_like(acc_s)
            accb_s[...] = jnp.zeros_like(accb_s)

            def f1(r0):
                rows = pl.ds(r0, RC)
                gv = [_gelu(pj_ref[rows, D + c0:D + c0 + CW]) for c0 in COLS]
                mu = sum(_rowsum(v) for v in gv) * (1.0 / D)
                dv = [v - mu for v in gv]
                var = sum(_rowsum(v * v) for v in dv) * (1.0 / D)
                rstd = lax.rsqrt(var + LN_EPS)
                rstd_s[rows, :] = jnp.broadcast_to(rstd, (RC, BLK))
                for k, c0 in enumerate(COLS):
                    cs = slice(c0, c0 + CW)
                    xh = dv[k] * rstd
                    xh_s[rows, cs] = xh
                    vn_s[rows, cs] = (xh * lng_ref[:, cs] + lnb_ref[:, cs]).astype(BF)
            _chunks(tm, f1, unroll=4)
            for g in range(GRP):
                bcol = sgbT_ref[:, g:g + 1]
                gc = slice(g * BLK, (g + 1) * BLK)
                for n in range(tm // BLK):
                    rr = slice(n * BLK, (n + 1) * BLK)
                    t_s[rr, gc] = jnp.dot(wm_s[g], vn_s[rr, gc], preferred_element_type=F32) + bcol

            def f2(r0):
                rows = pl.ds(r0, RC)
                brow = pl.ds(pl.multiple_of(r0 % BLK, RC), RC)
                for c0 in COLS:
                    cs = slice(c0, c0 + CW)
                    gu, ggu = _gelu_fg(pj_ref[rows, cs])
                    sz, sg = _silu_fg(pj_ref[rows, 2 * D + c0:2 * D + c0 + CW])
                    dya = dy_s[rows, cs]
                    mix = t_s[rows, cs]
                    dmix = dya * gu * sz
                    dpj_ref[0, rows, cs] = (dya * mix * sz * ggu).astype(BF)
                    dpj_ref[2, rows, cs] = (dya * gu * mix * sg).astype(BF)
                    dmix_s[rows, cs] = dmix.astype(BF)
                    accb_s[brow, cs] += dmix
            _chunks(tm, f2)
            for g in range(GRP):
                gc = slice(g * BLK, (g + 1) * BLK)
                dsbT_ref[:, g:g + 1] += _rowsum(accb_s[:, gc])
                for n in range(tm // BLK):
                    rr = slice(n * BLK, (n + 1) * BLK)
                    t_s[rr, gc] = jnp.dot(wmT_s[g], dmix_s[rr, gc], preferred_element_type=F32)
                    dsw_ref[g] += _mm_nt(dmix_s[rr, gc], vn_s[rr, gc])

            def f3(r0):
                rows = pl.ds(r0, RC)
                rstd = rstd_s[rows, 0:1]
                dxh, m1, m2 = [], 0.0, 0.0
                for k, c0 in enumerate(COLS):
                    cs = slice(c0, c0 + CW)
                    dvn = t_s[rows, cs]
                    xh = xh_s[rows, cs]
                    acc_s[0, :, cs] += dvn * xh
                    acc_s[1, :, cs] += dvn
                    dxh.append(dvn * lng_ref[:, cs])
                    m1 = m1 + _rowsum(dxh[k])
                    m2 = m2 + _rowsum(dxh[k] * xh)
                m1 = m1 * (1.0 / D)
                m2 = m2 * (1.0 / D)
                for k, c0 in enumerate(COLS):
                    cs = slice(c0, c0 + CW)
                    _, ggv = _gelu_fg(pj_ref[rows, D + c0:D + c0 + CW])
                    dpj_ref[1, rows, cs] = (rstd * (dxh[k] - m1 - xh_s[rows, cs] * m2) * ggv).astype(BF)
            _chunks(tm, f3)
            flush(2, (0, 1))

            @pl.when(i == nT - 1)
            def _():
                for g in range(GRP):
                    r = lax.broadcasted_iota(jnp.int32, (BLK, BLK), 0) // 64
                    c = lax.broadcasted_iota(jnp.int32, (BLK, BLK), 1) // 64
                    dsw_ref[g] = jnp.where(c <= r, dsw_ref[g], 0.0)

    def prev(col):
        return pl.BlockSpec((HALO, D), lambda i, p: (jnp.maximum((nT - 1 - i) * hb - 1, 0), col))

    vec = pl.BlockSpec((1, D), lambda i, p: (0, 0))
    const3 = lambda i, p: (0, 0, 0)
    return _pcall(
        body, name="bwd_mix", grid=(nT, NSH),
        in_specs=[
            pl.BlockSpec((tm, D), lambda i, p: (nT - 1 - i, 0)),
            pl.BlockSpec((3, tm, D), lambda i, p: (0, nT - 1 - i, 0)),
            pl.BlockSpec((tm, SHW), lambda i, p: (nT - 1 - i, 3 - p)),
            pl.BlockSpec((tm, D), lambda i, p: (nT - 1 - i, 5)),
            prev(3), prev(7), prev(5),
            pl.BlockSpec((NSH, D // NSH, D), _branch_spec(lambda i, p: 3 - p)),
            vec, vec,
            pl.BlockSpec((GRP, BLK, BLK), const3),
            pl.BlockSpec((BLK, GRP), lambda i, p: (0, 0)),
            pl.BlockSpec((NPG, PGW, PGW), const3),
            vec, vec,
            pl.BlockSpec((3, D), lambda i, p: (0, 0)), vec,
        ],
        out_specs=[
            pl.BlockSpec((3, tm, D), lambda i, p: (3 - p, nT - 1 - i, 0)),
            pl.BlockSpec((3, tm, D), lambda i, p: (0, nT - 1 - i, 0)),
            pl.BlockSpec((GRP, BLK, BLK), const3),
            pl.BlockSpec((BLK, GRP), lambda i, p: (0, 0)),
            pl.BlockSpec((8, D), lambda i, p: (0, 0)),
            pl.BlockSpec((NPG, PGW, PGW), const3),
        ],
        out_shape=[
            jax.ShapeDtypeStruct((12, T, D), BF), jax.ShapeDtypeStruct((3, T, D), BF),
            jax.ShapeDtypeStruct((GRP, BLK, BLK), F32), jax.ShapeDtypeStruct((BLK, GRP), F32),
            jax.ShapeDtypeStruct((8, D), F32), jax.ShapeDtypeStruct((NPG, PGW, PGW), F32),
        ],
        scratch_shapes=[
            pltpu.VMEM((tm, D), F32),
            pltpu.VMEM((tm + HALO, D), F32),
            pltpu.VMEM((tm + HALO, D), F32),
            pltpu.VMEM((2, HALO, D), F32),
            pltpu.VMEM((tm, D), F32),
            pltpu.VMEM((tm, D), BF),
            pltpu.VMEM((tm, D), F32),
            pltpu.VMEM((tm, D), BF),
            pltpu.VMEM((tm, D), BF),
            pltpu.VMEM((tm, D), BF),
            pltpu.VMEM((tm, D), F32),
            pltpu.VMEM((tm, BLK), F32),
            pltpu.VMEM((4, RC, D), F32),
            pltpu.VMEM((BLK, D), F32),
            pltpu.VMEM((GRP, BLK, BLK), BF), pltpu.VMEM((GRP, BLK, BLK), BF),
        ],
        compiler_params=_params(("arbitrary", "arbitrary"), 56),
    )(dxo, p3, proj, proj, proj, proj, proj, wl, lng, lnb, sgw, sgbT, pwb, pb, psc, cw, cb)


def _bwd_in(dproj, wl, x, dxo, ng):
    T = x.shape[0]
    tm = _tile(T, 512)

    def body(dpj_ref, w_ref, x_ref, dxo_ref, ng_ref, dx_ref, dng_ref, acc_s, g_s):
        i = pl.program_id(0)
        j = pl.program_id(1)

        @pl.when((i == 0) & (j == 0))
        def _():
            g_s[...] = jnp.zeros_like(g_s)

        part = sum(_mm_nt(dpj_ref[b], w_ref[b * D:(b + 1) * D, :]) for b in range(3))

        @pl.when(j == 0)
        def _():
            acc_s[...] = part

        @pl.when(j > 0)
        def _():
            acc_s[...] += part

        @pl.when(j == NSH - 1)
        def _():
            def f(r0):
                rows = pl.ds(r0, RC)
                xs = [x_ref[rows, c0:c0 + CW] for c0 in COLS]
                r = lax.rsqrt(sum(_rowsum(v * v) for v in xs) * (1.0 / D) + RMS_EPS)
                xh = [v * r for v in xs]
                dhg, m = [], 0.0
                for k, c0 in enumerate(COLS):
                    cs = slice(c0, c0 + CW)
                    dh = acc_s[rows, cs]
                    g_s[:, cs] += dh * xh[k]
                    dhg.append(dh * ng_ref[:, cs])
                    m = m + _rowsum(dhg[k] * xh[k])
                m = m * (1.0 / D)
                for k, c0 in enumerate(COLS):
                    cs = slice(c0, c0 + CW)
                    dx_ref[rows, cs] = dxo_ref[rows, cs] + r * (dhg[k] - xh[k] * m)
            _chunks(tm, f, unroll=4)

        @pl.when((i == pl.num_programs(0) - 1) & (j == NSH - 1))
        def _():
            dng_ref[...] = jnp.sum(g_s[...], axis=0, keepdims=True)

    vec = pl.BlockSpec((1, D), lambda i, j: (0, 0))
    tile = pl.BlockSpec((tm, D), lambda i, j: (i, 0))
    return _pcall(
        body, name="bwd_in", grid=(T // tm, NSH),
        in_specs=[
            pl.BlockSpec((3, tm, D), lambda i, j: (j, i, 0)),
            pl.BlockSpec((None, SHW, D), lambda i, j: (j, 0, 0)),
            tile, tile, vec,
        ],
        out_specs=[tile, vec],
        out_shape=[jax.ShapeDtypeStruct((T, D), F32), jax.ShapeDtypeStruct((1, D), F32)],
        scratch_shapes=[pltpu.VMEM((tm, D), F32), pltpu.VMEM((RC, D), F32)],
        compiler_params=_params(("arbitrary", "arbitrary"), 48),
    )(dproj, wl, x, dxo, ng)


def _tn_grad(a3, b3, split, out_map, name, into=None):
    nb, T, _ = b3.shape
    tk = _tile(T, 2048)
    nk = T // tk
    rows = D // split
    a_batched = a3.shape[0] > 1

    def body(a_ref, b_ref, *rest):
        o_ref, acc_s = rest[-2:]
        k = pl.program_id(1)

        @pl.when(k == 0)
        def _():
            acc_s[...] = _mm_tn(a_ref[...], b_ref[...])

        @pl.when((k > 0) & (k < nk))
        def _():
            acc_s[...] += _mm_tn(a_ref[...], b_ref[...])

        @pl.when(k >= nk - 1)
        def _():
            r0 = pl.multiple_of((k - (nk - 1)) * rows, rows)
            o_ref[...] = acc_s[pl.ds(r0, rows), :].astype(o_ref.dtype)

    def tok(k):
        return jnp.minimum(k, nk - 1)

    extra = [] if into is None else [into]
    return _pcall(
        body, name=name, grid=(nb, nk + split - 1),
        in_specs=[
            pl.BlockSpec((None, tk, D), (lambda n, k: (n, tok(k), 0)) if a_batched else (lambda n, k: (0, tok(k), 0))),
            pl.BlockSpec((None, tk, D), lambda n, k: (n, tok(k), 0)),
        ] + [ANY] * len(extra),
        out_specs=pl.BlockSpec((None, rows, D), lambda n, k: out_map(n, jnp.maximum(k - (nk - 1), 0))),
        out_shape=jax.ShapeDtypeStruct((NSH, PK_ROWS, D), BF),
        input_output_aliases={2: 0} if extra else {},
        scratch_shapes=[pltpu.VMEM((D, D), F32)],
        compiler_params=_params(("arbitrary", "arbitrary"), 56),
    )(a3, b3, *extra)


def _place():
    x, y, c = lax.axis_index("x"), lax.axis_index("y"), lax.axis_index("c")
    chips = [(1 - x, y), (x, 1 - y), (1 - x, 1 - y)]
    return x, y, c, chips


ANY = pl.BlockSpec(memory_space=pl.ANY)


HBM = pl.BlockSpec(memory_space=pltpu.HBM)
SEM = pl.BlockSpec(memory_space=pltpu.SEMAPHORE)
EFFECT = pltpu.SideEffectType.DATAFLOW_SIDE_EFFECTING


def _own_slot(src, from_slot, name):
    rows = src.shape[-2]
    rb = rows // 9
    me = (2 * lax.axis_index("x") + lax.axis_index("y")).astype(jnp.int32).reshape(1)

    def body(me_ref, src_ref, land_ref):
        land_ref[...] = src_ref[...]

    if from_slot:
        src_spec = pl.BlockSpec((None, rb, D), lambda i, me_ref: (me_ref[0], i, 0))
    else:
        src_spec = pl.BlockSpec((rb, D), lambda i, me_ref: (i, 0))
    return _pcall(
        body, name=name,
        grid_spec=pltpu.PrefetchScalarGridSpec(
            num_scalar_prefetch=1, grid=(rows // rb,), in_specs=[src_spec],
            out_specs=pl.BlockSpec((None, rb, D), lambda i, me_ref: (me_ref[0], i, 0))),
        out_shape=jax.ShapeDtypeStruct((NSH, rows, D), src.dtype),
        compiler_params=_params(("arbitrary",), 32),
    )(me, src)


def _push_start(name, srcs, lands, per_peer):
    n = len(srcs)

    def body(*refs):
        src, land = refs[:n], refs[n:2 * n]
        ssem, rsem = refs[2 * n:5 * n], refs[5 * n:8 * n]
        token = refs[-1]
        x, y, c, chips = _place()
        me = 2 * x + y
        for i in range(n):
            for j, (px, py) in enumerate(chips):
                pltpu.make_async_remote_copy(
                    src_ref=src[i].at[2 * px + py] if per_peer else src[i], dst_ref=land[i].at[me],
                    send_sem=ssem[3 * i + j], recv_sem=rsem[3 * i + j], device_id=(px, py, c), device_id_type=MESH).start()
        token[...] = jnp.zeros_like(token)

    ops = list(srcs) + list(lands)
    out = _pcall(
        body, name=name,
        out_shape=tuple([pltpu.SemaphoreType.DMA(())] * (6 * n) + [pltpu.HBM(a.shape, a.dtype) for a in ops]
                        + [jax.ShapeDtypeStruct((8, 128), F32)]),
        in_specs=[HBM] * (2 * n),
        out_specs=tuple([SEM] * (6 * n) + [HBM] * (2 * n) + [pl.BlockSpec(memory_space=pltpu.VMEM)]),
        input_output_aliases={i: 6 * n + i for i in range(2 * n)},
        compiler_params=pltpu.CompilerParams(has_side_effects=EFFECT),
    )(*[pltpu.with_memory_space_constraint(a, pltpu.HBM) for a in ops])
    return out[:3 * n], out[3 * n:6 * n], out[6 * n:7 * n], out[7 * n:8 * n], out[-1]


def _push_wait(name, src, land, ssem, rsem, after, per_peer):
    def body(src_ref, land_ref, s0, s1, s2, r0, r1, r2, after_ref, src_out, land_out):
        x, y, c, chips = _place()
        for (px, py), s, r in zip(chips, (s0, s1, s2), (r0, r1, r2)):
            k = 2 * px + py
            cp = pltpu.make_async_remote_copy(
                src_ref=src_ref.at[k] if per_peer else src_ref, dst_ref=land_ref.at[k], send_sem=s, recv_sem=r,
                device_id=(px, py, c), device_id_type=MESH)
            cp.wait_send()
            cp.wait_recv()

    return _pcall(
        body, name=name,
        out_shape=(pltpu.HBM(src.shape, src.dtype), pltpu.HBM(land.shape, land.dtype)),
        in_specs=[HBM, HBM] + [SEM] * 6 + [ANY], out_specs=(HBM, HBM),
        input_output_aliases={0: 0, 1: 1},
        compiler_params=pltpu.CompilerParams(has_side_effects=EFFECT),
    )(src, land, *ssem, *rsem, after)[1]


def _swap_sibling(arrs):
    n = len(arrs)

    def body(*refs):
        src, dst = refs[:n], refs[n:2 * n]
        ssem, rsem = refs[2 * n:]
        x, y, c, _ = _place()
        cps = [pltpu.make_async_remote_copy(src_ref=src[a], dst_ref=dst[a], send_sem=ssem.at[a], recv_sem=rsem.at[a],
                                            device_id=(x, y, 1 - c), device_id_type=MESH) for a in range(n)]
        for cp in cps:
            cp.start()
        for cp in cps:
            cp.wait()

    return _pcall(
        body, name="swap_sibling",
        in_specs=[ANY] * n, out_specs=[ANY] * n,
        out_shape=[jax.ShapeDtypeStruct(a.shape, a.dtype) for a in arrs],
        scratch_shapes=[pltpu.SemaphoreType.DMA((n,)), pltpu.SemaphoreType.DMA((n,))],
    )(*arrs)


def _gather_all(v):
    def body(src, dst, ssem, rsem, lsem):
        x, y, c, _ = _place()
        me = 4 * x + 2 * y + c
        peers = [(x, y, 1 - c), (1 - x, y, c), (1 - x, y, 1 - c), (x, 1 - y, c), (x, 1 - y, 1 - c),
                 (1 - x, 1 - y, c), (1 - x, 1 - y, 1 - c)]
        local = pltpu.make_async_copy(src, dst.at[me], lsem)
        local.start()
        sends = [pltpu.make_async_remote_copy(src_ref=src, dst_ref=dst.at[me], send_sem=ssem.at[j], recv_sem=rsem.at[j],
                                              device_id=pr, device_id_type=MESH) for j, pr in enumerate(peers)]
        for cp in sends:
            cp.start()
        for j, (px, py, pc) in enumerate(peers):
            pltpu.make_async_remote_copy(src_ref=src, dst_ref=dst.at[4 * px + 2 * py + pc], send_sem=ssem.at[j],
                                         recv_sem=rsem.at[j], device_id=(px, py, pc), device_id_type=MESH).wait_recv()
        for cp in sends:
            cp.wait_send()
        local.wait()

    return _pcall(
        body, name="gather_all", in_specs=[ANY], out_specs=ANY,
        out_shape=jax.ShapeDtypeStruct((8,) + v.shape, v.dtype),
        scratch_shapes=[pltpu.SemaphoreType.DMA((7,)), pltpu.SemaphoreType.DMA((7,)), pltpu.SemaphoreType.DMA(())],
    )(v)


def _sum_slots(r, rb):
    S = r.shape[0]

    def body(r_ref, o_ref):
        acc = r_ref[0].astype(F32)
        for s in range(1, S):
            acc = acc + r_ref[s].astype(F32)
        o_ref[...] = acc

    if r.ndim == 3:
        _, R, C = r.shape
        grid, blk, imap = (R // rb,), (S, rb, C), (lambda i: (0, i, 0))
        oblk, omap = (rb, C), (lambda i: (i, 0))
    else:
        _, K, R, C = r.shape
        grid, blk, imap = (K,), (S, None, R, C), (lambda i: (0, i, 0, 0))
        oblk, omap = (None, R, C), (lambda i: (i, 0, 0))
    return _pcall(
        body, name="sum_slots", grid=grid, in_specs=[pl.BlockSpec(blk, imap)], out_specs=pl.BlockSpec(oblk, omap),
        out_shape=jax.ShapeDtypeStruct(r.shape[1:], F32), compiler_params=_params(("arbitrary",), 48),
    )(r)


def _adamw(gs, g_spec, w, m, v, p_spec, prev, grid):
    ng = len(gs)
    bc1 = 1.0 - ADAM_B1 ** ADAM_STEP
    bc2 = 1.0 - ADAM_B2 ** ADAM_STEP

    def body(*refs):
        g = refs[0][...]
        for a in range(1, ng):
            g = g + refs[a][...]
        w_ref, m_ref, v_ref = refs[ng:ng + 3]
        go, do, mo, vo = refs[ng + 3 + 4:]
        mn = ADAM_B1 * m_ref[...] + (1.0 - ADAM_B1) * g
        vn = ADAM_B2 * v_ref[...] + (1.0 - ADAM_B2) * (g * g)
        go[...] = g
        mo[...] = mn
        vo[...] = vn
        do[...] = -ADAM_LR * ((mn / bc1) / (jnp.sqrt(vn / bc2) + ADAM_EPS) + ADAM_WD * w_ref[...])

    out = jax.ShapeDtypeStruct(w.shape, F32)
    k0 = ng + 3
    return _pcall(
        body, name="adamw", grid=grid,
        in_specs=[g_spec] * ng + [p_spec] * 3 + [ANY] * 4,
        out_specs=[p_spec] * 4, out_shape=[out] * 4,
        input_output_aliases={k0: 0, k0 + 1: 1, k0 + 2: 2, k0 + 3: 3},
        compiler_params=_params(("arbitrary",) * len(grid), 48),
    )(*gs, w, m, v, *prev)


def _empty4(w):
    return tuple(lax.empty(w.shape, F32) for _ in range(4))


N_SGW = L * GRP * BLK * BLK // D
O_NG, O_VEC, O_SGB, O_FG, O_SGW = 0, 8, 32, 40, 48
O_CW = O_SGW + N_SGW
N_PACK = O_CW + 16
PACK_RB = N_PACK // 3


def _pad_to(a, rows):
    return jnp.pad(a, ((0, rows - a.shape[0]), (0, 0)))


def _pack_small(ng, vecs, sgb, fg, sgw, cw):
    parts = [_pad_to(ng, 8), _pad_to(vecs.reshape(L * 5, D), 24), _pad_to(sgb.reshape(L, D), 8),
             _pad_to(fg.reshape(1, D), 8), sgw.reshape(N_SGW, D), _pad_to(cw, 16)]
    return jnp.concatenate(parts, axis=0)


def kernel(x, norm_g, w_in, sgu_ln_g, sgu_ln_b, sgu_w, sgu_b, pool_w, pool_b, pool_scale, conv_w, conv_b, w_branch_a, w_branch_b, w_branch_c, w_out, final_g, loss_target, m_norm_g, m_w_in, m_sgu_ln_g, m_sgu_ln_b, m_sgu_w, m_sgu_b, m_pool_w, m_pool_b, m_pool_scale, m_conv_w, m_conv_b, m_w_branch_a, m_w_branch_b, m_w_branch_c, m_w_out, m_final_g, v_norm_g, v_w_in, v_sgu_ln_g, v_sgu_ln_b, v_sgu_w, v_sgu_b, v_pool_w, v_pool_b, v_pool_scale, v_conv_w, v_conv_b, v_w_branch_a, v_w_branch_b, v_w_branch_c, v_w_out, v_final_g):
    cx, cy = lax.axis_index("x"), lax.axis_index("y")
    me = 2 * cx + cy
    xl, tgt = x[0], loss_target[0]
    q = D // NSH

    wq = w_in.astype(BF).reshape(L, D, 3, D).transpose(0, 2, 1, 3).reshape(L, SHW, D)
    brq = jnp.stack([w_branch_a, w_branch_b, w_branch_c, w_out], axis=1).astype(BF).reshape(L, D, D)
    pwq = pool_w.astype(BF).reshape(L, PK_CW - PK_PW, D)
    cwq = lax.bitcast_convert_type(conv_w, BF).reshape(L, 3 * q * 2)
    cwq = jnp.pad(cwq, ((0, 0), (0, 16 * D - 3 * q * 2))).reshape(L, 16, D)
    packs = [jnp.concatenate([wq[l], brq[l], pwq[l], cwq[l]], axis=0) for l in range(L)]
    lands = [_own_slot(packs[l], False, f"ag_own_{l}") for l in range(L)]
    ag_s, ag_r, packs, lands, tok = _push_start("ag_start", packs, lands, False)
    sgbT = sgu_b.transpose(0, 2, 1)

    def layer_weights(l, after):
        wl = _push_wait(f"ag_wait_{l}", packs[l], lands[l], ag_s[3 * l:3 * l + 3], ag_r[3 * l:3 * l + 3], after, False)
        pwb = wl[:, PK_PW:PK_CW].reshape(NSH, NPG, PGW // NSH, PGW).transpose(1, 0, 2, 3).reshape(NPG, PGW, PGW)
        cwb = wl[:, PK_CW:].reshape(NSH, 16 * D)[:, :3 * q * 2].reshape(NSH, 3, q, 2)
        cwf = lax.bitcast_convert_type(cwb, F32).transpose(1, 0, 2).reshape(3, D)
        small = (sgu_ln_g[l:l + 1], sgu_ln_b[l:l + 1], sgu_w[l], sgbT[l], pwb, pool_b[l:l + 1], pool_scale[l:l + 1],
                 cwf, conv_b[l:l + 1])
        return wl, small

    xs, saved, wts = [xl], [], []
    for l in range(L):
        wl, small = layer_weights(l, tok if l == 0 else xs[l])
        wts.append((wl, small))
        proj, h, y3 = _fwd_in(xs[l], norm_g[l:l + 1], wl, *small)
        p3, mg, xo = _fwd_out(y3, proj, xs[l], wl)
        saved.append((proj, h, y3, p3, mg))
        xs.append(xo)

    dx, sq, dfg = _loss_head(xs[L], final_g[None], tgt)
    loss = lax.psum(jnp.sum(sq) * (0.5 / D), ("x", "y", "c"))

    g_w_in = _empty4(w_in)
    g_br = [_empty4(w_out) for _ in range(4)]
    g_pw = _empty4(pool_w)
    dng, dvec, dsgw, dsgb = [None] * L, [None] * L, [None] * L, [None] * L
    branches = [(w_branch_a, m_w_branch_a, v_w_branch_a), (w_branch_b, m_w_branch_b, v_w_branch_b),
                (w_branch_c, m_w_branch_c, v_w_branch_c), (w_out, m_w_out, v_w_out)]
    nb = D // 128

    def finish(l, landed):
        nonlocal g_w_in, g_pw
        mine = _sum_slots(landed, PK_ROWS // 9)
        sums = [mine, _swap_sibling([mine])[0]]
        g_w_in = _adamw(sums, pl.BlockSpec((128, D), lambda b, i: (b * nb + i, 0)), w_in, m_w_in, v_w_in,
                        pl.BlockSpec((None, 128, D), lambda b, i: (l, i, b)), g_w_in, (3, nb))
        for k, (w, m, v) in enumerate(branches):
            g_br[k] = _adamw(sums, pl.BlockSpec((q, D), lambda i, k=k: (PK_BR // q + k, 0)), w, m, v,
                             pl.BlockSpec((None, q, D), lambda i: (l, 0, 0)), g_br[k], (1,))
        pools = [a[PK_PW:PK_CW].reshape(NPG, PGW // NSH, PGW) for a in sums]
        g_pw = _adamw(pools, pl.BlockSpec((None, PGW // NSH, PGW), lambda g: (g, 0, 0)), pool_w, m_pool_w, v_pool_w,
                      pl.BlockSpec((None, None, PGW // NSH, PGW), lambda g: (l, g, 0, 0)), g_pw, (NPG,))

    pend = None
    for l in reversed(range(L)):
        proj, h, y3, p3, mg = saved[l]
        wl, small = wts[l]
        dproj, dp3, dsgw[l], dsbT, dvec[l], dpw = _bwd_mix(dx, p3, proj, wl, *small)
        dsgb[l] = dsbT.T
        grads = _tn_grad(h[None], dproj, 1, lambda n, s: (n // 3, n % 3, 0), "grad_w_in")
        grads = _tn_grad(y3, dp3, NSH, lambda n, s: (s, PK_BR // q + n, 0), "grad_w_branch", into=grads)
        grads = _tn_grad(mg[None], dx[None], NSH, lambda n, s: (s, PK_BR // q + 3, 0), "grad_w_out", into=grads)
        dpq = dpw.astype(BF).reshape(NPG, NSH, PGW // NSH, PGW).transpose(1, 0, 2, 3).reshape(NSH, PK_CW - PK_PW, D)
        grads = lax.dynamic_update_slice(grads, jnp.pad(dpq, ((0, 0), (0, PK_ROWS - PK_CW), (0, 0))), (0, PK_PW, 0))
        if pend is not None:
            landed = _push_wait(f"rs_wait_{pend[0]}", *pend[1:], dproj, True)
        land = _own_slot(grads, True, f"rs_own_{l}")
        ss, rs, (grads,), (land,), tok = _push_start(f"rs_start_{l}", [grads], [land], True)
        if pend is not None:
            finish(pend[0], landed)
        dx, dng[l] = _bwd_in(dproj, wl, xs[l], dx, norm_g[l:l + 1] + tok[0, 0])
        pend = (l, grads, land, ss, rs)
    finish(pend[0], _push_wait(f"rs_wait_{pend[0]}", *pend[1:], dx, True))

    dvec = jnp.stack(dvec)
    small = _pack_small(jnp.concatenate(dng), dvec[:, 0:5], jnp.stack(dsgb), dfg[0], jnp.stack(dsgw),
                        dvec[:, 5:8].reshape(L * 3, D))
    gsmall = _sum_slots(_gather_all(small), PACK_RB)
    gcw = lax.dynamic_slice_in_dim(gsmall[O_CW:O_CW + L * 3], me * q, q, axis=1)
    gpack = jnp.concatenate([gsmall[:O_CW], _pad_to(gcw.reshape(L * 3 * q // D, D), 16)])

    def pack(ng, lg, lb, sw, sb, pb_, ps, cwv, cb_, fg):
        return _pack_small(ng, jnp.stack([lg, lb, pb_, ps, cb_], axis=1), sb, fg, sw, cwv.reshape(L * 3 * q // D, D))

    wp = pack(norm_g, sgu_ln_g, sgu_ln_b, sgu_w, sgu_b, pool_b, pool_scale, conv_w, conv_b, final_g)
    mp = pack(m_norm_g, m_sgu_ln_g, m_sgu_ln_b, m_sgu_w, m_sgu_b, m_pool_b, m_pool_scale, m_conv_w, m_conv_b, m_final_g)
    vp = pack(v_norm_g, v_sgu_ln_g, v_sgu_ln_b, v_sgu_w, v_sgu_b, v_pool_b, v_pool_scale, v_conv_w, v_conv_b, v_final_g)
    rows = pl.BlockSpec((PACK_RB, D), lambda i: (i, 0))
    sm = _adamw([gpack], rows, wp, mp, vp, rows, _empty4(wp), (N_PACK // PACK_RB,))

    def unpack(a):
        vv = a[O_VEC:O_VEC + L * 5].reshape(L, 5, D)
        sb = a[O_SGB:O_SGB + L].reshape(L, GRP, BLK)
        fg = a[O_FG]
        sw = a[O_SGW:O_SGW + N_SGW].reshape(L, GRP, BLK, BLK)
        cwv = a[O_CW:O_CW + L * 3 * q // D].reshape(L, 3, q)
        return dict(norm_g=a[O_NG:O_NG + L], w_in=None, sgu_ln_g=vv[:, 0], sgu_ln_b=vv[:, 1], sgu_w=sw, sgu_b=sb, pool_w=None,
                    pool_b=vv[:, 2], pool_scale=vv[:, 3], conv_w=cwv, conv_b=vv[:, 4], w_branch_a=None,
                    w_branch_b=None, w_branch_c=None, w_out=None, final_g=fg)

    outs = [loss, dx[None]]
    for kind in range(4):
        d = unpack(sm[kind])
        d.update(w_in=g_w_in[kind], pool_w=g_pw[kind], w_branch_a=g_br[0][kind], w_branch_b=g_br[1][kind],
                 w_branch_c=g_br[2][kind], w_out=g_br[3][kind])
        outs.extend(d[n] for n in ("norm_g", "w_in", "sgu_ln_g", "sgu_ln_b", "sgu_w", "sgu_b", "pool_w", "pool_b",
                                   "pool_scale", "conv_w", "conv_b", "w_branch_a", "w_branch_b", "w_branch_c", "w_out",
                                   "final_g"))
    return tuple(outs)
```

```python
import functools

import jax
import jax.numpy as jnp
from jax import lax
from jax.experimental import pallas as pl
from jax.experimental.pallas import tpu as pltpu

F32 = jnp.float32
BF = jnp.bfloat16
MESH = pl.DeviceIdType.MESH

D = 1024
L = 4
NSH = 4
SHW = 3 * D
NIN = NSH * SHW
GRP = 8
BLK = 128
NPG = 4
PGW = D // NPG
POOL_WINDOWS = (2, 4, 8, 16)
HALO = 16
RMS_EPS = 1e-6
LN_EPS = 1e-5
ADAM_LR, ADAM_B1, ADAM_B2, ADAM_EPS, ADAM_WD, ADAM_STEP = 0.001, 0.9, 0.999, 1e-8, 0.01, 10

RC = 16
CW = 512
COLS = tuple(range(0, D, CW))
MIB = 1 << 20

PK_BR = SHW
PK_PW = PK_BR + D
PK_CW = PK_PW + NPG * (PGW // NSH) * PGW // D
PK_ROWS = PK_CW + 16

_C0 = 0.7978845608028654
_C1 = 0.044715


def _pcall(body, **kw):
    return pl.pallas_call(body, **kw)


def _params(sem, vmem_mib):
    return pltpu.CompilerParams(dimension_semantics=sem, vmem_limit_bytes=vmem_mib * MIB)


def _mm(a, b):
    return jnp.dot(a.astype(BF), b.astype(BF), preferred_element_type=F32)


def _mm_nt(a, b):
    return lax.dot_general(a.astype(BF), b.astype(BF), (((1,), (1,)), ((), ())), preferred_element_type=F32)


def _mm_tn(a, b):
    return lax.dot_general(a.astype(BF), b.astype(BF), (((0,), (0,)), ((), ())), preferred_element_type=F32)


def _gelu(x):
    return 0.5 * x * (1.0 + jnp.tanh(_C0 * x * (1.0 + _C1 * x * x)))


def _gelu_fg(x):
    x2 = x * x
    t = jnp.tanh(_C0 * x * (1.0 + _C1 * x2))
    f = 0.5 * x * (1.0 + t)
    g = 0.5 * (1.0 + t) + 0.5 * x * (1.0 - t * t) * (_C0 * (1.0 + 3.0 * _C1 * x2))
    return f, g


def _sigmoid(x):
    return 0.5 * jnp.tanh(0.5 * x) + 0.5


def _silu(x):
    return x * _sigmoid(x)


def _silu_fg(x):
    s = _sigmoid(x)
    return x * s, s * (1.0 + x * (1.0 - s))


def _rowsum(v):
    return jnp.sum(v, axis=1, keepdims=True)


def _chunks(n_rows, fn, unroll=2):
    def body(c, carry):
        fn(pl.multiple_of(c * RC, RC))
        return carry
    lax.fori_loop(0, n_rows // RC, body, 0, unroll=unroll)


def _trail(e, win):
    s, sh = e, 1
    while sh < win:
        s = s + pltpu.roll(s, sh, 0)
        sh *= 2
    return s


def _lead(e, win):
    n = e.shape[0]
    s, sh = e, 1
    while sh < win:
        s = s + pltpu.roll(s, n - sh, 0)
        sh *= 2
    return s


def _inv_count(pos0, win):
    pos = pos0 + lax.broadcasted_iota(jnp.int32, (RC, 1), 0)
    return 1.0 / jnp.minimum(pos + 1, win).astype(F32)


def _masked_sgu(sgw_ref, g):
    r = lax.broadcasted_iota(jnp.int32, (BLK, BLK), 0) // 64
    c = lax.broadcasted_iota(jnp.int32, (BLK, BLK), 1) // 64
    return jnp.where(c <= r, sgw_ref[g], 0.0)


def _tile(t, want):
    return min(t, want)


def _fwd_in(x, ng, wl, lng, lnb, sgw, sgbT, pwb, pb, psc, cw, cb):
    T = x.shape[0]
    tm = _tile(T, 256)
    nT = T // tm

    def body(x_ref, ng_ref, w_hbm, lng_ref, lnb_ref, sgw_ref, sgbT_ref, pw_ref, pb_ref, psc_ref, cw_ref, cb_ref,
             proj_hbm, h_ref, y_ref, w_s, pja, pjb, pjc, pjd, h_s, vn_s, mix_s, d_s, wm_s, extp_s, extc_s, wsem, psem):
        i = pl.program_id(0)
        pj = (pja, pjb, pjc, pjd)

        def w_copy(p):
            return pltpu.make_async_copy(w_hbm.at[p, pl.ds(0, SHW), :], w_s.at[p], wsem.at[p])

        def p_copy(p):
            dst = proj_hbm.at[pl.ds(pl.multiple_of(i * tm, tm), tm), pl.ds(p * SHW, SHW)]
            return pltpu.make_async_copy(pj[p], dst, psem.at[p])

        def all_rows(fn):
            for c in range(tm // RC):
                fn(c * RC)

        def buffer_free(p):
            @pl.when(i == 0)
            def _():
                w_copy(p).wait()

            @pl.when(i > 0)
            def _():
                p_copy(p).wait()

        def project(p, blocks):
            for b in blocks:
                pj[p][:, b * D:(b + 1) * D] = jnp.dot(h_s[...], w_s[p, b * D:(b + 1) * D, :],
                                                      preferred_element_type=F32)

        @pl.when(i == 0)
        def _():
            for p in range(NSH):
                w_copy(p).start()
            extp_s[0:HALO, :] = jnp.zeros((HALO, D), F32)
            extc_s[0:HALO, :] = jnp.zeros((HALO, D), F32)
            for g in range(GRP):
                wm_s[g] = _masked_sgu(sgw_ref, g).astype(BF)

        def norm_rows(r0):
            rows = pl.ds(r0, RC)
            xs = [x_ref[rows, c0:c0 + CW] for c0 in COLS]
            ms = sum(_rowsum(v * v) for v in xs) * (1.0 / D)
            r = lax.rsqrt(ms + RMS_EPS)
            for k, c0 in enumerate(COLS):
                hb = (xs[k] * r * ng_ref[:, c0:c0 + CW]).astype(BF)
                h_s[rows, c0:c0 + CW] = hb
                h_ref[rows, c0:c0 + CW] = hb

        def gating(first, rest):
            proj_ref = pj[0]
            first()

            def f1(r0):
                rows = pl.ds(r0, RC)
                gv = [_gelu(proj_ref[rows, D + c0:D + c0 + CW]) for c0 in COLS]
                mu = sum(_rowsum(v) for v in gv) * (1.0 / D)
                dv = [v - mu for v in gv]
                var = sum(_rowsum(v * v) for v in dv) * (1.0 / D)
                rstd = lax.rsqrt(var + LN_EPS)
                for k, c0 in enumerate(COLS):
                    vn_s[rows, c0:c0 + CW] = (dv[k] * rstd * lng_ref[:, c0:c0 + CW] + lnb_ref[:, c0:c0 + CW]).astype(BF)
            all_rows(f1)
            for g in range(GRP):
                w = wm_s[g]
                bcol = sgbT_ref[:, g:g + 1]
                gc = slice(g * BLK, (g + 1) * BLK)
                for n in range(tm // BLK):
                    rr = slice(n * BLK, (n + 1) * BLK)
                    mix_s[rr, gc] = jnp.dot(w, vn_s[rr, gc], preferred_element_type=F32) + bcol
            rest()

            def f2(r0):
                rows = pl.ds(r0, RC)
                for c0 in COLS:
                    au = proj_ref[rows, c0:c0 + CW]
                    az = proj_ref[rows, 2 * D + c0:2 * D + c0 + CW]
                    y_ref[0, rows, c0:c0 + CW] = (_gelu(au) * mix_s[rows, c0:c0 + CW] * _silu(az)).astype(BF)
            all_rows(f2)

        def pooling(first, rest):
            proj_ref = pj[1]
            first()
            extp_s[HALO:HALO + tm, :] = proj_ref[:, 0:D]

            def f1(r0):
                rows = pl.ds(r0, RC)
                e = extp_s[pl.ds(r0, 2 * RC), :]
                for k, win in enumerate(POOL_WINDOWS):
                    gc = slice(k * PGW, (k + 1) * PGW)
                    s = _trail(e[:, gc], win)[RC:2 * RC]
                    d_s[rows, gc] = (s * _inv_count(i * tm + r0, win) - e[RC:2 * RC, gc]).astype(BF)
            all_rows(f1)
            extp_s[0:HALO, :] = extp_s[tm:tm + HALO, :]
            for k in range(NPG):
                gc = slice(k * PGW, (k + 1) * PGW)
                mix_s[:, gc] = jnp.dot(d_s[:, gc], pw_ref[k], preferred_element_type=F32)
            rest()

            def f2(r0):
                rows = pl.ds(r0, RC)
                for c0 in COLS:
                    cs = slice(c0, c0 + CW)
                    bz = proj_ref[rows, D + c0:D + c0 + CW]
                    y_ref[1, rows, cs] = ((mix_s[rows, cs] + pb_ref[:, cs]) * psc_ref[:, cs] * _silu(bz)).astype(BF)
            all_rows(f2)

        def convolution(first, rest):
            proj_ref = pj[2]
            first()

            def f1(r0):
                rows = pl.ds(r0, RC)
                extc_s[pl.ds(HALO + r0, RC), :] = proj_ref[rows, D:2 * D] * pj[1][rows, 2 * D:3 * D]
            all_rows(f1)
            rest()

            def f2(r0):
                rows = pl.ds(r0, RC)
                for c0 in COLS:
                    cs = slice(c0, c0 + CW)
                    e = extc_s[pl.ds(r0, 2 * RC), cs]
                    conv = (cw_ref[2:3, cs] * e + cw_ref[1:2, cs] * pltpu.roll(e, 1, 0)
                            + cw_ref[0:1, cs] * pltpu.roll(e, 2, 0))[RC:2 * RC] + cb_ref[:, cs]
                    cbv = proj_ref[rows, cs]
                    cz = proj_ref[rows, 2 * D + c0:2 * D + c0 + CW]
                    y_ref[2, rows, cs] = (cbv * conv * _silu(cz)).astype(BF)
            all_rows(f2)
            extc_s[0:HALO, :] = extc_s[tm:tm + HALO, :]

        all_rows(norm_rows)
        buffer_free(0)
        project(0, (0, 1, 2))
        p_copy(0).start()
        for p, mixer, early in ((1, gating, 2), (2, pooling, 1), (3, convolution, 1)):
            buffer_free(p)
            mixer(functools.partial(project, p, range(early)), functools.partial(project, p, range(early, 3)))
            p_copy(p).start()

        @pl.when(i == nT - 1)
        def _():
            for p in range(NSH):
                p_copy(p).wait()

    vec = pl.BlockSpec((1, D), lambda i: (0, 0))
    return _pcall(
        body, name="fwd_in", grid=(nT,),
        in_specs=[
            pl.BlockSpec((tm, D), lambda i: (i, 0)), vec, ANY, vec, vec,
            pl.BlockSpec((GRP, BLK, BLK), lambda i: (0, 0, 0)),
            pl.BlockSpec((BLK, GRP), lambda i: (0, 0)),
            pl.BlockSpec((NPG, PGW, PGW), lambda i: (0, 0, 0)),
            vec, vec,
            pl.BlockSpec((3, D), lambda i: (0, 0)), vec,
        ],
        out_specs=[ANY, pl.BlockSpec((tm, D), lambda i: (i, 0)), pl.BlockSpec((3, tm, D), lambda i: (0, i, 0))],
        out_shape=[jax.ShapeDtypeStruct((T, NIN), F32), jax.ShapeDtypeStruct((T, D), BF),
                   jax.ShapeDtypeStruct((3, T, D), BF)],
        scratch_shapes=[
            pltpu.VMEM((NSH, SHW, D), BF),
            pltpu.VMEM((tm, SHW), F32), pltpu.VMEM((tm, SHW), F32), pltpu.VMEM((tm, SHW), F32), pltpu.VMEM((tm, SHW), F32),
            pltpu.VMEM((tm, D), BF), pltpu.VMEM((tm, D), BF), pltpu.VMEM((tm, D), F32),
            pltpu.VMEM((tm, D), BF), pltpu.VMEM((GRP, BLK, BLK), BF),
            pltpu.VMEM((tm + HALO, D), F32), pltpu.VMEM((tm + HALO, D), F32),
            pltpu.SemaphoreType.DMA((NSH,)), pltpu.SemaphoreType.DMA((NSH,)),
        ],
        compiler_params=_params(("arbitrary",), 56),
    )(x, ng, wl, lng, lnb, sgw, sgbT, pwb, pb, psc, cw, cb)


def _branch_spec(which):
    q = D // NSH
    return lambda *g: (0, PK_BR // q + (which(*g) if callable(which) else which), 0)


def _fwd_out(y3, proj, x, wl):
    T = x.shape[0]
    tm = _tile(T, 256)
    q = D // NSH

    def body(y_ref, gl_ref, x_ref, wa_ref, wb_ref, wc_ref, wo_ref, p_ref, mg_ref, xo_ref):
        for k, w_ref in enumerate((wa_ref, wb_ref, wc_ref)):
            p_ref[k] = jnp.dot(y_ref[k], w_ref[...].reshape(D, D), preferred_element_type=F32)

        def f(r0):
            rows = pl.ds(r0, RC)
            for c0 in COLS:
                cs = slice(c0, c0 + CW)
                m = sum(_sigmoid(gl_ref[rows, k * D + c0:k * D + c0 + CW]) * p_ref[k, rows, cs] for k in range(3))
                mg_ref[rows, cs] = m.astype(BF)
        _chunks(tm, f)
        xo_ref[...] = x_ref[...] + jnp.dot(mg_ref[...], wo_ref[...].reshape(D, D), preferred_element_type=F32)

    return _pcall(
        body, name="fwd_out", grid=(T // tm,),
        in_specs=[
            pl.BlockSpec((3, tm, D), lambda i: (0, i, 0)),
            pl.BlockSpec((tm, SHW), lambda i: (i, 3)),
            pl.BlockSpec((tm, D), lambda i: (i, 0)),
        ] + [pl.BlockSpec((NSH, q, D), _branch_spec(k)) for k in range(4)],
        out_specs=[
            pl.BlockSpec((3, tm, D), lambda i: (0, i, 0)),
            pl.BlockSpec((tm, D), lambda i: (i, 0)),
            pl.BlockSpec((tm, D), lambda i: (i, 0)),
        ],
        out_shape=[jax.ShapeDtypeStruct((3, T, D), F32), jax.ShapeDtypeStruct((T, D), BF),
                   jax.ShapeDtypeStruct((T, D), F32)],
        compiler_params=_params(("arbitrary",), 48),
    )(y3, proj, x, wl, wl, wl, wl)


def _loss_head(x, fg, tgt):
    T = x.shape[0]
    tm = _tile(T, 512)

    def body(x_ref, g_ref, t_ref, dx_ref, sq_ref, dg_ref, acc_s):
        i = pl.program_id(0)

        @pl.when(i == 0)
        def _():
            acc_s[...] = jnp.zeros_like(acc_s)

        def f(r0):
            rows = pl.ds(r0, RC)
            xs = [x_ref[rows, c0:c0 + CW] for c0 in COLS]
            r = lax.rsqrt(sum(_rowsum(v * v) for v in xs) * (1.0 / D) + RMS_EPS)
            xh = [v * r for v in xs]
            dyg, m = [], 0.0
            for k, c0 in enumerate(COLS):
                cs = slice(c0, c0 + CW)
                err = xh[k] * g_ref[:, cs] - t_ref[rows, cs]
                acc_s[0, :, cs] += err * err
                dy = err * (1.0 / D)
                acc_s[1, :, cs] += dy * xh[k]
                dyg.append(dy * g_ref[:, cs])
                m = m + _rowsum(dyg[k] * xh[k])
            m = m * (1.0 / D)
            for k, c0 in enumerate(COLS):
                dx_ref[rows, c0:c0 + CW] = r * (dyg[k] - xh[k] * m)
        _chunks(tm, f)

        @pl.when(i == pl.num_programs(0) - 1)
        def _():
            sq_ref[...] = jnp.sum(acc_s[0], axis=0, keepdims=True)
            dg_ref[...] = jnp.sum(acc_s[1], axis=0, keepdims=True)

    vec = pl.BlockSpec((1, D), lambda i: (0, 0))
    tile = pl.BlockSpec((tm, D), lambda i: (i, 0))
    return _pcall(
        body, name="loss_head", grid=(T // tm,),
        in_specs=[tile, vec, tile], out_specs=[tile, vec, vec],
        out_shape=[jax.ShapeDtypeStruct((T, D), F32), jax.ShapeDtypeStruct((1, D), F32), jax.ShapeDtypeStruct((1, D), F32)],
        scratch_shapes=[pltpu.VMEM((2, RC, D), F32)],
        compiler_params=_params(("arbitrary",), 32),
    )(x, fg, tgt)


def _bwd_mix(dxo, p3, proj, wl, lng, lnb, sgw, sgbT, pwb, pb, psc, cw, cb):
    T = dxo.shape[0]
    tm = _tile(T, 256)
    nT = T // tm
    hb = tm // HALO

    def body(dxo_ref, p_ref, pj_ref, ch_ref, bpp_ref, ccp_ref, chp_ref, w_ref, lng_ref, lnb_ref, sgw_ref, sgbT_ref,
             pw_ref, pb_ref, psc_ref, cw_ref, cb_ref,
             dpj_ref, dp_ref, dsw_ref, dsbT_ref, vec_ref, dpw_ref,
             dy_s, ext_s, nxt_s, halo_s, dch_s, d_s, t_s, dy0_s, vn_s, dmix_s, xh_s, rstd_s, acc_s, accb_s, wm_s, wmT_s):
        i = pl.program_id(0)
        p = pl.program_id(1)
        ti = nT - 1 - i

        @pl.when((i == 0) & (p == 0))
        def _():
            dsw_ref[...] = jnp.zeros_like(dsw_ref)
            dsbT_ref[...] = jnp.zeros_like(dsbT_ref)
            vec_ref[...] = jnp.zeros_like(vec_ref)
            dpw_ref[...] = jnp.zeros_like(dpw_ref)
            halo_s[...] = jnp.zeros_like(halo_s)
            for g in range(GRP):
                wm = _masked_sgu(sgw_ref, g)
                wm_s[g] = wm.astype(BF)
                wmT_s[g] = wm.T.astype(BF)

        def flush(n_acc, rows_of):
            for a in range(n_acc):
                vec_ref[rows_of[a]:rows_of[a] + 1, :] += jnp.sum(acc_s[a], axis=0, keepdims=True)

        @pl.when(p == 0)
        def _():
            dy_s[...] = _mm_nt(dxo_ref[...], w_ref[...].reshape(D, D))

            def f(r0):
                rows = pl.ds(r0, RC)
                for c0 in COLS:
                    cs = slice(c0, c0 + CW)
                    dm = dy_s[rows, cs]
                    for k in range(3):
                        s = _sigmoid(pj_ref[rows, k * D + c0:k * D + c0 + CW])
                        dp_ref[k, rows, cs] = (s * dm).astype(BF)
                        dpj_ref[k, rows, cs] = (dm * p_ref[k, rows, cs] * s * (1.0 - s)).astype(BF)
            _chunks(tm, f)

        @pl.when(p == 1)
        def _():
            dy_s[...] = _mm_nt(dp_ref[2], w_ref[...].reshape(D, D))
            acc_s[...] = jnp.zeros_like(acc_s)
            ext_s[0:HALO, :] = jnp.where(ti > 0, ccp_ref[...] * chp_ref[...], 0.0)
            nxt_s[tm:tm + HALO, :] = halo_s[0]

            def f1(r0):
                rows = pl.ds(r0, RC)
                ext_s[pl.ds(HALO + r0, RC), :] = pj_ref[rows, D:2 * D] * ch_ref[rows, :]
            _chunks(tm, f1, unroll=4)

            def f2(r0):
                rows = pl.ds(r0, RC)
                for c0 in COLS:
                    cs = slice(c0, c0 + CW)
                    e = ext_s[pl.ds(r0, 2 * RC), cs]
                    e0, e1, e2 = e[RC:2 * RC], pltpu.roll(e, 1, 0)[RC:2 * RC], pltpu.roll(e, 2, 0)[RC:2 * RC]
                    conv = cw_ref[2:3, cs] * e0 + cw_ref[1:2, cs] * e1 + cw_ref[0:1, cs] * e2 + cb_ref[:, cs]
                    cbv = pj_ref[rows, cs]
                    sz, sg = _silu_fg(pj_ref[rows, 2 * D + c0:2 * D + c0 + CW])
                    dyc = dy_s[rows, cs]
                    dconv = dyc * cbv * sz
                    dpj_ref[0, rows, cs] = (dyc * conv * sz).astype(BF)
                    dpj_ref[2, rows, cs] = (dyc * cbv * conv * sg).astype(BF)
                    nxt_s[rows, cs] = dconv
                    acc_s[0, :, cs] += dconv
                    acc_s[1, :, cs] += dconv * e2
                    acc_s[2, :, cs] += dconv * e1
                    acc_s[3, :, cs] += dconv * e0
            _chunks(tm, f2)

            def f3(r0):
                rows = pl.ds(r0, RC)
                for c0 in COLS:
                    cs = slice(c0, c0 + CW)
                    e = nxt_s[pl.ds(r0, 2 * RC), cs]
                    dcc = (cw_ref[2:3, cs] * e + cw_ref[1:2, cs] * pltpu.roll(e, 2 * RC - 1, 0)
                           + cw_ref[0:1, cs] * pltpu.roll(e, 2 * RC - 2, 0))[0:RC]
                    dpj_ref[1, rows, cs] = (dcc * ch_ref[rows, cs]).astype(BF)
                    dch_s[rows, cs] = dcc * pj_ref[rows, D + c0:D + c0 + CW]
            _chunks(tm, f3)
            halo_s[0] = nxt_s[0:HALO, :]
            flush(4, (4, 5, 6, 7))

        @pl.when(p == 2)
        def _():
            dy_s[...] = _mm_nt(dp_ref[1], w_ref[...].reshape(D, D))
            acc_s[...] = jnp.zeros_like(acc_s)
            ext_s[0:HALO, :] = jnp.where(ti > 0, bpp_ref[...], 0.0)
            ext_s[HALO:HALO + tm, :] = pj_ref[:, 0:D]
            nxt_s[tm:tm + HALO, :] = halo_s[1]

            def f1(r0):
                rows = pl.ds(r0, RC)
                e = ext_s[pl.ds(r0, 2 * RC), :]
                for k, win in enumerate(POOL_WINDOWS):
                    gc = slice(k * PGW, (k + 1) * PGW)
                    s = _trail(e[:, gc], win)[RC:2 * RC]
                    d_s[rows, gc] = (s * _inv_count(ti * tm + r0, win) - e[RC:2 * RC, gc]).astype(BF)
            _chunks(tm, f1)
            for k in range(NPG):
                gc = slice(k * PGW, (k + 1) * PGW)
                t_s[:, gc] = jnp.dot(d_s[:, gc], pw_ref[k], preferred_element_type=F32)

            def f2(r0):
                rows = pl.ds(r0, RC)
                for c0 in COLS:
                    cs = slice(c0, c0 + CW)
                    y0 = t_s[rows, cs] + pb_ref[:, cs]
                    sz, sg = _silu_fg(pj_ref[rows, D + c0:D + c0 + CW])
                    dyb = dy_s[rows, cs]
                    dy0 = dyb * psc_ref[:, cs] * sz
                    acc_s[0, :, cs] += dy0
                    acc_s[1, :, cs] += dyb * y0 * sz
                    dy0_s[rows, cs] = dy0.astype(BF)
                    dpj_ref[1, rows, cs] = (dyb * y0 * psc_ref[:, cs] * sg).astype(BF)
            _chunks(tm, f2)
            for k in range(NPG):
                gc = slice(k * PGW, (k + 1) * PGW)
                dpw_ref[k] += _mm_tn(d_s[:, gc], dy0_s[:, gc])
                t_s[:, gc] = _mm_nt(dy0_s[:, gc], pw_ref[k])

            def f3(r0):
                rows = pl.ds(r0, RC)
                for k, win in enumerate(POOL_WINDOWS):
                    gc = slice(k * PGW, (k + 1) * PGW)
                    nxt_s[rows, gc] = t_s[rows, gc] * _inv_count(ti * tm + r0, win)
            _chunks(tm, f3, unroll=4)

            def f4(r0):
                rows = pl.ds(r0, RC)
                e = nxt_s[pl.ds(r0, 2 * RC), :]
                for k, win in enumerate(POOL_WINDOWS):
                    gc = slice(k * PGW, (k + 1) * PGW)
                    dpj_ref[0, rows, gc] = (_lead(e[:, gc], win)[0:RC] - t_s[rows, gc]).astype(BF)
                dpj_ref[2, rows, :] = dch_s[rows, :].astype(BF)
            _chunks(tm, f4)
            halo_s[1] = nxt_s[0:HALO, :]
            flush(2, (2, 3))

        @pl.when(p == 3)
        def _():
            dy_s[...] = _mm_nt(dp_ref[0], w_ref[...].reshape(D, D))
            acc_s[...] = jnp.zeros_like(acc_s)
            accb_s[...] = jnp.zeros_like(accb_s)

            def f1(r0):
                rows = pl.ds(r0, RC)
                gv = [_gelu(pj_ref[rows, D + c0:D + c0 + CW]) for c0 in COLS]
                mu = sum(_rowsum(v) for v in gv) * (1.0 / D)
                dv = [v - mu for v in gv]
                var = sum(_rowsum(v * v) for v in dv) * (1.0 / D)
                rstd = lax.rsqrt(var + LN_EPS)
                rstd_s[rows, :] = jnp.broadcast_to(rstd, (RC, BLK))
                for k, c0 in enumerate(COLS):
                    cs = slice(c0, c0 + CW)
                    xh = dv[k] * rstd
                    xh_s[rows, cs] = xh
                    vn_s[rows, cs] = (xh * lng_ref[:, cs] + lnb_ref[:, cs]).astype(BF)
            _chunks(tm, f1, unroll=4)
            for g in range(GRP):
                bcol = sgbT_ref[:, g:g + 1]
                gc = slice(g * BLK, (g + 1) * BLK)
                for n in range(tm // BLK):
                    rr = slice(n * BLK, (n + 1) * BLK)
                    t_s[rr, gc] = jnp.dot(wm_s[g], vn_s[rr, gc], preferred_element_type=F32) + bcol

            def f2(r0):
                rows = pl.ds(r0, RC)
                brow = pl.ds(pl.multiple_of(r0 % BLK, RC), RC)
                for c0 in COLS:
                    cs = slice(c0, c0 + CW)
                    gu, ggu = _gelu_fg(pj_ref[rows, cs])
                    sz, sg = _silu_fg(pj_ref[rows, 2 * D + c0:2 * D + c0 + CW])
                    dya = dy_s[rows, cs]
                    mix = t_s[rows, cs]
                    dmix = dya * gu * sz
                    dpj_ref[0, rows, cs] = (dya * mix * sz * ggu).astype(BF)
                    dpj_ref[2, rows, cs] = (dya * gu * mix * sg).astype(BF)
                    dmix_s[rows, cs] = dmix.astype(BF)
                    accb_s[brow, cs] += dmix
            _chunks(tm, f2)
            for g in range(GRP):
                gc = slice(g * BLK, (g + 1) * BLK)
                dsbT_ref[:, g:g + 1] += _rowsum(accb_s[:, gc])
                for n in range(tm // BLK):
                    rr = slice(n * BLK, (n + 1) * BLK)
                    t_s[rr, gc] = jnp.dot(wmT_s[g], dmix_s[rr, gc], preferred_element_type=F32)
                    dsw_ref[g] += _mm_nt(dmix_s[rr, gc], vn_s[rr, gc])

            def f3(r0):
                rows = pl.ds(r0, RC)
                rstd = rstd_s[rows, 0:1]
                dxh, m1, m2 = [], 0.0, 0.0
                for k, c0 in enumerate(COLS):
                    cs = slice(c0, c0 + CW)
                    dvn = t_s[rows, cs]
                    xh = xh_s[rows, cs]
                    acc_s[0, :, cs] += dvn * xh
                    acc_s[1, :, cs] += dvn
                    dxh.append(dvn * lng_ref[:, cs])
                    m1 = m1 + _rowsum(dxh[k])
                    m2 = m2 + _rowsum(dxh[k] * xh)
                m1 = m1 * (1.0 / D)
                m2 = m2 * (1.0 / D)
                for k, c0 in enumerate(COLS):
                    cs = slice(c0, c0 + CW)
                    _, ggv = _gelu_fg(pj_ref[rows, D + c0:D + c0 + CW])
                    dpj_ref[1, rows, cs] = (rstd * (dxh[k] - m1 - xh_s[rows, cs] * m2) * ggv).astype(BF)
            _chunks(tm, f3)
            flush(2, (0, 1))

            @pl.when(i == nT - 1)
            def _():
                for g in range(GRP):
                    r = lax.broadcasted_iota(jnp.int32, (BLK, BLK), 0) // 64
                    c = lax.broadcasted_iota(jnp.int32, (BLK, BLK), 1) // 64
                    dsw_ref[g] = jnp.where(c <= r, dsw_ref[g], 0.0)

    def prev(col):
        return pl.BlockSpec((HALO, D), lambda i, p: (jnp.maximum((nT - 1 - i) * hb - 1, 0), col))

    vec = pl.BlockSpec((1, D), lambda i, p: (0, 0))
    const3 = lambda i, p: (0, 0, 0)
    return _pcall(
        body, name="bwd_mix", grid=(nT, NSH),
        in_specs=[
            pl.BlockSpec((tm, D), lambda i, p: (nT - 1 - i, 0)),
            pl.BlockSpec((3, tm, D), lambda i, p: (0, nT - 1 - i, 0)),
            pl.BlockSpec((tm, SHW), lambda i, p: (nT - 1 - i, 3 - p)),
            pl.BlockSpec((tm, D), lambda i, p: (nT - 1 - i, 5)),
            prev(3), prev(7), prev(5),
            pl.BlockSpec((NSH, D // NSH, D), _branch_spec(lambda i, p: 3 - p)),
            vec, vec,
            pl.BlockSpec((GRP, BLK, BLK), const3),
            pl.BlockSpec((BLK, GRP), lambda i, p: (0, 0)),
            pl.BlockSpec((NPG, PGW, PGW), const3),
            vec, vec,
            pl.BlockSpec((3, D), lambda i, p: (0, 0)), vec,
        ],
        out_specs=[
            pl.BlockSpec((3, tm, D), lambda i, p: (3 - p, nT - 1 - i, 0)),
            pl.BlockSpec((3, tm, D), lambda i, p: (0, nT - 1 - i, 0)),
            pl.BlockSpec((GRP, BLK, BLK), const3),
            pl.BlockSpec((BLK, GRP), lambda i, p: (0, 0)),
            pl.BlockSpec((8, D), lambda i, p: (0, 0)),
            pl.BlockSpec((NPG, PGW, PGW), const3),
        ],
        out_shape=[
            jax.ShapeDtypeStruct((12, T, D), BF), jax.ShapeDtypeStruct((3, T, D), BF),
            jax.ShapeDtypeStruct((GRP, BLK, BLK), F32), jax.ShapeDtypeStruct((BLK, GRP), F32),
            jax.ShapeDtypeStruct((8, D), F32), jax.ShapeDtypeStruct((NPG, PGW, PGW), F32),
        ],
        scratch_shapes=[
            pltpu.VMEM((tm, D), F32),
            pltpu.VMEM((tm + HALO, D), F32),
            pltpu.VMEM((tm + HALO, D), F32),
            pltpu.VMEM((2, HALO, D), F32),
            pltpu.VMEM((tm, D), F32),
            pltpu.VMEM((tm, D), BF),
            pltpu.VMEM((tm, D), F32),
            pltpu.VMEM((tm, D), BF),
            pltpu.VMEM((tm, D), BF),
            pltpu.VMEM((tm, D), BF),
            pltpu.VMEM((tm, D), F32),
            pltpu.VMEM((tm, BLK), F32),
            pltpu.VMEM((4, RC, D), F32),
            pltpu.VMEM((BLK, D), F32),
            pltpu.VMEM((GRP, BLK, BLK), BF), pltpu.VMEM((GRP, BLK, BLK), BF),
        ],
        compiler_params=_params(("arbitrary", "arbitrary"), 56),
    )(dxo, p3, proj, proj, proj, proj, proj, wl, lng, lnb, sgw, sgbT, pwb, pb, psc, cw, cb)


def _bwd_in(dproj, wl, x, dxo, ng):
    T = x.shape[0]
    tm = _tile(T, 512)

    nT = T // tm

    def body(dpj_ref, w_ref, x_ref, dxo_ref, ng_ref, dx_ref, dng_ref, cur, prev, g_s):
        i = pl.program_id(0)
        j = pl.program_id(1)

        def part():
            return sum(_mm_nt(dpj_ref[b], w_ref[b * D:(b + 1) * D, :]) for b in range(3))

        def finish_rows(r0):
            rows = pl.ds(r0, RC)
            xs = [x_ref[rows, c0:c0 + CW] for c0 in COLS]
            r = lax.rsqrt(sum(_rowsum(v * v) for v in xs) * (1.0 / D) + RMS_EPS)
            xh = [v * r for v in xs]
            dhg, m = [], 0.0
            for k, c0 in enumerate(COLS):
                cs = slice(c0, c0 + CW)
                dh = prev[rows, cs]
                g_s[:, cs] += dh * xh[k]
                dhg.append(dh * ng_ref[:, cs])
                m = m + _rowsum(dhg[k] * xh[k])
            m = m * (1.0 / D)
            for k, c0 in enumerate(COLS):
                cs = slice(c0, c0 + CW)
                dx_ref[rows, cs] = dxo_ref[rows, cs] + r * (dhg[k] - xh[k] * m)

        def finish():
            for c in range(tm // RC):
                finish_rows(c * RC)

        @pl.when((i == 0) & (j == 0))
        def _():
            g_s[...] = jnp.zeros_like(g_s)
            cur[...] = part()

        @pl.when((i > 0) & (i < nT) & (j == 0))
        def _():
            cur[...] = part()
            finish()

        @pl.when((i < nT) & (j > 0))
        def _():
            cur[...] += part()

        @pl.when((i < nT) & (j == NSH - 1))
        def _():
            prev[...] = cur[...]

        @pl.when((i == nT) & (j == 0))
        def _():
            finish()
            dng_ref[...] = jnp.sum(g_s[...], axis=0, keepdims=True)

    vec = pl.BlockSpec((1, D), lambda i, j: (0, 0))
    tile = pl.BlockSpec((tm, D), lambda i, j: (jnp.maximum(i - 1, 0), 0))
    return _pcall(
        body, name="bwd_in", grid=(nT + 1, NSH),
        in_specs=[
            pl.BlockSpec((3, tm, D), lambda i, j: (j, jnp.minimum(i, nT - 1), 0)),
            pl.BlockSpec((None, SHW, D), lambda i, j: (j, 0, 0)),
            tile, tile, vec,
        ],
        out_specs=[tile, vec],
        out_shape=[jax.ShapeDtypeStruct((T, D), F32), jax.ShapeDtypeStruct((1, D), F32)],
        scratch_shapes=[pltpu.VMEM((tm, D), F32), pltpu.VMEM((tm, D), F32), pltpu.VMEM((RC, D), F32)],
        compiler_params=_params(("arbitrary", "arbitrary"), 52),
    )(dproj, wl, x, dxo, ng)


def _tn_grad(a3, b3, split, out_map, name, into=None):
    nb, T, _ = b3.shape
    tk = _tile(T, 2048)
    nk = T // tk
    rows = D // split
    a_batched = a3.shape[0] > 1

    def body(a_ref, b_ref, *rest):
        o_ref, acc_s = rest[-2:]
        k = pl.program_id(1)

        @pl.when(k == 0)
        def _():
            acc_s[...] = _mm_tn(a_ref[...], b_ref[...])

        @pl.when((k > 0) & (k < nk))
        def _():
            acc_s[...] += _mm_tn(a_ref[...], b_ref[...])

        @pl.when(k >= nk - 1)
        def _():
            r0 = pl.multiple_of((k - (nk - 1)) * rows, rows)
            o_ref[...] = acc_s[pl.ds(r0, rows), :].astype(o_ref.dtype)

    def tok(k):
        return jnp.minimum(k, nk - 1)

    extra = [] if into is None else [into]
    return _pcall(
        body, name=name, grid=(nb, nk + split - 1),
        in_specs=[
            pl.BlockSpec((None, tk, D), (lambda n, k: (n, tok(k), 0)) if a_batched else (lambda n, k: (0, tok(k), 0))),
            pl.BlockSpec((None, tk, D), lambda n, k: (n, tok(k), 0)),
        ] + [ANY] * len(extra),
        out_specs=pl.BlockSpec((None, rows, D), lambda n, k: out_map(n, jnp.maximum(k - (nk - 1), 0))),
        out_shape=jax.ShapeDtypeStruct((NSH, PK_ROWS, D), BF),
        input_output_aliases={2: 0} if extra else {},
        scratch_shapes=[pltpu.VMEM((D, D), F32)],
        compiler_params=_params(("arbitrary", "arbitrary"), 56),
    )(a3, b3, *extra)


def _place():
    x, y, c = lax.axis_index("x"), lax.axis_index("y"), lax.axis_index("c")
    chips = [(1 - x, y), (x, 1 - y), (1 - x, 1 - y)]
    return x, y, c, chips


ANY = pl.BlockSpec(memory_space=pl.ANY)


HBM = pl.BlockSpec(memory_space=pltpu.HBM)
SEM = pl.BlockSpec(memory_space=pltpu.SEMAPHORE)
EFFECT = pltpu.SideEffectType.DATAFLOW_SIDE_EFFECTING


def _own_slot(src, from_slot, name):
    rows = src.shape[-2]
    rb = rows // 9
    me = (2 * lax.axis_index("x") + lax.axis_index("y")).astype(jnp.int32).reshape(1)

    def body(me_ref, src_ref, land_ref):
        land_ref[...] = src_ref[...]

    if from_slot:
        src_spec = pl.BlockSpec((None, rb, D), lambda i, me_ref: (me_ref[0], i, 0))
    else:
        src_spec = pl.BlockSpec((rb, D), lambda i, me_ref: (i, 0))
    return _pcall(
        body, name=name,
        grid_spec=pltpu.PrefetchScalarGridSpec(
            num_scalar_prefetch=1, grid=(rows // rb,), in_specs=[src_spec],
            out_specs=pl.BlockSpec((None, rb, D), lambda i, me_ref: (me_ref[0], i, 0))),
        out_shape=jax.ShapeDtypeStruct((NSH, rows, D), src.dtype),
        compiler_params=_params(("arbitrary",), 32),
    )(me, src)


def _push_start(name, srcs, lands, per_peer):
    n = len(srcs)

    def body(*refs):
        src, land = refs[:n], refs[n:2 * n]
        ssem, rsem = refs[2 * n:5 * n], refs[5 * n:8 * n]
        token = refs[-1]
        x, y, c, chips = _place()
        me = 2 * x + y
        for i in range(n):
            for j, (px, py) in enumerate(chips):
                pltpu.make_async_remote_copy(
                    src_ref=src[i].at[2 * px + py] if per_peer else src[i], dst_ref=land[i].at[me],
                    send_sem=ssem[3 * i + j], recv_sem=rsem[3 * i + j], device_id=(px, py, c), device_id_type=MESH).start()
        token[...] = jnp.zeros_like(token)

    ops = list(srcs) + list(lands)
    out = _pcall(
        body, name=name,
        out_shape=tuple([pltpu.SemaphoreType.DMA(())] * (6 * n) + [pltpu.HBM(a.shape, a.dtype) for a in ops]
                        + [jax.ShapeDtypeStruct((8, 128), F32)]),
        in_specs=[HBM] * (2 * n),
        out_specs=tuple([SEM] * (6 * n) + [HBM] * (2 * n) + [pl.BlockSpec(memory_space=pltpu.VMEM)]),
        input_output_aliases={i: 6 * n + i for i in range(2 * n)},
        compiler_params=pltpu.CompilerParams(has_side_effects=EFFECT),
    )(*[pltpu.with_memory_space_constraint(a, pltpu.HBM) for a in ops])
    return out[:3 * n], out[3 * n:6 * n], out[6 * n:7 * n], out[7 * n:8 * n], out[-1]


def _push_wait(name, src, land, ssem, rsem, after, per_peer):
    def body(src_ref, land_ref, s0, s1, s2, r0, r1, r2, after_ref, src_out, land_out):
        x, y, c, chips = _place()
        for (px, py), s, r in zip(chips, (s0, s1, s2), (r0, r1, r2)):
            k = 2 * px + py
            cp = pltpu.make_async_remote_copy(
                src_ref=src_ref.at[k] if per_peer else src_ref, dst_ref=land_ref.at[k], send_sem=s, recv_sem=r,
                device_id=(px, py, c), device_id_type=MESH)
            cp.wait_send()
            cp.wait_recv()

    return _pcall(
        body, name=name,
        out_shape=(pltpu.HBM(src.shape, src.dtype), pltpu.HBM(land.shape, land.dtype)),
        in_specs=[HBM, HBM] + [SEM] * 6 + [ANY], out_specs=(HBM, HBM),
        input_output_aliases={0: 0, 1: 1},
        compiler_params=pltpu.CompilerParams(has_side_effects=EFFECT),
    )(src, land, *ssem, *rsem, after)[1]


def _swap_sibling(arrs):
    n = len(arrs)

    def body(*refs):
        src, dst = refs[:n], refs[n:2 * n]
        ssem, rsem = refs[2 * n:]
        x, y, c, _ = _place()
        cps = [pltpu.make_async_remote_copy(src_ref=src[a], dst_ref=dst[a], send_sem=ssem.at[a], recv_sem=rsem.at[a],
                                            device_id=(x, y, 1 - c), device_id_type=MESH) for a in range(n)]
        for cp in cps:
            cp.start()
        for cp in cps:
            cp.wait()

    return _pcall(
        body, name="swap_sibling",
        in_specs=[ANY] * n, out_specs=[ANY] * n,
        out_shape=[jax.ShapeDtypeStruct(a.shape, a.dtype) for a in arrs],
        scratch_shapes=[pltpu.SemaphoreType.DMA((n,)), pltpu.SemaphoreType.DMA((n,))],
    )(*arrs)


def _gather_all(v):
    def body(src, dst, ssem, rsem, lsem):
        x, y, c, _ = _place()
        me = 4 * x + 2 * y + c
        peers = [(x, y, 1 - c), (1 - x, y, c), (1 - x, y, 1 - c), (x, 1 - y, c), (x, 1 - y, 1 - c),
                 (1 - x, 1 - y, c), (1 - x, 1 - y, 1 - c)]
        local = pltpu.make_async_copy(src, dst.at[me], lsem)
        local.start()
        sends = [pltpu.make_async_remote_copy(src_ref=src, dst_ref=dst.at[me], send_sem=ssem.at[j], recv_sem=rsem.at[j],
                                              device_id=pr, device_id_type=MESH) for j, pr in enumerate(peers)]
        for cp in sends:
            cp.start()
        for j, (px, py, pc) in enumerate(peers):
            pltpu.make_async_remote_copy(src_ref=src, dst_ref=dst.at[4 * px + 2 * py + pc], send_sem=ssem.at[j],
                                         recv_sem=rsem.at[j], device_id=(px, py, pc), device_id_type=MESH).wait_recv()
        for cp in sends:
            cp.wait_send()
        local.wait()

    return _pcall(
        body, name="gather_all", in_specs=[ANY], out_specs=ANY,
        out_shape=jax.ShapeDtypeStruct((8,) + v.shape, v.dtype),
        scratch_shapes=[pltpu.SemaphoreType.DMA((7,)), pltpu.SemaphoreType.DMA((7,)), pltpu.SemaphoreType.DMA(())],
    )(v)


def _sum_slots(r, rb):
    S = r.shape[0]

    def body(r_ref, o_ref):
        acc = r_ref[0].astype(F32)
        for s in range(1, S):
            acc = acc + r_ref[s].astype(F32)
        o_ref[...] = acc

    if r.ndim == 3:
        _, R, C = r.shape
        grid, blk, imap = (R // rb,), (S, rb, C), (lambda i: (0, i, 0))
        oblk, omap = (rb, C), (lambda i: (i, 0))
    else:
        _, K, R, C = r.shape
        grid, blk, imap = (K,), (S, None, R, C), (lambda i: (0, i, 0, 0))
        oblk, omap = (None, R, C), (lambda i: (i, 0, 0))
    return _pcall(
        body, name="sum_slots", grid=grid, in_specs=[pl.BlockSpec(blk, imap)], out_specs=pl.BlockSpec(oblk, omap),
        out_shape=jax.ShapeDtypeStruct(r.shape[1:], F32), compiler_params=_params(("arbitrary",), 48),
    )(r)


def _adamw(gs, g_spec, w, m, v, p_spec, prev, grid):
    ng = len(gs)
    bc1 = 1.0 - ADAM_B1 ** ADAM_STEP
    bc2 = 1.0 - ADAM_B2 ** ADAM_STEP

    def body(*refs):
        g = refs[0][...]
        for a in range(1, ng):
            g = g + refs[a][...]
        w_ref, m_ref, v_ref = refs[ng:ng + 3]
        go, do, mo, vo = refs[ng + 3 + 4:]
        mn = ADAM_B1 * m_ref[...] + (1.0 - ADAM_B1) * g
        vn = ADAM_B2 * v_ref[...] + (1.0 - ADAM_B2) * (g * g)
        go[...] = g
        mo[...] = mn
        vo[...] = vn
        do[...] = -ADAM_LR * ((mn / bc1) / (jnp.sqrt(vn / bc2) + ADAM_EPS) + ADAM_WD * w_ref[...])

    out = jax.ShapeDtypeStruct(w.shape, F32)
    k0 = ng + 3
    return _pcall(
        body, name="adamw", grid=grid,
        in_specs=[g_spec] * ng + [p_spec] * 3 + [ANY] * 4,
        out_specs=[p_spec] * 4, out_shape=[out] * 4,
        input_output_aliases={k0: 0, k0 + 1: 1, k0 + 2: 2, k0 + 3: 3},
        compiler_params=_params(("arbitrary",) * len(grid), 48),
    )(*gs, w, m, v, *prev)


def _empty4(w):
    return tuple(lax.empty(w.shape, F32) for _ in range(4))


N_SGW = L * GRP * BLK * BLK // D
O_NG, O_VEC, O_SGB, O_FG, O_SGW = 0, 8, 32, 40, 48
O_CW = O_SGW + N_SGW
N_PACK = O_CW + 16
PACK_RB = N_PACK // 3


def _pad_to(a, rows):
    return jnp.pad(a, ((0, rows - a.shape[0]), (0, 0)))


def _pack_small(ng, vecs, sgb, fg, sgw, cw):
    parts = [_pad_to(ng, 8), _pad_to(vecs.reshape(L * 5, D), 24), _pad_to(sgb.reshape(L, D), 8),
             _pad_to(fg.reshape(1, D), 8), sgw.reshape(N_SGW, D), _pad_to(cw, 16)]
    return jnp.concatenate(parts, axis=0)


def kernel(x, norm_g, w_in, sgu_ln_g, sgu_ln_b, sgu_w, sgu_b, pool_w, pool_b, pool_scale, conv_w, conv_b, w_branch_a, w_branch_b, w_branch_c, w_out, final_g, loss_target, m_norm_g, m_w_in, m_sgu_ln_g, m_sgu_ln_b, m_sgu_w, m_sgu_b, m_pool_w, m_pool_b, m_pool_scale, m_conv_w, m_conv_b, m_w_branch_a, m_w_branch_b, m_w_branch_c, m_w_out, m_final_g, v_norm_g, v_w_in, v_sgu_ln_g, v_sgu_ln_b, v_sgu_w, v_sgu_b, v_pool_w, v_pool_b, v_pool_scale, v_conv_w, v_conv_b, v_w_branch_a, v_w_branch_b, v_w_branch_c, v_w_out, v_final_g):
    cx, cy = lax.axis_index("x"), lax.axis_index("y")
    me = 2 * cx + cy
    xl, tgt = x[0], loss_target[0]
    q = D // NSH

    wq = w_in.astype(BF).reshape(L, D, 3, D).transpose(0, 2, 1, 3).reshape(L, SHW, D)
    brq = jnp.stack([w_branch_a, w_branch_b, w_branch_c, w_out], axis=1).astype(BF).reshape(L, D, D)
    pwq = pool_w.astype(BF).reshape(L, PK_CW - PK_PW, D)
    cwq = lax.bitcast_convert_type(conv_w, BF).reshape(L, 3 * q * 2)
    cwq = jnp.pad(cwq, ((0, 0), (0, 16 * D - 3 * q * 2))).reshape(L, 16, D)
    packs = [jnp.concatenate([wq[l], brq[l], pwq[l], cwq[l]], axis=0) for l in range(L)]
    lands = [_own_slot(packs[l], False, f"ag_own_{l}") for l in range(L)]
    ag_s, ag_r, packs, lands, tok = _push_start("ag_start", packs, lands, False)
    sgbT = sgu_b.transpose(0, 2, 1)

    def layer_weights(l, after):
        wl = _push_wait(f"ag_wait_{l}", packs[l], lands[l], ag_s[3 * l:3 * l + 3], ag_r[3 * l:3 * l + 3], after, False)
        pwb = wl[:, PK_PW:PK_CW].reshape(NSH, NPG, PGW // NSH, PGW).transpose(1, 0, 2, 3).reshape(NPG, PGW, PGW)
        cwb = wl[:, PK_CW:].reshape(NSH, 16 * D)[:, :3 * q * 2].reshape(NSH, 3, q, 2)
        cwf = lax.bitcast_convert_type(cwb, F32).transpose(1, 0, 2).reshape(3, D)
        small = (sgu_ln_g[l:l + 1], sgu_ln_b[l:l + 1], sgu_w[l], sgbT[l], pwb, pool_b[l:l + 1], pool_scale[l:l + 1],
                 cwf, conv_b[l:l + 1])
        return wl, small

    xs, saved, wts = [xl], [], []
    for l in range(L):
        wl, small = layer_weights(l, tok if l == 0 else xs[l])
        wts.append((wl, small))
        proj, h, y3 = _fwd_in(xs[l], norm_g[l:l + 1], wl, *small)
        p3, mg, xo = _fwd_out(y3, proj, xs[l], wl)
        saved.append((proj, h, y3, p3, mg))
        xs.append(xo)

    dx, sq, dfg = _loss_head(xs[L], final_g[None], tgt)
    loss = lax.psum(jnp.sum(sq) * (0.5 / D), ("x", "y", "c"))

    g_w_in = _empty4(w_in)
    g_br = [_empty4(w_out) for _ in range(4)]
    g_pw = _empty4(pool_w)
    dng, dvec, dsgw, dsgb = [None] * L, [None] * L, [None] * L, [None] * L
    branches = [(w_branch_a, m_w_branch_a, v_w_branch_a), (w_branch_b, m_w_branch_b, v_w_branch_b),
                (w_branch_c, m_w_branch_c, v_w_branch_c), (w_out, m_w_out, v_w_out)]
    nb = D // 128

    def finish(l, landed):
        nonlocal g_w_in, g_pw
        mine = _sum_slots(landed, PK_ROWS // 9)
        sums = [mine, _swap_sibling([mine])[0]]
        g_w_in = _adamw(sums, pl.BlockSpec((128, D), lambda b, i: (b * nb + i, 0)), w_in, m_w_in, v_w_in,
                        pl.BlockSpec((None, 128, D), lambda b, i: (l, i, b)), g_w_in, (3, nb))
        for k, (w, m, v) in enumerate(branches):
            g_br[k] = _adamw(sums, pl.BlockSpec((q, D), lambda i, k=k: (PK_BR // q + k, 0)), w, m, v,
                             pl.BlockSpec((None, q, D), lambda i: (l, 0, 0)), g_br[k], (1,))
        pools = [a[PK_PW:PK_CW].reshape(NPG, PGW // NSH, PGW) for a in sums]
        g_pw = _adamw(pools, pl.BlockSpec((None, PGW // NSH, PGW), lambda g: (g, 0, 0)), pool_w, m_pool_w, v_pool_w,
                      pl.BlockSpec((None, None, PGW // NSH, PGW), lambda g: (l, g, 0, 0)), g_pw, (NPG,))

    pend = None
    for l in reversed(range(L)):
        proj, h, y3, p3, mg = saved[l]
        wl, small = wts[l]
        dproj, dp3, dsgw[l], dsbT, dvec[l], dpw = _bwd_mix(dx, p3, proj, wl, *small)
        dsgb[l] = dsbT.T
        grads = _tn_grad(h[None], dproj, 1, lambda n, s: (n // 3, n % 3, 0), "grad_w_in")
        grads = _tn_grad(y3, dp3, NSH, lambda n, s: (s, PK_BR // q + n, 0), "grad_w_branch", into=grads)
        grads = _tn_grad(mg[None], dx[None], NSH, lambda n, s: (s, PK_BR // q + 3, 0), "grad_w_out", into=grads)
        dpq = dpw.astype(BF).reshape(NPG, NSH, PGW // NSH, PGW).transpose(1, 0, 2, 3).reshape(NSH, PK_CW - PK_PW, D)
        grads = lax.dynamic_update_slice(grads, jnp.pad(dpq, ((0, 0), (0, PK_ROWS - PK_CW), (0, 0))), (0, PK_PW, 0))
        if pend is not None:
            landed = _push_wait(f"rs_wait_{pend[0]}", *pend[1:], dproj, True)
        land = _own_slot(grads, True, f"rs_own_{l}")
        ss, rs, (grads,), (land,), tok = _push_start(f"rs_start_{l}", [grads], [land], True)
        if pend is not None:
            finish(pend[0], landed)
        dx, dng[l] = _bwd_in(dproj, wl, xs[l], dx, norm_g[l:l + 1] + tok[0, 0])
        pend = (l, grads, land, ss, rs)
    finish(pend[0], _push_wait(f"rs_wait_{pend[0]}", *pend[1:], dx, True))

    dvec = jnp.stack(dvec)
    small = _pack_small(jnp.concatenate(dng), dvec[:, 0:5], jnp.stack(dsgb), dfg[0], jnp.stack(dsgw),
                        dvec[:, 5:8].reshape(L * 3, D))
    gsmall = _sum_slots(_gather_all(small), PACK_RB)
    gcw = lax.dynamic_slice_in_dim(gsmall[O_CW:O_CW + L * 3], me * q, q, axis=1)
    gpack = jnp.concatenate([gsmall[:O_CW], _pad_to(gcw.reshape(L * 3 * q // D, D), 16)])

    def pack(ng, lg, lb, sw, sb, pb_, ps, cwv, cb_, fg):
        return _pack_small(ng, jnp.stack([lg, lb, pb_, ps, cb_], axis=1), sb, fg, sw, cwv.reshape(L * 3 * q // D, D))

    wp = pack(norm_g, sgu_ln_g, sgu_ln_b, sgu_w, sgu_b, pool_b, pool_scale, conv_w, conv_b, final_g)
    mp = pack(m_norm_g, m_sgu_ln_g, m_sgu_ln_b, m_sgu_w, m_sgu_b, m_pool_b, m_pool_scale, m_conv_w, m_conv_b, m_final_g)
    vp = pack(v_norm_g, v_sgu_ln_g, v_sgu_ln_b, v_sgu_w, v_sgu_b, v_pool_b, v_pool_scale, v_conv_w, v_conv_b, v_final_g)
    rows = pl.BlockSpec((PACK_RB, D), lambda i: (i, 0))
    sm = _adamw([gpack], rows, wp, mp, vp, rows, _empty4(wp), (N_PACK // PACK_RB,))

    def unpack(a):
        vv = a[O_VEC:O_VEC + L * 5].reshape(L, 5, D)
        sb = a[O_SGB:O_SGB + L].reshape(L, GRP, BLK)
        fg = a[O_FG]
        sw = a[O_SGW:O_SGW + N_SGW].reshape(L, GRP, BLK, BLK)
        cwv = a[O_CW:O_CW + L * 3 * q // D].reshape(L, 3, q)
        return dict(norm_g=a[O_NG:O_NG + L], w_in=None, sgu_ln_g=vv[:, 0], sgu_ln_b=vv[:, 1], sgu_w=sw, sgu_b=sb, pool_w=None,
                    pool_b=vv[:, 2], pool_scale=vv[:, 3], conv_w=cwv, conv_b=vv[:, 4], w_branch_a=None,
                    w_branch_b=None, w_branch_c=None, w_out=None, final_g=fg)

    outs = [loss, dx[None]]
    for kind in range(4):
        d = unpack(sm[kind])
        d.update(w_in=g_w_in[kind], pool_w=g_pw[kind], w_branch_a=g_br[0][kind], w_branch_b=g_br[1][kind],
                 w_branch_c=g_br[2][kind], w_out=g_br[3][kind])
        outs.extend(d[n] for n in ("norm_g", "w_in", "sgu_ln_g", "sgu_ln_b", "sgu_w", "sgu_b", "pool_w", "pool_b",
                                   "pool_scale", "conv_w", "conv_b", "w_branch_a", "w_branch_b", "w_branch_c", "w_out",
                                   "final_g"))
    return tuple(outs)
```

```python
import functools

import jax
import jax.numpy as jnp
from jax import lax
from jax.experimental import pallas as pl
from jax.experimental.pallas import tpu as pltpu

F32 = jnp.float32
BF = jnp.bfloat16
MESH = pl.DeviceIdType.MESH

D = 1024
L = 4
NSH = 4
SHW = 3 * D
NIN = NSH * SHW
GRP = 8
BLK = 128
NPG = 4
PGW = D // NPG
POOL_WINDOWS = (2, 4, 8, 16)
HALO = 16
RMS_EPS = 1e-6
LN_EPS = 1e-5
ADAM_LR, ADAM_B1, ADAM_B2, ADAM_EPS, ADAM_WD, ADAM_STEP = 0.001, 0.9, 0.999, 1e-8, 0.01, 10

RC = 16
CW = 512
COLS = tuple(range(0, D, CW))
MIB = 1 << 20

PK_BR = SHW
PK_PW = PK_BR + D
PK_CW = PK_PW + NPG * (PGW // NSH) * PGW // D
PK_ROWS = PK_CW + 16

_C0 = 0.7978845608028654
_C1 = 0.044715


def _pcall(body, **kw):
    return pl.pallas_call(body, **kw)


def _params(sem, vmem_mib):
    return pltpu.CompilerParams(dimension_semantics=sem, vmem_limit_bytes=vmem_mib * MIB)


def _mm(a, b):
    return jnp.dot(a.astype(BF), b.astype(BF), preferred_element_type=F32)


def _mm_nt(a, b):
    return lax.dot_general(a.astype(BF), b.astype(BF), (((1,), (1,)), ((), ())), preferred_element_type=F32)


def _mm_tn(a, b):
    return lax.dot_general(a.astype(BF), b.astype(BF), (((0,), (0,)), ((), ())), preferred_element_type=F32)


def _gelu(x):
    return 0.5 * x * (1.0 + jnp.tanh(_C0 * x * (1.0 + _C1 * x * x)))


def _gelu_fg(x):
    x2 = x * x
    t = jnp.tanh(_C0 * x * (1.0 + _C1 * x2))
    f = 0.5 * x * (1.0 + t)
    g = 0.5 * (1.0 + t) + 0.5 * x * (1.0 - t * t) * (_C0 * (1.0 + 3.0 * _C1 * x2))
    return f, g


def _sigmoid(x):
    return 0.5 * jnp.tanh(0.5 * x) + 0.5


def _silu(x):
    return x * _sigmoid(x)


def _silu_fg(x):
    s = _sigmoid(x)
    return x * s, s * (1.0 + x * (1.0 - s))


def _rowsum(v):
    return jnp.sum(v, axis=1, keepdims=True)


def _chunks(n_rows, fn, unroll=2):
    def body(c, carry):
        fn(pl.multiple_of(c * RC, RC))
        return carry
    lax.fori_loop(0, n_rows // RC, body, 0, unroll=unroll)


def _trail(e, win):
    s, sh = e, 1
    while sh < win:
        s = s + pltpu.roll(s, sh, 0)
        sh *= 2
    return s


def _lead(e, win):
    n = e.shape[0]
    s, sh = e, 1
    while sh < win:
        s = s + pltpu.roll(s, n - sh, 0)
        sh *= 2
    return s


def _inv_count(pos0, win):
    pos = pos0 + lax.broadcasted_iota(jnp.int32, (RC, 1), 0)
    return 1.0 / jnp.minimum(pos + 1, win).astype(F32)


def _masked_sgu(sgw_ref, g):
    r = lax.broadcasted_iota(jnp.int32, (BLK, BLK), 0) // 64
    c = lax.broadcasted_iota(jnp.int32, (BLK, BLK), 1) // 64
    return jnp.where(c <= r, sgw_ref[g], 0.0)


def _tile(t, want):
    return min(t, want)


def _fwd_in(x, ng, wl, lng, lnb, sgw, sgbT, pwb, pb, psc, cw, cb):
    T = x.shape[0]
    tm = _tile(T, 256)
    nT = T // tm

    def body(x_ref, ng_ref, w_hbm, lng_ref, lnb_ref, sgw_ref, sgbT_ref, pw_ref, pb_ref, psc_ref, cw_ref, cb_ref,
             proj_hbm, h_ref, y_ref, w_s, pja, pjb, pjc, pjd, h_s, vn_s, mix_s, d_s, wm_s, extp_s, extc_s, wsem, psem):
        i = pl.program_id(0)
        pj = (pja, pjb, pjc, pjd)

        def w_copy(p):
            return pltpu.make_async_copy(w_hbm.at[p, pl.ds(0, SHW), :], w_s.at[p], wsem.at[p])

        def p_copy(p):
            dst = proj_hbm.at[pl.ds(pl.multiple_of(i * tm, tm), tm), pl.ds(p * SHW, SHW)]
            return pltpu.make_async_copy(pj[p], dst, psem.at[p])

        def all_rows(fn):
            for c in range(tm // RC):
                fn(c * RC)

        def buffer_free(p):
            @pl.when(i == 0)
            def _():
                w_copy(p).wait()

            @pl.when(i > 0)
            def _():
                p_copy(p).wait()

        def project(p, blocks):
            for b in blocks:
                pj[p][:, b * D:(b + 1) * D] = jnp.dot(h_s[...], w_s[p, b * D:(b + 1) * D, :],
                                                      preferred_element_type=F32)

        @pl.when(i == 0)
        def _():
            for p in range(NSH):
                w_copy(p).start()
            extp_s[0:HALO, :] = jnp.zeros((HALO, D), F32)
            extc_s[0:HALO, :] = jnp.zeros((HALO, D), F32)
            for g in range(GRP):
                wm_s[g] = _masked_sgu(sgw_ref, g).astype(BF)

        def norm_rows(r0):
            rows = pl.ds(r0, RC)
            xs = [x_ref[rows, c0:c0 + CW] for c0 in COLS]
            ms = sum(_rowsum(v * v) for v in xs) * (1.0 / D)
            r = lax.rsqrt(ms + RMS_EPS)
            for k, c0 in enumerate(COLS):
                hb = (xs[k] * r * ng_ref[:, c0:c0 + CW]).astype(BF)
                h_s[rows, c0:c0 + CW] = hb
                h_ref[rows, c0:c0 + CW] = hb

        def gating(first, rest):
            proj_ref = pj[0]
            first()

            def f1(r0):
                rows = pl.ds(r0, RC)
                gv = [_gelu(proj_ref[rows, D + c0:D + c0 + CW]) for c0 in COLS]
                mu = sum(_rowsum(v) for v in gv) * (1.0 / D)
                dv = [v - mu for v in gv]
                var = sum(_rowsum(v * v) for v in dv) * (1.0 / D)
                rstd = lax.rsqrt(var + LN_EPS)
                for k, c0 in enumerate(COLS):
                    vn_s[rows, c0:c0 + CW] = (dv[k] * rstd * lng_ref[:, c0:c0 + CW] + lnb_ref[:, c0:c0 + CW]).astype(BF)
            all_rows(f1)
            for g in range(GRP):
                w = wm_s[g]
                bcol = sgbT_ref[:, g:g + 1]
                gc = slice(g * BLK, (g + 1) * BLK)
                for n in range(tm // BLK):
                    rr = slice(n * BLK, (n + 1) * BLK)
                    mix_s[rr, gc] = jnp.dot(w, vn_s[rr, gc], preferred_element_type=F32) + bcol
            rest()

            def f2(r0):
                rows = pl.ds(r0, RC)
                for c0 in COLS:
                    au = proj_ref[rows, c0:c0 + CW]
                    az = proj_ref[rows, 2 * D + c0:2 * D + c0 + CW]
                    y_ref[0, rows, c0:c0 + CW] = (_gelu(au) * mix_s[rows, c0:c0 + CW] * _silu(az)).astype(BF)
            all_rows(f2)

        def pooling(first, rest):
            proj_ref = pj[1]
            first()
            extp_s[HALO:HALO + tm, :] = proj_ref[:, 0:D]

            def f1(r0):
                rows = pl.ds(r0, RC)
                e = extp_s[pl.ds(r0, 2 * RC), :]
                for k, win in enumerate(POOL_WINDOWS):
                    gc = slice(k * PGW, (k + 1) * PGW)
                    s = _trail(e[:, gc], win)[RC:2 * RC]
                    d_s[rows, gc] = (s * _inv_count(i * tm + r0, win) - e[RC:2 * RC, gc]).astype(BF)
            all_rows(f1)
            extp_s[0:HALO, :] = extp_s[tm:tm + HALO, :]
            for k in range(NPG):
                gc = slice(k * PGW, (k + 1) * PGW)
                mix_s[:, gc] = jnp.dot(d_s[:, gc], pw_ref[k], preferred_element_type=F32)
            rest()

            def f2(r0):
                rows = pl.ds(r0, RC)
                for c0 in COLS:
                    cs = slice(c0, c0 + CW)
                    bz = proj_ref[rows, D + c0:D + c0 + CW]
                    y_ref[1, rows, cs] = ((mix_s[rows, cs] + pb_ref[:, cs]) * psc_ref[:, cs] * _silu(bz)).astype(BF)
            all_rows(f2)

        def convolution(first, rest):
            proj_ref = pj[2]
            first()

            def f1(r0):
                rows = pl.ds(r0, RC)
                extc_s[pl.ds(HALO + r0, RC), :] = proj_ref[rows, D:2 * D] * pj[1][rows, 2 * D:3 * D]
            all_rows(f1)
            rest()

            def f2(r0):
                rows = pl.ds(r0, RC)
                for c0 in COLS:
                    cs = slice(c0, c0 + CW)
                    e = extc_s[pl.ds(r0, 2 * RC), cs]
                    conv = (cw_ref[2:3, cs] * e + cw_ref[1:2, cs] * pltpu.roll(e, 1, 0)
                            + cw_ref[0:1, cs] * pltpu.roll(e, 2, 0))[RC:2 * RC] + cb_ref[:, cs]
                    cbv = proj_ref[rows, cs]
                    cz = proj_ref[rows, 2 * D + c0:2 * D + c0 + CW]
                    y_ref[2, rows, cs] = (cbv * conv * _silu(cz)).astype(BF)
            all_rows(f2)
            extc_s[0:HALO, :] = extc_s[tm:tm + HALO, :]

        all_rows(norm_rows)
        buffer_free(0)
        project(0, (0, 1, 2))
        p_copy(0).start()
        for p, mixer, early in ((1, gating, 2), (2, pooling, 1), (3, convolution, 1)):
            buffer_free(p)
            mixer(functools.partial(project, p, range(early)), functools.partial(project, p, range(early, 3)))
            p_copy(p).start()

        @pl.when(i == nT - 1)
        def _():
            for p in range(NSH):
                p_copy(p).wait()

    vec = pl.BlockSpec((1, D), lambda i: (0, 0))
    return _pcall(
        body, name="fwd_in", grid=(nT,),
        in_specs=[
            pl.BlockSpec((tm, D), lambda i: (i, 0)), vec, ANY, vec, vec,
            pl.BlockSpec((GRP, BLK, BLK), lambda i: (0, 0, 0)),
            pl.BlockSpec((BLK, GRP), lambda i: (0, 0)),
            pl.BlockSpec((NPG, PGW, PGW), lambda i: (0, 0, 0)),
            vec, vec,
            pl.BlockSpec((3, D), lambda i: (0, 0)), vec,
        ],
        out_specs=[ANY, pl.BlockSpec((tm, D), lambda i: (i, 0)), pl.BlockSpec((3, tm, D), lambda i: (0, i, 0))],
        out_shape=[jax.ShapeDtypeStruct((T, NIN), F32), jax.ShapeDtypeStruct((T, D), BF),
                   jax.ShapeDtypeStruct((3, T, D), BF)],
        scratch_shapes=[
            pltpu.VMEM((NSH, SHW, D), BF),
            pltpu.VMEM((tm, SHW), F32), pltpu.VMEM((tm, SHW), F32), pltpu.VMEM((tm, SHW), F32), pltpu.VMEM((tm, SHW), F32),
            pltpu.VMEM((tm, D), BF), pltpu.VMEM((tm, D), BF), pltpu.VMEM((tm, D), F32),
            pltpu.VMEM((tm, D), BF), pltpu.VMEM((GRP, BLK, BLK), BF),
            pltpu.VMEM((tm + HALO, D), F32), pltpu.VMEM((tm + HALO, D), F32),
            pltpu.SemaphoreType.DMA((NSH,)), pltpu.SemaphoreType.DMA((NSH,)),
        ],
        compiler_params=_params(("arbitrary",), 56),
    )(x, ng, wl, lng, lnb, sgw, sgbT, pwb, pb, psc, cw, cb)


def _branch_spec(which):
    q = D // NSH
    return lambda *g: (0, PK_BR // q + (which(*g) if callable(which) else which), 0)


def _fwd_out(y3, proj, x, wl):
    T = x.shape[0]
    tm = _tile(T, 256)
    q = D // NSH

    nT = T // tm

    def body(y_ref, gl_ref, x_ref, wa_ref, wb_ref, wc_ref, wo_ref, p_ref, mg_ref, xo_ref, even_s, odd_s):
        i = pl.program_id(0)

        def branches(buf):
            for k, w_ref in enumerate((wa_ref, wb_ref, wc_ref)):
                buf[k] = jnp.dot(y_ref[k], w_ref[...].reshape(D, D), preferred_element_type=F32)

        def merge(buf):
            for c in range(tm // RC):
                rows = pl.ds(c * RC, RC)
                for c0 in COLS:
                    cs = slice(c0, c0 + CW)
                    m = 0.0
                    for k in range(3):
                        pk = buf[k, rows, cs]
                        p_ref[k, rows, cs] = pk
                        m = m + _sigmoid(gl_ref[rows, k * D + c0:k * D + c0 + CW]) * pk
                    mg_ref[rows, cs] = m.astype(BF)
            xo_ref[...] = x_ref[...] + jnp.dot(mg_ref[...], wo_ref[...].reshape(D, D), preferred_element_type=F32)

        bufs = (even_s, odd_s)

        @pl.when(i == 0)
        def _():
            branches(bufs[0])

        for par in (0, 1):
            @pl.when((i % 2 == par) & (i > 0) & (i < nT))
            def _():
                branches(bufs[par])
                merge(bufs[1 - par])

        @pl.when(i == nT)
        def _():
            merge(bufs[(nT - 1) % 2])

    prev = lambda i: jnp.maximum(i - 1, 0)
    return _pcall(
        body, name="fwd_out", grid=(nT + 1,),
        in_specs=[
            pl.BlockSpec((3, tm, D), lambda i: (0, jnp.minimum(i, nT - 1), 0)),
            pl.BlockSpec((tm, SHW), lambda i: (prev(i), 3)),
            pl.BlockSpec((tm, D), lambda i: (prev(i), 0)),
        ] + [pl.BlockSpec((NSH, q, D), _branch_spec(k)) for k in range(4)],
        out_specs=[
            pl.BlockSpec((3, tm, D), lambda i: (0, prev(i), 0)),
            pl.BlockSpec((tm, D), lambda i: (prev(i), 0)),
            pl.BlockSpec((tm, D), lambda i: (prev(i), 0)),
        ],
        out_shape=[jax.ShapeDtypeStruct((3, T, D), F32), jax.ShapeDtypeStruct((T, D), BF),
                   jax.ShapeDtypeStruct((T, D), F32)],
        scratch_shapes=[pltpu.VMEM((3, tm, D), F32), pltpu.VMEM((3, tm, D), F32)],
        compiler_params=_params(("arbitrary",), 52),
    )(y3, proj, x, wl, wl, wl, wl)


def _loss_head(x, fg, tgt):
    T = x.shape[0]
    tm = _tile(T, 512)

    def body(x_ref, g_ref, t_ref, dx_ref, sq_ref, dg_ref, acc_s):
        i = pl.program_id(0)

        @pl.when(i == 0)
        def _():
            acc_s[...] = jnp.zeros_like(acc_s)

        def f(r0):
            rows = pl.ds(r0, RC)
            xs = [x_ref[rows, c0:c0 + CW] for c0 in COLS]
            r = lax.rsqrt(sum(_rowsum(v * v) for v in xs) * (1.0 / D) + RMS_EPS)
            xh = [v * r for v in xs]
            dyg, m = [], 0.0
            for k, c0 in enumerate(COLS):
                cs = slice(c0, c0 + CW)
                err = xh[k] * g_ref[:, cs] - t_ref[rows, cs]
                acc_s[0, :, cs] += err * err
                dy = err * (1.0 / D)
                acc_s[1, :, cs] += dy * xh[k]
                dyg.append(dy * g_ref[:, cs])
                m = m + _rowsum(dyg[k] * xh[k])
            m = m * (1.0 / D)
            for k, c0 in enumerate(COLS):
                dx_ref[rows, c0:c0 + CW] = r * (dyg[k] - xh[k] * m)
        _chunks(tm, f)

        @pl.when(i == pl.num_programs(0) - 1)
        def _():
            sq_ref[...] = jnp.sum(acc_s[0], axis=0, keepdims=True)
            dg_ref[...] = jnp.sum(acc_s[1], axis=0, keepdims=True)

    vec = pl.BlockSpec((1, D), lambda i: (0, 0))
    tile = pl.BlockSpec((tm, D), lambda i: (i, 0))
    return _pcall(
        body, name="loss_head", grid=(T // tm,),
        in_specs=[tile, vec, tile], out_specs=[tile, vec, vec],
        out_shape=[jax.ShapeDtypeStruct((T, D), F32), jax.ShapeDtypeStruct((1, D), F32), jax.ShapeDtypeStruct((1, D), F32)],
        scratch_shapes=[pltpu.VMEM((2, RC, D), F32)],
        compiler_params=_params(("arbitrary",), 32),
    )(x, fg, tgt)


def _bwd_mix(dxo, p3, proj, wl, lng, lnb, sgw, sgbT, pwb, pb, psc, cw, cb):
    T = dxo.shape[0]
    tm = _tile(T, 256)
    nT = T // tm
    hb = tm // HALO

    def body(dxo_ref, p_ref, pj_ref, ch_ref, bpp_ref, ccp_ref, chp_ref, w_ref, lng_ref, lnb_ref, sgw_ref, sgbT_ref,
             pw_ref, pb_ref, psc_ref, cw_ref, cb_ref,
             dpj_ref, dp_ref, dsw_ref, dsbT_ref, vec_ref, dpw_ref,
             dy_s, ext_s, nxt_s, halo_s, dch_s, d_s, t_s, dy0_s, vn_s, dmix_s, xh_s, rstd_s, acc_s, accb_s, wm_s, wmT_s):
        i = pl.program_id(0)
        p = pl.program_id(1)
        ti = nT - 1 - i

        @pl.when((i == 0) & (p == 0))
        def _():
            dsw_ref[...] = jnp.zeros_like(dsw_ref)
            dsbT_ref[...] = jnp.zeros_like(dsbT_ref)
            vec_ref[...] = jnp.zeros_like(vec_ref)
            dpw_ref[...] = jnp.zeros_like(dpw_ref)
            halo_s[...] = jnp.zeros_like(halo_s)
            for g in range(GRP):
                wm = _masked_sgu(sgw_ref, g)
                wm_s[g] = wm.astype(BF)
                wmT_s[g] = wm.T.astype(BF)

        def flush(n_acc, rows_of):
            for a in range(n_acc):
                vec_ref[rows_of[a]:rows_of[a] + 1, :] += jnp.sum(acc_s[a], axis=0, keepdims=True)

        @pl.when(p == 0)
        def _():
            dy_s[...] = _mm_nt(dxo_ref[...], w_ref[...].reshape(D, D))

            def f(r0):
                rows = pl.ds(r0, RC)
                for c0 in COLS:
                    cs = slice(c0, c0 + CW)
                    dm = dy_s[rows, cs]
                    for k in range(3):
                        s = _sigmoid(pj_ref[rows, k * D + c0:k * D + c0 + CW])
                        dp_ref[k, rows, cs] = (s * dm).astype(BF)
                        dpj_ref[k, rows, cs] = (dm * p_ref[k, rows, cs] * s * (1.0 - s)).astype(BF)
            _chunks(tm, f)

        @pl.when(p == 1)
        def _():
            dy_s[...] = _mm_nt(dp_ref[2], w_ref[...].reshape(D, D))
            acc_s[...] = jnp.zeros_like(acc_s)
            ext_s[0:HALO, :] = jnp.where(ti > 0, ccp_ref[...] * chp_ref[...], 0.0)
            nxt_s[tm:tm + HALO, :] = halo_s[0]

            def f1(r0):
                rows = pl.ds(r0, RC)
                ext_s[pl.ds(HALO + r0, RC), :] = pj_ref[rows, D:2 * D] * ch_ref[rows, :]
            _chunks(tm, f1, unroll=4)

            def f2(r0):
                rows = pl.ds(r0, RC)
                for c0 in COLS:
                    cs = slice(c0, c0 + CW)
                    e = ext_s[pl.ds(r0, 2 * RC), cs]
                    e0, e1, e2 = e[RC:2 * RC], pltpu.roll(e, 1, 0)[RC:2 * RC], pltpu.roll(e, 2, 0)[RC:2 * RC]
                    conv = cw_ref[2:3, cs] * e0 + cw_ref[1:2, cs] * e1 + cw_ref[0:1, cs] * e2 + cb_ref[:, cs]
                    cbv = pj_ref[rows, cs]
                    sz, sg = _silu_fg(pj_ref[rows, 2 * D + c0:2 * D + c0 + CW])
                    dyc = dy_s[rows, cs]
                    dconv = dyc * cbv * sz
                    dpj_ref[0, rows, cs] = (dyc * conv * sz).astype(BF)
                    dpj_ref[2, rows, cs] = (dyc * cbv * conv * sg).astype(BF)
                    nxt_s[rows, cs] = dconv
                    acc_s[0, :, cs] += dconv
                    acc_s[1, :, cs] += dconv * e2
                    acc_s[2, :, cs] += dconv * e1
                    acc_s[3, :, cs] += dconv * e0
            _chunks(tm, f2)

            def f3(r0):
                rows = pl.ds(r0, RC)
                for c0 in COLS:
                    cs = slice(c0, c0 + CW)
                    e = nxt_s[pl.ds(r0, 2 * RC), cs]
                    dcc = (cw_ref[2:3, cs] * e + cw_ref[1:2, cs] * pltpu.roll(e, 2 * RC - 1, 0)
                           + cw_ref[0:1, cs] * pltpu.roll(e, 2 * RC - 2, 0))[0:RC]
                    dpj_ref[1, rows, cs] = (dcc * ch_ref[rows, cs]).astype(BF)
                    dch_s[rows, cs] = dcc * pj_ref[rows, D + c0:D + c0 + CW]
            _chunks(tm, f3)
            halo_s[0] = nxt_s[0:HALO, :]
            flush(4, (4, 5, 6, 7))

        @pl.when(p == 2)
        def _():
            dy_s[...] = _mm_nt(dp_ref[1], w_ref[...].reshape(D, D))
            acc_s[...] = jnp.zeros_like(acc_s)
            ext_s[0:HALO, :] = jnp.where(ti > 0, bpp_ref[...], 0.0)
            ext_s[HALO:HALO + tm, :] = pj_ref[:, 0:D]
            nxt_s[tm:tm + HALO, :] = halo_s[1]

            def f1(r0):
                rows = pl.ds(r0, RC)
                e = ext_s[pl.ds(r0, 2 * RC), :]
                for k, win in enumerate(POOL_WINDOWS):
                    gc = slice(k * PGW, (k + 1) * PGW)
                    s = _trail(e[:, gc], win)[RC:2 * RC]
                    d_s[rows, gc] = (s * _inv_count(ti * tm + r0, win) - e[RC:2 * RC, gc]).astype(BF)
            _chunks(tm, f1)
            for k in range(NPG):
                gc = slice(k * PGW, (k + 1) * PGW)
                t_s[:, gc] = jnp.dot(d_s[:, gc], pw_ref[k], preferred_element_type=F32)

            def f2(r0):
                rows = pl.ds(r0, RC)
                for c0 in COLS:
                    cs = slice(c0, c0 + CW)
                    y0 = t_s[rows, cs] + pb_ref[:, cs]
                    sz, sg = _silu_fg(pj_ref[rows, D + c0:D + c0 + CW])
                    dyb = dy_s[rows, cs]
                    dy0 = dyb * psc_ref[:, cs] * sz
                    acc_s[0, :, cs] += dy0
                    acc_s[1, :, cs] += dyb * y0 * sz
                    dy0_s[rows, cs] = dy0.astype(BF)
                    dpj_ref[1, rows, cs] = (dyb * y0 * psc_ref[:, cs] * sg).astype(BF)
            _chunks(tm, f2)
            for k in range(NPG):
                gc = slice(k * PGW, (k + 1) * PGW)
                dpw_ref[k] += _mm_tn(d_s[:, gc], dy0_s[:, gc])
                t_s[:, gc] = _mm_nt(dy0_s[:, gc], pw_ref[k])

            def f3(r0):
                rows = pl.ds(r0, RC)
                for k, win in enumerate(POOL_WINDOWS):
                    gc = slice(k * PGW, (k + 1) * PGW)
                    nxt_s[rows, gc] = t_s[rows, gc] * _inv_count(ti * tm + r0, win)
            _chunks(tm, f3, unroll=4)

            def f4(r0):
                rows = pl.ds(r0, RC)
                e = nxt_s[pl.ds(r0, 2 * RC), :]
                for k, win in enumerate(POOL_WINDOWS):
                    gc = slice(k * PGW, (k + 1) * PGW)
                    dpj_ref[0, rows, gc] = (_lead(e[:, gc], win)[0:RC] - t_s[rows, gc]).astype(BF)
                dpj_ref[2, rows, :] = dch_s[rows, :].astype(BF)
            _chunks(tm, f4)
            halo_s[1] = nxt_s[0:HALO, :]
            flush(2, (2, 3))

        @pl.when(p == 3)
        def _():
            dy_s[...] = _mm_nt(dp_ref[0], w_ref[...].reshape(D, D))
            acc_s[...] = jnp.zeros_like(acc_s)
            accb_s[...] = jnp.zeros_like(accb_s)

            def f1(r0):
                rows = pl.ds(r0, RC)
                gv = [_gelu(pj_ref[rows, D + c0:D + c0 + CW]) for c0 in COLS]
                mu = sum(_rowsum(v) for v in gv) * (1.0 / D)
                dv = [v - mu for v in gv]
                var = sum(_rowsum(v * v) for v in dv) * (1.0 / D)
                rstd = lax.rsqrt(var + LN_EPS)
                rstd_s[rows, :] = jnp.broadcast_to(rstd, (RC, BLK))
                for k, c0 in enumerate(COLS):
                    cs = slice(c0, c0 + CW)
                    xh = dv[k] * rstd
                    xh_s[rows, cs] = xh
                    vn_s[rows, cs] = (xh * lng_ref[:, cs] + lnb_ref[:, cs]).astype(BF)
            _chunks(tm, f1, unroll=4)
            for g in range(GRP):
                bcol = sgbT_ref[:, g:g + 1]
                gc = slice(g * BLK, (g + 1) * BLK)
                for n in range(tm // BLK):
                    rr = slice(n * BLK, (n + 1) * BLK)
                    t_s[rr, gc] = jnp.dot(wm_s[g], vn_s[rr, gc], preferred_element_type=F32) + bcol

            def f2(r0):
                rows = pl.ds(r0, RC)
                brow = pl.ds(pl.multiple_of(r0 % BLK, RC), RC)
                for c0 in COLS:
                    cs = slice(c0, c0 + CW)
                    gu, ggu = _gelu_fg(pj_ref[rows, cs])
                    sz, sg = _silu_fg(pj_ref[rows, 2 * D + c0:2 * D + c0 + CW])
                    dya = dy_s[rows, cs]
                    mix = t_s[rows, cs]
                    dmix = dya * gu * sz
                    dpj_ref[0, rows, cs] = (dya * mix * sz * ggu).astype(BF)
                    dpj_ref[2, rows, cs] = (dya * gu * mix * sg).astype(BF)
                    dmix_s[rows, cs] = dmix.astype(BF)
                    accb_s[brow, cs] += dmix
            _chunks(tm, f2)
            for g in range(GRP):
                gc = slice(g * BLK, (g + 1) * BLK)
                dsbT_ref[:, g:g + 1] += _rowsum(accb_s[:, gc])
                for n in range(tm // BLK):
                    rr = slice(n * BLK, (n + 1) * BLK)
                    t_s[rr, gc] = jnp.dot(wmT_s[g], dmix_s[rr, gc], preferred_element_type=F32)
                    dsw_ref[g] += _mm_nt(dmix_s[rr, gc], vn_s[rr, gc])

            def f3(r0):
                rows = pl.ds(r0, RC)
                rstd = rstd_s[rows, 0:1]
                dxh, m1, m2 = [], 0.0, 0.0
                for k, c0 in enumerate(COLS):
                    cs = slice(c0, c0 + CW)
                    dvn = t_s[rows, cs]
                    xh = xh_s[rows, cs]
                    acc_s[0, :, cs] += dvn * xh
                    acc_s[1, :, cs] += dvn
                    dxh.append(dvn * lng_ref[:, cs])
                    m1 = m1 + _rowsum(dxh[k])
                    m2 = m2 + _rowsum(dxh[k] * xh)
                m1 = m1 * (1.0 / D)
                m2 = m2 * (1.0 / D)
                for k, c0 in enumerate(COLS):
                    cs = slice(c0, c0 + CW)
                    _, ggv = _gelu_fg(pj_ref[rows, D + c0:D + c0 + CW])
                    dpj_ref[1, rows, cs] = (rstd * (dxh[k] - m1 - xh_s[rows, cs] * m2) * ggv).astype(BF)
            _chunks(tm, f3)
            flush(2, (0, 1))

            @pl.when(i == nT - 1)
            def _():
                for g in range(GRP):
                    r = lax.broadcasted_iota(jnp.int32, (BLK, BLK), 0) // 64
                    c = lax.broadcasted_iota(jnp.int32, (BLK, BLK), 1) // 64
                    dsw_ref[g] = jnp.where(c <= r, dsw_ref[g], 0.0)

    def prev(col):
        return pl.BlockSpec((HALO, D), lambda i, p: (jnp.maximum((nT - 1 - i) * hb - 1, 0), col))

    vec = pl.BlockSpec((1, D), lambda i, p: (0, 0))
    const3 = lambda i, p: (0, 0, 0)
    return _pcall(
        body, name="bwd_mix", grid=(nT, NSH),
        in_specs=[
            pl.BlockSpec((tm, D), lambda i, p: (nT - 1 - i, 0)),
            pl.BlockSpec((3, tm, D), lambda i, p: (0, nT - 1 - i, 0)),
            pl.BlockSpec((tm, SHW), lambda i, p: (nT - 1 - i, 3 - p)),
            pl.BlockSpec((tm, D), lambda i, p: (nT - 1 - i, 5)),
            prev(3), prev(7), prev(5),
            pl.BlockSpec((NSH, D // NSH, D), _branch_spec(lambda i, p: 3 - p)),
            vec, vec,
            pl.BlockSpec((GRP, BLK, BLK), const3),
            pl.BlockSpec((BLK, GRP), lambda i, p: (0, 0)),
            pl.BlockSpec((NPG, PGW, PGW), const3),
            vec, vec,
            pl.BlockSpec((3, D), lambda i, p: (0, 0)), vec,
        ],
        out_specs=[
            pl.BlockSpec((3, tm, D), lambda i, p: (3 - p, nT - 1 - i, 0)),
            pl.BlockSpec((3, tm, D), lambda i, p: (0, nT - 1 - i, 0)),
            pl.BlockSpec((GRP, BLK, BLK), const3),
            pl.BlockSpec((BLK, GRP), lambda i, p: (0, 0)),
            pl.BlockSpec((8, D), lambda i, p: (0, 0)),
            pl.BlockSpec((NPG, PGW, PGW), const3),
        ],
        out_shape=[
            jax.ShapeDtypeStruct((12, T, D), BF), jax.ShapeDtypeStruct((3, T, D), BF),
            jax.ShapeDtypeStruct((GRP, BLK, BLK), F32), jax.ShapeDtypeStruct((BLK, GRP), F32),
            jax.ShapeDtypeStruct((8, D), F32), jax.ShapeDtypeStruct((NPG, PGW, PGW), F32),
        ],
        scratch_shapes=[
            pltpu.VMEM((tm, D), F32),
            pltpu.VMEM((tm + HALO, D), F32),
            pltpu.VMEM((tm + HALO, D), F32),
            pltpu.VMEM((2, HALO, D), F32),
            pltpu.VMEM((tm, D), F32),
            pltpu.VMEM((tm, D), BF),
            pltpu.VMEM((tm, D), F32),
            pltpu.VMEM((tm, D), BF),
            pltpu.VMEM((tm, D), BF),
            pltpu.VMEM((tm, D), BF),
            pltpu.VMEM((tm, D), F32),
            pltpu.VMEM((tm, BLK), F32),
            pltpu.VMEM((4, RC, D), F32),
            pltpu.VMEM((BLK, D), F32),
            pltpu.VMEM((GRP, BLK, BLK), BF), pltpu.VMEM((GRP, BLK, BLK), BF),
        ],
        compiler_params=_params(("arbitrary", "arbitrary"), 56),
    )(dxo, p3, proj, proj, proj, proj, proj, wl, lng, lnb, sgw, sgbT, pwb, pb, psc, cw, cb)


def _bwd_in(dproj, wl, x, dxo, ng):
    T = x.shape[0]
    tm = _tile(T, 512)

    nT = T // tm

    def body(dpj_ref, w_ref, x_ref, dxo_ref, ng_ref, dx_ref, dng_ref, cur, prev, g_s):
        i = pl.program_id(0)
        j = pl.program_id(1)

        def part():
            return sum(_mm_nt(dpj_ref[b], w_ref[b * D:(b + 1) * D, :]) for b in range(3))

        def finish_rows(r0):
            rows = pl.ds(r0, RC)
            xs = [x_ref[rows, c0:c0 + CW] for c0 in COLS]
            r = lax.rsqrt(sum(_rowsum(v * v) for v in xs) * (1.0 / D) + RMS_EPS)
            xh = [v * r for v in xs]
            dhg, m = [], 0.0
            for k, c0 in enumerate(COLS):
                cs = slice(c0, c0 + CW)
                dh = prev[rows, cs]
                g_s[:, cs] += dh * xh[k]
                dhg.append(dh * ng_ref[:, cs])
                m = m + _rowsum(dhg[k] * xh[k])
            m = m * (1.0 / D)
            for k, c0 in enumerate(COLS):
                cs = slice(c0, c0 + CW)
                dx_ref[rows, cs] = dxo_ref[rows, cs] + r * (dhg[k] - xh[k] * m)

        def finish():
            for c in range(tm // RC):
                finish_rows(c * RC)

        @pl.when((i == 0) & (j == 0))
        def _():
            g_s[...] = jnp.zeros_like(g_s)
            cur[...] = part()

        @pl.when((i > 0) & (i < nT) & (j == 0))
        def _():
            cur[...] = part()
            finish()

        @pl.when((i < nT) & (j > 0))
        def _():
            cur[...] += part()

        @pl.when((i < nT) & (j == NSH - 1))
        def _():
            prev[...] = cur[...]

        @pl.when((i == nT) & (j == 0))
        def _():
            finish()
            dng_ref[...] = jnp.sum(g_s[...], axis=0, keepdims=True)

    vec = pl.BlockSpec((1, D), lambda i, j: (0, 0))
    tile = pl.BlockSpec((tm, D), lambda i, j: (jnp.maximum(i - 1, 0), 0))
    return _pcall(
        body, name="bwd_in", grid=(nT + 1, NSH),
        in_specs=[
            pl.BlockSpec((3, tm, D), lambda i, j: (j, jnp.minimum(i, nT - 1), 0)),
            pl.BlockSpec((None, SHW, D), lambda i, j: (j, 0, 0)),
            tile, tile, vec,
        ],
        out_specs=[tile, vec],
        out_shape=[jax.ShapeDtypeStruct((T, D), F32), jax.ShapeDtypeStruct((1, D), F32)],
        scratch_shapes=[pltpu.VMEM((tm, D), F32), pltpu.VMEM((tm, D), F32), pltpu.VMEM((RC, D), F32)],
        compiler_params=_params(("arbitrary", "arbitrary"), 52),
    )(dproj, wl, x, dxo, ng)


def _tn_grad(a3, b3, split, out_map, name, into=None):
    nb, T, _ = b3.shape
    tk = _tile(T, 2048)
    nk = T // tk
    rows = D // split
    a_batched = a3.shape[0] > 1

    def body(a_ref, b_ref, *rest):
        o_ref, acc_s = rest[-2:]
        k = pl.program_id(1)

        @pl.when(k == 0)
        def _():
            acc_s[...] = _mm_tn(a_ref[...], b_ref[...])

        @pl.when((k > 0) & (k < nk))
        def _():
            acc_s[...] += _mm_tn(a_ref[...], b_ref[...])

        @pl.when(k >= nk - 1)
        def _():
            r0 = pl.multiple_of((k - (nk - 1)) * rows, rows)
            o_ref[...] = acc_s[pl.ds(r0, rows), :].astype(o_ref.dtype)

    def tok(k):
        return jnp.minimum(k, nk - 1)

    extra = [] if into is None else [into]
    return _pcall(
        body, name=name, grid=(nb, nk + split - 1),
        in_specs=[
            pl.BlockSpec((None, tk, D), (lambda n, k: (n, tok(k), 0)) if a_batched else (lambda n, k: (0, tok(k), 0))),
            pl.BlockSpec((None, tk, D), lambda n, k: (n, tok(k), 0)),
        ] + [ANY] * len(extra),
        out_specs=pl.BlockSpec((None, rows, D), lambda n, k: out_map(n, jnp.maximum(k - (nk - 1), 0))),
        out_shape=jax.ShapeDtypeStruct((NSH, PK_ROWS, D), BF),
        input_output_aliases={2: 0} if extra else {},
        scratch_shapes=[pltpu.VMEM((D, D), F32)],
        compiler_params=_params(("arbitrary", "arbitrary"), 56),
    )(a3, b3, *extra)


def _place():
    x, y, c = lax.axis_index("x"), lax.axis_index("y"), lax.axis_index("c")
    chips = [(1 - x, y), (x, 1 - y), (1 - x, 1 - y)]
    return x, y, c, chips


def _peers(reach):
    x, y, c, chips = _place()
    if reach == "chips":
        return 2 * x + y, [((px, py, c), 2 * px + py) for px, py in chips]
    others = [(x, y, 1 - c)] + [(px, py, pc) for px, py in chips for pc in (c, 1 - c)]
    return 4 * x + 2 * y + c, [(pr, 4 * pr[0] + 2 * pr[1] + pr[2]) for pr in others]


ANY = pl.BlockSpec(memory_space=pl.ANY)


HBM = pl.BlockSpec(memory_space=pltpu.HBM)
SEM = pl.BlockSpec(memory_space=pltpu.SEMAPHORE)
EFFECT = pltpu.SideEffectType.DATAFLOW_SIDE_EFFECTING


def _own_slot(src, from_slot, name):
    rows = src.shape[-2]
    rb = rows // 9
    me = (2 * lax.axis_index("x") + lax.axis_index("y")).astype(jnp.int32).reshape(1)

    def body(me_ref, src_ref, land_ref):
        land_ref[...] = src_ref[...]

    if from_slot:
        src_spec = pl.BlockSpec((None, rb, D), lambda i, me_ref: (me_ref[0], i, 0))
    else:
        src_spec = pl.BlockSpec((rb, D), lambda i, me_ref: (i, 0))
    return _pcall(
        body, name=name,
        grid_spec=pltpu.PrefetchScalarGridSpec(
            num_scalar_prefetch=1, grid=(rows // rb,), in_specs=[src_spec],
            out_specs=pl.BlockSpec((None, rb, D), lambda i, me_ref: (me_ref[0], i, 0))),
        out_shape=jax.ShapeDtypeStruct((NSH, rows, D), src.dtype),
        compiler_params=_params(("arbitrary",), 32),
    )(me, src)


def _push_start(name, srcs, lands, per_peer, reach="chips"):
    n = len(srcs)
    npeer = 3 if reach == "chips" else 7
    ns = n * npeer

    def body(*refs):
        src, land = refs[:n], refs[n:2 * n]
        ssem, rsem = refs[2 * n:2 * n + ns], refs[2 * n + ns:2 * n + 2 * ns]
        token = refs[-1]
        me, peers = _peers(reach)
        for i in range(n):
            for j, (peer, slot) in enumerate(peers):
                pltpu.make_async_remote_copy(
                    src_ref=src[i].at[slot] if per_peer else src[i], dst_ref=land[i].at[me],
                    send_sem=ssem[npeer * i + j], recv_sem=rsem[npeer * i + j], device_id=peer, device_id_type=MESH).start()
        token[...] = jnp.zeros_like(token)

    ops = list(srcs) + list(lands)
    out = _pcall(
        body, name=name,
        out_shape=tuple([pltpu.SemaphoreType.DMA(())] * (2 * ns) + [pltpu.HBM(a.shape, a.dtype) for a in ops]
                        + [jax.ShapeDtypeStruct((8, 128), F32)]),
        in_specs=[HBM] * (2 * n),
        out_specs=tuple([SEM] * (2 * ns) + [HBM] * (2 * n) + [pl.BlockSpec(memory_space=pltpu.VMEM)]),
        input_output_aliases={i: 2 * ns + i for i in range(2 * n)},
        compiler_params=pltpu.CompilerParams(has_side_effects=EFFECT),
    )(*[pltpu.with_memory_space_constraint(a, pltpu.HBM) for a in ops])
    return out[:ns], out[ns:2 * ns], out[2 * ns:2 * ns + n], out[2 * ns + n:2 * ns + 2 * n], out[-1]


def _push_wait(name, src, land, ssem, rsem, after, per_peer, reach="chips"):
    npeer = len(ssem)

    def body(src_ref, land_ref, *rest):
        sems = rest[:2 * npeer]
        _, peers = _peers(reach)
        for j, (peer, slot) in enumerate(peers):
            cp = pltpu.make_async_remote_copy(
                src_ref=src_ref.at[slot] if per_peer else src_ref, dst_ref=land_ref.at[slot], send_sem=sems[j],
                recv_sem=sems[npeer + j], device_id=peer, device_id_type=MESH)
            cp.wait_send()
            cp.wait_recv()

    return _pcall(
        body, name=name,
        out_shape=(pltpu.HBM(src.shape, src.dtype), pltpu.HBM(land.shape, land.dtype)),
        in_specs=[HBM, HBM] + [SEM] * (2 * npeer) + [ANY], out_specs=(HBM, HBM),
        input_output_aliases={0: 0, 1: 1},
        compiler_params=pltpu.CompilerParams(has_side_effects=EFFECT),
    )(src, land, *ssem, *rsem, after)[1]


def _swap_sibling(arrs):
    n = len(arrs)

    def body(*refs):
        src, dst = refs[:n], refs[n:2 * n]
        ssem, rsem = refs[2 * n:]
        x, y, c, _ = _place()
        cps = [pltpu.make_async_remote_copy(src_ref=src[a], dst_ref=dst[a], send_sem=ssem.at[a], recv_sem=rsem.at[a],
                                            device_id=(x, y, 1 - c), device_id_type=MESH) for a in range(n)]
        for cp in cps:
            cp.start()
        for cp in cps:
            cp.wait()

    return _pcall(
        body, name="swap_sibling",
        in_specs=[ANY] * n, out_specs=[ANY] * n,
        out_shape=[jax.ShapeDtypeStruct(a.shape, a.dtype) for a in arrs],
        scratch_shapes=[pltpu.SemaphoreType.DMA((n,)), pltpu.SemaphoreType.DMA((n,))],
    )(*arrs)


def _gather_all(v):
    def body(src, dst, ssem, rsem, lsem):
        x, y, c, _ = _place()
        me = 4 * x + 2 * y + c
        peers = [(x, y, 1 - c), (1 - x, y, c), (1 - x, y, 1 - c), (x, 1 - y, c), (x, 1 - y, 1 - c),
                 (1 - x, 1 - y, c), (1 - x, 1 - y, 1 - c)]
        local = pltpu.make_async_copy(src, dst.at[me], lsem)
        local.start()
        sends = [pltpu.make_async_remote_copy(src_ref=src, dst_ref=dst.at[me], send_sem=ssem.at[j], recv_sem=rsem.at[j],
                                              device_id=pr, device_id_type=MESH) for j, pr in enumerate(peers)]
        for cp in sends:
            cp.start()
        for j, (px, py, pc) in enumerate(peers):
            pltpu.make_async_remote_copy(src_ref=src, dst_ref=dst.at[4 * px + 2 * py + pc], send_sem=ssem.at[j],
                                         recv_sem=rsem.at[j], device_id=(px, py, pc), device_id_type=MESH).wait_recv()
        for cp in sends:
            cp.wait_send()
        local.wait()

    return _pcall(
        body, name="gather_all", in_specs=[ANY], out_specs=ANY,
        out_shape=jax.ShapeDtypeStruct((8,) + v.shape, v.dtype),
        scratch_shapes=[pltpu.SemaphoreType.DMA((7,)), pltpu.SemaphoreType.DMA((7,)), pltpu.SemaphoreType.DMA(())],
    )(v)


def _sum_slots(r, rb):
    S = r.shape[0]

    def body(r_ref, o_ref):
        acc = r_ref[0].astype(F32)
        for s in range(1, S):
            acc = acc + r_ref[s].astype(F32)
        o_ref[...] = acc

    if r.ndim == 3:
        _, R, C = r.shape
        grid, blk, imap = (R // rb,), (S, rb, C), (lambda i: (0, i, 0))
        oblk, omap = (rb, C), (lambda i: (i, 0))
    else:
        _, K, R, C = r.shape
        grid, blk, imap = (K,), (S, None, R, C), (lambda i: (0, i, 0, 0))
        oblk, omap = (None, R, C), (lambda i: (i, 0, 0))
    return _pcall(
        body, name="sum_slots", grid=grid, in_specs=[pl.BlockSpec(blk, imap)], out_specs=pl.BlockSpec(oblk, omap),
        out_shape=jax.ShapeDtypeStruct(r.shape[1:], F32), compiler_params=_params(("arbitrary",), 48),
    )(r)


def _adamw(gs, g_spec, w, m, v, p_spec, prev, grid):
    ng = len(gs)
    bc1 = 1.0 - ADAM_B1 ** ADAM_STEP
    bc2 = 1.0 - ADAM_B2 ** ADAM_STEP

    def body(*refs):
        g = refs[0][...]
        for a in range(1, ng):
            g = g + refs[a][...]
        w_ref, m_ref, v_ref = refs[ng:ng + 3]
        go, do, mo, vo = refs[ng + 3 + 4:]
        mn = ADAM_B1 * m_ref[...] + (1.0 - ADAM_B1) * g
        vn = ADAM_B2 * v_ref[...] + (1.0 - ADAM_B2) * (g * g)
        go[...] = g
        mo[...] = mn
        vo[...] = vn
        do[...] = -ADAM_LR * ((mn / bc1) / (jnp.sqrt(vn / bc2) + ADAM_EPS) + ADAM_WD * w_ref[...])

    out = jax.ShapeDtypeStruct(w.shape, F32)
    k0 = ng + 3
    return _pcall(
        body, name="adamw", grid=grid,
        in_specs=[g_spec] * ng + [p_spec] * 3 + [ANY] * 4,
        out_specs=[p_spec] * 4, out_shape=[out] * 4,
        input_output_aliases={k0: 0, k0 + 1: 1, k0 + 2: 2, k0 + 3: 3},
        compiler_params=_params(("arbitrary",) * len(grid), 48),
    )(*gs, w, m, v, *prev)


def _empty4(w):
    return tuple(lax.empty(w.shape, F32) for _ in range(4))


N_SGW = L * GRP * BLK * BLK // D
O_NG, O_VEC, O_SGB, O_FG, O_SGW = 0, 8, 32, 40, 48
O_CW = O_SGW + N_SGW
N_PACK = O_CW + 16
PACK_RB = N_PACK // 3


def _pad_to(a, rows):
    return jnp.pad(a, ((0, rows - a.shape[0]), (0, 0)))


def _pack_small(ng, vecs, sgb, fg, sgw, cw):
    parts = [_pad_to(ng, 8), _pad_to(vecs.reshape(L * 5, D), 24), _pad_to(sgb.reshape(L, D), 8),
             _pad_to(fg.reshape(1, D), 8), sgw.reshape(N_SGW, D), _pad_to(cw, 16)]
    return jnp.concatenate(parts, axis=0)


def kernel(x, norm_g, w_in, sgu_ln_g, sgu_ln_b, sgu_w, sgu_b, pool_w, pool_b, pool_scale, conv_w, conv_b, w_branch_a, w_branch_b, w_branch_c, w_out, final_g, loss_target, m_norm_g, m_w_in, m_sgu_ln_g, m_sgu_ln_b, m_sgu_w, m_sgu_b, m_pool_w, m_pool_b, m_pool_scale, m_conv_w, m_conv_b, m_w_branch_a, m_w_branch_b, m_w_branch_c, m_w_out, m_final_g, v_norm_g, v_w_in, v_sgu_ln_g, v_sgu_ln_b, v_sgu_w, v_sgu_b, v_pool_w, v_pool_b, v_pool_scale, v_conv_w, v_conv_b, v_w_branch_a, v_w_branch_b, v_w_branch_c, v_w_out, v_final_g):
    cx, cy = lax.axis_index("x"), lax.axis_index("y")
    me = 2 * cx + cy
    xl, tgt = x[0], loss_target[0]
    q = D // NSH

    wq = w_in.astype(BF).reshape(L, D, 3, D).transpose(0, 2, 1, 3).reshape(L, SHW, D)
    brq = jnp.stack([w_branch_a, w_branch_b, w_branch_c, w_out], axis=1).astype(BF).reshape(L, D, D)
    pwq = pool_w.astype(BF).reshape(L, PK_CW - PK_PW, D)
    cwq = lax.bitcast_convert_type(conv_w, BF).reshape(L, 3 * q * 2)
    cwq = jnp.pad(cwq, ((0, 0), (0, 16 * D - 3 * q * 2))).reshape(L, 16, D)
    packs = [jnp.concatenate([wq[l], brq[l], pwq[l], cwq[l]], axis=0) for l in range(L)]
    lands = [_own_slot(packs[l], False, f"ag_own_{l}") for l in range(L)]
    ag_s, ag_r, packs, lands, tok = _push_start("ag_start", packs, lands, False)
    sgbT = sgu_b.transpose(0, 2, 1)

    def layer_weights(l, after):
        wl = _push_wait(f"ag_wait_{l}", packs[l], lands[l], ag_s[3 * l:3 * l + 3], ag_r[3 * l:3 * l + 3], after, False)
        pwb = wl[:, PK_PW:PK_CW].reshape(NSH, NPG, PGW // NSH, PGW).transpose(1, 0, 2, 3).reshape(NPG, PGW, PGW)
        cwb = wl[:, PK_CW:].reshape(NSH, 16 * D)[:, :3 * q * 2].reshape(NSH, 3, q, 2)
        cwf = lax.bitcast_convert_type(cwb, F32).transpose(1, 0, 2).reshape(3, D)
        small = (sgu_ln_g[l:l + 1], sgu_ln_b[l:l + 1], sgu_w[l], sgbT[l], pwb, pool_b[l:l + 1], pool_scale[l:l + 1],
                 cwf, conv_b[l:l + 1])
        return wl, small

    xs, saved, wts = [xl], [], []
    for l in range(L):
        wl, small = layer_weights(l, tok if l == 0 else xs[l])
        wts.append((wl, small))
        proj, h, y3 = _fwd_in(xs[l], norm_g[l:l + 1], wl, *small)
        p3, mg, xo = _fwd_out(y3, proj, xs[l], wl)
        saved.append((proj, h, y3, p3, mg))
        xs.append(xo)

    dx, sq, dfg = _loss_head(xs[L], final_g[None], tgt)
    loss = lax.psum(jnp.sum(sq) * (0.5 / D), ("x", "y", "c"))

    g_w_in = _empty4(w_in)
    g_br = [_empty4(w_out) for _ in range(4)]
    g_pw = _empty4(pool_w)
    dng, dvec, dsgw, dsgb = [None] * L, [None] * L, [None] * L, [None] * L
    branches = [(w_branch_a, m_w_branch_a, v_w_branch_a), (w_branch_b, m_w_branch_b, v_w_branch_b),
                (w_branch_c, m_w_branch_c, v_w_branch_c), (w_out, m_w_out, v_w_out)]
    nb = D // 128

    def finish(l, landed):
        nonlocal g_w_in, g_pw
        mine = _sum_slots(landed, PK_ROWS // 9)
        sums = [mine, _swap_sibling([mine])[0]]
        g_w_in = _adamw(sums, pl.BlockSpec((128, D), lambda b, i: (b * nb + i, 0)), w_in, m_w_in, v_w_in,
                        pl.BlockSpec((None, 128, D), lambda b, i: (l, i, b)), g_w_in, (3, nb))
        for k, (w, m, v) in enumerate(branches):
            g_br[k] = _adamw(sums, pl.BlockSpec((q, D), lambda i, k=k: (PK_BR // q + k, 0)), w, m, v,
                             pl.BlockSpec((None, q, D), lambda i: (l, 0, 0)), g_br[k], (1,))
        pools = [a[PK_PW:PK_CW].reshape(NPG, PGW // NSH, PGW) for a in sums]
        g_pw = _adamw(pools, pl.BlockSpec((None, PGW // NSH, PGW), lambda g: (g, 0, 0)), pool_w, m_pool_w, v_pool_w,
                      pl.BlockSpec((None, None, PGW // NSH, PGW), lambda g: (l, g, 0, 0)), g_pw, (NPG,))

    pend = None
    for l in reversed(range(L)):
        proj, h, y3, p3, mg = saved[l]
        wl, small = wts[l]
        dproj, dp3, dsgw[l], dsbT, dvec[l], dpw = _bwd_mix(dx, p3, proj, wl, *small)
        dsgb[l] = dsbT.T
        if l == 0:
            dv = jnp.stack(dvec)
            part = _pack_small(jnp.zeros((L, D), F32), dv[:, 0:5], jnp.stack(dsgb), dfg[0], jnp.stack(dsgw),
                               dv[:, 5:8].reshape(L * 3, D))
            zone = lax.dynamic_update_slice(lax.empty((8, N_PACK, D), F32), part[None],
                                            (2 * me + lax.axis_index("c"), 0, 0))
            sm_s, sm_r, (part,), (zone,), _ = _push_start("small_start", [part], [zone], False, "all")
        grads = _tn_grad(h[None], dproj, 1, lambda n, s: (n // 3, n % 3, 0), "grad_w_in")
        grads = _tn_grad(y3, dp3, NSH, lambda n, s: (s, PK_BR // q + n, 0), "grad_w_branch", into=grads)
        grads = _tn_grad(mg[None], dx[None], NSH, lambda n, s: (s, PK_BR // q + 3, 0), "grad_w_out", into=grads)
        dpq = dpw.astype(BF).reshape(NPG, NSH, PGW // NSH, PGW).transpose(1, 0, 2, 3).reshape(NSH, PK_CW - PK_PW, D)
        grads = lax.dynamic_update_slice(grads, jnp.pad(dpq, ((0, 0), (0, PK_ROWS - PK_CW), (0, 0))), (0, PK_PW, 0))
        if pend is not None:
            landed = _push_wait(f"rs_wait_{pend[0]}", *pend[1:], dproj, True)
        land = _own_slot(grads, True, f"rs_own_{l}")
        ss, rs, (grads,), (land,), tok = _push_start(f"rs_start_{l}", [grads], [land], True)
        if pend is not None:
            finish(pend[0], landed)
        dx, dng[l] = _bwd_in(dproj, wl, xs[l], dx, norm_g[l:l + 1] + tok[0, 0])
        pend = (l, grads, land, ss, rs)
    finish(pend[0], _push_wait(f"rs_wait_{pend[0]}", *pend[1:], dx, True))

    zone = _push_wait("small_wait", part, zone, sm_s, sm_r, dx, False, "all")
    gng = _sum_slots(_gather_all(_pad_to(jnp.concatenate(dng), O_VEC)), O_VEC)
    gsmall = jnp.concatenate([gng, _sum_slots(zone, PACK_RB)[O_VEC:]])
    gcw =lax.dynamic_slice_in_dim(gsmall[O_CW:O_CW + L * 3], me * q, q, axis=1)
    gpack = jnp.concatenate([gsmall[:O_CW], _pad_to(gcw.reshape(L * 3 * q // D, D), 16)])

    def pack(ng, lg, lb, sw, sb, pb_, ps, cwv, cb_, fg):
        return _pack_small(ng, jnp.stack([lg, lb, pb_, ps, cb_], axis=1), sb, fg, sw, cwv.reshape(L * 3 * q // D, D))

    wp = pack(norm_g, sgu_ln_g, sgu_ln_b, sgu_w, sgu_b, pool_b, pool_scale, conv_w, conv_b, final_g)
    mp = pack(m_norm_g, m_sgu_ln_g, m_sgu_ln_b, m_sgu_w, m_sgu_b, m_pool_b, m_pool_scale, m_conv_w, m_conv_b, m_final_g)
    vp = pack(v_norm_g, v_sgu_ln_g, v_sgu_ln_b, v_sgu_w, v_sgu_b, v_pool_b, v_pool_scale, v_conv_w, v_conv_b, v_final_g)
    rows = pl.BlockSpec((PACK_RB, D), lambda i: (i, 0))
    sm = _adamw([gpack], rows, wp, mp, vp, rows, _empty4(wp), (N_PACK // PACK_RB,))

    def unpack(a):
        vv = a[O_VEC:O_VEC + L * 5].reshape(L, 5, D)
        sb = a[O_SGB:O_SGB + L].reshape(L, GRP, BLK)
        fg = a[O_FG]
        sw = a[O_SGW:O_SGW + N_SGW].reshape(L, GRP, BLK, BLK)
        cwv = a[O_CW:O_CW + L * 3 * q // D].reshape(L, 3, q)
        return dict(norm_g=a[O_NG:O_NG + L], w_in=None, sgu_ln_g=vv[:, 0], sgu_ln_b=vv[:, 1], sgu_w=sw, sgu_b=sb, pool_w=None,
                    pool_b=vv[:, 2], pool_scale=vv[:, 3], conv_w=cwv, conv_b=vv[:, 4], w_branch_a=None,
                    w_branch_b=None, w_branch_c=None, w_out=None, final_g=fg)

    outs = [loss, dx[None]]
    for kind in range(4):
        d = unpack(sm[kind])
        d.update(w_in=g_w_in[kind], pool_w=g_pw[kind], w_branch_a=g_br[0][kind], w_branch_b=g_br[1][kind],
                 w_branch_c=g_br[2][kind], w_out=g_br[3][kind])
        outs.extend(d[n] for n in ("norm_g", "w_in", "sgu_ln_g", "sgu_ln_b", "sgu_w", "sgu_b", "pool_w", "pool_b",
                                   "pool_scale", "conv_w", "conv_b", "w_branch_a", "w_branch_b", "w_branch_c", "w_out",
                                   "final_g"))
    return tuple(outs)
```

```python
import functools

import jax
import jax.numpy as jnp
from jax import lax
from jax.experimental import pallas as pl
from jax.experimental.pallas import tpu as pltpu

F32 = jnp.float32
BF = jnp.bfloat16
MESH = pl.DeviceIdType.MESH

D = 1024
L = 4
NSH = 4
SHW = 3 * D
NIN = NSH * SHW
GRP = 8
BLK = 128
NPG = 4
PGW = D // NPG
POOL_WINDOWS = (2, 4, 8, 16)
HALO = 16
RMS_EPS = 1e-6
LN_EPS = 1e-5
ADAM_LR, ADAM_B1, ADAM_B2, ADAM_EPS, ADAM_WD, ADAM_STEP = 0.001, 0.9, 0.999, 1e-8, 0.01, 10

RC = 16
CW = 512
COLS = tuple(range(0, D, CW))
MIB = 1 << 20

PK_BR = SHW
PK_PW = PK_BR + D
PK_CW = PK_PW + NPG * (PGW // NSH) * PGW // D
PK_ROWS = PK_CW + 16

_C0 = 0.7978845608028654
_C1 = 0.044715


def _pcall(body, **kw):
    return pl.pallas_call(body, **kw)


def _params(sem, vmem_mib):
    return pltpu.CompilerParams(dimension_semantics=sem, vmem_limit_bytes=vmem_mib * MIB)


def _mm(a, b):
    return jnp.dot(a.astype(BF), b.astype(BF), preferred_element_type=F32)


def _mm_nt(a, b):
    return lax.dot_general(a.astype(BF), b.astype(BF), (((1,), (1,)), ((), ())), preferred_element_type=F32)


def _mm_tn(a, b):
    return lax.dot_general(a.astype(BF), b.astype(BF), (((0,), (0,)), ((), ())), preferred_element_type=F32)


def _gelu(x):
    return 0.5 * x * (1.0 + jnp.tanh(_C0 * x * (1.0 + _C1 * x * x)))


def _gelu_fg(x):
    x2 = x * x
    t = jnp.tanh(_C0 * x * (1.0 + _C1 * x2))
    f = 0.5 * x * (1.0 + t)
    g = 0.5 * (1.0 + t) + 0.5 * x * (1.0 - t * t) * (_C0 * (1.0 + 3.0 * _C1 * x2))
    return f, g


def _sigmoid(x):
    return 0.5 * jnp.tanh(0.5 * x) + 0.5


def _silu(x):
    return x * _sigmoid(x)


def _silu_fg(x):
    s = _sigmoid(x)
    return x * s, s * (1.0 + x * (1.0 - s))


def _rowsum(v):
    return jnp.sum(v, axis=1, keepdims=True)


def _chunks(n_rows, fn, unroll=2):
    def body(c, carry):
        fn(pl.multiple_of(c * RC, RC))
        return carry
    lax.fori_loop(0, n_rows // RC, body, 0, unroll=unroll)


def _trail(e, win):
    s, sh = e, 1
    while sh < win:
        s = s + pltpu.roll(s, sh, 0)
        sh *= 2
    return s


def _lead(e, win):
    n = e.shape[0]
    s, sh = e, 1
    while sh < win:
        s = s + pltpu.roll(s, n - sh, 0)
        sh *= 2
    return s


def _inv_count(pos0, win):
    pos = pos0 + lax.broadcasted_iota(jnp.int32, (RC, 1), 0)
    return 1.0 / jnp.minimum(pos + 1, win).astype(F32)


def _masked_sgu(sgw_ref, g):
    r = lax.broadcasted_iota(jnp.int32, (BLK, BLK), 0) // 64
    c = lax.broadcasted_iota(jnp.int32, (BLK, BLK), 1) // 64
    return jnp.where(c <= r, sgw_ref[g], 0.0)


class _AsF32:
    def __init__(self, ref):
        self.ref = ref

    def __getitem__(self, idx):
        return self.ref[idx].astype(F32)


def _tile(t, want):
    return min(t, want)


def _fwd_in(x, ng, wl, lng, lnb, sgw, sgbT, pwb, pb, psc, cw, cb):
    T = x.shape[0]
    tm = _tile(T, 256)
    nT = T // tm

    def body(x_ref, ng_ref, w_hbm, lng_ref, lnb_ref, sgw_ref, sgbT_ref, pw_ref, pb_ref, psc_ref, cw_ref, cb_ref,
             proj_hbm, h_ref, y_ref, w_s, pja, pjb, pjc, pjd, qa, qb, qc, qd, h_s, vn_s, mix_s, d_s, wm_s, extp_s, extc_s,
             wsem, psem):
        i = pl.program_id(0)
        pj = (pja, pjb, pjc, pjd)
        kept = (qa, qb, qc, qd)

        def w_copy(p):
            return pltpu.make_async_copy(w_hbm.at[p, pl.ds(0, SHW), :], w_s.at[p], wsem.at[p])

        def p_copy(p):
            dst = proj_hbm.at[pl.ds(pl.multiple_of(i * tm, tm), tm), pl.ds(p * SHW, SHW)]
            return pltpu.make_async_copy(kept[p], dst, psem.at[p])

        def keep(p):
            kept[p][...] = pj[p][...].astype(BF)
            p_copy(p).start()

        def all_rows(fn):
            for c in range(tm // RC):
                fn(c * RC)

        def buffer_free(p):
            @pl.when(i == 0)
            def _():
                w_copy(p).wait()

            @pl.when(i > 0)
            def _():
                p_copy(p).wait()

        def project(p, blocks):
            for b in blocks:
                pj[p][:, b * D:(b + 1) * D] = jnp.dot(h_s[...], w_s[p, b * D:(b + 1) * D, :],
                                                      preferred_element_type=F32)

        @pl.when(i == 0)
        def _():
            for p in range(NSH):
                w_copy(p).start()
            extp_s[0:HALO, :] = jnp.zeros((HALO, D), F32)
            extc_s[0:HALO, :] = jnp.zeros((HALO, D), F32)
            for g in range(GRP):
                wm_s[g] = _masked_sgu(sgw_ref, g).astype(BF)

        def norm_rows(r0):
            rows = pl.ds(r0, RC)
            xs = [x_ref[rows, c0:c0 + CW] for c0 in COLS]
            ms = sum(_rowsum(v * v) for v in xs) * (1.0 / D)
            r = lax.rsqrt(ms + RMS_EPS)
            for k, c0 in enumerate(COLS):
                hb = (xs[k] * r * ng_ref[:, c0:c0 + CW]).astype(BF)
                h_s[rows, c0:c0 + CW] = hb
                h_ref[rows, c0:c0 + CW] = hb

        def gating(first, rest):
            proj_ref = pj[0]
            first()

            def f1(r0):
                rows = pl.ds(r0, RC)
                gv = [_gelu(proj_ref[rows, D + c0:D + c0 + CW]) for c0 in COLS]
                mu = sum(_rowsum(v) for v in gv) * (1.0 / D)
                dv = [v - mu for v in gv]
                var = sum(_rowsum(v * v) for v in dv) * (1.0 / D)
                rstd = lax.rsqrt(var + LN_EPS)
                for k, c0 in enumerate(COLS):
                    vn_s[rows, c0:c0 + CW] = (dv[k] * rstd * lng_ref[:, c0:c0 + CW] + lnb_ref[:, c0:c0 + CW]).astype(BF)
            all_rows(f1)
            for g in range(GRP):
                w = wm_s[g]
                bcol = sgbT_ref[:, g:g + 1]
                gc = slice(g * BLK, (g + 1) * BLK)
                for n in range(tm // BLK):
                    rr = slice(n * BLK, (n + 1) * BLK)
                    mix_s[rr, gc] = jnp.dot(w, vn_s[rr, gc], preferred_element_type=F32) + bcol
            rest()

            def f2(r0):
                rows = pl.ds(r0, RC)
                for c0 in COLS:
                    au = proj_ref[rows, c0:c0 + CW]
                    az = proj_ref[rows, 2 * D + c0:2 * D + c0 + CW]
                    y_ref[0, rows, c0:c0 + CW] = (_gelu(au) * mix_s[rows, c0:c0 + CW] * _silu(az)).astype(BF)
            all_rows(f2)

        def pooling(first, rest):
            proj_ref = pj[1]
            first()
            extp_s[HALO:HALO + tm, :] = proj_ref[:, 0:D]

            def f1(r0):
                rows = pl.ds(r0, RC)
                e = extp_s[pl.ds(r0, 2 * RC), :]
                for k, win in enumerate(POOL_WINDOWS):
                    gc = slice(k * PGW, (k + 1) * PGW)
                    s = _trail(e[:, gc], win)[RC:2 * RC]
                    d_s[rows, gc] = (s * _inv_count(i * tm + r0, win) - e[RC:2 * RC, gc]).astype(BF)
            all_rows(f1)
            extp_s[0:HALO, :] = extp_s[tm:tm + HALO, :]
            for k in range(NPG):
                gc = slice(k * PGW, (k + 1) * PGW)
                mix_s[:, gc] = jnp.dot(d_s[:, gc], pw_ref[k], preferred_element_type=F32)
            rest()

            def f2(r0):
                rows = pl.ds(r0, RC)
                for c0 in COLS:
                    cs = slice(c0, c0 + CW)
                    bz = proj_ref[rows, D + c0:D + c0 + CW]
                    y_ref[1, rows, cs] = ((mix_s[rows, cs] + pb_ref[:, cs]) * psc_ref[:, cs] * _silu(bz)).astype(BF)
            all_rows(f2)

        def convolution(first, rest):
            proj_ref = pj[2]
            first()

            def f1(r0):
                rows = pl.ds(r0, RC)
                extc_s[pl.ds(HALO + r0, RC), :] = proj_ref[rows, D:2 * D] * pj[1][rows, 2 * D:3 * D]
            all_rows(f1)
            rest()

            def f2(r0):
                rows = pl.ds(r0, RC)
                for c0 in COLS:
                    cs = slice(c0, c0 + CW)
                    e = extc_s[pl.ds(r0, 2 * RC), cs]
                    conv = (cw_ref[2:3, cs] * e + cw_ref[1:2, cs] * pltpu.roll(e, 1, 0)
                            + cw_ref[0:1, cs] * pltpu.roll(e, 2, 0))[RC:2 * RC] + cb_ref[:, cs]
                    cbv = proj_ref[rows, cs]
                    cz = proj_ref[rows, 2 * D + c0:2 * D + c0 + CW]
                    y_ref[2, rows, cs] = (cbv * conv * _silu(cz)).astype(BF)
            all_rows(f2)
            extc_s[0:HALO, :] = extc_s[tm:tm + HALO, :]

        all_rows(norm_rows)
        buffer_free(0)
        project(0, (0, 1, 2))
        keep(0)
        for p, mixer, early in ((1, gating, 2), (2, pooling, 1), (3, convolution, 1)):
            buffer_free(p)
            mixer(functools.partial(project, p, range(early)), functools.partial(project, p, range(early, 3)))
            keep(p)

        @pl.when(i == nT - 1)
        def _():
            for p in range(NSH):
                p_copy(p).wait()

    vec = pl.BlockSpec((1, D), lambda i: (0, 0))
    return _pcall(
        body, name="fwd_in", grid=(nT,),
        in_specs=[
            pl.BlockSpec((tm, D), lambda i: (i, 0)), vec, ANY, vec, vec,
            pl.BlockSpec((GRP, BLK, BLK), lambda i: (0, 0, 0)),
            pl.BlockSpec((BLK, GRP), lambda i: (0, 0)),
            pl.BlockSpec((NPG, PGW, PGW), lambda i: (0, 0, 0)),
            vec, vec,
            pl.BlockSpec((3, D), lambda i: (0, 0)), vec,
        ],
        out_specs=[ANY, pl.BlockSpec((tm, D), lambda i: (i, 0)), pl.BlockSpec((3, tm, D), lambda i: (0, i, 0))],
        out_shape=[jax.ShapeDtypeStruct((T, NIN), BF), jax.ShapeDtypeStruct((T, D), BF),
                   jax.ShapeDtypeStruct((3, T, D), BF)],
        scratch_shapes=[
            pltpu.VMEM((NSH, SHW, D), BF),
            pltpu.VMEM((tm, SHW), F32), pltpu.VMEM((tm, SHW), F32), pltpu.VMEM((tm, SHW), F32), pltpu.VMEM((tm, SHW), F32),
            pltpu.VMEM((tm, SHW), BF), pltpu.VMEM((tm, SHW), BF), pltpu.VMEM((tm, SHW), BF), pltpu.VMEM((tm, SHW), BF),
            pltpu.VMEM((tm, D), BF), pltpu.VMEM((tm, D), BF), pltpu.VMEM((tm, D), F32),
            pltpu.VMEM((tm, D), BF), pltpu.VMEM((GRP, BLK, BLK), BF),
            pltpu.VMEM((tm + HALO, D), F32), pltpu.VMEM((tm + HALO, D), F32),
            pltpu.SemaphoreType.DMA((NSH,)), pltpu.SemaphoreType.DMA((NSH,)),
        ],
        compiler_params=_params(("arbitrary",), 58),
    )(x, ng, wl, lng, lnb, sgw, sgbT, pwb, pb, psc, cw, cb)


def _branch_spec(which):
    q = D // NSH
    return lambda *g: (0, PK_BR // q + (which(*g) if callable(which) else which), 0)


def _fwd_out(y3, proj, x, wl):
    T = x.shape[0]
    tm = _tile(T, 256)
    q = D // NSH

    nT = T // tm

    def body(y_ref, gl_ref, x_ref, wa_ref, wb_ref, wc_ref, wo_ref, p_ref, mg_ref, xo_ref, even_s, odd_s):
        i = pl.program_id(0)
        gl_ref = _AsF32(gl_ref)

        def branches(buf):
            for k, w_ref in enumerate((wa_ref, wb_ref, wc_ref)):
                buf[k] = jnp.dot(y_ref[k], w_ref[...].reshape(D, D), preferred_element_type=F32)

        def merge(buf):
            for c in range(tm // RC):
                rows = pl.ds(c * RC, RC)
                for c0 in COLS:
                    cs = slice(c0, c0 + CW)
                    m = 0.0
                    for k in range(3):
                        pk = buf[k, rows, cs]
                        p_ref[k, rows, cs] = pk.astype(BF)
                        m = m + _sigmoid(gl_ref[rows, k * D + c0:k * D + c0 + CW]) * pk
                    mg_ref[rows, cs] = m.astype(BF)
            xo_ref[...] = x_ref[...] + jnp.dot(mg_ref[...], wo_ref[...].reshape(D, D), preferred_element_type=F32)

        bufs = (even_s, odd_s)

        @pl.when(i == 0)
        def _():
            branches(bufs[0])

        for par in (0, 1):
            @pl.when((i % 2 == par) & (i > 0) & (i < nT))
            def _():
                branches(bufs[par])
                merge(bufs[1 - par])

        @pl.when(i == nT)
        def _():
            merge(bufs[(nT - 1) % 2])

    prev = lambda i: jnp.maximum(i - 1, 0)
    return _pcall(
        body, name="fwd_out", grid=(nT + 1,),
        in_specs=[
            pl.BlockSpec((3, tm, D), lambda i: (0, jnp.minimum(i, nT - 1), 0)),
            pl.BlockSpec((tm, SHW), lambda i: (prev(i), 3)),
            pl.BlockSpec((tm, D), lambda i: (prev(i), 0)),
        ] + [pl.BlockSpec((NSH, q, D), _branch_spec(k)) for k in range(4)],
        out_specs=[
            pl.BlockSpec((3, tm, D), lambda i: (0, prev(i), 0)),
            pl.BlockSpec((tm, D), lambda i: (prev(i), 0)),
            pl.BlockSpec((tm, D), lambda i: (prev(i), 0)),
        ],
        out_shape=[jax.ShapeDtypeStruct((3, T, D), BF), jax.ShapeDtypeStruct((T, D), BF),
                   jax.ShapeDtypeStruct((T, D), F32)],
        scratch_shapes=[pltpu.VMEM((3, tm, D), F32), pltpu.VMEM((3, tm, D), F32)],
        compiler_params=_params(("arbitrary",), 52),
    )(y3, proj, x, wl, wl, wl, wl)


def _loss_head(x, fg, tgt):
    T = x.shape[0]
    tm = _tile(T, 512)

    def body(x_ref, g_ref, t_ref, dx_ref, sq_ref, dg_ref, acc_s):
        i = pl.program_id(0)

        @pl.when(i == 0)
        def _():
            acc_s[...] = jnp.zeros_like(acc_s)

        def f(r0):
            rows = pl.ds(r0, RC)
            xs = [x_ref[rows, c0:c0 + CW] for c0 in COLS]
            r = lax.rsqrt(sum(_rowsum(v * v) for v in xs) * (1.0 / D) + RMS_EPS)
            xh = [v * r for v in xs]
            dyg, m = [], 0.0
            for k, c0 in enumerate(COLS):
                cs = slice(c0, c0 + CW)
                err = xh[k] * g_ref[:, cs] - t_ref[rows, cs]
                acc_s[0, :, cs] += err * err
                dy = err * (1.0 / D)
                acc_s[1, :, cs] += dy * xh[k]
                dyg.append(dy * g_ref[:, cs])
                m = m + _rowsum(dyg[k] * xh[k])
            m = m * (1.0 / D)
            for k, c0 in enumerate(COLS):
                dx_ref[rows, c0:c0 + CW] = r * (dyg[k] - xh[k] * m)
        _chunks(tm, f)

        @pl.when(i == pl.num_programs(0) - 1)
        def _():
            sq_ref[...] = jnp.sum(acc_s[0], axis=0, keepdims=True)
            dg_ref[...] = jnp.sum(acc_s[1], axis=0, keepdims=True)

    vec = pl.BlockSpec((1, D), lambda i: (0, 0))
    tile = pl.BlockSpec((tm, D), lambda i: (i, 0))
    return _pcall(
        body, name="loss_head", grid=(T // tm,),
        in_specs=[tile, vec, tile], out_specs=[tile, vec, vec],
        out_shape=[jax.ShapeDtypeStruct((T, D), F32), jax.ShapeDtypeStruct((1, D), F32), jax.ShapeDtypeStruct((1, D), F32)],
        scratch_shapes=[pltpu.VMEM((2, RC, D), F32)],
        compiler_params=_params(("arbitrary",), 32),
    )(x, fg, tgt)


def _bwd_mix(dxo, p3, proj, wl, lng, lnb, sgw, sgbT, pwb, pb, psc, cw, cb):
    T = dxo.shape[0]
    tm = _tile(T, 256)
    nT = T // tm
    hb = tm // HALO

    def body(dxo_ref, p_ref, pj_ref, ch_ref, bpp_ref, ccp_ref, chp_ref, w_ref, lng_ref, lnb_ref, sgw_ref, sgbT_ref,
             pw_ref, pb_ref, psc_ref, cw_ref, cb_ref,
             dpj_ref, dp_ref, dsw_ref, dsbT_ref, vec_ref, dpw_ref,
             dy_s, ext_s, nxt_s, halo_s, dch_s, d_s, t_s, dy0_s, vn_s, dmix_s, xh_s, rstd_s, acc_s, accb_s, wm_s, wmT_s):
        i = pl.program_id(0)
        p = pl.program_id(1)
        ti = nT - 1 - i
        p_ref, pj_ref, ch_ref, bpp_ref, ccp_ref, chp_ref = (
            _AsF32(r) for r in (p_ref, pj_ref, ch_ref, bpp_ref, ccp_ref, chp_ref))

        @pl.when((i == 0) & (p == 0))
        def _():
            dsw_ref[...] = jnp.zeros_like(dsw_ref)
            dsbT_ref[...] = jnp.zeros_like(dsbT_ref)
            vec_ref[...] = jnp.zeros_like(vec_ref)
            dpw_ref[...] = jnp.zeros_like(dpw_ref)
            halo_s[...] = jnp.zeros_like(halo_s)
            for g in range(GRP):
                wm = _masked_sgu(sgw_ref, g)
                wm_s[g] = wm.astype(BF)
                wmT_s[g] = wm.T.astype(BF)

        def flush(n_acc, rows_of):
            for a in range(n_acc):
                vec_ref[rows_of[a]:rows_of[a] + 1, :] += jnp.sum(acc_s[a], axis=0, keepdims=True)

        @pl.when(p == 0)
        def _():
            dy_s[...] = _mm_nt(dxo_ref[...], w_ref[...].reshape(D, D))

            def f(r0):
                rows = pl.ds(r0, RC)
                for c0 in COLS:
                    cs = slice(c0, c0 + CW)
                    dm = dy_s[rows, cs]
                    for k in range(3):
                        s = _sigmoid(pj_ref[rows, k * D + c0:k * D + c0 + CW])
                        dp_ref[k, rows, cs] = (s * dm).astype(BF)
                        dpj_ref[k, rows, cs] = (dm * p_ref[k, rows, cs] * s * (1.0 - s)).astype(BF)
            _chunks(tm, f)

        @pl.when(p == 1)
        def _():
            dy_s[...] = _mm_nt(dp_ref[2], w_ref[...].reshape(D, D))
            acc_s[...] = jnp.zeros_like(acc_s)
            ext_s[0:HALO, :] = jnp.where(ti > 0, ccp_ref[...] * chp_ref[...], 0.0)
            nxt_s[tm:tm + HALO, :] = halo_s[0]

            def f1(r0):
                rows = pl.ds(r0, RC)
                ext_s[pl.ds(HALO + r0, RC), :] = pj_ref[rows, D:2 * D] * ch_ref[rows, :]
            _chunks(tm, f1, unroll=4)

            def f2(r0):
                rows = pl.ds(r0, RC)
                for c0 in COLS:
                    cs = slice(c0, c0 + CW)
                    e = ext_s[pl.ds(r0, 2 * RC), cs]
                    e0, e1, e2 = e[RC:2 * RC], pltpu.roll(e, 1, 0)[RC:2 * RC], pltpu.roll(e, 2, 0)[RC:2 * RC]
                    conv = cw_ref[2:3, cs] * e0 + cw_ref[1:2, cs] * e1 + cw_ref[0:1, cs] * e2 + cb_ref[:, cs]
                    cbv = pj_ref[rows, cs]
                    sz, sg = _silu_fg(pj_ref[rows, 2 * D + c0:2 * D + c0 + CW])
                    dyc = dy_s[rows, cs]
                    dconv = dyc * cbv * sz
                    dpj_ref[0, rows, cs] = (dyc * conv * sz).astype(BF)
                    dpj_ref[2, rows, cs] = (dyc * cbv * conv * sg).astype(BF)
                    nxt_s[rows, cs] = dconv
                    acc_s[0, :, cs] += dconv
                    acc_s[1, :, cs] += dconv * e2
                    acc_s[2, :, cs] += dconv * e1
                    acc_s[3, :, cs] += dconv * e0
            _chunks(tm, f2)

            def f3(r0):
                rows = pl.ds(r0, RC)
                for c0 in COLS:
                    cs = slice(c0, c0 + CW)
                    e = nxt_s[pl.ds(r0, 2 * RC), cs]
                    dcc = (cw_ref[2:3, cs] * e + cw_ref[1:2, cs] * pltpu.roll(e, 2 * RC - 1, 0)
                           + cw_ref[0:1, cs] * pltpu.roll(e, 2 * RC - 2, 0))[0:RC]
                    dpj_ref[1, rows, cs] = (dcc * ch_ref[rows, cs]).astype(BF)
                    dch_s[rows, cs] = dcc * pj_ref[rows, D + c0:D + c0 + CW]
            _chunks(tm, f3)
            halo_s[0] = nxt_s[0:HALO, :]
            flush(4, (4, 5, 6, 7))

        @pl.when(p == 2)
        def _():
            dy_s[...] = _mm_nt(dp_ref[1], w_ref[...].reshape(D, D))
            acc_s[...] = jnp.zeros_like(acc_s)
            ext_s[0:HALO, :] = jnp.where(ti > 0, bpp_ref[...], 0.0)
            ext_s[HALO:HALO + tm, :] = pj_ref[:, 0:D]
            nxt_s[tm:tm + HALO, :] = halo_s[1]

            def f1(r0):
                rows = pl.ds(r0, RC)
                e = ext_s[pl.ds(r0, 2 * RC), :]
                for k, win in enumerate(POOL_WINDOWS):
                    gc = slice(k * PGW, (k + 1) * PGW)
                    s = _trail(e[:, gc], win)[RC:2 * RC]
                    d_s[rows, gc] = (s * _inv_count(ti * tm + r0, win) - e[RC:2 * RC, gc]).astype(BF)
            _chunks(tm, f1)
            for k in range(NPG):
                gc = slice(k * PGW, (k + 1) * PGW)
                t_s[:, gc] = jnp.dot(d_s[:, gc], pw_ref[k], preferred_element_type=F32)

            def f2(r0):
                rows = pl.ds(r0, RC)
                for c0 in COLS:
                    cs = slice(c0, c0 + CW)
                    y0 = t_s[rows, cs] + pb_ref[:, cs]
                    sz, sg = _silu_fg(pj_ref[rows, D + c0:D + c0 + CW])
                    dyb = dy_s[rows, cs]
                    dy0 = dyb * psc_ref[:, cs] * sz
                    acc_s[0, :, cs] += dy0
                    acc_s[1, :, cs] += dyb * y0 * sz
                    dy0_s[rows, cs] = dy0.astype(BF)
                    dpj_ref[1, rows, cs] = (dyb * y0 * psc_ref[:, cs] * sg).astype(BF)
            _chunks(tm, f2)
            for k in range(NPG):
                gc = slice(k * PGW, (k + 1) * PGW)
                dpw_ref[k] += _mm_tn(d_s[:, gc], dy0_s[:, gc])
                t_s[:, gc] = _mm_nt(dy0_s[:, gc], pw_ref[k])

            def f3(r0):
                rows = pl.ds(r0, RC)
                for k, win in enumerate(POOL_WINDOWS):
                    gc = slice(k * PGW, (k + 1) * PGW)
                    nxt_s[rows, gc] = t_s[rows, gc] * _inv_count(ti * tm + r0, win)
            _chunks(tm, f3, unroll=4)

            def f4(r0):
                rows = pl.ds(r0, RC)
                e = nxt_s[pl.ds(r0, 2 * RC), :]
                for k, win in enumerate(POOL_WINDOWS):
                    gc = slice(k * PGW, (k + 1) * PGW)
                    dpj_ref[0, rows, gc] = (_lead(e[:, gc], win)[0:RC] - t_s[rows, gc]).astype(BF)
                dpj_ref[2, rows, :] = dch_s[rows, :].astype(BF)
            _chunks(tm, f4)
            halo_s[1] = nxt_s[0:HALO, :]
            flush(2, (2, 3))

        @pl.when(p == 3)
        def _():
            dy_s[...] = _mm_nt(dp_ref[0], w_ref[...].reshape(D, D))
            acc_s[...] = jnp.zeros_like(acc_s)
            accb_s[...] = jnp.zeros_like(accb_s)

            def f1(r0):
                rows = pl.ds(r0, RC)
                gv = [_gelu(pj_ref[rows, D + c0:D + c0 + CW]) for c0 in COLS]
                mu = sum(_rowsum(v) for v in gv) * (1.0 / D)
                dv = [v - mu for v in gv]
                var = sum(_rowsum(v * v) for v in dv) * (1.0 / D)
                rstd = lax.rsqrt(var + LN_EPS)
                rstd_s[rows, :] = jnp.broadcast_to(rstd, (RC, BLK))
                for k, c0 in enumerate(COLS):
                    cs = slice(c0, c0 + CW)
                    xh = dv[k] * rstd
                    xh_s[rows, cs] = xh
                    vn_s[rows, cs] = (xh * lng_ref[:, cs] + lnb_ref[:, cs]).astype(BF)
            _chunks(tm, f1, unroll=4)
            for g in range(GRP):
                bcol = sgbT_ref[:, g:g + 1]
                gc = slice(g * BLK, (g + 1) * BLK)
                for n in range(tm // BLK):
                    rr = slice(n * BLK, (n + 1) * BLK)
                    t_s[rr, gc] = jnp.dot(wm_s[g], vn_s[rr, gc], preferred_element_type=F32) + bcol

            def f2(r0):
                rows = pl.ds(r0, RC)
                brow = pl.ds(pl.multiple_of(r0 % BLK, RC), RC)
                for c0 in COLS:
                    cs = slice(c0, c0 + CW)
                    gu, ggu = _gelu_fg(pj_ref[rows, cs])
                    sz, sg = _silu_fg(pj_ref[rows, 2 * D + c0:2 * D + c0 + CW])
                    dya = dy_s[rows, cs]
                    mix = t_s[rows, cs]
                    dmix = dya * gu * sz
                    dpj_ref[0, rows, cs] = (dya * mix * sz * ggu).astype(BF)
                    dpj_ref[2, rows, cs] = (dya * gu * mix * sg).astype(BF)
                    dmix_s[rows, cs] = dmix.astype(BF)
                    accb_s[brow, cs] += dmix
            _chunks(tm, f2)
            for g in range(GRP):
                gc = slice(g * BLK, (g + 1) * BLK)
                dsbT_ref[:, g:g + 1] += _rowsum(accb_s[:, gc])
                for n in range(tm // BLK):
                    rr = slice(n * BLK, (n + 1) * BLK)
                    t_s[rr, gc] = jnp.dot(wmT_s[g], dmix_s[rr, gc], preferred_element_type=F32)
                    dsw_ref[g] += _mm_nt(dmix_s[rr, gc], vn_s[rr, gc])

            def f3(r0):
                rows = pl.ds(r0, RC)
                rstd = rstd_s[rows, 0:1]
                dxh, m1, m2 = [], 0.0, 0.0
                for k, c0 in enumerate(COLS):
                    cs = slice(c0, c0 + CW)
                    dvn = t_s[rows, cs]
                    xh = xh_s[rows, cs]
                    acc_s[0, :, cs] += dvn * xh
                    acc_s[1, :, cs] += dvn
                    dxh.append(dvn * lng_ref[:, cs])
                    m1 = m1 + _rowsum(dxh[k])
                    m2 = m2 + _rowsum(dxh[k] * xh)
                m1 = m1 * (1.0 / D)
                m2 = m2 * (1.0 / D)
                for k, c0 in enumerate(COLS):
                    cs = slice(c0, c0 + CW)
                    _, ggv = _gelu_fg(pj_ref[rows, D + c0:D + c0 + CW])
                    dpj_ref[1, rows, cs] = (rstd * (dxh[k] - m1 - xh_s[rows, cs] * m2) * ggv).astype(BF)
            _chunks(tm, f3)
            flush(2, (0, 1))

            @pl.when(i == nT - 1)
            def _():
                for g in range(GRP):
                    r = lax.broadcasted_iota(jnp.int32, (BLK, BLK), 0) // 64
                    c = lax.broadcasted_iota(jnp.int32, (BLK, BLK), 1) // 64
                    dsw_ref[g] = jnp.where(c <= r, dsw_ref[g], 0.0)

    def prev(col):
        return pl.BlockSpec((HALO, D), lambda i, p: (jnp.maximum((nT - 1 - i) * hb - 1, 0), col))

    vec = pl.BlockSpec((1, D), lambda i, p: (0, 0))
    const3 = lambda i, p: (0, 0, 0)
    return _pcall(
        body, name="bwd_mix", grid=(nT, NSH),
        in_specs=[
            pl.BlockSpec((tm, D), lambda i, p: (nT - 1 - i, 0)),
            pl.BlockSpec((3, tm, D), lambda i, p: (0, nT - 1 - i, 0)),
            pl.BlockSpec((tm, SHW), lambda i, p: (nT - 1 - i, 3 - p)),
            pl.BlockSpec((tm, D), lambda i, p: (nT - 1 - i, 5)),
            prev(3), prev(7), prev(5),
            pl.BlockSpec((NSH, D // NSH, D), _branch_spec(lambda i, p: 3 - p)),
            vec, vec,
            pl.BlockSpec((GRP, BLK, BLK), const3),
            pl.BlockSpec((BLK, GRP), lambda i, p: (0, 0)),
            pl.BlockSpec((NPG, PGW, PGW), const3),
            vec, vec,
            pl.BlockSpec((3, D), lambda i, p: (0, 0)), vec,
        ],
        out_specs=[
            pl.BlockSpec((3, tm, D), lambda i, p: (3 - p, nT - 1 - i, 0)),
            pl.BlockSpec((3, tm, D), lambda i, p: (0, nT - 1 - i, 0)),
            pl.BlockSpec((GRP, BLK, BLK), const3),
            pl.BlockSpec((BLK, GRP), lambda i, p: (0, 0)),
            pl.BlockSpec((8, D), lambda i, p: (0, 0)),
            pl.BlockSpec((NPG, PGW, PGW), const3),
        ],
        out_shape=[
            jax.ShapeDtypeStruct((12, T, D), BF), jax.ShapeDtypeStruct((3, T, D), BF),
            jax.ShapeDtypeStruct((GRP, BLK, BLK), F32), jax.ShapeDtypeStruct((BLK, GRP), F32),
            jax.ShapeDtypeStruct((8, D), F32), jax.ShapeDtypeStruct((NPG, PGW, PGW), F32),
        ],
        scratch_shapes=[
            pltpu.VMEM((tm, D), F32),
            pltpu.VMEM((tm + HALO, D), F32),
            pltpu.VMEM((tm + HALO, D), F32),
            pltpu.VMEM((2, HALO, D), F32),
            pltpu.VMEM((tm, D), F32),
            pltpu.VMEM((tm, D), BF),
            pltpu.VMEM((tm, D), F32),
            pltpu.VMEM((tm, D), BF),
            pltpu.VMEM((tm, D), BF),
            pltpu.VMEM((tm, D), BF),
            pltpu.VMEM((tm, D), F32),
            pltpu.VMEM((tm, BLK), F32),
            pltpu.VMEM((4, RC, D), F32),
            pltpu.VMEM((BLK, D), F32),
            pltpu.VMEM((GRP, BLK, BLK), BF), pltpu.VMEM((GRP, BLK, BLK), BF),
        ],
        compiler_params=_params(("arbitrary", "arbitrary"), 56),
    )(dxo, p3, proj, proj, proj, proj, proj, wl, lng, lnb, sgw, sgbT, pwb, pb, psc, cw, cb)


def _bwd_in(dproj, wl, x, dxo, ng):
    T = x.shape[0]
    tm = _tile(T, 512)

    nT = T // tm

    def body(dpj_ref, w_ref, x_ref, dxo_ref, ng_ref, dx_ref, dng_ref, cur, prev, g_s):
        i = pl.program_id(0)
        j = pl.program_id(1)

        def part():
            return sum(_mm_nt(dpj_ref[b], w_ref[b * D:(b + 1) * D, :]) for b in range(3))

        def finish_rows(r0):
            rows = pl.ds(r0, RC)
            xs = [x_ref[rows, c0:c0 + CW] for c0 in COLS]
            r = lax.rsqrt(sum(_rowsum(v * v) for v in xs) * (1.0 / D) + RMS_EPS)
            xh = [v * r for v in xs]
            dhg, m = [], 0.0
            for k, c0 in enumerate(COLS):
                cs = slice(c0, c0 + CW)
                dh = prev[rows, cs]
                g_s[:, cs] += dh * xh[k]
                dhg.append(dh * ng_ref[:, cs])
                m = m + _rowsum(dhg[k] * xh[k])
            m = m * (1.0 / D)
            for k, c0 in enumerate(COLS):
                cs = slice(c0, c0 + CW)
                dx_ref[rows, cs] = dxo_ref[rows, cs] + r * (dhg[k] - xh[k] * m)

        def finish():
            for c in range(tm // RC):
                finish_rows(c * RC)

        @pl.when((i == 0) & (j == 0))
        def _():
            g_s[...] = jnp.zeros_like(g_s)
            cur[...] = part()

        @pl.when((i > 0) & (i < nT) & (j == 0))
        def _():
            cur[...] = part()
            finish()

        @pl.when((i < nT) & (j > 0))
        def _():
            cur[...] += part()

        @pl.when((i < nT) & (j == NSH - 1))
        def _():
            prev[...] = cur[...]

        @pl.when((i == nT) & (j == 0))
        def _():
            finish()
            dng_ref[...] = jnp.sum(g_s[...], axis=0, keepdims=True)

    vec = pl.BlockSpec((1, D), lambda i, j: (0, 0))
    tile = pl.BlockSpec((tm, D), lambda i, j: (jnp.maximum(i - 1, 0), 0))
    return _pcall(
        body, name="bwd_in", grid=(nT + 1, NSH),
        in_specs=[
            pl.BlockSpec((3, tm, D), lambda i, j: (j, jnp.minimum(i, nT - 1), 0)),
            pl.BlockSpec((None, SHW, D), lambda i, j: (j, 0, 0)),
            tile, tile, vec,
        ],
        out_specs=[tile, vec],
        out_shape=[jax.ShapeDtypeStruct((T, D), F32), jax.ShapeDtypeStruct((1, D), F32)],
        scratch_shapes=[pltpu.VMEM((tm, D), F32), pltpu.VMEM((tm, D), F32), pltpu.VMEM((RC, D), F32)],
        compiler_params=_params(("arbitrary", "arbitrary"), 52),
    )(dproj, wl, x, dxo, ng)


def _tn_grad(a3, b3, split, out_map, name, into=None, after=None):
    nb, T, _ = b3.shape
    tk = _tile(T, 2048)
    nk = T // tk
    rows = D // split
    a_batched = a3.shape[0] > 1

    def body(a_ref, b_ref, *rest):
        o_ref, acc_s = rest[-2:]
        k = pl.program_id(1)

        @pl.when(k == 0)
        def _():
            acc_s[...] = _mm_tn(a_ref[...], b_ref[...])

        @pl.when((k > 0) & (k < nk))
        def _():
            acc_s[...] += _mm_tn(a_ref[...], b_ref[...])

        @pl.when(k >= nk - 1)
        def _():
            r0 = pl.multiple_of((k - (nk - 1)) * rows, rows)
            o_ref[...] = acc_s[pl.ds(r0, rows), :].astype(o_ref.dtype)

    def tok(k):
        return jnp.minimum(k, nk - 1)

    extra = ([] if into is None else [into]) + ([] if after is None else [after])
    return _pcall(
        body, name=name, grid=(nb, nk + split - 1),
        in_specs=[
            pl.BlockSpec((None, tk, D), (lambda n, k: (n, tok(k), 0)) if a_batched else (lambda n, k: (0, tok(k), 0))),
            pl.BlockSpec((None, tk, D), lambda n, k: (n, tok(k), 0)),
        ] + [ANY] * len(extra),
        out_specs=pl.BlockSpec((None, rows, D), lambda n, k: out_map(n, jnp.maximum(k - (nk - 1), 0))),
        out_shape=jax.ShapeDtypeStruct((NSH, PK_ROWS, D), BF),
        input_output_aliases={} if into is None else {2: 0},
        scratch_shapes=[pltpu.VMEM((D, D), F32)],
        compiler_params=_params(("arbitrary", "arbitrary"), 56),
    )(a3, b3, *extra)


def _place():
    x, y, c = lax.axis_index("x"), lax.axis_index("y"), lax.axis_index("c")
    chips = [(1 - x, y), (x, 1 - y), (1 - x, 1 - y)]
    return x, y, c, chips


def _peers(reach):
    x, y, c, chips = _place()
    if reach == "chips":
        return 2 * x + y, [((px, py, c), 2 * px + py) for px, py in chips]
    others = [(x, y, 1 - c)] + [(px, py, pc) for px, py in chips for pc in (c, 1 - c)]
    return 4 * x + 2 * y + c, [(pr, 4 * pr[0] + 2 * pr[1] + pr[2]) for pr in others]


ANY = pl.BlockSpec(memory_space=pl.ANY)


HBM = pl.BlockSpec(memory_space=pltpu.HBM)
SEM = pl.BlockSpec(memory_space=pltpu.SEMAPHORE)
EFFECT = pltpu.SideEffectType.DATAFLOW_SIDE_EFFECTING


def _own_slot(src, from_slot, name):
    rows = src.shape[-2]
    rb = rows // 9
    me = (2 * lax.axis_index("x") + lax.axis_index("y")).astype(jnp.int32).reshape(1)

    def body(me_ref, src_ref, land_ref):
        land_ref[...] = src_ref[...]

    if from_slot:
        src_spec = pl.BlockSpec((None, rb, D), lambda i, me_ref: (me_ref[0], i, 0))
    else:
        src_spec = pl.BlockSpec((rb, D), lambda i, me_ref: (i, 0))
    return _pcall(
        body, name=name,
        grid_spec=pltpu.PrefetchScalarGridSpec(
            num_scalar_prefetch=1, grid=(rows // rb,), in_specs=[src_spec],
            out_specs=pl.BlockSpec((None, rb, D), lambda i, me_ref: (me_ref[0], i, 0))),
        out_shape=jax.ShapeDtypeStruct((NSH, rows, D), src.dtype),
        compiler_params=_params(("arbitrary",), 32),
    )(me, src)


def _push_start(name, srcs, lands, per_peer, reach="chips"):
    n = len(srcs)
    npeer = 3 if reach == "chips" else 7
    ns = n * npeer

    def body(*refs):
        src, land = refs[:n], refs[n:2 * n]
        ssem, rsem = refs[2 * n:2 * n + ns], refs[2 * n + ns:2 * n + 2 * ns]
        token = refs[-1]
        me, peers = _peers(reach)
        for i in range(n):
            for j, (peer, slot) in enumerate(peers):
                pltpu.make_async_remote_copy(
                    src_ref=src[i].at[slot] if per_peer else src[i], dst_ref=land[i].at[me],
                    send_sem=ssem[npeer * i + j], recv_sem=rsem[npeer * i + j], device_id=peer, device_id_type=MESH).start()
        token[...] = jnp.zeros_like(token)

    ops = list(srcs) + list(lands)
    out = _pcall(
        body, name=name,
        out_shape=tuple([pltpu.SemaphoreType.DMA(())] * (2 * ns) + [pltpu.HBM(a.shape, a.dtype) for a in ops]
                        + [jax.ShapeDtypeStruct((8, 128), F32)]),
        in_specs=[HBM] * (2 * n),
        out_specs=tuple([SEM] * (2 * ns) + [HBM] * (2 * n) + [pl.BlockSpec(memory_space=pltpu.VMEM)]),
        input_output_aliases={i: 2 * ns + i for i in range(2 * n)},
        compiler_params=pltpu.CompilerParams(has_side_effects=EFFECT),
    )(*[pltpu.with_memory_space_constraint(a, pltpu.HBM) for a in ops])
    return out[:ns], out[ns:2 * ns], out[2 * ns:2 * ns + n], out[2 * ns + n:2 * ns + 2 * n], out[-1]


def _push_wait(name, src, land, ssem, rsem, after, per_peer, reach="chips"):
    npeer = len(ssem)

    def body(src_ref, land_ref, *rest):
        sems = rest[:2 * npeer]
        _, peers = _peers(reach)
        for j, (peer, slot) in enumerate(peers):
            cp = pltpu.make_async_remote_copy(
                src_ref=src_ref.at[slot] if per_peer else src_ref, dst_ref=land_ref.at[slot], send_sem=sems[j],
                recv_sem=sems[npeer + j], device_id=peer, device_id_type=MESH)
            cp.wait_send()
            cp.wait_recv()

    return _pcall(
        body, name=name,
        out_shape=(pltpu.HBM(src.shape, src.dtype), pltpu.HBM(land.shape, land.dtype)),
        in_specs=[HBM, HBM] + [SEM] * (2 * npeer) + [ANY], out_specs=(HBM, HBM),
        input_output_aliases={0: 0, 1: 1},
        compiler_params=pltpu.CompilerParams(has_side_effects=EFFECT),
    )(src, land, *ssem, *rsem, after)[1]


def _swap_sibling(arrs):
    n = len(arrs)

    def body(*refs):
        src, dst = refs[:n], refs[n:2 * n]
        ssem, rsem = refs[2 * n:]
        x, y, c, _ = _place()
        cps = [pltpu.make_async_remote_copy(src_ref=src[a], dst_ref=dst[a], send_sem=ssem.at[a], recv_sem=rsem.at[a],
                                            device_id=(x, y, 1 - c), device_id_type=MESH) for a in range(n)]
        for cp in cps:
            cp.start()
        for cp in cps:
            cp.wait()

    return _pcall(
        body, name="swap_sibling",
        in_specs=[ANY] * n, out_specs=[ANY] * n,
        out_shape=[jax.ShapeDtypeStruct(a.shape, a.dtype) for a in arrs],
        scratch_shapes=[pltpu.SemaphoreType.DMA((n,)), pltpu.SemaphoreType.DMA((n,))],
    )(*arrs)


def _gather_all(v):
    def body(src, dst, ssem, rsem, lsem):
        x, y, c, _ = _place()
        me = 4 * x + 2 * y + c
        peers = [(x, y, 1 - c), (1 - x, y, c), (1 - x, y, 1 - c), (x, 1 - y, c), (x, 1 - y, 1 - c),
                 (1 - x, 1 - y, c), (1 - x, 1 - y, 1 - c)]
        local = pltpu.make_async_copy(src, dst.at[me], lsem)
        local.start()
        sends = [pltpu.make_async_remote_copy(src_ref=src, dst_ref=dst.at[me], send_sem=ssem.at[j], recv_sem=rsem.at[j],
                                              device_id=pr, device_id_type=MESH) for j, pr in enumerate(peers)]
        for cp in sends:
            cp.start()
        for j, (px, py, pc) in enumerate(peers):
            pltpu.make_async_remote_copy(src_ref=src, dst_ref=dst.at[4 * px + 2 * py + pc], send_sem=ssem.at[j],
                                         recv_sem=rsem.at[j], device_id=(px, py, pc), device_id_type=MESH).wait_recv()
        for cp in sends:
            cp.wait_send()
        local.wait()

    return _pcall(
        body, name="gather_all", in_specs=[ANY], out_specs=ANY,
        out_shape=jax.ShapeDtypeStruct((8,) + v.shape, v.dtype),
        scratch_shapes=[pltpu.SemaphoreType.DMA((7,)), pltpu.SemaphoreType.DMA((7,)), pltpu.SemaphoreType.DMA(())],
    )(v)


def _sum_slots(r, rb):
    S = r.shape[0]

    def body(r_ref, o_ref):
        acc = r_ref[0].astype(F32)
        for s in range(1, S):
            acc = acc + r_ref[s].astype(F32)
        o_ref[...] = acc

    if r.ndim == 3:
        _, R, C = r.shape
        grid, blk, imap = (R // rb,), (S, rb, C), (lambda i: (0, i, 0))
        oblk, omap = (rb, C), (lambda i: (i, 0))
    else:
        _, K, R, C = r.shape
        grid, blk, imap = (K,), (S, None, R, C), (lambda i: (0, i, 0, 0))
        oblk, omap = (None, R, C), (lambda i: (i, 0, 0))
    return _pcall(
        body, name="sum_slots", grid=grid, in_specs=[pl.BlockSpec(blk, imap)], out_specs=pl.BlockSpec(oblk, omap),
        out_shape=jax.ShapeDtypeStruct(r.shape[1:], F32), compiler_params=_params(("arbitrary",), 48),
    )(r)


def _adamw(gs, g_spec, w, m, v, p_spec, prev, grid):
    ng = len(gs)
    bc1 = 1.0 - ADAM_B1 ** ADAM_STEP
    bc2 = 1.0 - ADAM_B2 ** ADAM_STEP

    def body(*refs):
        g = refs[0][...]
        for a in range(1, ng):
            g = g + refs[a][...]
        w_ref, m_ref, v_ref = refs[ng:ng + 3]
        go, do, mo, vo = refs[ng + 3 + 4:]
        mn = ADAM_B1 * m_ref[...] + (1.0 - ADAM_B1) * g
        vn = ADAM_B2 * v_ref[...] + (1.0 - ADAM_B2) * (g * g)
        go[...] = g
        mo[...] = mn
        vo[...] = vn
        do[...] = -ADAM_LR * ((mn / bc1) / (jnp.sqrt(vn / bc2) + ADAM_EPS) + ADAM_WD * w_ref[...])

    out = jax.ShapeDtypeStruct(w.shape, F32)
    k0 = ng + 3
    return _pcall(
        body, name="adamw", grid=grid,
        in_specs=[g_spec] * ng + [p_spec] * 3 + [ANY] * 4,
        out_specs=[p_spec] * 4, out_shape=[out] * 4,
        input_output_aliases={k0: 0, k0 + 1: 1, k0 + 2: 2, k0 + 3: 3},
        compiler_params=_params(("arbitrary",) * len(grid), 48),
    )(*gs, w, m, v, *prev)


def _empty4(w):
    return tuple(lax.empty(w.shape, F32) for _ in range(4))


N_SGW = L * GRP * BLK * BLK // D
O_NG, O_VEC, O_SGB, O_FG, O_SGW = 0, 8, 32, 40, 48
O_CW = O_SGW + N_SGW
N_PACK = O_CW + 16
PACK_RB = N_PACK // 3


def _pad_to(a, rows):
    return jnp.pad(a, ((0, rows - a.shape[0]), (0, 0)))


def _pack_small(ng, vecs, sgb, fg, sgw, cw):
    parts = [_pad_to(ng, 8), _pad_to(vecs.reshape(L * 5, D), 24), _pad_to(sgb.reshape(L, D), 8),
             _pad_to(fg.reshape(1, D), 8), sgw.reshape(N_SGW, D), _pad_to(cw, 16)]
    return jnp.concatenate(parts, axis=0)


def kernel(x, norm_g, w_in, sgu_ln_g, sgu_ln_b, sgu_w, sgu_b, pool_w, pool_b, pool_scale, conv_w, conv_b, w_branch_a, w_branch_b, w_branch_c, w_out, final_g, loss_target, m_norm_g, m_w_in, m_sgu_ln_g, m_sgu_ln_b, m_sgu_w, m_sgu_b, m_pool_w, m_pool_b, m_pool_scale, m_conv_w, m_conv_b, m_w_branch_a, m_w_branch_b, m_w_branch_c, m_w_out, m_final_g, v_norm_g, v_w_in, v_sgu_ln_g, v_sgu_ln_b, v_sgu_w, v_sgu_b, v_pool_w, v_pool_b, v_pool_scale, v_conv_w, v_conv_b, v_w_branch_a, v_w_branch_b, v_w_branch_c, v_w_out, v_final_g):
    cx, cy = lax.axis_index("x"), lax.axis_index("y")
    me = 2 * cx + cy
    xl, tgt = x[0], loss_target[0]
    q = D // NSH

    wq = w_in.astype(BF).reshape(L, D, 3, D).transpose(0, 2, 1, 3).reshape(L, SHW, D)
    brq = jnp.stack([w_branch_a, w_branch_b, w_branch_c, w_out], axis=1).astype(BF).reshape(L, D, D)
    pwq = pool_w.astype(BF).reshape(L, PK_CW - PK_PW, D)
    cwq = lax.bitcast_convert_type(conv_w, BF).reshape(L, 3 * q * 2)
    cwq = jnp.pad(cwq, ((0, 0), (0, 16 * D - 3 * q * 2))).reshape(L, 16, D)
    packs = [jnp.concatenate([wq[l], brq[l], pwq[l], cwq[l]], axis=0) for l in range(L)]
    lands = [_own_slot(packs[l], False, f"ag_own_{l}") for l in range(L)]
    ag_s, ag_r, packs, lands, tok = _push_start("ag_start", packs, lands, False)
    sgbT = sgu_b.transpose(0, 2, 1)

    def layer_weights(l, after):
        wl = _push_wait(f"ag_wait_{l}", packs[l], lands[l], ag_s[3 * l:3 * l + 3], ag_r[3 * l:3 * l + 3], after, False)
        pwb = wl[:, PK_PW:PK_CW].reshape(NSH, NPG, PGW // NSH, PGW).transpose(1, 0, 2, 3).reshape(NPG, PGW, PGW)
        cwb = wl[:, PK_CW:].reshape(NSH, 16 * D)[:, :3 * q * 2].reshape(NSH, 3, q, 2)
        cwf = lax.bitcast_convert_type(cwb, F32).transpose(1, 0, 2).reshape(3, D)
        small = (sgu_ln_g[l:l + 1], sgu_ln_b[l:l + 1], sgu_w[l], sgbT[l], pwb, pool_b[l:l + 1], pool_scale[l:l + 1],
                 cwf, conv_b[l:l + 1])
        return wl, small

    xs, saved, wts = [xl], [], []
    for l in range(L):
        wl, small = layer_weights(l, tok if l == 0 else xs[l])
        wts.append((wl, small))
        proj, h, y3 = _fwd_in(xs[l], norm_g[l:l + 1], wl, *small)
        p3, mg, xo = _fwd_out(y3, proj, xs[l], wl)
        saved.append((proj, h, y3, p3, mg))
        xs.append(xo)

    dx, sq, dfg = _loss_head(xs[L], final_g[None], tgt)
    loss = lax.psum(jnp.sum(sq) * (0.5 / D), ("x", "y", "c"))

    g_w_in = _empty4(w_in)
    g_br = [_empty4(w_out) for _ in range(4)]
    g_pw = _empty4(pool_w)
    dng, dvec, dsgw, dsgb = [None] * L, [None] * L, [None] * L, [None] * L
    branches = [(w_branch_a, m_w_branch_a, v_w_branch_a), (w_branch_b, m_w_branch_b, v_w_branch_b),
                (w_branch_c, m_w_branch_c, v_w_branch_c), (w_out, m_w_out, v_w_out)]
    nb = D // 128

    def finish(l, landed):
        nonlocal g_w_in, g_pw
        mine = _sum_slots(landed, PK_ROWS // 9)
        sums = [mine, _swap_sibling([mine])[0]]
        g_w_in = _adamw(sums, pl.BlockSpec((128, D), lambda b, i: (b * nb + i, 0)), w_in, m_w_in, v_w_in,
                        pl.BlockSpec((None, 128, D), lambda b, i: (l, i, b)), g_w_in, (3, nb))
        for k, (w, m, v) in enumerate(branches):
            g_br[k] = _adamw(sums, pl.BlockSpec((q, D), lambda i, k=k: (PK_BR // q + k, 0)), w, m, v,
                             pl.BlockSpec((None, q, D), lambda i: (l, 0, 0)), g_br[k], (1,))
        pools = [a[PK_PW:PK_CW].reshape(NPG, PGW // NSH, PGW) for a in sums]
        g_pw = _adamw(pools, pl.BlockSpec((None, PGW // NSH, PGW), lambda g: (g, 0, 0)), pool_w, m_pool_w, v_pool_w,
                      pl.BlockSpec((None, None, PGW // NSH, PGW), lambda g: (l, g, 0, 0)), g_pw, (NPG,))

    pend = None
    for l in reversed(range(L)):
        proj, h, y3, p3, mg = saved[l]
        wl, small = wts[l]
        dproj, dp3, dsgw[l], dsbT, dvec[l], dpw = _bwd_mix(dx, p3, proj, wl, *small)
        dsgb[l] = dsbT.T
        if l == 0:
            dv = jnp.stack(dvec)
            part = _pack_small(jnp.zeros((L, D), F32), dv[:, 0:5], jnp.stack(dsgb), dfg[0], jnp.stack(dsgw),
                               dv[:, 5:8].reshape(L * 3, D))
            zone = lax.dynamic_update_slice(lax.empty((8, N_PACK, D), F32), part[None],
                                            (2 * me + lax.axis_index("c"), 0, 0))
            sm_s, sm_r, (part,), (zone,), started = _push_start("small_start", [part], [zone], False, "all")
        grads = _tn_grad(h[None], dproj, 1, lambda n, s: (n // 3, n % 3, 0), "grad_w_in", after=started if l == 0 else None)
        grads = _tn_grad(y3, dp3, NSH, lambda n, s: (s, PK_BR // q + n, 0), "grad_w_branch", into=grads)
        grads = _tn_grad(mg[None], dx[None], NSH, lambda n, s: (s, PK_BR // q + 3, 0), "grad_w_out", into=grads)
        dpq = dpw.astype(BF).reshape(NPG, NSH, PGW // NSH, PGW).transpose(1, 0, 2, 3).reshape(NSH, PK_CW - PK_PW, D)
        grads = lax.dynamic_update_slice(grads, jnp.pad(dpq, ((0, 0), (0, PK_ROWS - PK_CW), (0, 0))), (0, PK_PW, 0))
        if pend is not None:
            landed = _push_wait(f"rs_wait_{pend[0]}", *pend[1:], dproj, True)
        land = _own_slot(grads, True, f"rs_own_{l}")
        ss, rs, (grads,), (land,), tok = _push_start(f"rs_start_{l}", [grads], [land], True)
        if pend is not None:
            finish(pend[0], landed)
        dx, dng[l] = _bwd_in(dproj, wl, xs[l], dx, norm_g[l:l + 1] + tok[0, 0])
        pend = (l, grads, land, ss, rs)
    finish(pend[0], _push_wait(f"rs_wait_{pend[0]}", *pend[1:], dx, True))

    zone = _push_wait("small_wait", part, zone, sm_s, sm_r, dx, False, "all")
    gng = _sum_slots(_gather_all(_pad_to(jnp.concatenate(dng), O_VEC)), O_VEC)
    gsmall = jnp.concatenate([gng, _sum_slots(zone, PACK_RB)[O_VEC:]])
    gcw =lax.dynamic_slice_in_dim(gsmall[O_CW:O_CW + L * 3], me * q, q, axis=1)
    gpack = jnp.concatenate([gsmall[:O_CW], _pad_to(gcw.reshape(L * 3 * q // D, D), 16)])

    def pack(ng, lg, lb, sw, sb, pb_, ps, cwv, cb_, fg):
        return _pack_small(ng, jnp.stack([lg, lb, pb_, ps, cb_], axis=1), sb, fg, sw, cwv.reshape(L * 3 * q // D, D))

    wp = pack(norm_g, sgu_ln_g, sgu_ln_b, sgu_w, sgu_b, pool_b, pool_scale, conv_w, conv_b, final_g)
    mp = pack(m_norm_g, m_sgu_ln_g, m_sgu_ln_b, m_sgu_w, m_sgu_b, m_pool_b, m_pool_scale, m_conv_w, m_conv_b, m_final_g)
    vp = pack(v_norm_g, v_sgu_ln_g, v_sgu_ln_b, v_sgu_w, v_sgu_b, v_pool_b, v_pool_scale, v_conv_w, v_conv_b, v_final_g)
    rows = pl.BlockSpec((PACK_RB, D), lambda i: (i, 0))
    sm = _adamw([gpack], rows, wp, mp, vp, rows, _empty4(wp), (N_PACK // PACK_RB,))

    def unpack(a):
        vv = a[O_VEC:O_VEC + L * 5].reshape(L, 5, D)
        sb = a[O_SGB:O_SGB + L].reshape(L, GRP, BLK)
        fg = a[O_FG]
        sw = a[O_SGW:O_SGW + N_SGW].reshape(L, GRP, BLK, BLK)
        cwv = a[O_CW:O_CW + L * 3 * q // D].reshape(L, 3, q)
        return dict(norm_g=a[O_NG:O_NG + L], w_in=None, sgu_ln_g=vv[:, 0], sgu_ln_b=vv[:, 1], sgu_w=sw, sgu_b=sb, pool_w=None,
                    pool_b=vv[:, 2], pool_scale=vv[:, 3], conv_w=cwv, conv_b=vv[:, 4], w_branch_a=None,
                    w_branch_b=None, w_branch_c=None, w_out=None, final_g=fg)

    outs = [loss, dx[None]]
    for kind in range(4):
        d = unpack(sm[kind])
        d.update(w_in=g_w_in[kind], pool_w=g_pw[kind], w_branch_a=g_br[0][kind], w_branch_b=g_br[1][kind],
                 w_branch_c=g_br[2][kind], w_out=g_br[3][kind])
        outs.extend(d[n] for n in ("norm_g", "w_in", "sgu_ln_g", "sgu_ln_b", "sgu_w", "sgu_b", "pool_w", "pool_b",
                                   "pool_scale", "conv_w", "conv_b", "w_branch_a", "w_branch_b", "w_branch_c", "w_out",
                                   "final_g"))
    return tuple(outs)
```

```python
import functools

import jax
import jax.numpy as jnp
from jax import lax
from jax.experimental import pallas as pl
from jax.experimental.pallas import tpu as pltpu

F32 = jnp.float32
BF = jnp.bfloat16
MESH = pl.DeviceIdType.MESH

D = 1024
L = 4
NSH = 4
SHW = 3 * D
NIN = NSH * SHW
GRP = 8
BLK = 128
NPG = 4
PGW = D // NPG
POOL_WINDOWS = (2, 4, 8, 16)
HALO = 16
RMS_EPS = 1e-6
LN_EPS = 1e-5
ADAM_LR, ADAM_B1, ADAM_B2, ADAM_EPS, ADAM_WD, ADAM_STEP = 0.001, 0.9, 0.999, 1e-8, 0.01, 10

RC = 16
CW = 512
COLS = tuple(range(0, D, CW))
MIB = 1 << 20

PK_BR = SHW
PK_PW = PK_BR + D
PK_CW = PK_PW + NPG * (PGW // NSH) * PGW // D
PK_ROWS = PK_CW + 16

_C0 = 0.7978845608028654
_C1 = 0.044715


def _pcall(body, **kw):
    return pl.pallas_call(body, **kw)


def _params(sem, vmem_mib):
    return pltpu.CompilerParams(dimension_semantics=sem, vmem_limit_bytes=vmem_mib * MIB)


def _mm(a, b):
    return jnp.dot(a.astype(BF), b.astype(BF), preferred_element_type=F32)


def _mm_nt(a, b):
    return lax.dot_general(a.astype(BF), b.astype(BF), (((1,), (1,)), ((), ())), preferred_element_type=F32)


def _mm_tn(a, b):
    return lax.dot_general(a.astype(BF), b.astype(BF), (((0,), (0,)), ((), ())), preferred_element_type=F32)


def _gelu(x):
    return 0.5 * x * (1.0 + jnp.tanh(_C0 * x * (1.0 + _C1 * x * x)))


def _gelu_fg(x):
    x2 = x * x
    t = jnp.tanh(_C0 * x * (1.0 + _C1 * x2))
    f = 0.5 * x * (1.0 + t)
    g = 0.5 * (1.0 + t) + 0.5 * x * (1.0 - t * t) * (_C0 * (1.0 + 3.0 * _C1 * x2))
    return f, g


def _sigmoid(x):
    return 0.5 * jnp.tanh(0.5 * x) + 0.5


def _silu(x):
    return x * _sigmoid(x)


def _silu_fg(x):
    s = _sigmoid(x)
    return x * s, s * (1.0 + x * (1.0 - s))


def _rowsum(v):
    return jnp.sum(v, axis=1, keepdims=True)


def _chunks(n_rows, fn, unroll=2):
    def body(c, carry):
        fn(pl.multiple_of(c * RC, RC))
        return carry
    lax.fori_loop(0, n_rows // RC, body, 0, unroll=unroll)


def _trail(e, win):
    s, sh = e, 1
    while sh < win:
        s = s + pltpu.roll(s, sh, 0)
        sh *= 2
    return s


def _lead(e, win):
    n = e.shape[0]
    s, sh = e, 1
    while sh < win:
        s = s + pltpu.roll(s, n - sh, 0)
        sh *= 2
    return s


def _inv_count(pos0, win):
    pos = pos0 + lax.broadcasted_iota(jnp.int32, (RC, 1), 0)
    return 1.0 / jnp.minimum(pos + 1, win).astype(F32)


def _masked_sgu(sgw_ref, g):
    r = lax.broadcasted_iota(jnp.int32, (BLK, BLK), 0) // 64
    c = lax.broadcasted_iota(jnp.int32, (BLK, BLK), 1) // 64
    return jnp.where(c <= r, sgw_ref[g], 0.0)


class _AsF32:
    def __init__(self, ref, col0=0):
        self.ref, self.col0 = ref, col0

    def __getitem__(self, idx):
        if self.col0:
            rows, cols = idx
            idx = (rows, slice(cols.start + self.col0, cols.stop + self.col0))
        return self.ref[idx].astype(F32)


class _From:
    def __init__(self, ref, lead0):
        self.ref, self.lead0 = ref, lead0

    def __setitem__(self, idx, val):
        self.ref[(self.lead0 + idx[0],) + tuple(idx[1:])] = val


def _tile(t, want):
    return min(t, want)


def _fwd_in(x, ng, wl, lng, lnb, sgw, sgbT, pwb, pb, psc, cw, cb):
    T = x.shape[0]
    tm = _tile(T, 256)
    nT = T // tm

    def body(x_ref, ng_ref, w_hbm, lng_ref, lnb_ref, sgw_ref, sgbT_ref, pw_ref, pb_ref, psc_ref, cw_ref, cb_ref,
             proj_hbm, h_ref, y_ref, w_s, pja, pjb, pjc, pjd, qa, qb, qc, qd, h_s, vn_s, mix_s, d_s, wm_s, extp_s, extc_s,
             wsem, psem):
        i = pl.program_id(0)
        pj = (pja, pjb, pjc, pjd)
        kept = (qa, qb, qc, qd)

        def w_copy(p):
            return pltpu.make_async_copy(w_hbm.at[p, pl.ds(0, SHW), :], w_s.at[p], wsem.at[p])

        def p_copy(p):
            dst = proj_hbm.at[pl.ds(pl.multiple_of(i * tm, tm), tm), pl.ds(p * SHW, SHW)]
            return pltpu.make_async_copy(kept[p], dst, psem.at[p])

        def keep(p):
            kept[p][...] = pj[p][...].astype(BF)
            p_copy(p).start()

        def all_rows(fn):
            for c in range(tm // RC):
                fn(c * RC)

        def buffer_free(p):
            @pl.when(i == 0)
            def _():
                w_copy(p).wait()

            @pl.when(i > 0)
            def _():
                p_copy(p).wait()

        def project(p, blocks):
            for b in blocks:
                pj[p][:, b * D:(b + 1) * D] = jnp.dot(h_s[...], w_s[p, b * D:(b + 1) * D, :],
                                                      preferred_element_type=F32)

        @pl.when(i == 0)
        def _():
            for p in range(NSH):
                w_copy(p).start()
            extp_s[0:HALO, :] = jnp.zeros((HALO, D), F32)
            extc_s[0:HALO, :] = jnp.zeros((HALO, D), F32)
            for g in range(GRP):
                wm_s[g] = _masked_sgu(sgw_ref, g).astype(BF)

        def norm_rows(r0):
            rows = pl.ds(r0, RC)
            xs = [x_ref[rows, c0:c0 + CW] for c0 in COLS]
            ms = sum(_rowsum(v * v) for v in xs) * (1.0 / D)
            r = lax.rsqrt(ms + RMS_EPS)
            for k, c0 in enumerate(COLS):
                hb = (xs[k] * r * ng_ref[:, c0:c0 + CW]).astype(BF)
                h_s[rows, c0:c0 + CW] = hb
                h_ref[rows, c0:c0 + CW] = hb

        def gating(first, rest):
            proj_ref = pj[0]
            first()

            def f1(r0):
                rows = pl.ds(r0, RC)
                gv = [_gelu(proj_ref[rows, D + c0:D + c0 + CW]) for c0 in COLS]
                mu = sum(_rowsum(v) for v in gv) * (1.0 / D)
                dv = [v - mu for v in gv]
                var = sum(_rowsum(v * v) for v in dv) * (1.0 / D)
                rstd = lax.rsqrt(var + LN_EPS)
                for k, c0 in enumerate(COLS):
                    vn_s[rows, c0:c0 + CW] = (dv[k] * rstd * lng_ref[:, c0:c0 + CW] + lnb_ref[:, c0:c0 + CW]).astype(BF)
            all_rows(f1)
            for g in range(GRP):
                w = wm_s[g]
                bcol = sgbT_ref[:, g:g + 1]
                gc = slice(g * BLK, (g + 1) * BLK)
                for n in range(tm // BLK):
                    rr = slice(n * BLK, (n + 1) * BLK)
                    mix_s[rr, gc] = jnp.dot(w, vn_s[rr, gc], preferred_element_type=F32) + bcol
            rest()

            def f2(r0):
                rows = pl.ds(r0, RC)
                for c0 in COLS:
                    au = proj_ref[rows, c0:c0 + CW]
                    az = proj_ref[rows, 2 * D + c0:2 * D + c0 + CW]
                    y_ref[0, rows, c0:c0 + CW] = (_gelu(au) * mix_s[rows, c0:c0 + CW] * _silu(az)).astype(BF)
            all_rows(f2)

        def pooling(first, rest):
            proj_ref = pj[1]
            first()
            extp_s[HALO:HALO + tm, :] = proj_ref[:, 0:D]

            def f1(r0):
                rows = pl.ds(r0, RC)
                e = extp_s[pl.ds(r0, 2 * RC), :]
                for k, win in enumerate(POOL_WINDOWS):
                    gc = slice(k * PGW, (k + 1) * PGW)
                    s = _trail(e[:, gc], win)[RC:2 * RC]
                    d_s[rows, gc] = (s * _inv_count(i * tm + r0, win) - e[RC:2 * RC, gc]).astype(BF)
            all_rows(f1)
            extp_s[0:HALO, :] = extp_s[tm:tm + HALO, :]
            for k in range(NPG):
                gc = slice(k * PGW, (k + 1) * PGW)
                mix_s[:, gc] = jnp.dot(d_s[:, gc], pw_ref[k], preferred_element_type=F32)
            rest()

            def f2(r0):
                rows = pl.ds(r0, RC)
                for c0 in COLS:
                    cs = slice(c0, c0 + CW)
                    bz = proj_ref[rows, D + c0:D + c0 + CW]
                    y_ref[1, rows, cs] = ((mix_s[rows, cs] + pb_ref[:, cs]) * psc_ref[:, cs] * _silu(bz)).astype(BF)
            all_rows(f2)

        def convolution(first, rest):
            proj_ref = pj[2]
            first()

            def f1(r0):
                rows = pl.ds(r0, RC)
                extc_s[pl.ds(HALO + r0, RC), :] = proj_ref[rows, D:2 * D] * pj[1][rows, 2 * D:3 * D]
            all_rows(f1)
            rest()

            def f2(r0):
                rows = pl.ds(r0, RC)
                for c0 in COLS:
                    cs = slice(c0, c0 + CW)
                    e = extc_s[pl.ds(r0, 2 * RC), cs]
                    conv = (cw_ref[2:3, cs] * e + cw_ref[1:2, cs] * pltpu.roll(e, 1, 0)
                            + cw_ref[0:1, cs] * pltpu.roll(e, 2, 0))[RC:2 * RC] + cb_ref[:, cs]
                    cbv = proj_ref[rows, cs]
                    cz = proj_ref[rows, 2 * D + c0:2 * D + c0 + CW]
                    y_ref[2, rows, cs] = (cbv * conv * _silu(cz)).astype(BF)
            all_rows(f2)
            extc_s[0:HALO, :] = extc_s[tm:tm + HALO, :]

        all_rows(norm_rows)
        buffer_free(0)
        project(0, (0, 1, 2))
        keep(0)
        for p, mixer, early in ((1, gating, 2), (2, pooling, 1), (3, convolution, 1)):
            buffer_free(p)
            mixer(functools.partial(project, p, range(early)), functools.partial(project, p, range(early, 3)))
            keep(p)

        @pl.when(i == nT - 1)
        def _():
            for p in range(NSH):
                p_copy(p).wait()

    vec = pl.BlockSpec((1, D), lambda i: (0, 0))
    return _pcall(
        body, name="fwd_in", grid=(nT,),
        in_specs=[
            pl.BlockSpec((tm, D), lambda i: (i, 0)), vec, ANY, vec, vec,
            pl.BlockSpec((GRP, BLK, BLK), lambda i: (0, 0, 0)),
            pl.BlockSpec((BLK, GRP), lambda i: (0, 0)),
            pl.BlockSpec((NPG, PGW, PGW), lambda i: (0, 0, 0)),
            vec, vec,
            pl.BlockSpec((3, D), lambda i: (0, 0)), vec,
        ],
        out_specs=[ANY, pl.BlockSpec((tm, D), lambda i: (i, 0)), pl.BlockSpec((3, tm, D), lambda i: (0, i, 0))],
        out_shape=[jax.ShapeDtypeStruct((T, NIN), BF), jax.ShapeDtypeStruct((T, D), BF),
                   jax.ShapeDtypeStruct((3, T, D), BF)],
        scratch_shapes=[
            pltpu.VMEM((NSH, SHW, D), BF),
            pltpu.VMEM((tm, SHW), F32), pltpu.VMEM((tm, SHW), F32), pltpu.VMEM((tm, SHW), F32), pltpu.VMEM((tm, SHW), F32),
            pltpu.VMEM((tm, SHW), BF), pltpu.VMEM((tm, SHW), BF), pltpu.VMEM((tm, SHW), BF), pltpu.VMEM((tm, SHW), BF),
            pltpu.VMEM((tm, D), BF), pltpu.VMEM((tm, D), BF), pltpu.VMEM((tm, D), F32),
            pltpu.VMEM((tm, D), BF), pltpu.VMEM((GRP, BLK, BLK), BF),
            pltpu.VMEM((tm + HALO, D), F32), pltpu.VMEM((tm + HALO, D), F32),
            pltpu.SemaphoreType.DMA((NSH,)), pltpu.SemaphoreType.DMA((NSH,)),
        ],
        compiler_params=_params(("arbitrary",), 58),
    )(x, ng, wl, lng, lnb, sgw, sgbT, pwb, pb, psc, cw, cb)


def _branch_spec(which):
    q = D // NSH
    return lambda *g: (0, PK_BR // q + (which(*g) if callable(which) else which), 0)


def _fwd_out(y3, proj, x, wl):
    T = x.shape[0]
    tm = _tile(T, 256)
    q = D // NSH

    nT = T // tm

    def body(y_ref, gl_ref, x_ref, wa_ref, wb_ref, wc_ref, wo_ref, p_ref, mg_ref, xo_ref, even_s, odd_s):
        i = pl.program_id(0)
        gl_ref = _AsF32(gl_ref)

        def branches(buf):
            for k, w_ref in enumerate((wa_ref, wb_ref, wc_ref)):
                buf[k] = jnp.dot(y_ref[k], w_ref[...].reshape(D, D), preferred_element_type=F32)

        def merge(buf):
            for c in range(tm // RC):
                rows = pl.ds(c * RC, RC)
                for c0 in COLS:
                    cs = slice(c0, c0 + CW)
                    m = 0.0
                    for k in range(3):
                        pk = buf[k, rows, cs]
                        p_ref[k, rows, cs] = pk.astype(BF)
                        m = m + _sigmoid(gl_ref[rows, k * D + c0:k * D + c0 + CW]) * pk
                    mg_ref[rows, cs] = m.astype(BF)
            xo_ref[...] = x_ref[...] + jnp.dot(mg_ref[...], wo_ref[...].reshape(D, D), preferred_element_type=F32)

        bufs = (even_s, odd_s)

        @pl.when(i == 0)
        def _():
            branches(bufs[0])

        for par in (0, 1):
            @pl.when((i % 2 == par) & (i > 0) & (i < nT))
            def _():
                branches(bufs[par])
                merge(bufs[1 - par])

        @pl.when(i == nT)
        def _():
            merge(bufs[(nT - 1) % 2])

    prev = lambda i: jnp.maximum(i - 1, 0)
    return _pcall(
        body, name="fwd_out", grid=(nT + 1,),
        in_specs=[
            pl.BlockSpec((3, tm, D), lambda i: (0, jnp.minimum(i, nT - 1), 0)),
            pl.BlockSpec((tm, SHW), lambda i: (prev(i), 3)),
            pl.BlockSpec((tm, D), lambda i: (prev(i), 0)),
        ] + [pl.BlockSpec((NSH, q, D), _branch_spec(k)) for k in range(4)],
        out_specs=[
            pl.BlockSpec((3, tm, D), lambda i: (0, prev(i), 0)),
            pl.BlockSpec((tm, D), lambda i: (prev(i), 0)),
            pl.BlockSpec((tm, D), lambda i: (prev(i), 0)),
        ],
        out_shape=[jax.ShapeDtypeStruct((3, T, D), BF), jax.ShapeDtypeStruct((T, D), BF),
                   jax.ShapeDtypeStruct((T, D), F32)],
        scratch_shapes=[pltpu.VMEM((3, tm, D), F32), pltpu.VMEM((3, tm, D), F32)],
        compiler_params=_params(("arbitrary",), 52),
    )(y3, proj, x, wl, wl, wl, wl)


def _loss_head(x, fg, tgt):
    T = x.shape[0]
    tm = _tile(T, 512)

    def body(x_ref, g_ref, t_ref, dx_ref, sq_ref, dg_ref, acc_s):
        i = pl.program_id(0)

        @pl.when(i == 0)
        def _():
            acc_s[...] = jnp.zeros_like(acc_s)

        def f(r0):
            rows = pl.ds(r0, RC)
            xs = [x_ref[rows, c0:c0 + CW] for c0 in COLS]
            r = lax.rsqrt(sum(_rowsum(v * v) for v in xs) * (1.0 / D) + RMS_EPS)
            xh = [v * r for v in xs]
            dyg, m = [], 0.0
            for k, c0 in enumerate(COLS):
                cs = slice(c0, c0 + CW)
                err = xh[k] * g_ref[:, cs] - t_ref[rows, cs]
                acc_s[0, :, cs] += err * err
                dy = err * (1.0 / D)
                acc_s[1, :, cs] += dy * xh[k]
                dyg.append(dy * g_ref[:, cs])
                m = m + _rowsum(dyg[k] * xh[k])
            m = m * (1.0 / D)
            for k, c0 in enumerate(COLS):
                dx_ref[rows, c0:c0 + CW] = r * (dyg[k] - xh[k] * m)
        _chunks(tm, f)

        @pl.when(i == pl.num_programs(0) - 1)
        def _():
            sq_ref[...] = jnp.sum(acc_s[0], axis=0, keepdims=True)
            dg_ref[...] = jnp.sum(acc_s[1], axis=0, keepdims=True)

    vec = pl.BlockSpec((1, D), lambda i: (0, 0))
    tile = pl.BlockSpec((tm, D), lambda i: (i, 0))
    return _pcall(
        body, name="loss_head", grid=(T // tm,),
        in_specs=[tile, vec, tile], out_specs=[tile, vec, vec],
        out_shape=[jax.ShapeDtypeStruct((T, D), F32), jax.ShapeDtypeStruct((1, D), F32), jax.ShapeDtypeStruct((1, D), F32)],
        scratch_shapes=[pltpu.VMEM((2, RC, D), F32)],
        compiler_params=_params(("arbitrary",), 32),
    )(x, fg, tgt)


def _bwd_mix(dxo, p3, proj, wl, lng, lnb, sgw, sgbT, pwb, pb, psc, cw, cb):
    T = dxo.shape[0]
    tm = _tile(T, 256)
    nT = T // tm
    hb = tm // HALO

    def body(dxo_ref, p_ref, pj2_ref, ch_ref, bpp_ref, ccp_ref, chp_ref, w_ref, w2_ref, lng_ref, lnb_ref, sgw_ref,
             sgbT_ref, pw_ref, pb_ref, psc_ref, cw_ref, cb_ref,
             dpj2_ref, dp_ref, dsw_ref, dsbT_ref, vec_ref, dpw_ref,
             dy_s, dy2_s, ext_s, nxt_s, halo_s, dch_s, d_s, t_s, dy0_s, vn_s, dmix_s, xh_s, rstd_s, acc_s, accb_s,
             wm_s, wmT_s):
        i = pl.program_id(0)
        step = pl.program_id(1)
        ti = nT - 1 - i
        p_ref, ch_ref, bpp_ref, ccp_ref, chp_ref = (_AsF32(r) for r in (p_ref, ch_ref, bpp_ref, ccp_ref, chp_ref))

        @pl.when((i == 0) & (step == 0))
        def _():
            dsw_ref[...] = jnp.zeros_like(dsw_ref)
            dsbT_ref[...] = jnp.zeros_like(dsbT_ref)
            vec_ref[...] = jnp.zeros_like(vec_ref)
            dpw_ref[...] = jnp.zeros_like(dpw_ref)
            halo_s[...] = jnp.zeros_like(halo_s)
            for g in range(GRP):
                wm = _masked_sgu(sgw_ref, g)
                wm_s[g] = wm.astype(BF)
                wmT_s[g] = wm.T.astype(BF)

        def flush(n_acc, rows_of):
            for a in range(n_acc):
                vec_ref[rows_of[a]:rows_of[a] + 1, :] += jnp.sum(acc_s[a], axis=0, keepdims=True)

        def all_rows(fn):
            for c in range(tm // RC):
                fn(c * RC)

        @pl.when(step == 0)
        def _():
            pj_ref, dpj_ref = _AsF32(pj2_ref, SHW), _From(dpj2_ref, 3)
            dy_s[...] = _mm_nt(dxo_ref[...], w_ref[...].reshape(D, D))

            def f(r0):
                rows = pl.ds(r0, RC)
                for c0 in COLS:
                    cs = slice(c0, c0 + CW)
                    dm = dy_s[rows, cs]
                    for k in range(3):
                        s = _sigmoid(pj_ref[rows, k * D + c0:k * D + c0 + CW])
                        dp_ref[k, rows, cs] = (s * dm).astype(BF)
                        dpj_ref[k, rows, cs] = (dm * p_ref[k, rows, cs] * s * (1.0 - s)).astype(BF)
            _chunks(tm, f)

        @pl.when(step == 0)
        def _():
            pj_ref, dpj_ref = _AsF32(pj2_ref), _From(dpj2_ref, 0)
            dy_s[...] = _mm_nt(dp_ref[2], w2_ref[...].reshape(D, D))
            acc_s[...] = jnp.zeros_like(acc_s)
            ext_s[0:HALO, :] = jnp.where(ti > 0, ccp_ref[...] * chp_ref[...], 0.0)
            nxt_s[tm:tm + HALO, :] = halo_s[0]

            def f1(r0):
                rows = pl.ds(r0, RC)
                ext_s[pl.ds(HALO + r0, RC), :] = pj_ref[rows, D:2 * D] * ch_ref[rows, :]
            _chunks(tm, f1, unroll=4)

            def f2(r0):
                rows = pl.ds(r0, RC)
                for c0 in COLS:
                    cs = slice(c0, c0 + CW)
                    e = ext_s[pl.ds(r0, 2 * RC), cs]
                    e0, e1, e2 = e[RC:2 * RC], pltpu.roll(e, 1, 0)[RC:2 * RC], pltpu.roll(e, 2, 0)[RC:2 * RC]
                    conv = cw_ref[2:3, cs] * e0 + cw_ref[1:2, cs] * e1 + cw_ref[0:1, cs] * e2 + cb_ref[:, cs]
                    cbv = pj_ref[rows, cs]
                    sz, sg = _silu_fg(pj_ref[rows, 2 * D + c0:2 * D + c0 + CW])
                    dyc = dy_s[rows, cs]
                    dconv = dyc * cbv * sz
                    dpj_ref[0, rows, cs] = (dyc * conv * sz).astype(BF)
                    dpj_ref[2, rows, cs] = (dyc * cbv * conv * sg).astype(BF)
                    nxt_s[rows, cs] = dconv
                    acc_s[0, :, cs] += dconv
                    acc_s[1, :, cs] += dconv * e2
                    acc_s[2, :, cs] += dconv * e1
                    acc_s[3, :, cs] += dconv * e0
            _chunks(tm, f2)

            def f3(r0):
                rows = pl.ds(r0, RC)
                for c0 in COLS:
                    cs = slice(c0, c0 + CW)
                    e = nxt_s[pl.ds(r0, 2 * RC), cs]
                    dcc = (cw_ref[2:3, cs] * e + cw_ref[1:2, cs] * pltpu.roll(e, 2 * RC - 1, 0)
                           + cw_ref[0:1, cs] * pltpu.roll(e, 2 * RC - 2, 0))[0:RC]
                    dpj_ref[1, rows, cs] = (dcc * ch_ref[rows, cs]).astype(BF)
                    dch_s[rows, cs] = dcc * pj_ref[rows, D + c0:D + c0 + CW]
            _chunks(tm, f3)
            halo_s[0] = nxt_s[0:HALO, :]
            flush(4, (4, 5, 6, 7))

        @pl.when(step == 1)
        def _():
            pj_ref, dpj_ref = _AsF32(pj2_ref, SHW), _From(dpj2_ref, 3)
            dy_s[...] = _mm_nt(dp_ref[1], w_ref[...].reshape(D, D))
            dy2_s[...] = _mm_nt(dp_ref[0], w2_ref[...].reshape(D, D))
            acc_s[...] = jnp.zeros_like(acc_s)
            ext_s[0:HALO, :] = jnp.where(ti > 0, bpp_ref[...], 0.0)
            ext_s[HALO:HALO + tm, :] = pj_ref[:, 0:D]
            nxt_s[tm:tm + HALO, :] = halo_s[1]

            def f1(r0):
                rows = pl.ds(r0, RC)
                e = ext_s[pl.ds(r0, 2 * RC), :]
                for k, win in enumerate(POOL_WINDOWS):
                    gc = slice(k * PGW, (k + 1) * PGW)
                    s = _trail(e[:, gc], win)[RC:2 * RC]
                    d_s[rows, gc] = (s * _inv_count(ti * tm + r0, win) - e[RC:2 * RC, gc]).astype(BF)
            all_rows(f1)
            for k in range(NPG):
                gc = slice(k * PGW, (k + 1) * PGW)
                t_s[:, gc] = jnp.dot(d_s[:, gc], pw_ref[k], preferred_element_type=F32)

            def f2(r0):
                rows = pl.ds(r0, RC)
                for c0 in COLS:
                    cs = slice(c0, c0 + CW)
                    y0 = t_s[rows, cs] + pb_ref[:, cs]
                    sz, sg = _silu_fg(pj_ref[rows, D + c0:D + c0 + CW])
                    dyb = dy_s[rows, cs]
                    dy0 = dyb * psc_ref[:, cs] * sz
                    acc_s[0, :, cs] += dy0
                    acc_s[1, :, cs] += dyb * y0 * sz
                    dy0_s[rows, cs] = dy0.astype(BF)
                    dpj_ref[1, rows, cs] = (dyb * y0 * psc_ref[:, cs] * sg).astype(BF)
            all_rows(f2)
            for k in range(NPG):
                gc = slice(k * PGW, (k + 1) * PGW)
                dpw_ref[k] += _mm_tn(d_s[:, gc], dy0_s[:, gc])
                t_s[:, gc] = _mm_nt(dy0_s[:, gc], pw_ref[k])

            def f3(r0):
                rows = pl.ds(r0, RC)
                for k, win in enumerate(POOL_WINDOWS):
                    gc = slice(k * PGW, (k + 1) * PGW)
                    nxt_s[rows, gc] = t_s[rows, gc] * _inv_count(ti * tm + r0, win)
            all_rows(f3)

            def f4(r0):
                rows = pl.ds(r0, RC)
                e = nxt_s[pl.ds(r0, 2 * RC), :]
                for k, win in enumerate(POOL_WINDOWS):
                    gc = slice(k * PGW, (k + 1) * PGW)
                    dpj_ref[0, rows, gc] = (_lead(e[:, gc], win)[0:RC] - t_s[rows, gc]).astype(BF)
                dpj_ref[2, rows, :] = dch_s[rows, :].astype(BF)
            all_rows(f4)
            halo_s[1] = nxt_s[0:HALO, :]
            flush(2, (2, 3))

        @pl.when(step == 1)
        def _():
            pj_ref, dpj_ref = _AsF32(pj2_ref), _From(dpj2_ref, 0)
            acc_s[...] = jnp.zeros_like(acc_s)
            accb_s[...] = jnp.zeros_like(accb_s)

            def f1(r0):
                rows = pl.ds(r0, RC)
                gv = [_gelu(pj_ref[rows, D + c0:D + c0 + CW]) for c0 in COLS]
                mu = sum(_rowsum(v) for v in gv) * (1.0 / D)
                dv = [v - mu for v in gv]
                var = sum(_rowsum(v * v) for v in dv) * (1.0 / D)
                rstd = lax.rsqrt(var + LN_EPS)
                rstd_s[rows, :] = jnp.broadcast_to(rstd, (RC, BLK))
                for k, c0 in enumerate(COLS):
                    cs = slice(c0, c0 + CW)
                    xh = dv[k] * rstd
                    xh_s[rows, cs] = xh
                    vn_s[rows, cs] = (xh * lng_ref[:, cs] + lnb_ref[:, cs]).astype(BF)
            _chunks(tm, f1, unroll=4)
            for g in range(GRP):
                bcol = sgbT_ref[:, g:g + 1]
                gc = slice(g * BLK, (g + 1) * BLK)
                for n in range(tm // BLK):
                    rr = slice(n * BLK, (n + 1) * BLK)
                    t_s[rr, gc] = jnp.dot(wm_s[g], vn_s[rr, gc], preferred_element_type=F32) + bcol

            def f2(r0):
                rows = pl.ds(r0, RC)
                brow = pl.ds(pl.multiple_of(r0 % BLK, RC), RC)
                for c0 in COLS:
                    cs = slice(c0, c0 + CW)
                    gu, ggu = _gelu_fg(pj_ref[rows, cs])
                    sz, sg = _silu_fg(pj_ref[rows, 2 * D + c0:2 * D + c0 + CW])
                    dya = dy2_s[rows, cs]
                    mix = t_s[rows, cs]
                    dmix = dya * gu * sz
                    dpj_ref[0, rows, cs] = (dya * mix * sz * ggu).astype(BF)
                    dpj_ref[2, rows, cs] = (dya * gu * mix * sg).astype(BF)
                    dmix_s[rows, cs] = dmix.astype(BF)
                    accb_s[brow, cs] += dmix
            _chunks(tm, f2)
            for g in range(GRP):
                gc = slice(g * BLK, (g + 1) * BLK)
                dsbT_ref[:, g:g + 1] += _rowsum(accb_s[:, gc])
                for n in range(tm // BLK):
                    rr = slice(n * BLK, (n + 1) * BLK)
                    t_s[rr, gc] = jnp.dot(wmT_s[g], dmix_s[rr, gc], preferred_element_type=F32)
                    dsw_ref[g] += _mm_nt(dmix_s[rr, gc], vn_s[rr, gc])

            def f3(r0):
                rows = pl.ds(r0, RC)
                rstd = rstd_s[rows, 0:1]
                dxh, m1, m2 = [], 0.0, 0.0
                for k, c0 in enumerate(COLS):
                    cs = slice(c0, c0 + CW)
                    dvn = t_s[rows, cs]
                    xh = xh_s[rows, cs]
                    acc_s[0, :, cs] += dvn * xh
                    acc_s[1, :, cs] += dvn
                    dxh.append(dvn * lng_ref[:, cs])
                    m1 = m1 + _rowsum(dxh[k])
                    m2 = m2 + _rowsum(dxh[k] * xh)
                m1 = m1 * (1.0 / D)
                m2 = m2 * (1.0 / D)
                for k, c0 in enumerate(COLS):
                    cs = slice(c0, c0 + CW)
                    _, ggv = _gelu_fg(pj_ref[rows, D + c0:D + c0 + CW])
                    dpj_ref[1, rows, cs] = (rstd * (dxh[k] - m1 - xh_s[rows, cs] * m2) * ggv).astype(BF)
            _chunks(tm, f3)
            flush(2, (0, 1))

            @pl.when(i == nT - 1)
            def _():
                for g in range(GRP):
                    r = lax.broadcasted_iota(jnp.int32, (BLK, BLK), 0) // 64
                    c = lax.broadcasted_iota(jnp.int32, (BLK, BLK), 1) // 64
                    dsw_ref[g] = jnp.where(c <= r, dsw_ref[g], 0.0)

    def prev(col):
        return pl.BlockSpec((HALO, D), lambda i, p: (jnp.maximum((nT - 1 - i) * hb - 1, 0), col))

    vec = pl.BlockSpec((1, D), lambda i, p: (0, 0))
    const3 = lambda i, p: (0, 0, 0)
    return _pcall(
        body, name="bwd_mix", grid=(nT, 2),
        in_specs=[
            pl.BlockSpec((tm, D), lambda i, p: (nT - 1 - i, 0)),
            pl.BlockSpec((3, tm, D), lambda i, p: (0, nT - 1 - i, 0)),
            pl.BlockSpec((tm, 2 * SHW), lambda i, p: (nT - 1 - i, 1 - p)),
            pl.BlockSpec((tm, D), lambda i, p: (nT - 1 - i, 5)),
            prev(3), prev(7), prev(5),
            pl.BlockSpec((NSH, D // NSH, D), _branch_spec(lambda i, p: 3 - 2 * p)),
            pl.BlockSpec((NSH, D // NSH, D), _branch_spec(lambda i, p: 2 - 2 * p)),
            vec, vec,
            pl.BlockSpec((GRP, BLK, BLK), const3),
            pl.BlockSpec((BLK, GRP), lambda i, p: (0, 0)),
            pl.BlockSpec((NPG, PGW, PGW), const3),
            vec, vec,
            pl.BlockSpec((3, D), lambda i, p: (0, 0)), vec,
        ],
        out_specs=[
            pl.BlockSpec((6, tm, D), lambda i, p: (1 - p, nT - 1 - i, 0)),
            pl.BlockSpec((3, tm, D), lambda i, p: (0, nT - 1 - i, 0)),
            pl.BlockSpec((GRP, BLK, BLK), const3),
            pl.BlockSpec((BLK, GRP), lambda i, p: (0, 0)),
            pl.BlockSpec((8, D), lambda i, p: (0, 0)),
            pl.BlockSpec((NPG, PGW, PGW), const3),
        ],
        out_shape=[
            jax.ShapeDtypeStruct((12, T, D), BF), jax.ShapeDtypeStruct((3, T, D), BF),
            jax.ShapeDtypeStruct((GRP, BLK, BLK), F32), jax.ShapeDtypeStruct((BLK, GRP), F32),
            jax.ShapeDtypeStruct((8, D), F32), jax.ShapeDtypeStruct((NPG, PGW, PGW), F32),
        ],
        scratch_shapes=[
            pltpu.VMEM((tm, D), F32),
            pltpu.VMEM((tm, D), F32),
            pltpu.VMEM((tm + HALO, D), F32),
            pltpu.VMEM((tm + HALO, D), F32),
            pltpu.VMEM((2, HALO, D), F32),
            pltpu.VMEM((tm, D), F32),
            pltpu.VMEM((tm, D), BF),
            pltpu.VMEM((tm, D), F32),
            pltpu.VMEM((tm, D), BF),
            pltpu.VMEM((tm, D), BF),
            pltpu.VMEM((tm, D), BF),
            pltpu.VMEM((tm, D), F32),
            pltpu.VMEM((tm, BLK), F32),
            pltpu.VMEM((4, RC, D), F32),
            pltpu.VMEM((BLK, D), F32),
            pltpu.VMEM((GRP, BLK, BLK), BF), pltpu.VMEM((GRP, BLK, BLK), BF),
        ],
        compiler_params=_params(("arbitrary", "arbitrary"), 56),
    )(dxo, p3, proj, proj, proj, proj, proj, wl, wl, lng, lnb, sgw, sgbT, pwb, pb, psc, cw, cb)


def _bwd_in(dproj, wl, x, dxo, ng):
    T = x.shape[0]
    tm = _tile(T, 512)

    nT = T // tm

    def body(dpj_ref, w_ref, x_ref, dxo_ref, ng_ref, dx_ref, dng_ref, cur, prev, g_s):
        i = pl.program_id(0)
        j = pl.program_id(1)

        def part():
            return sum(_mm_nt(dpj_ref[b], w_ref[b * D:(b + 1) * D, :]) for b in range(3))

        def finish_rows(r0):
            rows = pl.ds(r0, RC)
            xs = [x_ref[rows, c0:c0 + CW] for c0 in COLS]
            r = lax.rsqrt(sum(_rowsum(v * v) for v in xs) * (1.0 / D) + RMS_EPS)
            xh = [v * r for v in xs]
            dhg, m = [], 0.0
            for k, c0 in enumerate(COLS):
                cs = slice(c0, c0 + CW)
                dh = prev[rows, cs]
                g_s[:, cs] += dh * xh[k]
                dhg.append(dh * ng_ref[:, cs])
                m = m + _rowsum(dhg[k] * xh[k])
            m = m * (1.0 / D)
            for k, c0 in enumerate(COLS):
                cs = slice(c0, c0 + CW)
                dx_ref[rows, cs] = dxo_ref[rows, cs] + r * (dhg[k] - xh[k] * m)

        def finish():
            for c in range(tm // RC):
                finish_rows(c * RC)

        @pl.when((i == 0) & (j == 0))
        def _():
            g_s[...] = jnp.zeros_like(g_s)
            cur[...] = part()

        @pl.when((i > 0) & (i < nT) & (j == 0))
        def _():
            cur[...] = part()
            finish()

        @pl.when((i < nT) & (j > 0))
        def _():
            cur[...] += part()

        @pl.when((i < nT) & (j == NSH - 1))
        def _():
            prev[...] = cur[...]

        @pl.when((i == nT) & (j == 0))
        def _():
            finish()
            dng_ref[...] = jnp.sum(g_s[...], axis=0, keepdims=True)

    vec = pl.BlockSpec((1, D), lambda i, j: (0, 0))
    tile = pl.BlockSpec((tm, D), lambda i, j: (jnp.maximum(i - 1, 0), 0))
    return _pcall(
        body, name="bwd_in", grid=(nT + 1, NSH),
        in_specs=[
            pl.BlockSpec((3, tm, D), lambda i, j: (j, jnp.minimum(i, nT - 1), 0)),
            pl.BlockSpec((None, SHW, D), lambda i, j: (j, 0, 0)),
            tile, tile, vec,
        ],
        out_specs=[tile, vec],
        out_shape=[jax.ShapeDtypeStruct((T, D), F32), jax.ShapeDtypeStruct((1, D), F32)],
        scratch_shapes=[pltpu.VMEM((tm, D), F32), pltpu.VMEM((tm, D), F32), pltpu.VMEM((RC, D), F32)],
        compiler_params=_params(("arbitrary", "arbitrary"), 52),
    )(dproj, wl, x, dxo, ng)


def _tn_grad(a3, b3, split, out_map, name, into=None, after=None):
    nb, T, _ = b3.shape
    tk = _tile(T, 2048)
    nk = T // tk
    rows = D // split
    a_batched = a3.shape[0] > 1

    def body(a_ref, b_ref, *rest):
        o_ref, acc_s = rest[-2:]
        k = pl.program_id(1)

        @pl.when(k == 0)
        def _():
            acc_s[...] = _mm_tn(a_ref[...], b_ref[...])

        @pl.when((k > 0) & (k < nk))
        def _():
            acc_s[...] += _mm_tn(a_ref[...], b_ref[...])

        @pl.when(k >= nk - 1)
        def _():
            r0 = pl.multiple_of((k - (nk - 1)) * rows, rows)
            o_ref[...] = acc_s[pl.ds(r0, rows), :].astype(o_ref.dtype)

    def tok(k):
        return jnp.minimum(k, nk - 1)

    extra = ([] if into is None else [into]) + ([] if after is None else [after])
    return _pcall(
        body, name=name, grid=(nb, nk + split - 1),
        in_specs=[
            pl.BlockSpec((None, tk, D), (lambda n, k: (n, tok(k), 0)) if a_batched else (lambda n, k: (0, tok(k), 0))),
            pl.BlockSpec((None, tk, D), lambda n, k: (n, tok(k), 0)),
        ] + [ANY] * len(extra),
        out_specs=pl.BlockSpec((None, rows, D), lambda n, k: out_map(n, jnp.maximum(k - (nk - 1), 0))),
        out_shape=jax.ShapeDtypeStruct((NSH, PK_ROWS, D), BF),
        input_output_aliases={} if into is None else {2: 0},
        scratch_shapes=[pltpu.VMEM((D, D), F32)],
        compiler_params=_params(("arbitrary", "arbitrary"), 56),
    )(a3, b3, *extra)


def _place():
    x, y, c = lax.axis_index("x"), lax.axis_index("y"), lax.axis_index("c")
    chips = [(1 - x, y), (x, 1 - y), (1 - x, 1 - y)]
    return x, y, c, chips


def _peers(reach):
    x, y, c, chips = _place()
    if reach == "chips":
        return 2 * x + y, [((px, py, c), 2 * px + py) for px, py in chips]
    others = [(x, y, 1 - c)] + [(px, py, pc) for px, py in chips for pc in (c, 1 - c)]
    return 4 * x + 2 * y + c, [(pr, 4 * pr[0] + 2 * pr[1] + pr[2]) for pr in others]


ANY = pl.BlockSpec(memory_space=pl.ANY)


HBM = pl.BlockSpec(memory_space=pltpu.HBM)
SEM = pl.BlockSpec(memory_space=pltpu.SEMAPHORE)
EFFECT = pltpu.SideEffectType.DATAFLOW_SIDE_EFFECTING


def _own_slot(src, from_slot, name):
    rows = src.shape[-2]
    rb = rows // 9
    me = (2 * lax.axis_index("x") + lax.axis_index("y")).astype(jnp.int32).reshape(1)

    def body(me_ref, src_ref, land_ref):
        land_ref[...] = src_ref[...]

    if from_slot:
        src_spec = pl.BlockSpec((None, rb, D), lambda i, me_ref: (me_ref[0], i, 0))
    else:
        src_spec = pl.BlockSpec((rb, D), lambda i, me_ref: (i, 0))
    return _pcall(
        body, name=name,
        grid_spec=pltpu.PrefetchScalarGridSpec(
            num_scalar_prefetch=1, grid=(rows // rb,), in_specs=[src_spec],
            out_specs=pl.BlockSpec((None, rb, D), lambda i, me_ref: (me_ref[0], i, 0))),
        out_shape=jax.ShapeDtypeStruct((NSH, rows, D), src.dtype),
        compiler_params=_params(("arbitrary",), 32),
    )(me, src)


def _push_start(name, srcs, lands, per_peer, reach="chips"):
    n = len(srcs)
    npeer = 3 if reach == "chips" else 7
    ns = n * npeer

    def body(*refs):
        src, land = refs[:n], refs[n:2 * n]
        ssem, rsem = refs[2 * n:2 * n + ns], refs[2 * n + ns:2 * n + 2 * ns]
        token = refs[-1]
        me, peers = _peers(reach)
        for i in range(n):
            for j, (peer, slot) in enumerate(peers):
                pltpu.make_async_remote_copy(
                    src_ref=src[i].at[slot] if per_peer else src[i], dst_ref=land[i].at[me],
                    send_sem=ssem[npeer * i + j], recv_sem=rsem[npeer * i + j], device_id=peer, device_id_type=MESH).start()
        token[...] = jnp.zeros_like(token)

    ops = list(srcs) + list(lands)
    out = _pcall(
        body, name=name,
        out_shape=tuple([pltpu.SemaphoreType.DMA(())] * (2 * ns) + [pltpu.HBM(a.shape, a.dtype) for a in ops]
                        + [jax.ShapeDtypeStruct((8, 128), F32)]),
        in_specs=[HBM] * (2 * n),
        out_specs=tuple([SEM] * (2 * ns) + [HBM] * (2 * n) + [pl.BlockSpec(memory_space=pltpu.VMEM)]),
        input_output_aliases={i: 2 * ns + i for i in range(2 * n)},
        compiler_params=pltpu.CompilerParams(has_side_effects=EFFECT),
    )(*[pltpu.with_memory_space_constraint(a, pltpu.HBM) for a in ops])
    return out[:ns], out[ns:2 * ns], out[2 * ns:2 * ns + n], out[2 * ns + n:2 * ns + 2 * n], out[-1]


def _push_wait(name, src, land, ssem, rsem, after, per_peer, reach="chips"):
    npeer = len(ssem)

    def body(src_ref, land_ref, *rest):
        sems = rest[:2 * npeer]
        _, peers = _peers(reach)
        for j, (peer, slot) in enumerate(peers):
            cp = pltpu.make_async_remote_copy(
                src_ref=src_ref.at[slot] if per_peer else src_ref, dst_ref=land_ref.at[slot], send_sem=sems[j],
                recv_sem=sems[npeer + j], device_id=peer, device_id_type=MESH)
            cp.wait_send()
            cp.wait_recv()

    return _pcall(
        body, name=name,
        out_shape=(pltpu.HBM(src.shape, src.dtype), pltpu.HBM(land.shape, land.dtype)),
        in_specs=[HBM, HBM] + [SEM] * (2 * npeer) + [ANY], out_specs=(HBM, HBM),
        input_output_aliases={0: 0, 1: 1},
        compiler_params=pltpu.CompilerParams(has_side_effects=EFFECT),
    )(src, land, *ssem, *rsem, after)[1]


def _swap_sibling(arrs):
    n = len(arrs)

    def body(*refs):
        src, dst = refs[:n], refs[n:2 * n]
        ssem, rsem = refs[2 * n:]
        x, y, c, _ = _place()
        cps = [pltpu.make_async_remote_copy(src_ref=src[a], dst_ref=dst[a], send_sem=ssem.at[a], recv_sem=rsem.at[a],
                                            device_id=(x, y, 1 - c), device_id_type=MESH) for a in range(n)]
        for cp in cps:
            cp.start()
        for cp in cps:
            cp.wait()

    return _pcall(
        body, name="swap_sibling",
        in_specs=[ANY] * n, out_specs=[ANY] * n,
        out_shape=[jax.ShapeDtypeStruct(a.shape, a.dtype) for a in arrs],
        scratch_shapes=[pltpu.SemaphoreType.DMA((n,)), pltpu.SemaphoreType.DMA((n,))],
    )(*arrs)


def _gather_all(v):
    def body(src, dst, ssem, rsem, lsem):
        x, y, c, _ = _place()
        me = 4 * x + 2 * y + c
        peers = [(x, y, 1 - c), (1 - x, y, c), (1 - x, y, 1 - c), (x, 1 - y, c), (x, 1 - y, 1 - c),
                 (1 - x, 1 - y, c), (1 - x, 1 - y, 1 - c)]
        local = pltpu.make_async_copy(src, dst.at[me], lsem)
        local.start()
        sends = [pltpu.make_async_remote_copy(src_ref=src, dst_ref=dst.at[me], send_sem=ssem.at[j], recv_sem=rsem.at[j],
                                              device_id=pr, device_id_type=MESH) for j, pr in enumerate(peers)]
        for cp in sends:
            cp.start()
        for j, (px, py, pc) in enumerate(peers):
            pltpu.make_async_remote_copy(src_ref=src, dst_ref=dst.at[4 * px + 2 * py + pc], send_sem=ssem.at[j],
                                         recv_sem=rsem.at[j], device_id=(px, py, pc), device_id_type=MESH).wait_recv()
        for cp in sends:
            cp.wait_send()
        local.wait()

    return _pcall(
        body, name="gather_all", in_specs=[ANY], out_specs=ANY,
        out_shape=jax.ShapeDtypeStruct((8,) + v.shape, v.dtype),
        scratch_shapes=[pltpu.SemaphoreType.DMA((7,)), pltpu.SemaphoreType.DMA((7,)), pltpu.SemaphoreType.DMA(())],
    )(v)


def _sum_slots(r, rb):
    S = r.shape[0]

    def body(r_ref, o_ref):
        acc = r_ref[0].astype(F32)
        for s in range(1, S):
            acc = acc + r_ref[s].astype(F32)
        o_ref[...] = acc

    if r.ndim == 3:
        _, R, C = r.shape
        grid, blk, imap = (R // rb,), (S, rb, C), (lambda i: (0, i, 0))
        oblk, omap = (rb, C), (lambda i: (i, 0))
    else:
        _, K, R, C = r.shape
        grid, blk, imap = (K,), (S, None, R, C), (lambda i: (0, i, 0, 0))
        oblk, omap = (None, R, C), (lambda i: (i, 0, 0))
    return _pcall(
        body, name="sum_slots", grid=grid, in_specs=[pl.BlockSpec(blk, imap)], out_specs=pl.BlockSpec(oblk, omap),
        out_shape=jax.ShapeDtypeStruct(r.shape[1:], F32), compiler_params=_params(("arbitrary",), 48),
    )(r)


def _adamw(gs, g_spec, w, m, v, p_spec, prev, grid):
    ng = len(gs)
    bc1 = 1.0 - ADAM_B1 ** ADAM_STEP
    bc2 = 1.0 - ADAM_B2 ** ADAM_STEP

    def body(*refs):
        g = refs[0][...]
        for a in range(1, ng):
            g = g + refs[a][...]
        w_ref, m_ref, v_ref = refs[ng:ng + 3]
        go, do, mo, vo = refs[ng + 3 + 4:]
        mn = ADAM_B1 * m_ref[...] + (1.0 - ADAM_B1) * g
        vn = ADAM_B2 * v_ref[...] + (1.0 - ADAM_B2) * (g * g)
        go[...] = g
        mo[...] = mn
        vo[...] = vn
        do[...] = -ADAM_LR * ((mn / bc1) / (jnp.sqrt(vn / bc2) + ADAM_EPS) + ADAM_WD * w_ref[...])

    out = jax.ShapeDtypeStruct(w.shape, F32)
    k0 = ng + 3
    return _pcall(
        body, name="adamw", grid=grid,
        in_specs=[g_spec] * ng + [p_spec] * 3 + [ANY] * 4,
        out_specs=[p_spec] * 4, out_shape=[out] * 4,
        input_output_aliases={k0: 0, k0 + 1: 1, k0 + 2: 2, k0 + 3: 3},
        compiler_params=_params(("arbitrary",) * len(grid), 48),
    )(*gs, w, m, v, *prev)


def _empty4(w):
    return tuple(lax.empty(w.shape, F32) for _ in range(4))


N_SGW = L * GRP * BLK * BLK // D
O_NG, O_VEC, O_SGB, O_FG, O_SGW = 0, 8, 32, 40, 48
O_CW = O_SGW + N_SGW
N_PACK = O_CW + 16
PACK_RB = N_PACK // 3


def _pad_to(a, rows):
    return jnp.pad(a, ((0, rows - a.shape[0]), (0, 0)))


def _pack_small(ng, vecs, sgb, fg, sgw, cw):
    parts = [_pad_to(ng, 8), _pad_to(vecs.reshape(L * 5, D), 24), _pad_to(sgb.reshape(L, D), 8),
             _pad_to(fg.reshape(1, D), 8), sgw.reshape(N_SGW, D), _pad_to(cw, 16)]
    return jnp.concatenate(parts, axis=0)


def kernel(x, norm_g, w_in, sgu_ln_g, sgu_ln_b, sgu_w, sgu_b, pool_w, pool_b, pool_scale, conv_w, conv_b, w_branch_a, w_branch_b, w_branch_c, w_out, final_g, loss_target, m_norm_g, m_w_in, m_sgu_ln_g, m_sgu_ln_b, m_sgu_w, m_sgu_b, m_pool_w, m_pool_b, m_pool_scale, m_conv_w, m_conv_b, m_w_branch_a, m_w_branch_b, m_w_branch_c, m_w_out, m_final_g, v_norm_g, v_w_in, v_sgu_ln_g, v_sgu_ln_b, v_sgu_w, v_sgu_b, v_pool_w, v_pool_b, v_pool_scale, v_conv_w, v_conv_b, v_w_branch_a, v_w_branch_b, v_w_branch_c, v_w_out, v_final_g):
    cx, cy = lax.axis_index("x"), lax.axis_index("y")
    me = 2 * cx + cy
    xl, tgt = x[0], loss_target[0]
    q = D // NSH

    wq = w_in.astype(BF).reshape(L, D, 3, D).transpose(0, 2, 1, 3).reshape(L, SHW, D)
    brq = jnp.stack([w_branch_a, w_branch_b, w_branch_c, w_out], axis=1).astype(BF).reshape(L, D, D)
    pwq = pool_w.astype(BF).reshape(L, PK_CW - PK_PW, D)
    cwq = lax.bitcast_convert_type(conv_w, BF).reshape(L, 3 * q * 2)
    cwq = jnp.pad(cwq, ((0, 0), (0, 16 * D - 3 * q * 2))).reshape(L, 16, D)
    packs = [jnp.concatenate([wq[l], brq[l], pwq[l], cwq[l]], axis=0) for l in range(L)]
    lands = [_own_slot(packs[l], False, f"ag_own_{l}") for l in range(L)]
    ag_s, ag_r, packs, lands, tok = _push_start("ag_start", packs, lands, False)
    sgbT = sgu_b.transpose(0, 2, 1)

    def layer_weights(l, after):
        wl = _push_wait(f"ag_wait_{l}", packs[l], lands[l], ag_s[3 * l:3 * l + 3], ag_r[3 * l:3 * l + 3], after, False)
        pwb = wl[:, PK_PW:PK_CW].reshape(NSH, NPG, PGW // NSH, PGW).transpose(1, 0, 2, 3).reshape(NPG, PGW, PGW)
        cwb = wl[:, PK_CW:].reshape(NSH, 16 * D)[:, :3 * q * 2].reshape(NSH, 3, q, 2)
        cwf = lax.bitcast_convert_type(cwb, F32).transpose(1, 0, 2).reshape(3, D)
        small = (sgu_ln_g[l:l + 1], sgu_ln_b[l:l + 1], sgu_w[l], sgbT[l], pwb, pool_b[l:l + 1], pool_scale[l:l + 1],
                 cwf, conv_b[l:l + 1])
        return wl, small

    xs, saved, wts = [xl], [], []
    for l in range(L):
        wl, small = layer_weights(l, tok if l == 0 else xs[l])
        wts.append((wl, small))
        proj, h, y3 = _fwd_in(xs[l], norm_g[l:l + 1], wl, *small)
        p3, mg, xo = _fwd_out(y3, proj, xs[l], wl)
        saved.append((proj, h, y3, p3, mg))
        xs.append(xo)

    dx, sq, dfg = _loss_head(xs[L], final_g[None], tgt)
    loss = lax.psum(jnp.sum(sq) * (0.5 / D), ("x", "y", "c"))

    g_w_in = _empty4(w_in)
    g_br = [_empty4(w_out) for _ in range(4)]
    g_pw = _empty4(pool_w)
    dng, dvec, dsgw, dsgb = [None] * L, [None] * L, [None] * L, [None] * L
    branches = [(w_branch_a, m_w_branch_a, v_w_branch_a), (w_branch_b, m_w_branch_b, v_w_branch_b),
                (w_branch_c, m_w_branch_c, v_w_branch_c), (w_out, m_w_out, v_w_out)]
    nb = D // 128

    def finish(l, landed):
        nonlocal g_w_in, g_pw
        mine = _sum_slots(landed, PK_ROWS // 9)
        sums = [mine, _swap_sibling([mine])[0]]
        g_w_in = _adamw(sums, pl.BlockSpec((128, D), lambda b, i: (b * nb + i, 0)), w_in, m_w_in, v_w_in,
                        pl.BlockSpec((None, 128, D), lambda b, i: (l, i, b)), g_w_in, (3, nb))
        for k, (w, m, v) in enumerate(branches):
            g_br[k] = _adamw(sums, pl.BlockSpec((q, D), lambda i, k=k: (PK_BR // q + k, 0)), w, m, v,
                             pl.BlockSpec((None, q, D), lambda i: (l, 0, 0)), g_br[k], (1,))
        pools = [a[PK_PW:PK_CW].reshape(NPG, PGW // NSH, PGW) for a in sums]
        g_pw = _adamw(pools, pl.BlockSpec((None, PGW // NSH, PGW), lambda g: (g, 0, 0)), pool_w, m_pool_w, v_pool_w,
                      pl.BlockSpec((None, None, PGW // NSH, PGW), lambda g: (l, g, 0, 0)), g_pw, (NPG,))

    pend = None
    for l in reversed(range(L)):
        proj, h, y3, p3, mg = saved[l]
        wl, small = wts[l]
        dproj, dp3, dsgw[l], dsbT, dvec[l], dpw = _bwd_mix(dx, p3, proj, wl, *small)
        dsgb[l] = dsbT.T
        if l == 0:
            dv = jnp.stack(dvec)
            part = _pack_small(jnp.zeros((L, D), F32), dv[:, 0:5], jnp.stack(dsgb), dfg[0], jnp.stack(dsgw),
                               dv[:, 5:8].reshape(L * 3, D))
            zone = lax.dynamic_update_slice(lax.empty((8, N_PACK, D), F32), part[None],
                                            (2 * me + lax.axis_index("c"), 0, 0))
            sm_s, sm_r, (part,), (zone,), started = _push_start("small_start", [part], [zone], False, "all")
        grads = _tn_grad(h[None], dproj, 1, lambda n, s: (n // 3, n % 3, 0), "grad_w_in", after=started if l == 0 else None)
        grads = _tn_grad(y3, dp3, NSH, lambda n, s: (s, PK_BR // q + n, 0), "grad_w_branch", into=grads)
        grads = _tn_grad(mg[None], dx[None], NSH, lambda n, s: (s, PK_BR // q + 3, 0), "grad_w_out", into=grads)
        dpq = dpw.astype(BF).reshape(NPG, NSH, PGW // NSH, PGW).transpose(1, 0, 2, 3).reshape(NSH, PK_CW - PK_PW, D)
        grads = lax.dynamic_update_slice(grads, jnp.pad(dpq, ((0, 0), (0, PK_ROWS - PK_CW), (0, 0))), (0, PK_PW, 0))
        if pend is not None:
            landed = _push_wait(f"rs_wait_{pend[0]}", *pend[1:], dproj, True)
        land = _own_slot(grads, True, f"rs_own_{l}")
        ss, rs, (grads,), (land,), tok = _push_start(f"rs_start_{l}", [grads], [land], True)
        if pend is not None:
            finish(pend[0], landed)
        dx, dng[l] = _bwd_in(dproj, wl, xs[l], dx, norm_g[l:l + 1] + tok[0, 0])
        pend = (l, grads, land, ss, rs)
    finish(pend[0], _push_wait(f"rs_wait_{pend[0]}", *pend[1:], dx, True))

    zone = _push_wait("small_wait", part, zone, sm_s, sm_r, dx, False, "all")
    gng = _sum_slots(_gather_all(_pad_to(jnp.concatenate(dng), O_VEC)), O_VEC)
    gsmall = jnp.concatenate([gng, _sum_slots(zone, PACK_RB)[O_VEC:]])
    gcw =lax.dynamic_slice_in_dim(gsmall[O_CW:O_CW + L * 3], me * q, q, axis=1)
    gpack = jnp.concatenate([gsmall[:O_CW], _pad_to(gcw.reshape(L * 3 * q // D, D), 16)])

    def pack(ng, lg, lb, sw, sb, pb_, ps, cwv, cb_, fg):
        return _pack_small(ng, jnp.stack([lg, lb, pb_, ps, cb_], axis=1), sb, fg, sw, cwv.reshape(L * 3 * q // D, D))

    wp = pack(norm_g, sgu_ln_g, sgu_ln_b, sgu_w, sgu_b, pool_b, pool_scale, conv_w, conv_b, final_g)
    mp = pack(m_norm_g, m_sgu_ln_g, m_sgu_ln_b, m_sgu_w, m_sgu_b, m_pool_b, m_pool_scale, m_conv_w, m_conv_b, m_final_g)
    vp = pack(v_norm_g, v_sgu_ln_g, v_sgu_ln_b, v_sgu_w, v_sgu_b, v_pool_b, v_pool_scale, v_conv_w, v_conv_b, v_final_g)
    rows = pl.BlockSpec((PACK_RB, D), lambda i: (i, 0))
    sm = _adamw([gpack], rows, wp, mp, vp, rows, _empty4(wp), (N_PACK // PACK_RB,))

    def unpack(a):
        vv = a[O_VEC:O_VEC + L * 5].reshape(L, 5, D)
        sb = a[O_SGB:O_SGB + L].reshape(L, GRP, BLK)
        fg = a[O_FG]
        sw = a[O_SGW:O_SGW + N_SGW].reshape(L, GRP, BLK, BLK)
        cwv = a[O_CW:O_CW + L * 3 * q // D].reshape(L, 3, q)
        return dict(norm_g=a[O_NG:O_NG + L], w_in=None, sgu_ln_g=vv[:, 0], sgu_ln_b=vv[:, 1], sgu_w=sw, sgu_b=sb, pool_w=None,
                    pool_b=vv[:, 2], pool_scale=vv[:, 3], conv_w=cwv, conv_b=vv[:, 4], w_branch_a=None,
                    w_branch_b=None, w_branch_c=None, w_out=None, final_g=fg)

    outs = [loss, dx[None]]
    for kind in range(4):
        d = unpack(sm[kind])
        d.update(w_in=g_w_in[kind], pool_w=g_pw[kind], w_branch_a=g_br[0][kind], w_branch_b=g_br[1][kind],
                 w_branch_c=g_br[2][kind], w_out=g_br[3][kind])
        outs.extend(d[n] for n in ("norm_g", "w_in", "sgu_ln_g", "sgu_ln_b", "sgu_w", "sgu_b", "pool_w", "pool_b",
                                   "pool_scale", "conv_w", "conv_b", "w_branch_a", "w_branch_b", "w_branch_c", "w_out",
                                   "final_g"))
    return tuple(outs)
```

```python
import functools

import jax
import jax.numpy as jnp
from jax import lax
from jax.experimental import pallas as pl
from jax.experimental.pallas import tpu as pltpu

F32 = jnp.float32
BF = jnp.bfloat16
MESH = pl.DeviceIdType.MESH

D = 1024
L = 4
NSH = 4
SHW = 3 * D
NIN = NSH * SHW
GRP = 8
BLK = 128
NPG = 4
PGW = D // NPG
POOL_WINDOWS = (2, 4, 8, 16)
HALO = 16
RMS_EPS = 1e-6
LN_EPS = 1e-5
ADAM_LR, ADAM_B1, ADAM_B2, ADAM_EPS, ADAM_WD, ADAM_STEP = 0.001, 0.9, 0.999, 1e-8, 0.01, 10

RC = 16
CW = 512
COLS = tuple(range(0, D, CW))
MIB = 1 << 20

PK_BR = SHW
PK_PW = PK_BR + D
PK_CW = PK_PW + NPG * (PGW // NSH) * PGW // D
PK_ROWS = PK_CW + 16

_C0 = 0.7978845608028654
_C1 = 0.044715


def _pcall(body, **kw):
    return pl.pallas_call(body, **kw)


def _params(sem, vmem_mib):
    return pltpu.CompilerParams(dimension_semantics=sem, vmem_limit_bytes=vmem_mib * MIB)


def _mm(a, b):
    return jnp.dot(a.astype(BF), b.astype(BF), preferred_element_type=F32)


def _mm_nt(a, b):
    return lax.dot_general(a.astype(BF), b.astype(BF), (((1,), (1,)), ((), ())), preferred_element_type=F32)


def _mm_tn(a, b):
    return lax.dot_general(a.astype(BF), b.astype(BF), (((0,), (0,)), ((), ())), preferred_element_type=F32)


def _gelu(x):
    return 0.5 * x * (1.0 + jnp.tanh(_C0 * x * (1.0 + _C1 * x * x)))


def _gelu_fg(x):
    x2 = x * x
    t = jnp.tanh(_C0 * x * (1.0 + _C1 * x2))
    f = 0.5 * x * (1.0 + t)
    g = 0.5 * (1.0 + t) + 0.5 * x * (1.0 - t * t) * (_C0 * (1.0 + 3.0 * _C1 * x2))
    return f, g


def _sigmoid(x):
    return 0.5 * jnp.tanh(0.5 * x) + 0.5


def _silu(x):
    return x * _sigmoid(x)


def _silu_fg(x):
    s = _sigmoid(x)
    return x * s, s * (1.0 + x * (1.0 - s))


def _rowsum(v):
    return jnp.sum(v, axis=1, keepdims=True)


def _chunks(n_rows, fn, unroll=2):
    def body(c, carry):
        fn(pl.multiple_of(c * RC, RC))
        return carry
    lax.fori_loop(0, n_rows // RC, body, 0, unroll=unroll)


def _trail(e, win):
    s, sh = e, 1
    while sh < win:
        s = s + pltpu.roll(s, sh, 0)
        sh *= 2
    return s


def _lead(e, win):
    n = e.shape[0]
    s, sh = e, 1
    while sh < win:
        s = s + pltpu.roll(s, n - sh, 0)
        sh *= 2
    return s


def _inv_count(pos0, win):
    pos = pos0 + lax.broadcasted_iota(jnp.int32, (RC, 1), 0)
    return 1.0 / jnp.minimum(pos + 1, win).astype(F32)


def _masked_sgu(sgw_ref, g):
    r = lax.broadcasted_iota(jnp.int32, (BLK, BLK), 0) // 64
    c = lax.broadcasted_iota(jnp.int32, (BLK, BLK), 1) // 64
    return jnp.where(c <= r, sgw_ref[g], 0.0)


class _AsF32:
    def __init__(self, ref, col0=0):
        self.ref, self.col0 = ref, col0

    def __getitem__(self, idx):
        if self.col0:
            rows, cols = idx
            idx = (rows, slice(cols.start + self.col0, cols.stop + self.col0))
        return self.ref[idx].astype(F32)


class _From:
    def __init__(self, ref, lead0):
        self.ref, self.lead0 = ref, lead0

    def __setitem__(self, idx, val):
        self.ref[(self.lead0 + idx[0],) + tuple(idx[1:])] = val


def _tile(t, want):
    return min(t, want)


def _fwd_in(x, ng, wl, lng, lnb, sgw, sgbT, pwb, pb, psc, cw, cb):
    T = x.shape[0]
    tm = _tile(T, 256)
    nT = T // tm

    def body(x_ref, ng_ref, w_hbm, lng_ref, lnb_ref, sgw_ref, sgbT_ref, pw_ref, pb_ref, psc_ref, cw_ref, cb_ref,
             proj_hbm, h_ref, y_ref, w_s, pja, pjb, pjc, pjd, qa, qb, qc, qd, h_s, vn_s, mix_s, d_s, wm_s, extp_s, extc_s,
             wsem, psem):
        i = pl.program_id(0)
        pj = (pja, pjb, pjc, pjd)
        kept = (qa, qb, qc, qd)

        def w_copy(p):
            return pltpu.make_async_copy(w_hbm.at[p, pl.ds(0, SHW), :], w_s.at[p], wsem.at[p])

        def p_copy(p):
            dst = proj_hbm.at[pl.ds(pl.multiple_of(i * tm, tm), tm), pl.ds(p * SHW, SHW)]
            return pltpu.make_async_copy(kept[p], dst, psem.at[p])

        def keep(p):
            kept[p][...] = pj[p][...].astype(BF)
            p_copy(p).start()

        def all_rows(fn):
            for c in range(tm // RC):
                fn(c * RC)

        def buffer_free(p):
            @pl.when(i == 0)
            def _():
                w_copy(p).wait()

            @pl.when(i > 0)
            def _():
                p_copy(p).wait()

        def project(p, blocks):
            for b in blocks:
                pj[p][:, b * D:(b + 1) * D] = jnp.dot(h_s[...], w_s[p, b * D:(b + 1) * D, :],
                                                      preferred_element_type=F32)

        @pl.when(i == 0)
        def _():
            for p in range(NSH):
                w_copy(p).start()
            extp_s[0:HALO, :] = jnp.zeros((HALO, D), F32)
            extc_s[0:HALO, :] = jnp.zeros((HALO, D), F32)
            for g in range(GRP):
                wm_s[g] = _masked_sgu(sgw_ref, g).astype(BF)

        def norm_rows(r0):
            rows = pl.ds(r0, RC)
            xs = [x_ref[rows, c0:c0 + CW] for c0 in COLS]
            ms = sum(_rowsum(v * v) for v in xs) * (1.0 / D)
            r = lax.rsqrt(ms + RMS_EPS)
            for k, c0 in enumerate(COLS):
                hb = (xs[k] * r * ng_ref[:, c0:c0 + CW]).astype(BF)
                h_s[rows, c0:c0 + CW] = hb
                h_ref[rows, c0:c0 + CW] = hb

        def gating(first, rest):
            proj_ref = pj[0]
            first()

            def f1(r0):
                rows = pl.ds(r0, RC)
                gv = [_gelu(proj_ref[rows, D + c0:D + c0 + CW]) for c0 in COLS]
                mu = sum(_rowsum(v) for v in gv) * (1.0 / D)
                dv = [v - mu for v in gv]
                var = sum(_rowsum(v * v) for v in dv) * (1.0 / D)
                rstd = lax.rsqrt(var + LN_EPS)
                for k, c0 in enumerate(COLS):
                    vn_s[rows, c0:c0 + CW] = (dv[k] * rstd * lng_ref[:, c0:c0 + CW] + lnb_ref[:, c0:c0 + CW]).astype(BF)
            all_rows(f1)
            for g in range(GRP):
                w = wm_s[g]
                bcol = sgbT_ref[:, g:g + 1]
                gc = slice(g * BLK, (g + 1) * BLK)
                for n in range(tm // BLK):
                    rr = slice(n * BLK, (n + 1) * BLK)
                    mix_s[rr, gc] = jnp.dot(w, vn_s[rr, gc], preferred_element_type=F32) + bcol
            rest()

            def f2(r0):
                rows = pl.ds(r0, RC)
                for c0 in COLS:
                    au = proj_ref[rows, c0:c0 + CW]
                    az = proj_ref[rows, 2 * D + c0:2 * D + c0 + CW]
                    y_ref[0, rows, c0:c0 + CW] = (_gelu(au) * mix_s[rows, c0:c0 + CW] * _silu(az)).astype(BF)
            all_rows(f2)

        def pooling(first, rest):
            proj_ref = pj[1]
            first()
            extp_s[HALO:HALO + tm, :] = proj_ref[:, 0:D]

            def f1(r0):
                rows = pl.ds(r0, RC)
                e = extp_s[pl.ds(r0, 2 * RC), :]
                for k, win in enumerate(POOL_WINDOWS):
                    gc = slice(k * PGW, (k + 1) * PGW)
                    s = _trail(e[:, gc], win)[RC:2 * RC]
                    d_s[rows, gc] = (s * _inv_count(i * tm + r0, win) - e[RC:2 * RC, gc]).astype(BF)
            all_rows(f1)
            extp_s[0:HALO, :] = extp_s[tm:tm + HALO, :]
            for k in range(NPG):
                gc = slice(k * PGW, (k + 1) * PGW)
                mix_s[:, gc] = jnp.dot(d_s[:, gc], pw_ref[k], preferred_element_type=F32)
            rest()

            def f2(r0):
                rows = pl.ds(r0, RC)
                for c0 in COLS:
                    cs = slice(c0, c0 + CW)
                    bz = proj_ref[rows, D + c0:D + c0 + CW]
                    y_ref[1, rows, cs] = ((mix_s[rows, cs] + pb_ref[:, cs]) * psc_ref[:, cs] * _silu(bz)).astype(BF)
            all_rows(f2)

        def convolution(first, rest):
            proj_ref = pj[2]
            first()

            def f1(r0):
                rows = pl.ds(r0, RC)
                extc_s[pl.ds(HALO + r0, RC), :] = proj_ref[rows, D:2 * D] * pj[1][rows, 2 * D:3 * D]
            all_rows(f1)
            rest()

            def f2(r0):
                rows = pl.ds(r0, RC)
                for c0 in COLS:
                    cs = slice(c0, c0 + CW)
                    e = extc_s[pl.ds(r0, 2 * RC), cs]
                    conv = (cw_ref[2:3, cs] * e + cw_ref[1:2, cs] * pltpu.roll(e, 1, 0)
                            + cw_ref[0:1, cs] * pltpu.roll(e, 2, 0))[RC:2 * RC] + cb_ref[:, cs]
                    cbv = proj_ref[rows, cs]
                    cz = proj_ref[rows, 2 * D + c0:2 * D + c0 + CW]
                    y_ref[2, rows, cs] = (cbv * conv * _silu(cz)).astype(BF)
            all_rows(f2)
            extc_s[0:HALO, :] = extc_s[tm:tm + HALO, :]

        all_rows(norm_rows)
        buffer_free(0)
        project(0, (0, 1, 2))
        keep(0)
        for p, mixer, early in ((1, gating, 2), (2, pooling, 1), (3, convolution, 1)):
            buffer_free(p)
            mixer(functools.partial(project, p, range(early)), functools.partial(project, p, range(early, 3)))
            keep(p)

        @pl.when(i == nT - 1)
        def _():
            for p in range(NSH):
                p_copy(p).wait()

    vec = pl.BlockSpec((1, D), lambda i: (0, 0))
    return _pcall(
        body, name="fwd_in", grid=(nT,),
        in_specs=[
            pl.BlockSpec((tm, D), lambda i: (i, 0)), vec, ANY, vec, vec,
            pl.BlockSpec((GRP, BLK, BLK), lambda i: (0, 0, 0)),
            pl.BlockSpec((BLK, GRP), lambda i: (0, 0)),
            pl.BlockSpec((NPG, PGW, PGW), lambda i: (0, 0, 0)),
            vec, vec,
            pl.BlockSpec((3, D), lambda i: (0, 0)), vec,
        ],
        out_specs=[ANY, pl.BlockSpec((tm, D), lambda i: (i, 0)), pl.BlockSpec((3, tm, D), lambda i: (0, i, 0))],
        out_shape=[jax.ShapeDtypeStruct((T, NIN), BF), jax.ShapeDtypeStruct((T, D), BF),
                   jax.ShapeDtypeStruct((3, T, D), BF)],
        scratch_shapes=[
            pltpu.VMEM((NSH, SHW, D), BF),
            pltpu.VMEM((tm, SHW), F32), pltpu.VMEM((tm, SHW), F32), pltpu.VMEM((tm, SHW), F32), pltpu.VMEM((tm, SHW), F32),
            pltpu.VMEM((tm, SHW), BF), pltpu.VMEM((tm, SHW), BF), pltpu.VMEM((tm, SHW), BF), pltpu.VMEM((tm, SHW), BF),
            pltpu.VMEM((tm, D), BF), pltpu.VMEM((tm, D), BF), pltpu.VMEM((tm, D), F32),
            pltpu.VMEM((tm, D), BF), pltpu.VMEM((GRP, BLK, BLK), BF),
            pltpu.VMEM((tm + HALO, D), F32), pltpu.VMEM((tm + HALO, D), F32),
            pltpu.SemaphoreType.DMA((NSH,)), pltpu.SemaphoreType.DMA((NSH,)),
        ],
        compiler_params=_params(("arbitrary",), 58),
    )(x, ng, wl, lng, lnb, sgw, sgbT, pwb, pb, psc, cw, cb)


def _branch_spec(which):
    q = D // NSH
    return lambda *g: (0, PK_BR // q + (which(*g) if callable(which) else which), 0)


def _fwd_out(y3, proj, x, wl):
    T = x.shape[0]
    tm = _tile(T, 256)
    q = D // NSH

    nT = T // tm

    def body(y_ref, gl_ref, x_ref, wa_ref, wb_ref, wc_ref, wo_ref, p_ref, mg_ref, xo_ref, even_s, odd_s):
        i = pl.program_id(0)
        gl_ref = _AsF32(gl_ref)

        def branches(buf):
            for k, w_ref in enumerate((wa_ref, wb_ref, wc_ref)):
                buf[k] = jnp.dot(y_ref[k], w_ref[...].reshape(D, D), preferred_element_type=F32)

        def merge(buf):
            for c in range(tm // RC):
                rows = pl.ds(c * RC, RC)
                for c0 in COLS:
                    cs = slice(c0, c0 + CW)
                    m = 0.0
                    for k in range(3):
                        pk = buf[k, rows, cs]
                        p_ref[k, rows, cs] = pk.astype(BF)
                        m = m + _sigmoid(gl_ref[rows, k * D + c0:k * D + c0 + CW]) * pk
                    mg_ref[rows, cs] = m.astype(BF)
            xo_ref[...] = x_ref[...] + jnp.dot(mg_ref[...], wo_ref[...].reshape(D, D), preferred_element_type=F32)

        bufs = (even_s, odd_s)

        @pl.when(i == 0)
        def _():
            branches(bufs[0])

        for par in (0, 1):
            @pl.when((i % 2 == par) & (i > 0) & (i < nT))
            def _():
                branches(bufs[par])
                merge(bufs[1 - par])

        @pl.when(i == nT)
        def _():
            merge(bufs[(nT - 1) % 2])

    prev = lambda i: jnp.maximum(i - 1, 0)
    return _pcall(
        body, name="fwd_out", grid=(nT + 1,),
        in_specs=[
            pl.BlockSpec((3, tm, D), lambda i: (0, jnp.minimum(i, nT - 1), 0)),
            pl.BlockSpec((tm, SHW), lambda i: (prev(i), 3)),
            pl.BlockSpec((tm, D), lambda i: (prev(i), 0)),
        ] + [pl.BlockSpec((NSH, q, D), _branch_spec(k)) for k in range(4)],
        out_specs=[
            pl.BlockSpec((3, tm, D), lambda i: (0, prev(i), 0)),
            pl.BlockSpec((tm, D), lambda i: (prev(i), 0)),
            pl.BlockSpec((tm, D), lambda i: (prev(i), 0)),
        ],
        out_shape=[jax.ShapeDtypeStruct((3, T, D), BF), jax.ShapeDtypeStruct((T, D), BF),
                   jax.ShapeDtypeStruct((T, D), F32)],
        scratch_shapes=[pltpu.VMEM((3, tm, D), F32), pltpu.VMEM((3, tm, D), F32)],
        compiler_params=_params(("arbitrary",), 52),
    )(y3, proj, x, wl, wl, wl, wl)


def _loss_head(x, fg, tgt):
    T = x.shape[0]
    tm = _tile(T, 512)

    def body(x_ref, g_ref, t_ref, dx_ref, sq_ref, dg_ref, acc_s):
        i = pl.program_id(0)

        @pl.when(i == 0)
        def _():
            acc_s[...] = jnp.zeros_like(acc_s)

        def f(r0):
            rows = pl.ds(r0, RC)
            xs = [x_ref[rows, c0:c0 + CW] for c0 in COLS]
            r = lax.rsqrt(sum(_rowsum(v * v) for v in xs) * (1.0 / D) + RMS_EPS)
            xh = [v * r for v in xs]
            dyg, m = [], 0.0
            for k, c0 in enumerate(COLS):
                cs = slice(c0, c0 + CW)
                err = xh[k] * g_ref[:, cs] - t_ref[rows, cs]
                acc_s[0, :, cs] += err * err
                dy = err * (1.0 / D)
                acc_s[1, :, cs] += dy * xh[k]
                dyg.append(dy * g_ref[:, cs])
                m = m + _rowsum(dyg[k] * xh[k])
            m = m * (1.0 / D)
            for k, c0 in enumerate(COLS):
                dx_ref[rows, c0:c0 + CW] = r * (dyg[k] - xh[k] * m)
        _chunks(tm, f)

        @pl.when(i == pl.num_programs(0) - 1)
        def _():
            sq_ref[...] = jnp.sum(acc_s[0], axis=0, keepdims=True)
            dg_ref[...] = jnp.sum(acc_s[1], axis=0, keepdims=True)

    vec = pl.BlockSpec((1, D), lambda i: (0, 0))
    tile = pl.BlockSpec((tm, D), lambda i: (i, 0))
    return _pcall(
        body, name="loss_head", grid=(T // tm,),
        in_specs=[tile, vec, tile], out_specs=[tile, vec, vec],
        out_shape=[jax.ShapeDtypeStruct((T, D), F32), jax.ShapeDtypeStruct((1, D), F32), jax.ShapeDtypeStruct((1, D), F32)],
        scratch_shapes=[pltpu.VMEM((2, RC, D), F32)],
        compiler_params=_params(("arbitrary",), 32),
    )(x, fg, tgt)


def _bwd_mix(dxo, p3, proj, wl, lng, lnb, sgw, sgbT, pwb, pb, psc, cw, cb):
    T = dxo.shape[0]
    tm = _tile(T, 256)
    nT = T // tm
    hb = tm // HALO

    def body(dxo_ref, p_ref, pj2_ref, ch_ref, bpp_ref, ccp_ref, chp_ref, w_ref, w2_ref, w3_ref, lng_ref, lnb_ref, sgw_ref,
             sgbT_ref, pw_ref, pb_ref, psc_ref, cw_ref, cb_ref,
             dpj2_ref, dp_ref, dsw_ref, dsbT_ref, vec_ref, dpw_ref,
             dy_s, dy2_s, dy3_s, dyc_s, ext_s, nxt_s, halo_s, dch_s, d_s, t_s, dy0_s, vn_s, dmix_s, xh_s, rstd_s, acc_s, accb_s,
             wm_s, wmT_s):
        i = pl.program_id(0)
        step = pl.program_id(1)
        ti = nT - 1 - i
        p_ref, ch_ref, bpp_ref, ccp_ref, chp_ref = (_AsF32(r) for r in (p_ref, ch_ref, bpp_ref, ccp_ref, chp_ref))

        @pl.when((i == 0) & (step == 0))
        def _():
            dsw_ref[...] = jnp.zeros_like(dsw_ref)
            dsbT_ref[...] = jnp.zeros_like(dsbT_ref)
            vec_ref[...] = jnp.zeros_like(vec_ref)
            dpw_ref[...] = jnp.zeros_like(dpw_ref)
            halo_s[...] = jnp.zeros_like(halo_s)
            for g in range(GRP):
                wm = _masked_sgu(sgw_ref, g)
                wm_s[g] = wm.astype(BF)
                wmT_s[g] = wm.T.astype(BF)

        def flush(n_acc, rows_of):
            for a in range(n_acc):
                vec_ref[rows_of[a]:rows_of[a] + 1, :] += jnp.sum(acc_s[a], axis=0, keepdims=True)

        def all_rows(fn):
            for c in range(tm // RC):
                fn(c * RC)

        @pl.when(step == 0)
        def _():
            pj_ref, dpj_ref = _AsF32(pj2_ref, SHW), _From(dpj2_ref, 3)
            dy_s[...] = _mm_nt(dxo_ref[...], w_ref[...].reshape(D, D))

            def gate(k, r0):
                rows = pl.ds(r0, RC)
                for c0 in COLS:
                    cs = slice(c0, c0 + CW)
                    dm = dy_s[rows, cs]
                    s = _sigmoid(pj_ref[rows, k * D + c0:k * D + c0 + CW])
                    dp_ref[k, rows, cs] = (s * dm).astype(BF)
                    dpj_ref[k, rows, cs] = (dm * p_ref[k, rows, cs] * s * (1.0 - s)).astype(BF)

            all_rows(functools.partial(gate, 2))
            dyc_s[...] = _mm_nt(dp_ref[2], w2_ref[...].reshape(D, D))
            all_rows(functools.partial(gate, 0))
            all_rows(functools.partial(gate, 1))

        @pl.when(step == 0)
        def _():
            pj_ref, dpj_ref = _AsF32(pj2_ref), _From(dpj2_ref, 0)
            dy3_s[...] = _mm_nt(dp_ref[1], w3_ref[...].reshape(D, D))
            acc_s[...] = jnp.zeros_like(acc_s)
            ext_s[0:HALO, :] = jnp.where(ti > 0, ccp_ref[...] * chp_ref[...], 0.0)
            nxt_s[tm:tm + HALO, :] = halo_s[0]

            def f1(r0):
                rows = pl.ds(r0, RC)
                ext_s[pl.ds(HALO + r0, RC), :] = pj_ref[rows, D:2 * D] * ch_ref[rows, :]
            all_rows(f1)

            def f2(r0):
                rows = pl.ds(r0, RC)
                for c0 in COLS:
                    cs = slice(c0, c0 + CW)
                    e = ext_s[pl.ds(r0, 2 * RC), cs]
                    e0, e1, e2 = e[RC:2 * RC], pltpu.roll(e, 1, 0)[RC:2 * RC], pltpu.roll(e, 2, 0)[RC:2 * RC]
                    conv = cw_ref[2:3, cs] * e0 + cw_ref[1:2, cs] * e1 + cw_ref[0:1, cs] * e2 + cb_ref[:, cs]
                    cbv = pj_ref[rows, cs]
                    sz, sg = _silu_fg(pj_ref[rows, 2 * D + c0:2 * D + c0 + CW])
                    dyc = dyc_s[rows, cs]
                    dconv = dyc * cbv * sz
                    dpj_ref[0, rows, cs] = (dyc * conv * sz).astype(BF)
                    dpj_ref[2, rows, cs] = (dyc * cbv * conv * sg).astype(BF)
                    nxt_s[rows, cs] = dconv
                    acc_s[0, :, cs] += dconv
                    acc_s[1, :, cs] += dconv * e2
                    acc_s[2, :, cs] += dconv * e1
                    acc_s[3, :, cs] += dconv * e0
            all_rows(f2)

            def f3(r0):
                rows = pl.ds(r0, RC)
                for c0 in COLS:
                    cs = slice(c0, c0 + CW)
                    e = nxt_s[pl.ds(r0, 2 * RC), cs]
                    dcc = (cw_ref[2:3, cs] * e + cw_ref[1:2, cs] * pltpu.roll(e, 2 * RC - 1, 0)
                           + cw_ref[0:1, cs] * pltpu.roll(e, 2 * RC - 2, 0))[0:RC]
                    dpj_ref[1, rows, cs] = (dcc * ch_ref[rows, cs]).astype(BF)
                    dch_s[rows, cs] = dcc * pj_ref[rows, D + c0:D + c0 + CW]
            all_rows(f3)
            halo_s[0] = nxt_s[0:HALO, :]
            flush(4, (4, 5, 6, 7))

        @pl.when(step == 1)
        def _():
            pj_ref, dpj_ref = _AsF32(pj2_ref, SHW), _From(dpj2_ref, 3)
            dy2_s[...] = _mm_nt(dp_ref[0], w2_ref[...].reshape(D, D))
            acc_s[...] = jnp.zeros_like(acc_s)
            ext_s[0:HALO, :] = jnp.where(ti > 0, bpp_ref[...], 0.0)
            ext_s[HALO:HALO + tm, :] = pj_ref[:, 0:D]
            nxt_s[tm:tm + HALO, :] = halo_s[1]

            def f1(r0):
                rows = pl.ds(r0, RC)
                e = ext_s[pl.ds(r0, 2 * RC), :]
                for k, win in enumerate(POOL_WINDOWS):
                    gc = slice(k * PGW, (k + 1) * PGW)
                    s = _trail(e[:, gc], win)[RC:2 * RC]
                    d_s[rows, gc] = (s * _inv_count(ti * tm + r0, win) - e[RC:2 * RC, gc]).astype(BF)
            all_rows(f1)
            for k in range(NPG):
                gc = slice(k * PGW, (k + 1) * PGW)
                t_s[:, gc] = jnp.dot(d_s[:, gc], pw_ref[k], preferred_element_type=F32)

            def f2(r0):
                rows = pl.ds(r0, RC)
                for c0 in COLS:
                    cs = slice(c0, c0 + CW)
                    y0 = t_s[rows, cs] + pb_ref[:, cs]
                    sz, sg = _silu_fg(pj_ref[rows, D + c0:D + c0 + CW])
                    dyb = dy3_s[rows, cs]
                    dy0 = dyb * psc_ref[:, cs] * sz
                    acc_s[0, :, cs] += dy0
                    acc_s[1, :, cs] += dyb * y0 * sz
                    dy0_s[rows, cs] = dy0.astype(BF)
                    dpj_ref[1, rows, cs] = (dyb * y0 * psc_ref[:, cs] * sg).astype(BF)
            all_rows(f2)
            for k in range(NPG):
                gc = slice(k * PGW, (k + 1) * PGW)
                dpw_ref[k] += _mm_tn(d_s[:, gc], dy0_s[:, gc])
                t_s[:, gc] = _mm_nt(dy0_s[:, gc], pw_ref[k])

            def f3(r0):
                rows = pl.ds(r0, RC)
                for k, win in enumerate(POOL_WINDOWS):
                    gc = slice(k * PGW, (k + 1) * PGW)
                    nxt_s[rows, gc] = t_s[rows, gc] * _inv_count(ti * tm + r0, win)
            all_rows(f3)

            def f4(r0):
                rows = pl.ds(r0, RC)
                e = nxt_s[pl.ds(r0, 2 * RC), :]
                for k, win in enumerate(POOL_WINDOWS):
                    gc = slice(k * PGW, (k + 1) * PGW)
                    dpj_ref[0, rows, gc] = (_lead(e[:, gc], win)[0:RC] - t_s[rows, gc]).astype(BF)
                dpj_ref[2, rows, :] = dch_s[rows, :].astype(BF)
            all_rows(f4)
            halo_s[1] = nxt_s[0:HALO, :]
            flush(2, (2, 3))

        @pl.when(step == 1)
        def _():
            pj_ref, dpj_ref = _AsF32(pj2_ref), _From(dpj2_ref, 0)
            acc_s[...] = jnp.zeros_like(acc_s)
            accb_s[...] = jnp.zeros_like(accb_s)

            def f1(r0):
                rows = pl.ds(r0, RC)
                gv = [_gelu(pj_ref[rows, D + c0:D + c0 + CW]) for c0 in COLS]
                mu = sum(_rowsum(v) for v in gv) * (1.0 / D)
                dv = [v - mu for v in gv]
                var = sum(_rowsum(v * v) for v in dv) * (1.0 / D)
                rstd = lax.rsqrt(var + LN_EPS)
                rstd_s[rows, :] = jnp.broadcast_to(rstd, (RC, BLK))
                for k, c0 in enumerate(COLS):
                    cs = slice(c0, c0 + CW)
                    xh = dv[k] * rstd
                    xh_s[rows, cs] = xh
                    vn_s[rows, cs] = (xh * lng_ref[:, cs] + lnb_ref[:, cs]).astype(BF)
            all_rows(f1)
            for g in range(GRP):
                bcol = sgbT_ref[:, g:g + 1]
                gc = slice(g * BLK, (g + 1) * BLK)
                for n in range(tm // BLK):
                    rr = slice(n * BLK, (n + 1) * BLK)
                    t_s[rr, gc] = jnp.dot(wm_s[g], vn_s[rr, gc], preferred_element_type=F32) + bcol

            def f2(r0):
                rows = pl.ds(r0, RC)
                brow = pl.ds(pl.multiple_of(r0 % BLK, RC), RC)
                for c0 in COLS:
                    cs = slice(c0, c0 + CW)
                    gu, ggu = _gelu_fg(pj_ref[rows, cs])
                    sz, sg = _silu_fg(pj_ref[rows, 2 * D + c0:2 * D + c0 + CW])
                    dya = dy2_s[rows, cs]
                    mix = t_s[rows, cs]
                    dmix = dya * gu * sz
                    dpj_ref[0, rows, cs] = (dya * mix * sz * ggu).astype(BF)
                    dpj_ref[2, rows, cs] = (dya * gu * mix * sg).astype(BF)
                    dmix_s[rows, cs] = dmix.astype(BF)
                    accb_s[brow, cs] += dmix
            _chunks(tm, f2)
            for g in range(GRP):
                gc = slice(g * BLK, (g + 1) * BLK)
                dsbT_ref[:, g:g + 1] += _rowsum(accb_s[:, gc])
                for n in range(tm // BLK):
                    rr = slice(n * BLK, (n + 1) * BLK)
                    t_s[rr, gc] = jnp.dot(wmT_s[g], dmix_s[rr, gc], preferred_element_type=F32)
                    dsw_ref[g] += _mm_nt(dmix_s[rr, gc], vn_s[rr, gc])

            def f3(r0):
                rows = pl.ds(r0, RC)
                rstd = rstd_s[rows, 0:1]
                dxh, m1, m2 = [], 0.0, 0.0
                for k, c0 in enumerate(COLS):
                    cs = slice(c0, c0 + CW)
                    dvn = t_s[rows, cs]
                    xh = xh_s[rows, cs]
                    acc_s[0, :, cs] += dvn * xh
                    acc_s[1, :, cs] += dvn
                    dxh.append(dvn * lng_ref[:, cs])
                    m1 = m1 + _rowsum(dxh[k])
                    m2 = m2 + _rowsum(dxh[k] * xh)
                m1 = m1 * (1.0 / D)
                m2 = m2 * (1.0 / D)
                for k, c0 in enumerate(COLS):
                    cs = slice(c0, c0 + CW)
                    _, ggv = _gelu_fg(pj_ref[rows, D + c0:D + c0 + CW])
                    dpj_ref[1, rows, cs] = (rstd * (dxh[k] - m1 - xh_s[rows, cs] * m2) * ggv).astype(BF)
            _chunks(tm, f3)
            flush(2, (0, 1))

            @pl.when(i == nT - 1)
            def _():
                for g in range(GRP):
                    r = lax.broadcasted_iota(jnp.int32, (BLK, BLK), 0) // 64
                    c = lax.broadcasted_iota(jnp.int32, (BLK, BLK), 1) // 64
                    dsw_ref[g] = jnp.where(c <= r, dsw_ref[g], 0.0)

    def prev(col):
        return pl.BlockSpec((HALO, D), lambda i, p: (jnp.maximum((nT - 1 - i) * hb - 1, 0), col))

    vec = pl.BlockSpec((1, D), lambda i, p: (0, 0))
    const3 = lambda i, p: (0, 0, 0)
    return _pcall(
        body, name="bwd_mix", grid=(nT, 2),
        in_specs=[
            pl.BlockSpec((tm, D), lambda i, p: (nT - 1 - i, 0)),
            pl.BlockSpec((3, tm, D), lambda i, p: (0, nT - 1 - i, 0)),
            pl.BlockSpec((tm, 2 * SHW), lambda i, p: (nT - 1 - i, 1 - p)),
            pl.BlockSpec((tm, D), lambda i, p: (nT - 1 - i, 5)),
            prev(3), prev(7), prev(5),
            pl.BlockSpec((NSH, D // NSH, D), _branch_spec(3)),
            pl.BlockSpec((NSH, D // NSH, D), _branch_spec(lambda i, p: 2 - 2 * p)),
            pl.BlockSpec((NSH, D // NSH, D), _branch_spec(1)),
            vec, vec,
            pl.BlockSpec((GRP, BLK, BLK), const3),
            pl.BlockSpec((BLK, GRP), lambda i, p: (0, 0)),
            pl.BlockSpec((NPG, PGW, PGW), const3),
            vec, vec,
            pl.BlockSpec((3, D), lambda i, p: (0, 0)), vec,
        ],
        out_specs=[
            pl.BlockSpec((6, tm, D), lambda i, p: (1 - p, nT - 1 - i, 0)),
            pl.BlockSpec((3, tm, D), lambda i, p: (0, nT - 1 - i, 0)),
            pl.BlockSpec((GRP, BLK, BLK), const3),
            pl.BlockSpec((BLK, GRP), lambda i, p: (0, 0)),
            pl.BlockSpec((8, D), lambda i, p: (0, 0)),
            pl.BlockSpec((NPG, PGW, PGW), const3),
        ],
        out_shape=[
            jax.ShapeDtypeStruct((12, T, D), BF), jax.ShapeDtypeStruct((3, T, D), BF),
            jax.ShapeDtypeStruct((GRP, BLK, BLK), F32), jax.ShapeDtypeStruct((BLK, GRP), F32),
            jax.ShapeDtypeStruct((8, D), F32), jax.ShapeDtypeStruct((NPG, PGW, PGW), F32),
        ],
        scratch_shapes=[
            pltpu.VMEM((tm, D), F32),
            pltpu.VMEM((tm, D), F32),
            pltpu.VMEM((tm, D), F32),
            pltpu.VMEM((tm, D), F32),
            pltpu.VMEM((tm + HALO, D), F32),
            pltpu.VMEM((tm + HALO, D), F32),
            pltpu.VMEM((2, HALO, D), F32),
            pltpu.VMEM((tm, D), F32),
            pltpu.VMEM((tm, D), BF),
            pltpu.VMEM((tm, D), F32),
            pltpu.VMEM((tm, D), BF),
            pltpu.VMEM((tm, D), BF),
            pltpu.VMEM((tm, D), BF),
            pltpu.VMEM((tm, D), F32),
            pltpu.VMEM((tm, BLK), F32),
            pltpu.VMEM((4, RC, D), F32),
            pltpu.VMEM((BLK, D), F32),
            pltpu.VMEM((GRP, BLK, BLK), BF), pltpu.VMEM((GRP, BLK, BLK), BF),
        ],
        compiler_params=_params(("arbitrary", "arbitrary"), 56),
    )(dxo, p3, proj, proj, proj, proj, proj, wl, wl, wl, lng, lnb, sgw, sgbT, pwb, pb, psc, cw, cb)


def _bwd_in(dproj, wl, x, dxo, ng):
    T = x.shape[0]
    tm = _tile(T, 256)
    nT = T // tm

    def body(dpj_ref, w_hbm, x_ref, dxo_ref, ng_ref, dx_ref, dng_ref, w_s, even_s, odd_s, g_s, wsem):
        i = pl.program_id(0)
        bufs = (even_s, odd_s)

        def w_copy(j):
            return pltpu.make_async_copy(w_hbm.at[j, pl.ds(0, SHW), :], w_s.at[j], wsem.at[j])

        def d_h():
            return sum(_mm_nt(dpj_ref[3 * j + b], w_s[j, b * D:(b + 1) * D, :]) for j in range(NSH) for b in range(3))

        def finish(prev):
            for c in range(tm // RC):
                finish_rows(prev, c * RC)

        def finish_rows(prev, r0):
            rows = pl.ds(r0, RC)
            xs = [x_ref[rows, c0:c0 + CW] for c0 in COLS]
            r = lax.rsqrt(sum(_rowsum(v * v) for v in xs) * (1.0 / D) + RMS_EPS)
            xh = [v * r for v in xs]
            dhg, m = [], 0.0
            for k, c0 in enumerate(COLS):
                cs = slice(c0, c0 + CW)
                dh = prev[rows, cs]
                g_s[:, cs] += dh * xh[k]
                dhg.append(dh * ng_ref[:, cs])
                m = m + _rowsum(dhg[k] * xh[k])
            m = m * (1.0 / D)
            for k, c0 in enumerate(COLS):
                cs = slice(c0, c0 + CW)
                dx_ref[rows, cs] = dxo_ref[rows, cs] + r * (dhg[k] - xh[k] * m)

        @pl.when(i == 0)
        def _():
            for j in range(NSH):
                w_copy(j).start()
            g_s[...] = jnp.zeros_like(g_s)
            for j in range(NSH):
                w_copy(j).wait()
            bufs[0][...] = d_h()

        for par in (0, 1):
            @pl.when((i % 2 == par) & (i > 0) & (i < nT))
            def _():
                bufs[par][...] = d_h()
                finish(bufs[1 - par])

        @pl.when(i == nT)
        def _():
            finish(bufs[(nT - 1) % 2])
            dng_ref[...] = jnp.sum(g_s[...], axis=0, keepdims=True)

    vec = pl.BlockSpec((1, D), lambda i: (0, 0))
    tile = pl.BlockSpec((tm, D), lambda i: (jnp.maximum(i - 1, 0), 0))
    return _pcall(
        body, name="bwd_in", grid=(nT + 1,),
        in_specs=[pl.BlockSpec((3 * NSH, tm, D), lambda i: (0, jnp.minimum(i, nT - 1), 0)), ANY, tile, tile, vec],
        out_specs=[tile, vec],
        out_shape=[jax.ShapeDtypeStruct((T, D), F32), jax.ShapeDtypeStruct((1, D), F32)],
        scratch_shapes=[pltpu.VMEM((NSH, SHW, D), BF), pltpu.VMEM((tm, D), F32), pltpu.VMEM((tm, D), F32),
                        pltpu.VMEM((RC, D), F32), pltpu.SemaphoreType.DMA((NSH,))],
        compiler_params=_params(("arbitrary",), 52),
    )(dproj, wl, x, dxo, ng)


def _tn_grad(a3, b3, split, out_map, name, into=None, after=None):
    nb, T, _ = b3.shape
    tk = _tile(T, 2048)
    nk = T // tk
    rows = D // split
    a_batched = a3.shape[0] > 1

    def body(a_ref, b_ref, *rest):
        o_ref, acc_s = rest[-2:]
        k = pl.program_id(1)

        @pl.when(k == 0)
        def _():
            acc_s[...] = _mm_tn(a_ref[...], b_ref[...])

        @pl.when((k > 0) & (k < nk))
        def _():
            acc_s[...] += _mm_tn(a_ref[...], b_ref[...])

        @pl.when(k >= nk - 1)
        def _():
            r0 = pl.multiple_of((k - (nk - 1)) * rows, rows)
            o_ref[...] = acc_s[pl.ds(r0, rows), :].astype(o_ref.dtype)

    def tok(k):
        return jnp.minimum(k, nk - 1)

    extra = ([] if into is None else [into]) + ([] if after is None else [after])
    return _pcall(
        body, name=name, grid=(nb, nk + split - 1),
        in_specs=[
            pl.BlockSpec((None, tk, D), (lambda n, k: (n, tok(k), 0)) if a_batched else (lambda n, k: (0, tok(k), 0))),
            pl.BlockSpec((None, tk, D), lambda n, k: (n, tok(k), 0)),
        ] + [ANY] * len(extra),
        out_specs=pl.BlockSpec((None, rows, D), lambda n, k: out_map(n, jnp.maximum(k - (nk - 1), 0))),
        out_shape=jax.ShapeDtypeStruct((NSH, PK_ROWS, D), BF),
        input_output_aliases={} if into is None else {2: 0},
        scratch_shapes=[pltpu.VMEM((D, D), F32)],
        compiler_params=_params(("arbitrary", "arbitrary"), 56),
    )(a3, b3, *extra)


def _place():
    x, y, c = lax.axis_index("x"), lax.axis_index("y"), lax.axis_index("c")
    chips = [(1 - x, y), (x, 1 - y), (1 - x, 1 - y)]
    return x, y, c, chips


def _peers(reach):
    x, y, c, chips = _place()
    if reach == "chips":
        return 2 * x + y, [((px, py, c), 2 * px + py) for px, py in chips]
    others = [(x, y, 1 - c)] + [(px, py, pc) for px, py in chips for pc in (c, 1 - c)]
    return 4 * x + 2 * y + c, [(pr, 4 * pr[0] + 2 * pr[1] + pr[2]) for pr in others]


ANY = pl.BlockSpec(memory_space=pl.ANY)


HBM = pl.BlockSpec(memory_space=pltpu.HBM)
SEM = pl.BlockSpec(memory_space=pltpu.SEMAPHORE)
EFFECT = pltpu.SideEffectType.DATAFLOW_SIDE_EFFECTING


def _own_slot(src, from_slot, name):
    rows = src.shape[-2]
    rb = rows // 9
    me = (2 * lax.axis_index("x") + lax.axis_index("y")).astype(jnp.int32).reshape(1)

    def body(me_ref, src_ref, land_ref):
        land_ref[...] = src_ref[...]

    if from_slot:
        src_spec = pl.BlockSpec((None, rb, D), lambda i, me_ref: (me_ref[0], i, 0))
    else:
        src_spec = pl.BlockSpec((rb, D), lambda i, me_ref: (i, 0))
    return _pcall(
        body, name=name,
        grid_spec=pltpu.PrefetchScalarGridSpec(
            num_scalar_prefetch=1, grid=(rows // rb,), in_specs=[src_spec],
            out_specs=pl.BlockSpec((None, rb, D), lambda i, me_ref: (me_ref[0], i, 0))),
        out_shape=jax.ShapeDtypeStruct((NSH, rows, D), src.dtype),
        compiler_params=_params(("arbitrary",), 32),
    )(me, src)


def _push_start(name, srcs, lands, per_peer, reach="chips"):
    n = len(srcs)
    npeer = 3 if reach == "chips" else 7
    ns = n * npeer

    def body(*refs):
        src, land = refs[:n], refs[n:2 * n]
        ssem, rsem = refs[2 * n:2 * n + ns], refs[2 * n + ns:2 * n + 2 * ns]
        token = refs[-1]
        me, peers = _peers(reach)
        for i in range(n):
            for j, (peer, slot) in enumerate(peers):
                pltpu.make_async_remote_copy(
                    src_ref=src[i].at[slot] if per_peer else src[i], dst_ref=land[i].at[me],
                    send_sem=ssem[npeer * i + j], recv_sem=rsem[npeer * i + j], device_id=peer, device_id_type=MESH).start()
        token[...] = jnp.zeros_like(token)

    ops = list(srcs) + list(lands)
    out = _pcall(
        body, name=name,
        out_shape=tuple([pltpu.SemaphoreType.DMA(())] * (2 * ns) + [pltpu.HBM(a.shape, a.dtype) for a in ops]
                        + [jax.ShapeDtypeStruct((8, 128), F32)]),
        in_specs=[HBM] * (2 * n),
        out_specs=tuple([SEM] * (2 * ns) + [HBM] * (2 * n) + [pl.BlockSpec(memory_space=pltpu.VMEM)]),
        input_output_aliases={i: 2 * ns + i for i in range(2 * n)},
        compiler_params=pltpu.CompilerParams(has_side_effects=EFFECT),
    )(*[pltpu.with_memory_space_constraint(a, pltpu.HBM) for a in ops])
    return out[:ns], out[ns:2 * ns], out[2 * ns:2 * ns + n], out[2 * ns + n:2 * ns + 2 * n], out[-1]


def _push_wait(name, src, land, ssem, rsem, after, per_peer, reach="chips"):
    npeer = len(ssem)

    def body(src_ref, land_ref, *rest):
        sems = rest[:2 * npeer]
        _, peers = _peers(reach)
        for j, (peer, slot) in enumerate(peers):
            cp = pltpu.make_async_remote_copy(
                src_ref=src_ref.at[slot] if per_peer else src_ref, dst_ref=land_ref.at[slot], send_sem=sems[j],
                recv_sem=sems[npeer + j], device_id=peer, device_id_type=MESH)
            cp.wait_send()
            cp.wait_recv()

    return _pcall(
        body, name=name,
        out_shape=(pltpu.HBM(src.shape, src.dtype), pltpu.HBM(land.shape, land.dtype)),
        in_specs=[HBM, HBM] + [SEM] * (2 * npeer) + [ANY], out_specs=(HBM, HBM),
        input_output_aliases={0: 0, 1: 1},
        compiler_params=pltpu.CompilerParams(has_side_effects=EFFECT),
    )(src, land, *ssem, *rsem, after)[1]


def _swap_sibling(arrs):
    n = len(arrs)

    def body(*refs):
        src, dst = refs[:n], refs[n:2 * n]
        ssem, rsem = refs[2 * n:]
        x, y, c, _ = _place()
        cps = [pltpu.make_async_remote_copy(src_ref=src[a], dst_ref=dst[a], send_sem=ssem.at[a], recv_sem=rsem.at[a],
                                            device_id=(x, y, 1 - c), device_id_type=MESH) for a in range(n)]
        for cp in cps:
            cp.start()
        for cp in cps:
            cp.wait()

    return _pcall(
        body, name="swap_sibling",
        in_specs=[ANY] * n, out_specs=[ANY] * n,
        out_shape=[jax.ShapeDtypeStruct(a.shape, a.dtype) for a in arrs],
        scratch_shapes=[pltpu.SemaphoreType.DMA((n,)), pltpu.SemaphoreType.DMA((n,))],
    )(*arrs)


def _gather_all(v):
    def body(src, dst, ssem, rsem, lsem):
        x, y, c, _ = _place()
        me = 4 * x + 2 * y + c
        peers = [(x, y, 1 - c), (1 - x, y, c), (1 - x, y, 1 - c), (x, 1 - y, c), (x, 1 - y, 1 - c),
                 (1 - x, 1 - y, c), (1 - x, 1 - y, 1 - c)]
        local = pltpu.make_async_copy(src, dst.at[me], lsem)
        local.start()
        sends = [pltpu.make_async_remote_copy(src_ref=src, dst_ref=dst.at[me], send_sem=ssem.at[j], recv_sem=rsem.at[j],
                                              device_id=pr, device_id_type=MESH) for j, pr in enumerate(peers)]
        for cp in sends:
            cp.start()
        for j, (px, py, pc) in enumerate(peers):
            pltpu.make_async_remote_copy(src_ref=src, dst_ref=dst.at[4 * px + 2 * py + pc], send_sem=ssem.at[j],
                                         recv_sem=rsem.at[j], device_id=(px, py, pc), device_id_type=MESH).wait_recv()
        for cp in sends:
            cp.wait_send()
        local.wait()

    return _pcall(
        body, name="gather_all", in_specs=[ANY], out_specs=ANY,
        out_shape=jax.ShapeDtypeStruct((8,) + v.shape, v.dtype),
        scratch_shapes=[pltpu.SemaphoreType.DMA((7,)), pltpu.SemaphoreType.DMA((7,)), pltpu.SemaphoreType.DMA(())],
    )(v)


def _sum_slots(r, rb):
    S = r.shape[0]

    def body(r_ref, o_ref):
        acc = r_ref[0].astype(F32)
        for s in range(1, S):
            acc = acc + r_ref[s].astype(F32)
        o_ref[...] = acc

    if r.ndim == 3:
        _, R, C = r.shape
        grid, blk, imap = (R // rb,), (S, rb, C), (lambda i: (0, i, 0))
        oblk, omap = (rb, C), (lambda i: (i, 0))
    else:
        _, K, R, C = r.shape
        grid, blk, imap = (K,), (S, None, R, C), (lambda i: (0, i, 0, 0))
        oblk, omap = (None, R, C), (lambda i: (i, 0, 0))
    return _pcall(
        body, name="sum_slots", grid=grid, in_specs=[pl.BlockSpec(blk, imap)], out_specs=pl.BlockSpec(oblk, omap),
        out_shape=jax.ShapeDtypeStruct(r.shape[1:], F32), compiler_params=_params(("arbitrary",), 48),
    )(r)


def _adamw(gs, g_spec, w, m, v, p_spec, prev, grid):
    ng = len(gs)
    bc1 = 1.0 - ADAM_B1 ** ADAM_STEP
    bc2 = 1.0 - ADAM_B2 ** ADAM_STEP

    def body(*refs):
        g = refs[0][...]
        for a in range(1, ng):
            g = g + refs[a][...]
        w_ref, m_ref, v_ref = refs[ng:ng + 3]
        go, do, mo, vo = refs[ng + 3 + 4:]
        mn = ADAM_B1 * m_ref[...] + (1.0 - ADAM_B1) * g
        vn = ADAM_B2 * v_ref[...] + (1.0 - ADAM_B2) * (g * g)
        go[...] = g
        mo[...] = mn
        vo[...] = vn
        do[...] = -ADAM_LR * ((mn / bc1) / (jnp.sqrt(vn / bc2) + ADAM_EPS) + ADAM_WD * w_ref[...])

    out = jax.ShapeDtypeStruct(w.shape, F32)
    k0 = ng + 3
    return _pcall(
        body, name="adamw", grid=grid,
        in_specs=[g_spec] * ng + [p_spec] * 3 + [ANY] * 4,
        out_specs=[p_spec] * 4, out_shape=[out] * 4,
        input_output_aliases={k0: 0, k0 + 1: 1, k0 + 2: 2, k0 + 3: 3},
        compiler_params=_params(("arbitrary",) * len(grid), 48),
    )(*gs, w, m, v, *prev)


def _empty4(w):
    return tuple(lax.empty(w.shape, F32) for _ in range(4))


N_SGW = L * GRP * BLK * BLK // D
O_NG, O_VEC, O_SGB, O_FG, O_SGW = 0, 8, 32, 40, 48
O_CW = O_SGW + N_SGW
N_PACK = O_CW + 16
PACK_RB = N_PACK // 3


def _pad_to(a, rows):
    return jnp.pad(a, ((0, rows - a.shape[0]), (0, 0)))


def _pack_small(ng, vecs, sgb, fg, sgw, cw):
    parts = [_pad_to(ng, 8), _pad_to(vecs.reshape(L * 5, D), 24), _pad_to(sgb.reshape(L, D), 8),
             _pad_to(fg.reshape(1, D), 8), sgw.reshape(N_SGW, D), _pad_to(cw, 16)]
    return jnp.concatenate(parts, axis=0)


def kernel(x, norm_g, w_in, sgu_ln_g, sgu_ln_b, sgu_w, sgu_b, pool_w, pool_b, pool_scale, conv_w, conv_b, w_branch_a, w_branch_b, w_branch_c, w_out, final_g, loss_target, m_norm_g, m_w_in, m_sgu_ln_g, m_sgu_ln_b, m_sgu_w, m_sgu_b, m_pool_w, m_pool_b, m_pool_scale, m_conv_w, m_conv_b, m_w_branch_a, m_w_branch_b, m_w_branch_c, m_w_out, m_final_g, v_norm_g, v_w_in, v_sgu_ln_g, v_sgu_ln_b, v_sgu_w, v_sgu_b, v_pool_w, v_pool_b, v_pool_scale, v_conv_w, v_conv_b, v_w_branch_a, v_w_branch_b, v_w_branch_c, v_w_out, v_final_g):
    cx, cy = lax.axis_index("x"), lax.axis_index("y")
    me = 2 * cx + cy
    xl, tgt = x[0], loss_target[0]
    q = D // NSH

    wq = w_in.astype(BF).reshape(L, D, 3, D).transpose(0, 2, 1, 3).reshape(L, SHW, D)
    brq = jnp.stack([w_branch_a, w_branch_b, w_branch_c, w_out], axis=1).astype(BF).reshape(L, D, D)
    pwq = pool_w.astype(BF).reshape(L, PK_CW - PK_PW, D)
    cwq = lax.bitcast_convert_type(conv_w, BF).reshape(L, 3 * q * 2)
    cwq = jnp.pad(cwq, ((0, 0), (0, 16 * D - 3 * q * 2))).reshape(L, 16, D)
    packs = [jnp.concatenate([wq[l], brq[l], pwq[l], cwq[l]], axis=0) for l in range(L)]
    lands = [_own_slot(packs[l], False, f"ag_own_{l}") for l in range(L)]
    ag_s, ag_r, packs, lands, tok = _push_start("ag_start", packs, lands, False)
    sgbT = sgu_b.transpose(0, 2, 1)

    def layer_weights(l, after):
        wl = _push_wait(f"ag_wait_{l}", packs[l], lands[l], ag_s[3 * l:3 * l + 3], ag_r[3 * l:3 * l + 3], after, False)
        pwb = wl[:, PK_PW:PK_CW].reshape(NSH, NPG, PGW // NSH, PGW).transpose(1, 0, 2, 3).reshape(NPG, PGW, PGW)
        cwb = wl[:, PK_CW:].reshape(NSH, 16 * D)[:, :3 * q * 2].reshape(NSH, 3, q, 2)
        cwf = lax.bitcast_convert_type(cwb, F32).transpose(1, 0, 2).reshape(3, D)
        small = (sgu_ln_g[l:l + 1], sgu_ln_b[l:l + 1], sgu_w[l], sgbT[l], pwb, pool_b[l:l + 1], pool_scale[l:l + 1],
                 cwf, conv_b[l:l + 1])
        return wl, small

    xs, saved, wts = [xl], [], []
    for l in range(L):
        wl, small = layer_weights(l, tok if l == 0 else xs[l])
        wts.append((wl, small))
        proj, h, y3 = _fwd_in(xs[l], norm_g[l:l + 1], wl, *small)
        p3, mg, xo = _fwd_out(y3, proj, xs[l], wl)
        saved.append((proj, h, y3, p3, mg))
        xs.append(xo)

    dx, sq, dfg = _loss_head(xs[L], final_g[None], tgt)
    loss = lax.psum(jnp.sum(sq) * (0.5 / D), ("x", "y", "c"))

    g_w_in = _empty4(w_in)
    g_br = [_empty4(w_out) for _ in range(4)]
    g_pw = _empty4(pool_w)
    dng, dvec, dsgw, dsgb = [None] * L, [None] * L, [None] * L, [None] * L
    branches = [(w_branch_a, m_w_branch_a, v_w_branch_a), (w_branch_b, m_w_branch_b, v_w_branch_b),
                (w_branch_c, m_w_branch_c, v_w_branch_c), (w_out, m_w_out, v_w_out)]
    nb = D // 128

    def finish(l, landed):
        nonlocal g_w_in, g_pw
        mine = _sum_slots(landed, PK_ROWS // 9)
        sums = [mine, _swap_sibling([mine])[0]]
        g_w_in = _adamw(sums, pl.BlockSpec((128, D), lambda b, i: (b * nb + i, 0)), w_in, m_w_in, v_w_in,
                        pl.BlockSpec((None, 128, D), lambda b, i: (l, i, b)), g_w_in, (3, nb))
        for k, (w, m, v) in enumerate(branches):
            g_br[k] = _adamw(sums, pl.BlockSpec((q, D), lambda i, k=k: (PK_BR // q + k, 0)), w, m, v,
                             pl.BlockSpec((None, q, D), lambda i: (l, 0, 0)), g_br[k], (1,))
        pools = [a[PK_PW:PK_CW].reshape(NPG, PGW // NSH, PGW) for a in sums]
        g_pw = _adamw(pools, pl.BlockSpec((None, PGW // NSH, PGW), lambda g: (g, 0, 0)), pool_w, m_pool_w, v_pool_w,
                      pl.BlockSpec((None, None, PGW // NSH, PGW), lambda g: (l, g, 0, 0)), g_pw, (NPG,))

    pend = None
    for l in reversed(range(L)):
        proj, h, y3, p3, mg = saved[l]
        wl, small = wts[l]
        dproj, dp3, dsgw[l], dsbT, dvec[l], dpw = _bwd_mix(dx, p3, proj, wl, *small)
        dsgb[l] = dsbT.T
        if l == 0:
            dv = jnp.stack(dvec)
            part = _pack_small(jnp.zeros((L, D), F32), dv[:, 0:5], jnp.stack(dsgb), dfg[0], jnp.stack(dsgw),
                               dv[:, 5:8].reshape(L * 3, D))
            zone = lax.dynamic_update_slice(lax.empty((8, N_PACK, D), F32), part[None],
                                            (2 * me + lax.axis_index("c"), 0, 0))
            sm_s, sm_r, (part,), (zone,), started = _push_start("small_start", [part], [zone], False, "all")
        grads = _tn_grad(h[None], dproj, 1, lambda n, s: (n // 3, n % 3, 0), "grad_w_in", after=started if l == 0 else None)
        grads = _tn_grad(y3, dp3, NSH, lambda n, s: (s, PK_BR // q + n, 0), "grad_w_branch", into=grads)
        grads = _tn_grad(mg[None], dx[None], NSH, lambda n, s: (s, PK_BR // q + 3, 0), "grad_w_out", into=grads)
        dpq = dpw.astype(BF).reshape(NPG, NSH, PGW // NSH, PGW).transpose(1, 0, 2, 3).reshape(NSH, PK_CW - PK_PW, D)
        grads = lax.dynamic_update_slice(grads, jnp.pad(dpq, ((0, 0), (0, PK_ROWS - PK_CW), (0, 0))), (0, PK_PW, 0))
        if pend is not None:
            landed = _push_wait(f"rs_wait_{pend[0]}", *pend[1:], dproj, True)
        land = _own_slot(grads, True, f"rs_own_{l}")
        ss, rs, (grads,), (land,), tok = _push_start(f"rs_start_{l}", [grads], [land], True)
        if pend is not None:
            finish(pend[0], landed)
        dx, dng[l] = _bwd_in(dproj, wl, xs[l], dx, norm_g[l:l + 1] + tok[0, 0])
        pend = (l, grads, land, ss, rs)
    finish(pend[0], _push_wait(f"rs_wait_{pend[0]}", *pend[1:], dx, True))

    zone = _push_wait("small_wait", part, zone, sm_s, sm_r, dx, False, "all")
    gng = _sum_slots(_gather_all(_pad_to(jnp.concatenate(dng), O_VEC)), O_VEC)
    gsmall = jnp.concatenate([gng, _sum_slots(zone, PACK_RB)[O_VEC:]])
    gcw =lax.dynamic_slice_in_dim(gsmall[O_CW:O_CW + L * 3], me * q, q, axis=1)
    gpack = jnp.concatenate([gsmall[:O_CW], _pad_to(gcw.reshape(L * 3 * q // D, D), 16)])

    def pack(ng, lg, lb, sw, sb, pb_, ps, cwv, cb_, fg):
        return _pack_small(ng, jnp.stack([lg, lb, pb_, ps, cb_], axis=1), sb, fg, sw, cwv.reshape(L * 3 * q // D, D))

    wp = pack(norm_g, sgu_ln_g, sgu_ln_b, sgu_w, sgu_b, pool_b, pool_scale, conv_w, conv_b, final_g)
    mp = pack(m_norm_g, m_sgu_ln_g, m_sgu_ln_b, m_sgu_w, m_sgu_b, m_pool_b, m_pool_scale, m_conv_w, m_conv_b, m_final_g)
    vp = pack(v_norm_g, v_sgu_ln_g, v_sgu_ln_b, v_sgu_w, v_sgu_b, v_pool_b, v_pool_scale, v_conv_w, v_conv_b, v_final_g)
    rows = pl.BlockSpec((PACK_RB, D), lambda i: (i, 0))
    sm = _adamw([gpack], rows, wp, mp, vp, rows, _empty4(wp), (N_PACK // PACK_RB,))

    def unpack(a):
        vv = a[O_VEC:O_VEC + L * 5].reshape(L, 5, D)
        sb = a[O_SGB:O_SGB + L].reshape(L, GRP, BLK)
        fg = a[O_FG]
        sw = a[O_SGW:O_SGW + N_SGW].reshape(L, GRP, BLK, BLK)
        cwv = a[O_CW:O_CW + L * 3 * q // D].reshape(L, 3, q)
        return dict(norm_g=a[O_NG:O_NG + L], w_in=None, sgu_ln_g=vv[:, 0], sgu_ln_b=vv[:, 1], sgu_w=sw, sgu_b=sb, pool_w=None,
                    pool_b=vv[:, 2], pool_scale=vv[:, 3], conv_w=cwv, conv_b=vv[:, 4], w_branch_a=None,
                    w_branch_b=None, w_branch_c=None, w_out=None, final_g=fg)

    outs = [loss, dx[None]]
    for kind in range(4):
        d = unpack(sm[kind])
        d.update(w_in=g_w_in[kind], pool_w=g_pw[kind], w_branch_a=g_br[0][kind], w_branch_b=g_br[1][kind],
                 w_branch_c=g_br[2][kind], w_out=g_br[3][kind])
        outs.extend(d[n] for n in ("norm_g", "w_in", "sgu_ln_g", "sgu_ln_b", "sgu_w", "sgu_b", "pool_w", "pool_b",
                                   "pool_scale", "conv_w", "conv_b", "w_branch_a", "w_branch_b", "w_branch_c", "w_out",
                                   "final_g"))
    return tuple(outs)
```

```python
import functools

import jax
import jax.numpy as jnp
from jax import lax
from jax.experimental import pallas as pl
from jax.experimental.pallas import tpu as pltpu

F32 = jnp.float32
BF = jnp.bfloat16
MESH = pl.DeviceIdType.MESH

D = 1024
L = 4
NSH = 4
SHW = 3 * D
NIN = NSH * SHW
GRP = 8
BLK = 128
NPG = 4
PGW = D // NPG
POOL_WINDOWS = (2, 4, 8, 16)
HALO = 16
RMS_EPS = 1e-6
LN_EPS = 1e-5
ADAM_LR, ADAM_B1, ADAM_B2, ADAM_EPS, ADAM_WD, ADAM_STEP = 0.001, 0.9, 0.999, 1e-8, 0.01, 10

RC = 16
CW = 512
COLS = tuple(range(0, D, CW))
MIB = 1 << 20

PK_BR = SHW
PK_PW = PK_BR + D
PK_CW = PK_PW + NPG * (PGW // NSH) * PGW // D
PK_ROWS = PK_CW + 16

_C0 = 0.7978845608028654
_C1 = 0.044715


def _pcall(body, **kw):
    return pl.pallas_call(body, **kw)


def _params(sem, vmem_mib):
    return pltpu.CompilerParams(dimension_semantics=sem, vmem_limit_bytes=vmem_mib * MIB)


def _mm(a, b):
    return jnp.dot(a.astype(BF), b.astype(BF), preferred_element_type=F32)


def _mm_nt(a, b):
    return lax.dot_general(a.astype(BF), b.astype(BF), (((1,), (1,)), ((), ())), preferred_element_type=F32)


def _mm_tn(a, b):
    return lax.dot_general(a.astype(BF), b.astype(BF), (((0,), (0,)), ((), ())), preferred_element_type=F32)


def _gelu(x):
    return 0.5 * x * (1.0 + jnp.tanh(_C0 * x * (1.0 + _C1 * x * x)))


def _gelu_fg(x):
    x2 = x * x
    t = jnp.tanh(_C0 * x * (1.0 + _C1 * x2))
    f = 0.5 * x * (1.0 + t)
    g = 0.5 * (1.0 + t) + 0.5 * x * (1.0 - t * t) * (_C0 * (1.0 + 3.0 * _C1 * x2))
    return f, g


def _sigmoid(x):
    return 0.5 * jnp.tanh(0.5 * x) + 0.5


def _silu(x):
    return x * _sigmoid(x)


def _silu_fg(x):
    s = _sigmoid(x)
    return x * s, s * (1.0 + x * (1.0 - s))


def _rowsum(v):
    return jnp.sum(v, axis=1, keepdims=True)


def _chunks(n_rows, fn, unroll=2):
    def body(c, carry):
        fn(pl.multiple_of(c * RC, RC))
        return carry
    lax.fori_loop(0, n_rows // RC, body, 0, unroll=unroll)


def _trail(e, win):
    s, sh = e, 1
    while sh < win:
        s = s + pltpu.roll(s, sh, 0)
        sh *= 2
    return s


def _lead(e, win):
    n = e.shape[0]
    s, sh = e, 1
    while sh < win:
        s = s + pltpu.roll(s, n - sh, 0)
        sh *= 2
    return s


def _inv_count(pos0, win):
    pos = pos0 + lax.broadcasted_iota(jnp.int32, (RC, 1), 0)
    return 1.0 / jnp.minimum(pos + 1, win).astype(F32)


def _masked_sgu(sgw_ref, g):
    r = lax.broadcasted_iota(jnp.int32, (BLK, BLK), 0) // 64
    c = lax.broadcasted_iota(jnp.int32, (BLK, BLK), 1) // 64
    return jnp.where(c <= r, sgw_ref[g], 0.0)


class _AsF32:
    def __init__(self, ref, col0=0):
        self.ref, self.col0 = ref, col0

    def __getitem__(self, idx):
        if self.col0:
            rows, cols = idx
            idx = (rows, slice(cols.start + self.col0, cols.stop + self.col0))
        return self.ref[idx].astype(F32)


class _From:
    def __init__(self, ref, lead0):
        self.ref, self.lead0 = ref, lead0

    def __setitem__(self, idx, val):
        self.ref[(self.lead0 + idx[0],) + tuple(idx[1:])] = val


def _tile(t, want):
    return min(t, want)


def _fwd_in(x, ng, wl, lng, lnb, sgw, sgbT, pwb, pb, psc, cw, cb):
    T = x.shape[0]
    tm = _tile(T, 256)
    nT = T // tm

    def body(x_ref, ng_ref, w_hbm, lng_ref, lnb_ref, sgw_ref, sgbT_ref, pw_ref, pb_ref, psc_ref, cw_ref, cb_ref,
             proj_hbm, h_ref, y_ref, w_s, pja, pjb, pjc, pjd, qa, qb, qc, qd, h_s, vn_s, mix_s, d_s, wm_s, extp_s, extc_s,
             wsem, psem):
        i = pl.program_id(0)
        pj = (pja, pjb, pjc, pjd)
        kept = (qa, qb, qc, qd)

        def w_copy(p):
            return pltpu.make_async_copy(w_hbm.at[p, pl.ds(0, SHW), :], w_s.at[p], wsem.at[p])

        def p_copy(p):
            dst = proj_hbm.at[pl.ds(pl.multiple_of(i * tm, tm), tm), pl.ds(p * SHW, SHW)]
            return pltpu.make_async_copy(kept[p], dst, psem.at[p])

        def keep(p):
            kept[p][...] = pj[p][...].astype(BF)
            p_copy(p).start()

        def all_rows(fn):
            for c in range(tm // RC):
                fn(c * RC)

        def buffer_free(p):
            @pl.when(i == 0)
            def _():
                w_copy(p).wait()

            @pl.when(i > 0)
            def _():
                p_copy(p).wait()

        def project(p, blocks):
            for b in blocks:
                pj[p][:, b * D:(b + 1) * D] = jnp.dot(h_s[...], w_s[p, b * D:(b + 1) * D, :],
                                                      preferred_element_type=F32)

        @pl.when(i == 0)
        def _():
            for p in range(NSH):
                w_copy(p).start()
            extp_s[0:HALO, :] = jnp.zeros((HALO, D), F32)
            extc_s[0:HALO, :] = jnp.zeros((HALO, D), F32)
            for g in range(GRP):
                wm_s[g] = _masked_sgu(sgw_ref, g).astype(BF)

        def norm_rows(r0):
            rows = pl.ds(r0, RC)
            xs = [x_ref[rows, c0:c0 + CW] for c0 in COLS]
            ms = sum(_rowsum(v * v) for v in xs) * (1.0 / D)
            r = lax.rsqrt(ms + RMS_EPS)
            for k, c0 in enumerate(COLS):
                hb = (xs[k] * r * ng_ref[:, c0:c0 + CW]).astype(BF)
                h_s[rows, c0:c0 + CW] = hb
                h_ref[rows, c0:c0 + CW] = hb

        def gating(first, rest):
            proj_ref = pj[0]
            first()

            def f1(r0):
                rows = pl.ds(r0, RC)
                gv = [_gelu(proj_ref[rows, D + c0:D + c0 + CW]) for c0 in COLS]
                mu = sum(_rowsum(v) for v in gv) * (1.0 / D)
                dv = [v - mu for v in gv]
                var = sum(_rowsum(v * v) for v in dv) * (1.0 / D)
                rstd = lax.rsqrt(var + LN_EPS)
                for k, c0 in enumerate(COLS):
                    vn_s[rows, c0:c0 + CW] = (dv[k] * rstd * lng_ref[:, c0:c0 + CW] + lnb_ref[:, c0:c0 + CW]).astype(BF)
            all_rows(f1)
            for g in range(GRP):
                w = wm_s[g]
                bcol = sgbT_ref[:, g:g + 1]
                gc = slice(g * BLK, (g + 1) * BLK)
                for n in range(tm // BLK):
                    rr = slice(n * BLK, (n + 1) * BLK)
                    mix_s[rr, gc] = jnp.dot(w, vn_s[rr, gc], preferred_element_type=F32) + bcol
            rest()

            def f2(r0):
                rows = pl.ds(r0, RC)
                for c0 in COLS:
                    au = proj_ref[rows, c0:c0 + CW]
                    az = proj_ref[rows, 2 * D + c0:2 * D + c0 + CW]
                    y_ref[0, rows, c0:c0 + CW] = (_gelu(au) * mix_s[rows, c0:c0 + CW] * _silu(az)).astype(BF)
            all_rows(f2)

        def pooling(first, rest):
            proj_ref = pj[1]
            first()
            extp_s[HALO:HALO + tm, :] = proj_ref[:, 0:D]

            def f1(r0):
                rows = pl.ds(r0, RC)
                e = extp_s[pl.ds(r0, 2 * RC), :]
                for k, win in enumerate(POOL_WINDOWS):
                    gc = slice(k * PGW, (k + 1) * PGW)
                    s = _trail(e[:, gc], win)[RC:2 * RC]
                    d_s[rows, gc] = (s * _inv_count(i * tm + r0, win) - e[RC:2 * RC, gc]).astype(BF)
            all_rows(f1)
            extp_s[0:HALO, :] = extp_s[tm:tm + HALO, :]
            for k in range(NPG):
                gc = slice(k * PGW, (k + 1) * PGW)
                mix_s[:, gc] = jnp.dot(d_s[:, gc], pw_ref[k], preferred_element_type=F32)
            rest()

            def f2(r0):
                rows = pl.ds(r0, RC)
                for c0 in COLS:
                    cs = slice(c0, c0 + CW)
                    bz = proj_ref[rows, D + c0:D + c0 + CW]
                    y_ref[1, rows, cs] = ((mix_s[rows, cs] + pb_ref[:, cs]) * psc_ref[:, cs] * _silu(bz)).astype(BF)
            all_rows(f2)

        def convolution(first, rest):
            proj_ref = pj[2]
            first()

            def f1(r0):
                rows = pl.ds(r0, RC)
                extc_s[pl.ds(HALO + r0, RC), :] = proj_ref[rows, D:2 * D] * pj[1][rows, 2 * D:3 * D]
            all_rows(f1)
            rest()

            def f2(r0):
                rows = pl.ds(r0, RC)
                for c0 in COLS:
                    cs = slice(c0, c0 + CW)
                    e = extc_s[pl.ds(r0, 2 * RC), cs]
                    conv = (cw_ref[2:3, cs] * e + cw_ref[1:2, cs] * pltpu.roll(e, 1, 0)
                            + cw_ref[0:1, cs] * pltpu.roll(e, 2, 0))[RC:2 * RC] + cb_ref[:, cs]
                    cbv = proj_ref[rows, cs]
                    cz = proj_ref[rows, 2 * D + c0:2 * D + c0 + CW]
                    y_ref[2, rows, cs] = (cbv * conv * _silu(cz)).astype(BF)
            all_rows(f2)
            extc_s[0:HALO, :] = extc_s[tm:tm + HALO, :]

        all_rows(norm_rows)
        buffer_free(0)
        project(0, (0, 1, 2))
        keep(0)
        for p, mixer, early in ((1, gating, 2), (2, pooling, 1), (3, convolution, 1)):
            buffer_free(p)
            mixer(functools.partial(project, p, range(early)), functools.partial(project, p, range(early, 3)))
            keep(p)

        @pl.when(i == nT - 1)
        def _():
            for p in range(NSH):
                p_copy(p).wait()

    vec = pl.BlockSpec((1, D), lambda i: (0, 0))
    return _pcall(
        body, name="fwd_in", grid=(nT,),
        in_specs=[
            pl.BlockSpec((tm, D), lambda i: (i, 0)), vec, ANY, vec, vec,
            pl.BlockSpec((GRP, BLK, BLK), lambda i: (0, 0, 0)),
            pl.BlockSpec((BLK, GRP), lambda i: (0, 0)),
            pl.BlockSpec((NPG, PGW, PGW), lambda i: (0, 0, 0)),
            vec, vec,
            pl.BlockSpec((3, D), lambda i: (0, 0)), vec,
        ],
        out_specs=[ANY, pl.BlockSpec((tm, D), lambda i: (i, 0)), pl.BlockSpec((3, tm, D), lambda i: (0, i, 0))],
        out_shape=[jax.ShapeDtypeStruct((T, NIN), BF), jax.ShapeDtypeStruct((T, D), BF),
                   jax.ShapeDtypeStruct((3, T, D), BF)],
        scratch_shapes=[
            pltpu.VMEM((NSH, SHW, D), BF),
            pltpu.VMEM((tm, SHW), F32), pltpu.VMEM((tm, SHW), F32), pltpu.VMEM((tm, SHW), F32), pltpu.VMEM((tm, SHW), F32),
            pltpu.VMEM((tm, SHW), BF), pltpu.VMEM((tm, SHW), BF), pltpu.VMEM((tm, SHW), BF), pltpu.VMEM((tm, SHW), BF),
            pltpu.VMEM((tm, D), BF), pltpu.VMEM((tm, D), BF), pltpu.VMEM((tm, D), F32),
            pltpu.VMEM((tm, D), BF), pltpu.VMEM((GRP, BLK, BLK), BF),
            pltpu.VMEM((tm + HALO, D), F32), pltpu.VMEM((tm + HALO, D), F32),
            pltpu.SemaphoreType.DMA((NSH,)), pltpu.SemaphoreType.DMA((NSH,)),
        ],
        compiler_params=_params(("arbitrary",), 58),
    )(x, ng, wl, lng, lnb, sgw, sgbT, pwb, pb, psc, cw, cb)


def _branch_spec(which):
    q = D // NSH
    return lambda *g: (0, PK_BR // q + (which(*g) if callable(which) else which), 0)


def _fwd_out(y3, proj, x, wl):
    T = x.shape[0]
    tm = _tile(T, 256)
    q = D // NSH

    nT = T // tm

    def body(y_ref, gl_ref, x_ref, wa_ref, wb_ref, wc_ref, wo_ref, p_ref, mg_ref, xo_ref, even_s, odd_s):
        i = pl.program_id(0)
        gl_ref = _AsF32(gl_ref)

        def branches(buf):
            for k, w_ref in enumerate((wa_ref, wb_ref, wc_ref)):
                buf[k] = jnp.dot(y_ref[k], w_ref[...].reshape(D, D), preferred_element_type=F32)

        def merge(buf):
            for c in range(tm // RC):
                rows = pl.ds(c * RC, RC)
                for c0 in COLS:
                    cs = slice(c0, c0 + CW)
                    m = 0.0
                    for k in range(3):
                        pk = buf[k, rows, cs]
                        p_ref[k, rows, cs] = pk.astype(BF)
                        m = m + _sigmoid(gl_ref[rows, k * D + c0:k * D + c0 + CW]) * pk
                    mg_ref[rows, cs] = m.astype(BF)
            xo_ref[...] = x_ref[...] + jnp.dot(mg_ref[...], wo_ref[...].reshape(D, D), preferred_element_type=F32)

        bufs = (even_s, odd_s)

        @pl.when(i == 0)
        def _():
            branches(bufs[0])

        for par in (0, 1):
            @pl.when((i % 2 == par) & (i > 0) & (i < nT))
            def _():
                branches(bufs[par])
                merge(bufs[1 - par])

        @pl.when(i == nT)
        def _():
            merge(bufs[(nT - 1) % 2])

    prev = lambda i: jnp.maximum(i - 1, 0)
    return _pcall(
        body, name="fwd_out", grid=(nT + 1,),
        in_specs=[
            pl.BlockSpec((3, tm, D), lambda i: (0, jnp.minimum(i, nT - 1), 0)),
            pl.BlockSpec((tm, SHW), lambda i: (prev(i), 3)),
            pl.BlockSpec((tm, D), lambda i: (prev(i), 0)),
        ] + [pl.BlockSpec((NSH, q, D), _branch_spec(k)) for k in range(4)],
        out_specs=[
            pl.BlockSpec((3, tm, D), lambda i: (0, prev(i), 0)),
            pl.BlockSpec((tm, D), lambda i: (prev(i), 0)),
            pl.BlockSpec((tm, D), lambda i: (prev(i), 0)),
        ],
        out_shape=[jax.ShapeDtypeStruct((3, T, D), BF), jax.ShapeDtypeStruct((T, D), BF),
                   jax.ShapeDtypeStruct((T, D), F32)],
        scratch_shapes=[pltpu.VMEM((3, tm, D), F32), pltpu.VMEM((3, tm, D), F32)],
        compiler_params=_params(("arbitrary",), 52),
    )(y3, proj, x, wl, wl, wl, wl)


def _loss_head(x, fg, tgt):
    T = x.shape[0]
    tm = _tile(T, 512)

    def body(x_ref, g_ref, t_ref, dx_ref, sq_ref, dg_ref, acc_s):
        i = pl.program_id(0)

        @pl.when(i == 0)
        def _():
            acc_s[...] = jnp.zeros_like(acc_s)

        def f(r0):
            rows = pl.ds(r0, RC)
            xs = [x_ref[rows, c0:c0 + CW] for c0 in COLS]
            r = lax.rsqrt(sum(_rowsum(v * v) for v in xs) * (1.0 / D) + RMS_EPS)
            xh = [v * r for v in xs]
            dyg, m = [], 0.0
            for k, c0 in enumerate(COLS):
                cs = slice(c0, c0 + CW)
                err = xh[k] * g_ref[:, cs] - t_ref[rows, cs]
                acc_s[0, :, cs] += err * err
                dy = err * (1.0 / D)
                acc_s[1, :, cs] += dy * xh[k]
                dyg.append(dy * g_ref[:, cs])
                m = m + _rowsum(dyg[k] * xh[k])
            m = m * (1.0 / D)
            for k, c0 in enumerate(COLS):
                dx_ref[rows, c0:c0 + CW] = r * (dyg[k] - xh[k] * m)
        _chunks(tm, f)

        @pl.when(i == pl.num_programs(0) - 1)
        def _():
            sq_ref[...] = jnp.sum(acc_s[0], axis=0, keepdims=True)
            dg_ref[...] = jnp.sum(acc_s[1], axis=0, keepdims=True)

    vec = pl.BlockSpec((1, D), lambda i: (0, 0))
    tile = pl.BlockSpec((tm, D), lambda i: (i, 0))
    return _pcall(
        body, name="loss_head", grid=(T // tm,),
        in_specs=[tile, vec, tile], out_specs=[tile, vec, vec],
        out_shape=[jax.ShapeDtypeStruct((T, D), F32), jax.ShapeDtypeStruct((1, D), F32), jax.ShapeDtypeStruct((1, D), F32)],
        scratch_shapes=[pltpu.VMEM((2, RC, D), F32)],
        compiler_params=_params(("arbitrary",), 32),
    )(x, fg, tgt)


def _bwd_mix(dxo, p3, proj, wl, lng, lnb, sgw, sgbT, pwb, pb, psc, cw, cb):
    T = dxo.shape[0]
    tm = _tile(T, 256)
    nT = T // tm
    hb = tm // HALO

    def body(dxo_ref, p_ref, pj2_ref, ch_ref, bpp_ref, ccp_ref, chp_ref, w_ref, w2_ref, w3_ref, lng_ref, lnb_ref, sgw_ref,
             sgbT_ref, pw_ref, pb_ref, psc_ref, cw_ref, cb_ref,
             dpj2_ref, dp_ref, dsw_ref, dsbT_ref, vec_ref, dpw_ref,
             dy_s, dy2_s, dy3_s, dyc_s, ext_s, nxt_s, halo_s, dch_s, d_s, t_s, dy0_s, xh_s, rstd_s, acc_s, accb_s,
             wm_s, wmT_s, *blk):
        i = pl.program_id(0)
        step = pl.program_id(1)
        ti = nT - 1 - i
        p_ref, ch_ref, bpp_ref, ccp_ref, chp_ref = (_AsF32(r) for r in (p_ref, ch_ref, bpp_ref, ccp_ref, chp_ref))

        @pl.when((i == 0) & (step == 0))
        def _():
            dsw_ref[...] = jnp.zeros_like(dsw_ref)
            dsbT_ref[...] = jnp.zeros_like(dsbT_ref)
            vec_ref[...] = jnp.zeros_like(vec_ref)
            dpw_ref[...] = jnp.zeros_like(dpw_ref)
            halo_s[...] = jnp.zeros_like(halo_s)
            for g in range(GRP):
                wm = _masked_sgu(sgw_ref, g)
                wm_s[g] = wm.astype(BF)
                wmT_s[g] = wm.T.astype(BF)

        def flush(n_acc, rows_of):
            for a in range(n_acc):
                vec_ref[rows_of[a]:rows_of[a] + 1, :] += jnp.sum(acc_s[a], axis=0, keepdims=True)

        def all_rows(fn):
            for c in range(tm // RC):
                fn(c * RC)

        @pl.when(step == 0)
        def _():
            pj_ref, dpj_ref = _AsF32(pj2_ref, SHW), _From(dpj2_ref, 3)
            dy_s[...] = _mm_nt(dxo_ref[...], w_ref[...].reshape(D, D))

            def gate(k, r0):
                rows = pl.ds(r0, RC)
                for c0 in COLS:
                    cs = slice(c0, c0 + CW)
                    dm = dy_s[rows, cs]
                    s = _sigmoid(pj_ref[rows, k * D + c0:k * D + c0 + CW])
                    dp_ref[k, rows, cs] = (s * dm).astype(BF)
                    dpj_ref[k, rows, cs] = (dm * p_ref[k, rows, cs] * s * (1.0 - s)).astype(BF)

            all_rows(functools.partial(gate, 2))
            dyc_s[...] = _mm_nt(dp_ref[2], w2_ref[...].reshape(D, D))
            all_rows(functools.partial(gate, 0))
            all_rows(functools.partial(gate, 1))

        @pl.when(step == 0)
        def _():
            pj_ref, dpj_ref = _AsF32(pj2_ref), _From(dpj2_ref, 0)
            dy3_s[...] = _mm_nt(dp_ref[1], w3_ref[...].reshape(D, D))
            acc_s[...] = jnp.zeros_like(acc_s)
            ext_s[0:HALO, :] = jnp.where(ti > 0, ccp_ref[...] * chp_ref[...], 0.0)
            nxt_s[tm:tm + HALO, :] = halo_s[0]

            def f1(r0):
                rows = pl.ds(r0, RC)
                ext_s[pl.ds(HALO + r0, RC), :] = pj_ref[rows, D:2 * D] * ch_ref[rows, :]
            all_rows(f1)

            def f2(r0):
                rows = pl.ds(r0, RC)
                for c0 in COLS:
                    cs = slice(c0, c0 + CW)
                    e = ext_s[pl.ds(r0, 2 * RC), cs]
                    e0, e1, e2 = e[RC:2 * RC], pltpu.roll(e, 1, 0)[RC:2 * RC], pltpu.roll(e, 2, 0)[RC:2 * RC]
                    conv = cw_ref[2:3, cs] * e0 + cw_ref[1:2, cs] * e1 + cw_ref[0:1, cs] * e2 + cb_ref[:, cs]
                    cbv = pj_ref[rows, cs]
                    sz, sg = _silu_fg(pj_ref[rows, 2 * D + c0:2 * D + c0 + CW])
                    dyc = dyc_s[rows, cs]
                    dconv = dyc * cbv * sz
                    dpj_ref[0, rows, cs] = (dyc * conv * sz).astype(BF)
                    dpj_ref[2, rows, cs] = (dyc * cbv * conv * sg).astype(BF)
                    nxt_s[rows, cs] = dconv
                    acc_s[0, :, cs] += dconv
                    acc_s[1, :, cs] += dconv * e2
                    acc_s[2, :, cs] += dconv * e1
                    acc_s[3, :, cs] += dconv * e0
            all_rows(f2)

            def f3(r0):
                rows = pl.ds(r0, RC)
                for c0 in COLS:
                    cs = slice(c0, c0 + CW)
                    e = nxt_s[pl.ds(r0, 2 * RC), cs]
                    dcc = (cw_ref[2:3, cs] * e + cw_ref[1:2, cs] * pltpu.roll(e, 2 * RC - 1, 0)
                           + cw_ref[0:1, cs] * pltpu.roll(e, 2 * RC - 2, 0))[0:RC]
                    dpj_ref[1, rows, cs] = (dcc * ch_ref[rows, cs]).astype(BF)
                    dch_s[rows, cs] = dcc * pj_ref[rows, D + c0:D + c0 + CW]
            all_rows(f3)
            halo_s[0] = nxt_s[0:HALO, :]
            flush(4, (4, 5, 6, 7))

        @pl.when(step == 1)
        def _():
            pj_ref, dpj_ref = _AsF32(pj2_ref, SHW), _From(dpj2_ref, 3)
            dy2_s[...] = _mm_nt(dp_ref[0], w2_ref[...].reshape(D, D))
            acc_s[...] = jnp.zeros_like(acc_s)
            ext_s[0:HALO, :] = jnp.where(ti > 0, bpp_ref[...], 0.0)
            ext_s[HALO:HALO + tm, :] = pj_ref[:, 0:D]
            nxt_s[tm:tm + HALO, :] = halo_s[1]

            def f1(r0):
                rows = pl.ds(r0, RC)
                e = ext_s[pl.ds(r0, 2 * RC), :]
                for k, win in enumerate(POOL_WINDOWS):
                    gc = slice(k * PGW, (k + 1) * PGW)
                    s = _trail(e[:, gc], win)[RC:2 * RC]
                    d_s[rows, gc] = (s * _inv_count(ti * tm + r0, win) - e[RC:2 * RC, gc]).astype(BF)
            all_rows(f1)
            for k in range(NPG):
                gc = slice(k * PGW, (k + 1) * PGW)
                t_s[:, gc] = jnp.dot(d_s[:, gc], pw_ref[k], preferred_element_type=F32)

            def f2(r0):
                rows = pl.ds(r0, RC)
                for c0 in COLS:
                    cs = slice(c0, c0 + CW)
                    y0 = t_s[rows, cs] + pb_ref[:, cs]
                    sz, sg = _silu_fg(pj_ref[rows, D + c0:D + c0 + CW])
                    dyb = dy3_s[rows, cs]
                    dy0 = dyb * psc_ref[:, cs] * sz
                    acc_s[0, :, cs] += dy0
                    acc_s[1, :, cs] += dyb * y0 * sz
                    dy0_s[rows, cs] = dy0.astype(BF)
                    dpj_ref[1, rows, cs] = (dyb * y0 * psc_ref[:, cs] * sg).astype(BF)
            all_rows(f2)
            for k in range(NPG):
                gc = slice(k * PGW, (k + 1) * PGW)
                dpw_ref[k] += _mm_tn(d_s[:, gc], dy0_s[:, gc])
                t_s[:, gc] = _mm_nt(dy0_s[:, gc], pw_ref[k])

            def f3(r0):
                rows = pl.ds(r0, RC)
                for k, win in enumerate(POOL_WINDOWS):
                    gc = slice(k * PGW, (k + 1) * PGW)
                    nxt_s[rows, gc] = t_s[rows, gc] * _inv_count(ti * tm + r0, win)
            all_rows(f3)

            def f4(r0):
                rows = pl.ds(r0, RC)
                e = nxt_s[pl.ds(r0, 2 * RC), :]
                for k, win in enumerate(POOL_WINDOWS):
                    gc = slice(k * PGW, (k + 1) * PGW)
                    dpj_ref[0, rows, gc] = (_lead(e[:, gc], win)[0:RC] - t_s[rows, gc]).astype(BF)
                dpj_ref[2, rows, :] = dch_s[rows, :].astype(BF)
            all_rows(f4)
            halo_s[1] = nxt_s[0:HALO, :]
            flush(2, (2, 3))

        @pl.when(step == 1)
        def _():
            pj_ref, dpj_ref = _AsF32(pj2_ref), _From(dpj2_ref, 0)
            acc_s[...] = jnp.zeros_like(acc_s)
            accb_s[...] = jnp.zeros_like(accb_s)
            nblk = tm // BLK
            vn_b, t_b, dmix_b = blk[:nblk], blk[nblk:2 * nblk], blk[2 * nblk:]

            def block_rows(fn, n):
                for c in range(BLK // RC):
                    fn(n, c * RC)

            def f1(n, b0):
                rows, brow = pl.ds(n * BLK + b0, RC), pl.ds(b0, RC)
                gv = [_gelu(pj_ref[rows, D + c0:D + c0 + CW]) for c0 in COLS]
                mu = sum(_rowsum(v) for v in gv) * (1.0 / D)
                dv = [v - mu for v in gv]
                var = sum(_rowsum(v * v) for v in dv) * (1.0 / D)
                rstd = lax.rsqrt(var + LN_EPS)
                rstd_s[rows, :] = jnp.broadcast_to(rstd, (RC, BLK))
                for k, c0 in enumerate(COLS):
                    cs = slice(c0, c0 + CW)
                    xh = dv[k] * rstd
                    xh_s[rows, cs] = xh
                    vn_b[n][brow, cs] = (xh * lng_ref[:, cs] + lnb_ref[:, cs]).astype(BF)

            def mix(n):
                for g in range(GRP):
                    gc = slice(g * BLK, (g + 1) * BLK)
                    t_b[n][:, gc] = jnp.dot(wm_s[g], vn_b[n][:, gc], preferred_element_type=F32) + sgbT_ref[:, g:g + 1]

            def f2(n, b0):
                rows, brow = pl.ds(n * BLK + b0, RC), pl.ds(b0, RC)
                for c0 in COLS:
                    cs = slice(c0, c0 + CW)
                    gu, ggu = _gelu_fg(pj_ref[rows, cs])
                    sz, sg = _silu_fg(pj_ref[rows, 2 * D + c0:2 * D + c0 + CW])
                    dya = dy2_s[rows, cs]
                    mx = t_b[n][brow, cs]
                    dmix = dya * gu * sz
                    dpj_ref[0, rows, cs] = (dya * mx * sz * ggu).astype(BF)
                    dpj_ref[2, rows, cs] = (dya * gu * mx * sg).astype(BF)
                    dmix_b[n][brow, cs] = dmix.astype(BF)
                    accb_s[brow, cs] += dmix

            def mix_back(n):
                for g in range(GRP):
                    gc = slice(g * BLK, (g + 1) * BLK)
                    t_b[n][:, gc] = jnp.dot(wmT_s[g], dmix_b[n][:, gc], preferred_element_type=F32)
                    dsw_ref[g] += _mm_nt(dmix_b[n][:, gc], vn_b[n][:, gc])

            def f3(n, b0):
                rows, brow = pl.ds(n * BLK + b0, RC), pl.ds(b0, RC)
                rstd = rstd_s[rows, 0:1]
                dxh, m1, m2 = [], 0.0, 0.0
                for k, c0 in enumerate(COLS):
                    cs = slice(c0, c0 + CW)
                    dvn = t_b[n][brow, cs]
                    xh = xh_s[rows, cs]
                    acc_s[0, :, cs] += dvn * xh
                    acc_s[1, :, cs] += dvn
                    dxh.append(dvn * lng_ref[:, cs])
                    m1 = m1 + _rowsum(dxh[k])
                    m2 = m2 + _rowsum(dxh[k] * xh)
                m1 = m1 * (1.0 / D)
                m2 = m2 * (1.0 / D)
                for k, c0 in enumerate(COLS):
                    cs = slice(c0, c0 + CW)
                    _, ggv = _gelu_fg(pj_ref[rows, D + c0:D + c0 + CW])
                    dpj_ref[1, rows, cs] = (rstd * (dxh[k] - m1 - xh_s[rows, cs] * m2) * ggv).astype(BF)

            stages = (functools.partial(block_rows, f1), mix, functools.partial(block_rows, f2), mix_back,
                      functools.partial(block_rows, f3))
            for t in range(nblk + len(stages) - 1):
                for s in reversed(range(len(stages))):
                    if 0 <= t - s < nblk:
                        stages[s](t - s)
            for g in range(GRP):
                dsbT_ref[:, g:g + 1] += _rowsum(accb_s[:, g * BLK:(g + 1) * BLK])
            flush(2, (0, 1))

            @pl.when(i == nT - 1)
            def _():
                for g in range(GRP):
                    r = lax.broadcasted_iota(jnp.int32, (BLK, BLK), 0) // 64
                    c = lax.broadcasted_iota(jnp.int32, (BLK, BLK), 1) // 64
                    dsw_ref[g] = jnp.where(c <= r, dsw_ref[g], 0.0)

    def prev(col):
        return pl.BlockSpec((HALO, D), lambda i, p: (jnp.maximum((nT - 1 - i) * hb - 1, 0), col))

    vec = pl.BlockSpec((1, D), lambda i, p: (0, 0))
    const3 = lambda i, p: (0, 0, 0)
    return _pcall(
        body, name="bwd_mix", grid=(nT, 2),
        in_specs=[
            pl.BlockSpec((tm, D), lambda i, p: (nT - 1 - i, 0)),
            pl.BlockSpec((3, tm, D), lambda i, p: (0, nT - 1 - i, 0)),
            pl.BlockSpec((tm, 2 * SHW), lambda i, p: (nT - 1 - i, 1 - p)),
            pl.BlockSpec((tm, D), lambda i, p: (nT - 1 - i, 5)),
            prev(3), prev(7), prev(5),
            pl.BlockSpec((NSH, D // NSH, D), _branch_spec(3)),
            pl.BlockSpec((NSH, D // NSH, D), _branch_spec(lambda i, p: 2 - 2 * p)),
            pl.BlockSpec((NSH, D // NSH, D), _branch_spec(1)),
            vec, vec,
            pl.BlockSpec((GRP, BLK, BLK), const3),
            pl.BlockSpec((BLK, GRP), lambda i, p: (0, 0)),
            pl.BlockSpec((NPG, PGW, PGW), const3),
            vec, vec,
            pl.BlockSpec((3, D), lambda i, p: (0, 0)), vec,
        ],
        out_specs=[
            pl.BlockSpec((6, tm, D), lambda i, p: (1 - p, nT - 1 - i, 0)),
            pl.BlockSpec((3, tm, D), lambda i, p: (0, nT - 1 - i, 0)),
            pl.BlockSpec((GRP, BLK, BLK), const3),
            pl.BlockSpec((BLK, GRP), lambda i, p: (0, 0)),
            pl.BlockSpec((8, D), lambda i, p: (0, 0)),
            pl.BlockSpec((NPG, PGW, PGW), const3),
        ],
        out_shape=[
            jax.ShapeDtypeStruct((12, T, D), BF), jax.ShapeDtypeStruct((3, T, D), BF),
            jax.ShapeDtypeStruct((GRP, BLK, BLK), F32), jax.ShapeDtypeStruct((BLK, GRP), F32),
            jax.ShapeDtypeStruct((8, D), F32), jax.ShapeDtypeStruct((NPG, PGW, PGW), F32),
        ],
        scratch_shapes=[
            pltpu.VMEM((tm, D), F32),
            pltpu.VMEM((tm, D), F32),
            pltpu.VMEM((tm, D), F32),
            pltpu.VMEM((tm, D), F32),
            pltpu.VMEM((tm + HALO, D), F32),
            pltpu.VMEM((tm + HALO, D), F32),
            pltpu.VMEM((2, HALO, D), F32),
            pltpu.VMEM((tm, D), F32),
            pltpu.VMEM((tm, D), BF),
            pltpu.VMEM((tm, D), F32),
            pltpu.VMEM((tm, D), BF),
            pltpu.VMEM((tm, D), F32),
            pltpu.VMEM((tm, BLK), F32),
            pltpu.VMEM((4, RC, D), F32),
            pltpu.VMEM((BLK, D), F32),
            pltpu.VMEM((GRP, BLK, BLK), BF), pltpu.VMEM((GRP, BLK, BLK), BF),
        ] + [pltpu.VMEM((BLK, D), dt) for dt in (BF, F32, BF) for _ in range(tm // BLK)],
        compiler_params=_params(("arbitrary", "arbitrary"), 56),
    )(dxo, p3, proj, proj, proj, proj, proj, wl, wl, wl, lng, lnb, sgw, sgbT, pwb, pb, psc, cw, cb)


def _bwd_in(dproj, wl, x, dxo, ng):
    T = x.shape[0]
    tm = _tile(T, 256)
    nT = T // tm

    def body(dpj_ref, w_hbm, x_ref, dxo_ref, ng_ref, dx_ref, dng_ref, w_s, even_s, odd_s, g_s, wsem):
        i = pl.program_id(0)
        bufs = (even_s, odd_s)

        def w_copy(j):
            return pltpu.make_async_copy(w_hbm.at[j, pl.ds(0, SHW), :], w_s.at[j], wsem.at[j])

        def d_h():
            return sum(_mm_nt(dpj_ref[3 * j + b], w_s[j, b * D:(b + 1) * D, :]) for j in range(NSH) for b in range(3))

        def finish(prev):
            for c in range(tm // RC):
                finish_rows(prev, c * RC)

        def finish_rows(prev, r0):
            rows = pl.ds(r0, RC)
            xs = [x_ref[rows, c0:c0 + CW] for c0 in COLS]
            r = lax.rsqrt(sum(_rowsum(v * v) for v in xs) * (1.0 / D) + RMS_EPS)
            xh = [v * r for v in xs]
            dhg, m = [], 0.0
            for k, c0 in enumerate(COLS):
                cs = slice(c0, c0 + CW)
                dh = prev[rows, cs]
                g_s[:, cs] += dh * xh[k]
                dhg.append(dh * ng_ref[:, cs])
                m = m + _rowsum(dhg[k] * xh[k])
            m = m * (1.0 / D)
            for k, c0 in enumerate(COLS):
                cs = slice(c0, c0 + CW)
                dx_ref[rows, cs] = dxo_ref[rows, cs] + r * (dhg[k] - xh[k] * m)

        @pl.when(i == 0)
        def _():
            for j in range(NSH):
                w_copy(j).start()
            g_s[...] = jnp.zeros_like(g_s)
            for j in range(NSH):
                w_copy(j).wait()
            bufs[0][...] = d_h()

        for par in (0, 1):
            @pl.when((i % 2 == par) & (i > 0) & (i < nT))
            def _():
                bufs[par][...] = d_h()
                finish(bufs[1 - par])

        @pl.when(i == nT)
        def _():
            finish(bufs[(nT - 1) % 2])
            dng_ref[...] = jnp.sum(g_s[...], axis=0, keepdims=True)

    vec = pl.BlockSpec((1, D), lambda i: (0, 0))
    tile = pl.BlockSpec((tm, D), lambda i: (jnp.maximum(i - 1, 0), 0))
    return _pcall(
        body, name="bwd_in", grid=(nT + 1,),
        in_specs=[pl.BlockSpec((3 * NSH, tm, D), lambda i: (0, jnp.minimum(i, nT - 1), 0)), ANY, tile, tile, vec],
        out_specs=[tile, vec],
        out_shape=[jax.ShapeDtypeStruct((T, D), F32), jax.ShapeDtypeStruct((1, D), F32)],
        scratch_shapes=[pltpu.VMEM((NSH, SHW, D), BF), pltpu.VMEM((tm, D), F32), pltpu.VMEM((tm, D), F32),
                        pltpu.VMEM((RC, D), F32), pltpu.SemaphoreType.DMA((NSH,))],
        compiler_params=_params(("arbitrary",), 52),
    )(dproj, wl, x, dxo, ng)


def _tn_grad(a3, b3, split, out_map, name, into=None, after=None):
    nb, T, _ = b3.shape
    tk = _tile(T, 2048)
    nk = T // tk
    rows = D // split
    a_batched = a3.shape[0] > 1

    def body(a_ref, b_ref, *rest):
        o_ref, acc_s = rest[-2:]
        k = pl.program_id(1)

        @pl.when(k == 0)
        def _():
            acc_s[...] = _mm_tn(a_ref[...], b_ref[...])

        @pl.when((k > 0) & (k < nk))
        def _():
            acc_s[...] += _mm_tn(a_ref[...], b_ref[...])

        @pl.when(k >= nk - 1)
        def _():
            r0 = pl.multiple_of((k - (nk - 1)) * rows, rows)
            o_ref[...] = acc_s[pl.ds(r0, rows), :].astype(o_ref.dtype)

    def tok(k):
        return jnp.minimum(k, nk - 1)

    extra = ([] if into is None else [into]) + ([] if after is None else [after])
    return _pcall(
        body, name=name, grid=(nb, nk + split - 1),
        in_specs=[
            pl.BlockSpec((None, tk, D), (lambda n, k: (n, tok(k), 0)) if a_batched else (lambda n, k: (0, tok(k), 0))),
            pl.BlockSpec((None, tk, D), lambda n, k: (n, tok(k), 0)),
        ] + [ANY] * len(extra),
        out_specs=pl.BlockSpec((None, rows, D), lambda n, k: out_map(n, jnp.maximum(k - (nk - 1), 0))),
        out_shape=jax.ShapeDtypeStruct((NSH, PK_ROWS, D), BF),
        input_output_aliases={} if into is None else {2: 0},
        scratch_shapes=[pltpu.VMEM((D, D), F32)],
        compiler_params=_params(("arbitrary", "arbitrary"), 56),
    )(a3, b3, *extra)


def _place():
    x, y, c = lax.axis_index("x"), lax.axis_index("y"), lax.axis_index("c")
    chips = [(1 - x, y), (x, 1 - y), (1 - x, 1 - y)]
    return x, y, c, chips


def _peers(reach):
    x, y, c, chips = _place()
    if reach == "chips":
        return 2 * x + y, [((px, py, c), 2 * px + py) for px, py in chips]
    others = [(x, y, 1 - c)] + [(px, py, pc) for px, py in chips for pc in (c, 1 - c)]
    return 4 * x + 2 * y + c, [(pr, 4 * pr[0] + 2 * pr[1] + pr[2]) for pr in others]


ANY = pl.BlockSpec(memory_space=pl.ANY)


HBM = pl.BlockSpec(memory_space=pltpu.HBM)
SEM = pl.BlockSpec(memory_space=pltpu.SEMAPHORE)
EFFECT = pltpu.SideEffectType.DATAFLOW_SIDE_EFFECTING


def _own_slot(src, from_slot, name):
    rows = src.shape[-2]
    rb = rows // 9
    me = (2 * lax.axis_index("x") + lax.axis_index("y")).astype(jnp.int32).reshape(1)

    def body(me_ref, src_ref, land_ref):
        land_ref[...] = src_ref[...]

    if from_slot:
        src_spec = pl.BlockSpec((None, rb, D), lambda i, me_ref: (me_ref[0], i, 0))
    else:
        src_spec = pl.BlockSpec((rb, D), lambda i, me_ref: (i, 0))
    return _pcall(
        body, name=name,
        grid_spec=pltpu.PrefetchScalarGridSpec(
            num_scalar_prefetch=1, grid=(rows // rb,), in_specs=[src_spec],
            out_specs=pl.BlockSpec((None, rb, D), lambda i, me_ref: (me_ref[0], i, 0))),
        out_shape=jax.ShapeDtypeStruct((NSH, rows, D), src.dtype),
        compiler_params=_params(("arbitrary",), 32),
    )(me, src)


def _push_start(name, srcs, lands, per_peer, reach="chips"):
    n = len(srcs)
    npeer = 3 if reach == "chips" else 7
    ns = n * npeer

    def body(*refs):
        src, land = refs[:n], refs[n:2 * n]
        ssem, rsem = refs[2 * n:2 * n + ns], refs[2 * n + ns:2 * n + 2 * ns]
        token = refs[-1]
        me, peers = _peers(reach)
        for i in range(n):
            for j, (peer, slot) in enumerate(peers):
                pltpu.make_async_remote_copy(
                    src_ref=src[i].at[slot] if per_peer else src[i], dst_ref=land[i].at[me],
                    send_sem=ssem[npeer * i + j], recv_sem=rsem[npeer * i + j], device_id=peer, device_id_type=MESH).start()
        token[...] = jnp.zeros_like(token)

    ops = list(srcs) + list(lands)
    out = _pcall(
        body, name=name,
        out_shape=tuple([pltpu.SemaphoreType.DMA(())] * (2 * ns) + [pltpu.HBM(a.shape, a.dtype) for a in ops]
                        + [jax.ShapeDtypeStruct((8, 128), F32)]),
        in_specs=[HBM] * (2 * n),
        out_specs=tuple([SEM] * (2 * ns) + [HBM] * (2 * n) + [pl.BlockSpec(memory_space=pltpu.VMEM)]),
        input_output_aliases={i: 2 * ns + i for i in range(2 * n)},
        compiler_params=pltpu.CompilerParams(has_side_effects=EFFECT),
    )(*[pltpu.with_memory_space_constraint(a, pltpu.HBM) for a in ops])
    return out[:ns], out[ns:2 * ns], out[2 * ns:2 * ns + n], out[2 * ns + n:2 * ns + 2 * n], out[-1]


def _push_wait(name, src, land, ssem, rsem, after, per_peer, reach="chips"):
    npeer = len(ssem)

    def body(src_ref, land_ref, *rest):
        sems = rest[:2 * npeer]
        _, peers = _peers(reach)
        for j, (peer, slot) in enumerate(peers):
            cp = pltpu.make_async_remote_copy(
                src_ref=src_ref.at[slot] if per_peer else src_ref, dst_ref=land_ref.at[slot], send_sem=sems[j],
                recv_sem=sems[npeer + j], device_id=peer, device_id_type=MESH)
            cp.wait_send()
            cp.wait_recv()

    return _pcall(
        body, name=name,
        out_shape=(pltpu.HBM(src.shape, src.dtype), pltpu.HBM(land.shape, land.dtype)),
        in_specs=[HBM, HBM] + [SEM] * (2 * npeer) + [ANY], out_specs=(HBM, HBM),
        input_output_aliases={0: 0, 1: 1},
        compiler_params=pltpu.CompilerParams(has_side_effects=EFFECT),
    )(src, land, *ssem, *rsem, after)[1]


def _swap_sibling(arrs):
    n = len(arrs)

    def body(*refs):
        src, dst = refs[:n], refs[n:2 * n]
        ssem, rsem = refs[2 * n:]
        x, y, c, _ = _place()
        cps = [pltpu.make_async_remote_copy(src_ref=src[a], dst_ref=dst[a], send_sem=ssem.at[a], recv_sem=rsem.at[a],
                                            device_id=(x, y, 1 - c), device_id_type=MESH) for a in range(n)]
        for cp in cps:
            cp.start()
        for cp in cps:
            cp.wait()

    return _pcall(
        body, name="swap_sibling",
        in_specs=[ANY] * n, out_specs=[ANY] * n,
        out_shape=[jax.ShapeDtypeStruct(a.shape, a.dtype) for a in arrs],
        scratch_shapes=[pltpu.SemaphoreType.DMA((n,)), pltpu.SemaphoreType.DMA((n,))],
    )(*arrs)


def _gather_all(v):
    def body(src, dst, ssem, rsem, lsem):
        x, y, c, _ = _place()
        me = 4 * x + 2 * y + c
        peers = [(x, y, 1 - c), (1 - x, y, c), (1 - x, y, 1 - c), (x, 1 - y, c), (x, 1 - y, 1 - c),
                 (1 - x, 1 - y, c), (1 - x, 1 - y, 1 - c)]
        local = pltpu.make_async_copy(src, dst.at[me], lsem)
        local.start()
        sends = [pltpu.make_async_remote_copy(src_ref=src, dst_ref=dst.at[me], send_sem=ssem.at[j], recv_sem=rsem.at[j],
                                              device_id=pr, device_id_type=MESH) for j, pr in enumerate(peers)]
        for cp in sends:
            cp.start()
        for j, (px, py, pc) in enumerate(peers):
            pltpu.make_async_remote_copy(src_ref=src, dst_ref=dst.at[4 * px + 2 * py + pc], send_sem=ssem.at[j],
                                         recv_sem=rsem.at[j], device_id=(px, py, pc), device_id_type=MESH).wait_recv()
        for cp in sends:
            cp.wait_send()
        local.wait()

    return _pcall(
        body, name="gather_all", in_specs=[ANY], out_specs=ANY,
        out_shape=jax.ShapeDtypeStruct((8,) + v.shape, v.dtype),
        scratch_shapes=[pltpu.SemaphoreType.DMA((7,)), pltpu.SemaphoreType.DMA((7,)), pltpu.SemaphoreType.DMA(())],
    )(v)


def _sum_slots(r, rb):
    S = r.shape[0]

    def body(r_ref, o_ref):
        acc = r_ref[0].astype(F32)
        for s in range(1, S):
            acc = acc + r_ref[s].astype(F32)
        o_ref[...] = acc

    if r.ndim == 3:
        _, R, C = r.shape
        grid, blk, imap = (R // rb,), (S, rb, C), (lambda i: (0, i, 0))
        oblk, omap = (rb, C), (lambda i: (i, 0))
    else:
        _, K, R, C = r.shape
        grid, blk, imap = (K,), (S, None, R, C), (lambda i: (0, i, 0, 0))
        oblk, omap = (None, R, C), (lambda i: (i, 0, 0))
    return _pcall(
        body, name="sum_slots", grid=grid, in_specs=[pl.BlockSpec(blk, imap)], out_specs=pl.BlockSpec(oblk, omap),
        out_shape=jax.ShapeDtypeStruct(r.shape[1:], F32), compiler_params=_params(("arbitrary",), 48),
    )(r)


def _adamw(gs, g_spec, w, m, v, p_spec, prev, grid):
    ng = len(gs)
    bc1 = 1.0 - ADAM_B1 ** ADAM_STEP
    bc2 = 1.0 - ADAM_B2 ** ADAM_STEP

    def body(*refs):
        g = refs[0][...]
        for a in range(1, ng):
            g = g + refs[a][...]
        w_ref, m_ref, v_ref = refs[ng:ng + 3]
        go, do, mo, vo = refs[ng + 3 + 4:]
        mn = ADAM_B1 * m_ref[...] + (1.0 - ADAM_B1) * g
        vn = ADAM_B2 * v_ref[...] + (1.0 - ADAM_B2) * (g * g)
        go[...] = g
        mo[...] = mn
        vo[...] = vn
        do[...] = -ADAM_LR * ((mn / bc1) / (jnp.sqrt(vn / bc2) + ADAM_EPS) + ADAM_WD * w_ref[...])

    out = jax.ShapeDtypeStruct(w.shape, F32)
    k0 = ng + 3
    return _pcall(
        body, name="adamw", grid=grid,
        in_specs=[g_spec] * ng + [p_spec] * 3 + [ANY] * 4,
        out_specs=[p_spec] * 4, out_shape=[out] * 4,
        input_output_aliases={k0: 0, k0 + 1: 1, k0 + 2: 2, k0 + 3: 3},
        compiler_params=_params(("arbitrary",) * len(grid), 48),
    )(*gs, w, m, v, *prev)


def _empty4(w):
    return tuple(lax.empty(w.shape, F32) for _ in range(4))


N_SGW = L * GRP * BLK * BLK // D
O_NG, O_VEC, O_SGB, O_FG, O_SGW = 0, 8, 32, 40, 48
O_CW = O_SGW + N_SGW
N_PACK = O_CW + 16
PACK_RB = N_PACK // 3


def _pad_to(a, rows):
    return jnp.pad(a, ((0, rows - a.shape[0]), (0, 0)))


def _pack_small(ng, vecs, sgb, fg, sgw, cw):
    parts = [_pad_to(ng, 8), _pad_to(vecs.reshape(L * 5, D), 24), _pad_to(sgb.reshape(L, D), 8),
             _pad_to(fg.reshape(1, D), 8), sgw.reshape(N_SGW, D), _pad_to(cw, 16)]
    return jnp.concatenate(parts, axis=0)


def kernel(x, norm_g, w_in, sgu_ln_g, sgu_ln_b, sgu_w, sgu_b, pool_w, pool_b, pool_scale, conv_w, conv_b, w_branch_a, w_branch_b, w_branch_c, w_out, final_g, loss_target, m_norm_g, m_w_in, m_sgu_ln_g, m_sgu_ln_b, m_sgu_w, m_sgu_b, m_pool_w, m_pool_b, m_pool_scale, m_conv_w, m_conv_b, m_w_branch_a, m_w_branch_b, m_w_branch_c, m_w_out, m_final_g, v_norm_g, v_w_in, v_sgu_ln_g, v_sgu_ln_b, v_sgu_w, v_sgu_b, v_pool_w, v_pool_b, v_pool_scale, v_conv_w, v_conv_b, v_w_branch_a, v_w_branch_b, v_w_branch_c, v_w_out, v_final_g):
    cx, cy = lax.axis_index("x"), lax.axis_index("y")
    me = 2 * cx + cy
    xl, tgt = x[0], loss_target[0]
    q = D // NSH

    wq = w_in.astype(BF).reshape(L, D, 3, D).transpose(0, 2, 1, 3).reshape(L, SHW, D)
    brq = jnp.stack([w_branch_a, w_branch_b, w_branch_c, w_out], axis=1).astype(BF).reshape(L, D, D)
    pwq = pool_w.astype(BF).reshape(L, PK_CW - PK_PW, D)
    cwq = lax.bitcast_convert_type(conv_w, BF).reshape(L, 3 * q * 2)
    cwq = jnp.pad(cwq, ((0, 0), (0, 16 * D - 3 * q * 2))).reshape(L, 16, D)
    packs = [jnp.concatenate([wq[l], brq[l], pwq[l], cwq[l]], axis=0) for l in range(L)]
    lands = [_own_slot(packs[l], False, f"ag_own_{l}") for l in range(L)]
    ag_s, ag_r, packs, lands, tok = _push_start("ag_start", packs, lands, False)
    sgbT = sgu_b.transpose(0, 2, 1)

    def layer_weights(l, after):
        wl = _push_wait(f"ag_wait_{l}", packs[l], lands[l], ag_s[3 * l:3 * l + 3], ag_r[3 * l:3 * l + 3], after, False)
        pwb = wl[:, PK_PW:PK_CW].reshape(NSH, NPG, PGW // NSH, PGW).transpose(1, 0, 2, 3).reshape(NPG, PGW, PGW)
        cwb = wl[:, PK_CW:].reshape(NSH, 16 * D)[:, :3 * q * 2].reshape(NSH, 3, q, 2)
        cwf = lax.bitcast_convert_type(cwb, F32).transpose(1, 0, 2).reshape(3, D)
        small = (sgu_ln_g[l:l + 1], sgu_ln_b[l:l + 1], sgu_w[l], sgbT[l], pwb, pool_b[l:l + 1], pool_scale[l:l + 1],
                 cwf, conv_b[l:l + 1])
        return wl, small

    xs, saved, wts = [xl], [], []
    for l in range(L):
        wl, small = layer_weights(l, tok if l == 0 else xs[l])
        wts.append((wl, small))
        proj, h, y3 = _fwd_in(xs[l], norm_g[l:l + 1], wl, *small)
        p3, mg, xo = _fwd_out(y3, proj, xs[l], wl)
        saved.append((proj, h, y3, p3, mg))
        xs.append(xo)

    dx, sq, dfg = _loss_head(xs[L], final_g[None], tgt)
    loss = lax.psum(jnp.sum(sq) * (0.5 / D), ("x", "y", "c"))

    g_w_in = _empty4(w_in)
    g_br = [_empty4(w_out) for _ in range(4)]
    g_pw = _empty4(pool_w)
    dng, dvec, dsgw, dsgb = [None] * L, [None] * L, [None] * L, [None] * L
    branches = [(w_branch_a, m_w_branch_a, v_w_branch_a), (w_branch_b, m_w_branch_b, v_w_branch_b),
                (w_branch_c, m_w_branch_c, v_w_branch_c), (w_out, m_w_out, v_w_out)]
    nb = D // 128

    def finish(l, landed):
        nonlocal g_w_in, g_pw
        mine = _sum_slots(landed, PK_ROWS // 9)
        sums = [mine, _swap_sibling([mine])[0]]
        g_w_in = _adamw(sums, pl.BlockSpec((128, D), lambda b, i: (b * nb + i, 0)), w_in, m_w_in, v_w_in,
                        pl.BlockSpec((None, 128, D), lambda b, i: (l, i, b)), g_w_in, (3, nb))
        for k, (w, m, v) in enumerate(branches):
            g_br[k] = _adamw(sums, pl.BlockSpec((q, D), lambda i, k=k: (PK_BR // q + k, 0)), w, m, v,
                             pl.BlockSpec((None, q, D), lambda i: (l, 0, 0)), g_br[k], (1,))
        pools = [a[PK_PW:PK_CW].reshape(NPG, PGW // NSH, PGW) for a in sums]
        g_pw = _adamw(pools, pl.BlockSpec((None, PGW // NSH, PGW), lambda g: (g, 0, 0)), pool_w, m_pool_w, v_pool_w,
                      pl.BlockSpec((None, None, PGW // NSH, PGW), lambda g: (l, g, 0, 0)), g_pw, (NPG,))

    pend = None
    for l in reversed(range(L)):
        proj, h, y3, p3, mg = saved[l]
        wl, small = wts[l]
        dproj, dp3, dsgw[l], dsbT, dvec[l], dpw = _bwd_mix(dx, p3, proj, wl, *small)
        dsgb[l] = dsbT.T
        if l == 0:
            dv = jnp.stack(dvec)
            part = _pack_small(jnp.zeros((L, D), F32), dv[:, 0:5], jnp.stack(dsgb), dfg[0], jnp.stack(dsgw),
                               dv[:, 5:8].reshape(L * 3, D))
            zone = lax.dynamic_update_slice(lax.empty((8, N_PACK, D), F32), part[None],
                                            (2 * me + lax.axis_index("c"), 0, 0))
            sm_s, sm_r, (part,), (zone,), started = _push_start("small_start", [part], [zone], False, "all")
        grads = _tn_grad(h[None], dproj, 1, lambda n, s: (n // 3, n % 3, 0), "grad_w_in", after=started if l == 0 else None)
        grads = _tn_grad(y3, dp3, NSH, lambda n, s: (s, PK_BR // q + n, 0), "grad_w_branch", into=grads)
        grads = _tn_grad(mg[None], dx[None], NSH, lambda n, s: (s, PK_BR // q + 3, 0), "grad_w_out", into=grads)
        dpq = dpw.astype(BF).reshape(NPG, NSH, PGW // NSH, PGW).transpose(1, 0, 2, 3).reshape(NSH, PK_CW - PK_PW, D)
        grads = lax.dynamic_update_slice(grads, jnp.pad(dpq, ((0, 0), (0, PK_ROWS - PK_CW), (0, 0))), (0, PK_PW, 0))
        if pend is not None:
            landed = _push_wait(f"rs_wait_{pend[0]}", *pend[1:], dproj, True)
        land = _own_slot(grads, True, f"rs_own_{l}")
        ss, rs, (grads,), (land,), tok = _push_start(f"rs_start_{l}", [grads], [land], True)
        if pend is not None:
            finish(pend[0], landed)
        dx, dng[l] = _bwd_in(dproj, wl, xs[l], dx, norm_g[l:l + 1] + tok[0, 0])
        pend = (l, grads, land, ss, rs)
    finish(pend[0], _push_wait(f"rs_wait_{pend[0]}", *pend[1:], dx, True))

    zone = _push_wait("small_wait", part, zone, sm_s, sm_r, dx, False, "all")
    gng = _sum_slots(_gather_all(_pad_to(jnp.concatenate(dng), O_VEC)), O_VEC)
    gsmall = jnp.concatenate([gng, _sum_slots(zone, PACK_RB)[O_VEC:]])
    gcw =lax.dynamic_slice_in_dim(gsmall[O_CW:O_CW + L * 3], me * q, q, axis=1)
    gpack = jnp.concatenate([gsmall[:O_CW], _pad_to(gcw.reshape(L * 3 * q // D, D), 16)])

    def pack(ng, lg, lb, sw, sb, pb_, ps, cwv, cb_, fg):
        return _pack_small(ng, jnp.stack([lg, lb, pb_, ps, cb_], axis=1), sb, fg, sw, cwv.reshape(L * 3 * q // D, D))

    wp = pack(norm_g, sgu_ln_g, sgu_ln_b, sgu_w, sgu_b, pool_b, pool_scale, conv_w, conv_b, final_g)
    mp = pack(m_norm_g, m_sgu_ln_g, m_sgu_ln_b, m_sgu_w, m_sgu_b, m_pool_b, m_pool_scale, m_conv_w, m_conv_b, m_final_g)
    vp = pack(v_norm_g, v_sgu_ln_g, v_sgu_ln_b, v_sgu_w, v_sgu_b, v_pool_b, v_pool_scale, v_conv_w, v_conv_b, v_final_g)
    rows = pl.BlockSpec((PACK_RB, D), lambda i: (i, 0))
    sm = _adamw([gpack], rows, wp, mp, vp, rows, _empty4(wp), (N_PACK // PACK_RB,))

    def unpack(a):
        vv = a[O_VEC:O_VEC + L * 5].reshape(L, 5, D)
        sb = a[O_SGB:O_SGB + L].reshape(L, GRP, BLK)
        fg = a[O_FG]
        sw = a[O_SGW:O_SGW + N_SGW].reshape(L, GRP, BLK, BLK)
        cwv = a[O_CW:O_CW + L * 3 * q // D].reshape(L, 3, q)
        return dict(norm_g=a[O_NG:O_NG + L], w_in=None, sgu_ln_g=vv[:, 0], sgu_ln_b=vv[:, 1], sgu_w=sw, sgu_b=sb, pool_w=None,
                    pool_b=vv[:, 2], pool_scale=vv[:, 3], conv_w=cwv, conv_b=vv[:, 4], w_branch_a=None,
                    w_branch_b=None, w_branch_c=None, w_out=None, final_g=fg)

    outs = [loss, dx[None]]
    for kind in range(4):
        d = unpack(sm[kind])
        d.update(w_in=g_w_in[kind], pool_w=g_pw[kind], w_branch_a=g_br[0][kind], w_branch_b=g_br[1][kind],
                 w_branch_c=g_br[2][kind], w_out=g_br[3][kind])
        outs.extend(d[n] for n in ("norm_g", "w_in", "sgu_ln_g", "sgu_ln_b", "sgu_w", "sgu_b", "pool_w", "pool_b",
                                   "pool_scale", "conv_w", "conv_b", "w_branch_a", "w_branch_b", "w_branch_c", "w_out",
                                   "final_g"))
    return tuple(outs)
```

```python
import functools

import jax
import jax.numpy as jnp
from jax import lax
from jax.experimental import pallas as pl
from jax.experimental.pallas import tpu as pltpu

F32 = jnp.float32
BF = jnp.bfloat16
MESH = pl.DeviceIdType.MESH

D = 1024
L = 4
NSH = 4
SHW = 3 * D
NIN = NSH * SHW
GRP = 8
BLK = 128
NPG = 4
PGW = D // NPG
POOL_WINDOWS = (2, 4, 8, 16)
HALO = 16
RMS_EPS = 1e-6
LN_EPS = 1e-5
ADAM_LR, ADAM_B1, ADAM_B2, ADAM_EPS, ADAM_WD, ADAM_STEP = 0.001, 0.9, 0.999, 1e-8, 0.01, 10

RC = 16
CW = 512
COLS = tuple(range(0, D, CW))
MIB = 1 << 20

PK_BR = SHW
PK_PW = PK_BR + D
PK_CW = PK_PW + NPG * (PGW // NSH) * PGW // D
PK_ROWS = PK_CW + 64
PK_HALF = PK_ROWS // 2

_C0 = 0.7978845608028654
_C1 = 0.044715


def _pcall(body, **kw):
    return pl.pallas_call(body, **kw)


def _params(sem, vmem_mib):
    return pltpu.CompilerParams(dimension_semantics=sem, vmem_limit_bytes=vmem_mib * MIB)


def _mm(a, b):
    return jnp.dot(a.astype(BF), b.astype(BF), preferred_element_type=F32)


def _mm_nt(a, b):
    return lax.dot_general(a.astype(BF), b.astype(BF), (((1,), (1,)), ((), ())), preferred_element_type=F32)


def _mm_tn(a, b):
    return lax.dot_general(a.astype(BF), b.astype(BF), (((0,), (0,)), ((), ())), preferred_element_type=F32)


def _gelu(x):
    return 0.5 * x * (1.0 + jnp.tanh(_C0 * x * (1.0 + _C1 * x * x)))


def _gelu_fg(x):
    x2 = x * x
    t = jnp.tanh(_C0 * x * (1.0 + _C1 * x2))
    f = 0.5 * x * (1.0 + t)
    g = 0.5 * (1.0 + t) + 0.5 * x * (1.0 - t * t) * (_C0 * (1.0 + 3.0 * _C1 * x2))
    return f, g


def _sigmoid(x):
    return 0.5 * jnp.tanh(0.5 * x) + 0.5


def _silu(x):
    return x * _sigmoid(x)


def _silu_fg(x):
    s = _sigmoid(x)
    return x * s, s * (1.0 + x * (1.0 - s))


def _rowsum(v):
    return jnp.sum(v, axis=1, keepdims=True)


def _chunks(n_rows, fn, unroll=2):
    def body(c, carry):
        fn(pl.multiple_of(c * RC, RC))
        return carry
    lax.fori_loop(0, n_rows // RC, body, 0, unroll=unroll)


def _trail(e, win):
    s, sh = e, 1
    while sh < win:
        s = s + pltpu.roll(s, sh, 0)
        sh *= 2
    return s


def _lead(e, win):
    n = e.shape[0]
    s, sh = e, 1
    while sh < win:
        s = s + pltpu.roll(s, n - sh, 0)
        sh *= 2
    return s


def _inv_count(pos0, win):
    pos = pos0 + lax.broadcasted_iota(jnp.int32, (RC, 1), 0)
    return 1.0 / jnp.minimum(pos + 1, win).astype(F32)


def _masked_sgu(sgw_ref, g):
    r = lax.broadcasted_iota(jnp.int32, (BLK, BLK), 0) // 64
    c = lax.broadcasted_iota(jnp.int32, (BLK, BLK), 1) // 64
    return jnp.where(c <= r, sgw_ref[g], 0.0)


class _AsF32:
    def __init__(self, ref, col0=0):
        self.ref, self.col0 = ref, col0

    def __getitem__(self, idx):
        if self.col0:
            rows, cols = idx
            idx = (rows, slice(cols.start + self.col0, cols.stop + self.col0))
        return self.ref[idx].astype(F32)


class _From:
    def __init__(self, ref, lead0):
        self.ref, self.lead0 = ref, lead0

    def __setitem__(self, idx, val):
        self.ref[(self.lead0 + idx[0],) + tuple(idx[1:])] = val


def _tile(t, want):
    return min(t, want)


def _fwd_in(x, ng, wl, lng, lnb, sgw, sgbT, pwb, pb, psc, cw, cb):
    T = x.shape[0]
    tm = _tile(T, 256)
    nT = T // tm

    def body(x_ref, ng_ref, w_hbm, lng_ref, lnb_ref, sgw_ref, sgbT_ref, pw_ref, pb_ref, psc_ref, cw_ref, cb_ref,
             proj_hbm, h_ref, y_ref, w_s, pja, pjb, pjc, pjd, qa, qb, qc, qd, h_s, vn_s, mix_s, d_s, wm_s, extp_s, extc_s,
             wsem, psem):
        i = pl.program_id(0)
        pj = (pja, pjb, pjc, pjd)
        kept = (qa, qb, qc, qd)

        def w_copy(p):
            return pltpu.make_async_copy(w_hbm.at[p, pl.ds(0, SHW), :], w_s.at[p], wsem.at[p])

        def p_copy(p):
            dst = proj_hbm.at[pl.ds(pl.multiple_of(i * tm, tm), tm), pl.ds(p * SHW, SHW)]
            return pltpu.make_async_copy(kept[p], dst, psem.at[p])

        def keep(p):
            kept[p][...] = pj[p][...].astype(BF)
            p_copy(p).start()

        def all_rows(fn):
            for c in range(tm // RC):
                fn(c * RC)

        def buffer_free(p):
            @pl.when(i == 0)
            def _():
                w_copy(p).wait()

            @pl.when(i > 0)
            def _():
                p_copy(p).wait()

        def project(p, blocks):
            for b in blocks:
                pj[p][:, b * D:(b + 1) * D] = jnp.dot(h_s[...], w_s[p, b * D:(b + 1) * D, :],
                                                      preferred_element_type=F32)

        @pl.when(i == 0)
        def _():
            for p in range(NSH):
                w_copy(p).start()
            extp_s[0:HALO, :] = jnp.zeros((HALO, D), F32)
            extc_s[0:HALO, :] = jnp.zeros((HALO, D), F32)
            for g in range(GRP):
                wm_s[g] = _masked_sgu(sgw_ref, g).astype(BF)

        def norm_rows(r0):
            rows = pl.ds(r0, RC)
            xs = [x_ref[rows, c0:c0 + CW] for c0 in COLS]
            ms = sum(_rowsum(v * v) for v in xs) * (1.0 / D)
            r = lax.rsqrt(ms + RMS_EPS)
            for k, c0 in enumerate(COLS):
                hb = (xs[k] * r * ng_ref[:, c0:c0 + CW]).astype(BF)
                h_s[rows, c0:c0 + CW] = hb
                h_ref[rows, c0:c0 + CW] = hb

        def gating(first, rest):
            proj_ref = pj[0]
            first()

            def f1(r0):
                rows = pl.ds(r0, RC)
                gv = [_gelu(proj_ref[rows, D + c0:D + c0 + CW]) for c0 in COLS]
                mu = sum(_rowsum(v) for v in gv) * (1.0 / D)
                dv = [v - mu for v in gv]
                var = sum(_rowsum(v * v) for v in dv) * (1.0 / D)
                rstd = lax.rsqrt(var + LN_EPS)
                for k, c0 in enumerate(COLS):
                    vn_s[rows, c0:c0 + CW] = (dv[k] * rstd * lng_ref[:, c0:c0 + CW] + lnb_ref[:, c0:c0 + CW]).astype(BF)
            all_rows(f1)
            for g in range(GRP):
                w = wm_s[g]
                bcol = sgbT_ref[:, g:g + 1]
                gc = slice(g * BLK, (g + 1) * BLK)
                for n in range(tm // BLK):
                    rr = slice(n * BLK, (n + 1) * BLK)
                    mix_s[rr, gc] = jnp.dot(w, vn_s[rr, gc], preferred_element_type=F32) + bcol
            rest()

            def f2(r0):
                rows = pl.ds(r0, RC)
                for c0 in COLS:
                    au = proj_ref[rows, c0:c0 + CW]
                    az = proj_ref[rows, 2 * D + c0:2 * D + c0 + CW]
                    y_ref[0, rows, c0:c0 + CW] = (_gelu(au) * mix_s[rows, c0:c0 + CW] * _silu(az)).astype(BF)
            all_rows(f2)

        def pooling(first, rest):
            proj_ref = pj[1]
            first()
            extp_s[HALO:HALO + tm, :] = proj_ref[:, 0:D]

            def f1(r0):
                rows = pl.ds(r0, RC)
                e = extp_s[pl.ds(r0, 2 * RC), :]
                for k, win in enumerate(POOL_WINDOWS):
                    gc = slice(k * PGW, (k + 1) * PGW)
                    s = _trail(e[:, gc], win)[RC:2 * RC]
                    d_s[rows, gc] = (s * _inv_count(i * tm + r0, win) - e[RC:2 * RC, gc]).astype(BF)
            all_rows(f1)
            extp_s[0:HALO, :] = extp_s[tm:tm + HALO, :]
            for k in range(NPG):
                gc = slice(k * PGW, (k + 1) * PGW)
                mix_s[:, gc] = jnp.dot(d_s[:, gc], pw_ref[k], preferred_element_type=F32)
            rest()

            def f2(r0):
                rows = pl.ds(r0, RC)
                for c0 in COLS:
                    cs = slice(c0, c0 + CW)
                    bz = proj_ref[rows, D + c0:D + c0 + CW]
                    y_ref[1, rows, cs] = ((mix_s[rows, cs] + pb_ref[:, cs]) * psc_ref[:, cs] * _silu(bz)).astype(BF)
            all_rows(f2)

        def convolution(first, rest):
            proj_ref = pj[2]
            first()

            def f1(r0):
                rows = pl.ds(r0, RC)
                extc_s[pl.ds(HALO + r0, RC), :] = proj_ref[rows, D:2 * D] * pj[1][rows, 2 * D:3 * D]
            all_rows(f1)
            rest()

            def f2(r0):
                rows = pl.ds(r0, RC)
                for c0 in COLS:
                    cs = slice(c0, c0 + CW)
                    e = extc_s[pl.ds(r0, 2 * RC), cs]
                    conv = (cw_ref[2:3, cs] * e + cw_ref[1:2, cs] * pltpu.roll(e, 1, 0)
                            + cw_ref[0:1, cs] * pltpu.roll(e, 2, 0))[RC:2 * RC] + cb_ref[:, cs]
                    cbv = proj_ref[rows, cs]
                    cz = proj_ref[rows, 2 * D + c0:2 * D + c0 + CW]
                    y_ref[2, rows, cs] = (cbv * conv * _silu(cz)).astype(BF)
            all_rows(f2)
            extc_s[0:HALO, :] = extc_s[tm:tm + HALO, :]

        all_rows(norm_rows)
        buffer_free(0)
        project(0, (0, 1, 2))
        keep(0)
        for p, mixer, early in ((1, gating, 2), (2, pooling, 1), (3, convolution, 1)):
            buffer_free(p)
            mixer(functools.partial(project, p, range(early)), functools.partial(project, p, range(early, 3)))
            keep(p)

        @pl.when(i == nT - 1)
        def _():
            for p in range(NSH):
                p_copy(p).wait()

    vec = pl.BlockSpec((1, D), lambda i: (0, 0))
    return _pcall(
        body, name="fwd_in", grid=(nT,),
        in_specs=[
            pl.BlockSpec((tm, D), lambda i: (i, 0)), vec, ANY, vec, vec,
            pl.BlockSpec((GRP, BLK, BLK), lambda i: (0, 0, 0)),
            pl.BlockSpec((BLK, GRP), lambda i: (0, 0)),
            pl.BlockSpec((NPG, PGW, PGW), lambda i: (0, 0, 0)),
            vec, vec,
            pl.BlockSpec((3, D), lambda i: (0, 0)), vec,
        ],
        out_specs=[ANY, pl.BlockSpec((tm, D), lambda i: (i, 0)), pl.BlockSpec((3, tm, D), lambda i: (0, i, 0))],
        out_shape=[jax.ShapeDtypeStruct((T, NIN), BF), jax.ShapeDtypeStruct((T, D), BF),
                   jax.ShapeDtypeStruct((3, T, D), BF)],
        scratch_shapes=[
            pltpu.VMEM((NSH, SHW, D), BF),
            pltpu.VMEM((tm, SHW), F32), pltpu.VMEM((tm, SHW), F32), pltpu.VMEM((tm, SHW), F32), pltpu.VMEM((tm, SHW), F32),
            pltpu.VMEM((tm, SHW), BF), pltpu.VMEM((tm, SHW), BF), pltpu.VMEM((tm, SHW), BF), pltpu.VMEM((tm, SHW), BF),
            pltpu.VMEM((tm, D), BF), pltpu.VMEM((tm, D), BF), pltpu.VMEM((tm, D), F32),
            pltpu.VMEM((tm, D), BF), pltpu.VMEM((GRP, BLK, BLK), BF),
            pltpu.VMEM((tm + HALO, D), F32), pltpu.VMEM((tm + HALO, D), F32),
            pltpu.SemaphoreType.DMA((NSH,)), pltpu.SemaphoreType.DMA((NSH,)),
        ],
        compiler_params=_params(("arbitrary",), 58),
    )(x, ng, wl, lng, lnb, sgw, sgbT, pwb, pb, psc, cw, cb)


def _branch_spec(which):
    q = D // NSH
    return lambda *g: (0, PK_BR // q + (which(*g) if callable(which) else which), 0)


def _fwd_out(y3, proj, x, wl):
    T = x.shape[0]
    tm = _tile(T, 256)
    q = D // NSH

    nT = T // tm

    def body(y_ref, gl_ref, x_ref, wa_ref, wb_ref, wc_ref, wo_ref, p_ref, mg_ref, xo_ref, even_s, odd_s):
        i = pl.program_id(0)
        gl_ref = _AsF32(gl_ref)

        def branches(buf):
            for k, w_ref in enumerate((wa_ref, wb_ref, wc_ref)):
                buf[k] = jnp.dot(y_ref[k], w_ref[...].reshape(D, D), preferred_element_type=F32)

        def merge(buf):
            for c in range(tm // RC):
                rows = pl.ds(c * RC, RC)
                for c0 in COLS:
                    cs = slice(c0, c0 + CW)
                    m = 0.0
                    for k in range(3):
                        pk = buf[k, rows, cs]
                        p_ref[k, rows, cs] = pk.astype(BF)
                        m = m + _sigmoid(gl_ref[rows, k * D + c0:k * D + c0 + CW]) * pk
                    mg_ref[rows, cs] = m.astype(BF)
            xo_ref[...] = x_ref[...] + jnp.dot(mg_ref[...], wo_ref[...].reshape(D, D), preferred_element_type=F32)

        bufs = (even_s, odd_s)

        @pl.when(i == 0)
        def _():
            branches(bufs[0])

        for par in (0, 1):
            @pl.when((i % 2 == par) & (i > 0) & (i < nT))
            def _():
                branches(bufs[par])
                merge(bufs[1 - par])

        @pl.when(i == nT)
        def _():
            merge(bufs[(nT - 1) % 2])

    prev = lambda i: jnp.maximum(i - 1, 0)
    return _pcall(
        body, name="fwd_out", grid=(nT + 1,),
        in_specs=[
            pl.BlockSpec((3, tm, D), lambda i: (0, jnp.minimum(i, nT - 1), 0)),
            pl.BlockSpec((tm, SHW), lambda i: (prev(i), 3)),
            pl.BlockSpec((tm, D), lambda i: (prev(i), 0)),
        ] + [pl.BlockSpec((NSH, q, D), _branch_spec(k)) for k in range(4)],
        out_specs=[
            pl.BlockSpec((3, tm, D), lambda i: (0, prev(i), 0)),
            pl.BlockSpec((tm, D), lambda i: (prev(i), 0)),
            pl.BlockSpec((tm, D), lambda i: (prev(i), 0)),
        ],
        out_shape=[jax.ShapeDtypeStruct((3, T, D), BF), jax.ShapeDtypeStruct((T, D), BF),
                   jax.ShapeDtypeStruct((T, D), F32)],
        scratch_shapes=[pltpu.VMEM((3, tm, D), F32), pltpu.VMEM((3, tm, D), F32)],
        compiler_params=_params(("arbitrary",), 52),
    )(y3, proj, x, wl, wl, wl, wl)


def _loss_head(x, fg, tgt):
    T = x.shape[0]
    tm = _tile(T, 512)

    def body(x_ref, g_ref, t_ref, dx_ref, sq_ref, dg_ref, acc_s):
        i = pl.program_id(0)

        @pl.when(i == 0)
        def _():
            acc_s[...] = jnp.zeros_like(acc_s)

        def f(r0):
            rows = pl.ds(r0, RC)
            xs = [x_ref[rows, c0:c0 + CW] for c0 in COLS]
            r = lax.rsqrt(sum(_rowsum(v * v) for v in xs) * (1.0 / D) + RMS_EPS)
            xh = [v * r for v in xs]
            dyg, m = [], 0.0
            for k, c0 in enumerate(COLS):
                cs = slice(c0, c0 + CW)
                err = xh[k] * g_ref[:, cs] - t_ref[rows, cs]
                acc_s[0, :, cs] += err * err
                dy = err * (1.0 / D)
                acc_s[1, :, cs] += dy * xh[k]
                dyg.append(dy * g_ref[:, cs])
                m = m + _rowsum(dyg[k] * xh[k])
            m = m * (1.0 / D)
            for k, c0 in enumerate(COLS):
                dx_ref[rows, c0:c0 + CW] = r * (dyg[k] - xh[k] * m)
        _chunks(tm, f)

        @pl.when(i == pl.num_programs(0) - 1)
        def _():
            sq_ref[...] = jnp.sum(acc_s[0], axis=0, keepdims=True)
            dg_ref[...] = jnp.sum(acc_s[1], axis=0, keepdims=True)

    vec = pl.BlockSpec((1, D), lambda i: (0, 0))
    tile = pl.BlockSpec((tm, D), lambda i: (i, 0))
    return _pcall(
        body, name="loss_head", grid=(T // tm,),
        in_specs=[tile, vec, tile], out_specs=[tile, vec, vec],
        out_shape=[jax.ShapeDtypeStruct((T, D), F32), jax.ShapeDtypeStruct((1, D), F32), jax.ShapeDtypeStruct((1, D), F32)],
        scratch_shapes=[pltpu.VMEM((2, RC, D), F32)],
        compiler_params=_params(("arbitrary",), 32),
    )(x, fg, tgt)


def _bwd_mix(dxo, p3, proj, wl, lng, lnb, sgw, sgbT, pwb, pb, psc, cw, cb):
    T = dxo.shape[0]
    tm = _tile(T, 256)
    nT = T // tm
    hb = tm // HALO

    def body(dxo_ref, p_ref, pj2_ref, ch_ref, bpp_ref, ccp_ref, chp_ref, w_ref, w2_ref, w3_ref, lng_ref, lnb_ref, sgw_ref,
             sgbT_ref, pw_ref, pb_ref, psc_ref, cw_ref, cb_ref,
             dpj2_ref, dp_ref, dsw_ref, dsbT_ref, vec_ref, dpw_ref,
             dy_s, dy2_s, dy3_s, dyc_s, ext_s, nxt_s, halo_s, dch_s, xh_s, rstd_s, acc_s, accb_s,
             wm_s, wmT_s, *blk):
        i = pl.program_id(0)
        step = pl.program_id(1)
        ti = nT - 1 - i
        p_ref, ch_ref, bpp_ref, ccp_ref, chp_ref = (_AsF32(r) for r in (p_ref, ch_ref, bpp_ref, ccp_ref, chp_ref))

        @pl.when((i == 0) & (step == 0))
        def _():
            dsw_ref[...] = jnp.zeros_like(dsw_ref)
            dsbT_ref[...] = jnp.zeros_like(dsbT_ref)
            vec_ref[...] = jnp.zeros_like(vec_ref)
            dpw_ref[...] = jnp.zeros_like(dpw_ref)
            halo_s[...] = jnp.zeros_like(halo_s)
            for g in range(GRP):
                wm = _masked_sgu(sgw_ref, g)
                wm_s[g] = wm.astype(BF)
                wmT_s[g] = wm.T.astype(BF)

        def flush(n_acc, rows_of):
            for a in range(n_acc):
                vec_ref[rows_of[a]:rows_of[a] + 1, :] += jnp.sum(acc_s[a], axis=0, keepdims=True)

        def all_rows(fn):
            for c in range(tm // RC):
                fn(c * RC)

        @pl.when(step == 0)
        def _():
            pj_ref, dpj_ref = _AsF32(pj2_ref, SHW), _From(dpj2_ref, 3)
            dy_s[...] = _mm_nt(dxo_ref[...], w_ref[...].reshape(D, D))

            def gate(k, r0):
                rows = pl.ds(r0, RC)
                for c0 in COLS:
                    cs = slice(c0, c0 + CW)
                    dm = dy_s[rows, cs]
                    s = _sigmoid(pj_ref[rows, k * D + c0:k * D + c0 + CW])
                    dp_ref[k, rows, cs] = (s * dm).astype(BF)
                    dpj_ref[k, rows, cs] = (dm * p_ref[k, rows, cs] * s * (1.0 - s)).astype(BF)

            all_rows(functools.partial(gate, 2))
            dyc_s[...] = _mm_nt(dp_ref[2], w2_ref[...].reshape(D, D))
            all_rows(functools.partial(gate, 0))
            all_rows(functools.partial(gate, 1))

        @pl.when(step == 0)
        def _():
            pj_ref, dpj_ref = _AsF32(pj2_ref), _From(dpj2_ref, 0)
            dy3_s[...] = _mm_nt(dp_ref[1], w3_ref[...].reshape(D, D))
            acc_s[...] = jnp.zeros_like(acc_s)
            ext_s[0:HALO, :] = jnp.where(ti > 0, ccp_ref[...] * chp_ref[...], 0.0)
            nxt_s[tm:tm + HALO, :] = halo_s[0]

            def f1(r0):
                rows = pl.ds(r0, RC)
                ext_s[pl.ds(HALO + r0, RC), :] = pj_ref[rows, D:2 * D] * ch_ref[rows, :]
            all_rows(f1)

            def f2(r0):
                rows = pl.ds(r0, RC)
                for c0 in COLS:
                    cs = slice(c0, c0 + CW)
                    e = ext_s[pl.ds(r0, 2 * RC), cs]
                    e0, e1, e2 = e[RC:2 * RC], pltpu.roll(e, 1, 0)[RC:2 * RC], pltpu.roll(e, 2, 0)[RC:2 * RC]
                    conv = cw_ref[2:3, cs] * e0 + cw_ref[1:2, cs] * e1 + cw_ref[0:1, cs] * e2 + cb_ref[:, cs]
                    cbv = pj_ref[rows, cs]
                    sz, sg = _silu_fg(pj_ref[rows, 2 * D + c0:2 * D + c0 + CW])
                    dyc = dyc_s[rows, cs]
                    dconv = dyc * cbv * sz
                    dpj_ref[0, rows, cs] = (dyc * conv * sz).astype(BF)
                    dpj_ref[2, rows, cs] = (dyc * cbv * conv * sg).astype(BF)
                    nxt_s[rows, cs] = dconv
                    acc_s[0, :, cs] += dconv
                    acc_s[1, :, cs] += dconv * e2
                    acc_s[2, :, cs] += dconv * e1
                    acc_s[3, :, cs] += dconv * e0
            all_rows(f2)

            def f3(r0):
                rows = pl.ds(r0, RC)
                for c0 in COLS:
                    cs = slice(c0, c0 + CW)
                    e = nxt_s[pl.ds(r0, 2 * RC), cs]
                    dcc = (cw_ref[2:3, cs] * e + cw_ref[1:2, cs] * pltpu.roll(e, 2 * RC - 1, 0)
                           + cw_ref[0:1, cs] * pltpu.roll(e, 2 * RC - 2, 0))[0:RC]
                    dpj_ref[1, rows, cs] = (dcc * ch_ref[rows, cs]).astype(BF)
                    dch_s[rows, cs] = dcc * pj_ref[rows, D + c0:D + c0 + CW]
            all_rows(f3)
            halo_s[0] = nxt_s[0:HALO, :]
            flush(4, (4, 5, 6, 7))

        @pl.when(step == 1)
        def _():
            pj_ref, dpj_ref = _AsF32(pj2_ref, SHW), _From(dpj2_ref, 3)
            dy2_s[...] = _mm_nt(dp_ref[0], w2_ref[...].reshape(D, D))
            acc_s[...] = jnp.zeros_like(acc_s)
            ext_s[0:HALO, :] = jnp.where(ti > 0, bpp_ref[...], 0.0)
            ext_s[HALO:HALO + tm, :] = pj_ref[:, 0:D]
            nxt_s[tm:tm + HALO, :] = halo_s[1]

            nblk = tm // BLK
            d_b, t_b, dy0_b = blk[:nblk], blk[nblk:2 * nblk], blk[2 * nblk:]

            def block_rows(fn, n):
                for c in range(BLK // RC):
                    fn(n, c * RC)

            def f1(n, b0):
                r0, brow = n * BLK + b0, pl.ds(b0, RC)
                e = ext_s[pl.ds(r0, 2 * RC), :]
                for k, win in enumerate(POOL_WINDOWS):
                    gc = slice(k * PGW, (k + 1) * PGW)
                    s = _trail(e[:, gc], win)[RC:2 * RC]
                    d_b[n][brow, gc] = (s * _inv_count(ti * tm + r0, win) - e[RC:2 * RC, gc]).astype(BF)

            def pool(n):
                for k in range(NPG):
                    gc = slice(k * PGW, (k + 1) * PGW)
                    t_b[n][:, gc] = jnp.dot(d_b[n][:, gc], pw_ref[k], preferred_element_type=F32)

            def f2(n, b0):
                rows, brow = pl.ds(n * BLK + b0, RC), pl.ds(b0, RC)
                for c0 in COLS:
                    cs = slice(c0, c0 + CW)
                    y0 = t_b[n][brow, cs] + pb_ref[:, cs]
                    sz, sg = _silu_fg(pj_ref[rows, D + c0:D + c0 + CW])
                    dyb = dy3_s[rows, cs]
                    dy0 = dyb * psc_ref[:, cs] * sz
                    acc_s[0, :, cs] += dy0
                    acc_s[1, :, cs] += dyb * y0 * sz
                    dy0_b[n][brow, cs] = dy0.astype(BF)
                    dpj_ref[1, rows, cs] = (dyb * y0 * psc_ref[:, cs] * sg).astype(BF)

            def pool_back(n):
                for k in range(NPG):
                    gc = slice(k * PGW, (k + 1) * PGW)
                    dpw_ref[k] += _mm_tn(d_b[n][:, gc], dy0_b[n][:, gc])
                    t_b[n][:, gc] = _mm_nt(dy0_b[n][:, gc], pw_ref[k])

            def f3(n, b0):
                r0, brow = n * BLK + b0, pl.ds(b0, RC)
                for k, win in enumerate(POOL_WINDOWS):
                    gc = slice(k * PGW, (k + 1) * PGW)
                    nxt_s[pl.ds(r0, RC), gc] = t_b[n][brow, gc] * _inv_count(ti * tm + r0, win)

            def f4(n, b0):
                r0, brow = n * BLK + b0, pl.ds(b0, RC)
                rows = pl.ds(r0, RC)
                e = nxt_s[pl.ds(r0, 2 * RC), :]
                for k, win in enumerate(POOL_WINDOWS):
                    gc = slice(k * PGW, (k + 1) * PGW)
                    dpj_ref[0, rows, gc] = (_lead(e[:, gc], win)[0:RC] - t_b[n][brow, gc]).astype(BF)
                dpj_ref[2, rows, :] = dch_s[rows, :].astype(BF)

            def spread(n):
                block_rows(f3, n)
                block_rows(f4, n)

            stages = (functools.partial(block_rows, f1), pool, functools.partial(block_rows, f2), pool_back, spread)
            for t in range(nblk + len(stages) - 1):
                for s in reversed(range(len(stages))):
                    if 0 <= t - s < nblk:
                        stages[s](nblk - 1 - (t - s))
            halo_s[1] = nxt_s[0:HALO, :]
            flush(2, (2, 3))

        @pl.when(step == 1)
        def _():
            pj_ref, dpj_ref = _AsF32(pj2_ref), _From(dpj2_ref, 0)
            acc_s[...] = jnp.zeros_like(acc_s)
            accb_s[...] = jnp.zeros_like(accb_s)
            nblk = tm // BLK
            vn_b, t_b, dmix_b = blk[:nblk], blk[nblk:2 * nblk], blk[2 * nblk:]

            def block_rows(fn, n):
                for c in range(BLK // RC):
                    fn(n, c * RC)

            def f1(n, b0):
                rows, brow = pl.ds(n * BLK + b0, RC), pl.ds(b0, RC)
                gv = [_gelu(pj_ref[rows, D + c0:D + c0 + CW]) for c0 in COLS]
                mu = sum(_rowsum(v) for v in gv) * (1.0 / D)
                dv = [v - mu for v in gv]
                var = sum(_rowsum(v * v) for v in dv) * (1.0 / D)
                rstd = lax.rsqrt(var + LN_EPS)
                rstd_s[rows, :] = jnp.broadcast_to(rstd, (RC, BLK))
                for k, c0 in enumerate(COLS):
                    cs = slice(c0, c0 + CW)
                    xh = dv[k] * rstd
                    xh_s[rows, cs] = xh
                    vn_b[n][brow, cs] = (xh * lng_ref[:, cs] + lnb_ref[:, cs]).astype(BF)

            def mix(n):
                for g in range(GRP):
                    gc = slice(g * BLK, (g + 1) * BLK)
                    t_b[n][:, gc] = jnp.dot(wm_s[g], vn_b[n][:, gc], preferred_element_type=F32) + sgbT_ref[:, g:g + 1]

            def f2(n, b0):
                rows, brow = pl.ds(n * BLK + b0, RC), pl.ds(b0, RC)
                for c0 in COLS:
                    cs = slice(c0, c0 + CW)
                    gu, ggu = _gelu_fg(pj_ref[rows, cs])
                    sz, sg = _silu_fg(pj_ref[rows, 2 * D + c0:2 * D + c0 + CW])
                    dya = dy2_s[rows, cs]
                    mx = t_b[n][brow, cs]
                    dmix = dya * gu * sz
                    dpj_ref[0, rows, cs] = (dya * mx * sz * ggu).astype(BF)
                    dpj_ref[2, rows, cs] = (dya * gu * mx * sg).astype(BF)
                    dmix_b[n][brow, cs] = dmix.astype(BF)
                    accb_s[brow, cs] += dmix

            def mix_back(n):
                for g in range(GRP):
                    gc = slice(g * BLK, (g + 1) * BLK)
                    t_b[n][:, gc] = jnp.dot(wmT_s[g], dmix_b[n][:, gc], preferred_element_type=F32)
                    dsw_ref[g] += _mm_nt(dmix_b[n][:, gc], vn_b[n][:, gc])

            def f3(n, b0):
                rows, brow = pl.ds(n * BLK + b0, RC), pl.ds(b0, RC)
                rstd = rstd_s[rows, 0:1]
                dxh, m1, m2 = [], 0.0, 0.0
                for k, c0 in enumerate(COLS):
                    cs = slice(c0, c0 + CW)
                    dvn = t_b[n][brow, cs]
                    xh = xh_s[rows, cs]
                    acc_s[0, :, cs] += dvn * xh
                    acc_s[1, :, cs] += dvn
                    dxh.append(dvn * lng_ref[:, cs])
                    m1 = m1 + _rowsum(dxh[k])
                    m2 = m2 + _rowsum(dxh[k] * xh)
                m1 = m1 * (1.0 / D)
                m2 = m2 * (1.0 / D)
                for k, c0 in enumerate(COLS):
                    cs = slice(c0, c0 + CW)
                    _, ggv = _gelu_fg(pj_ref[rows, D + c0:D + c0 + CW])
                    dpj_ref[1, rows, cs] = (rstd * (dxh[k] - m1 - xh_s[rows, cs] * m2) * ggv).astype(BF)

            stages = (functools.partial(block_rows, f1), mix, functools.partial(block_rows, f2), mix_back,
                      functools.partial(block_rows, f3))
            for t in range(nblk + len(stages) - 1):
                for s in reversed(range(len(stages))):
                    if 0 <= t - s < nblk:
                        stages[s](t - s)
            for g in range(GRP):
                dsbT_ref[:, g:g + 1] += _rowsum(accb_s[:, g * BLK:(g + 1) * BLK])
            flush(2, (0, 1))

            @pl.when(i == nT - 1)
            def _():
                for g in range(GRP):
                    r = lax.broadcasted_iota(jnp.int32, (BLK, BLK), 0) // 64
                    c = lax.broadcasted_iota(jnp.int32, (BLK, BLK), 1) // 64
                    dsw_ref[g] = jnp.where(c <= r, dsw_ref[g], 0.0)

    def prev(col):
        return pl.BlockSpec((HALO, D), lambda i, p: (jnp.maximum((nT - 1 - i) * hb - 1, 0), col))

    vec = pl.BlockSpec((1, D), lambda i, p: (0, 0))
    const3 = lambda i, p: (0, 0, 0)
    return _pcall(
        body, name="bwd_mix", grid=(nT, 2),
        in_specs=[
            pl.BlockSpec((tm, D), lambda i, p: (nT - 1 - i, 0)),
            pl.BlockSpec((3, tm, D), lambda i, p: (0, nT - 1 - i, 0)),
            pl.BlockSpec((tm, 2 * SHW), lambda i, p: (nT - 1 - i, 1 - p)),
            pl.BlockSpec((tm, D), lambda i, p: (nT - 1 - i, 5)),
            prev(3), prev(7), prev(5),
            pl.BlockSpec((NSH, D // NSH, D), _branch_spec(3)),
            pl.BlockSpec((NSH, D // NSH, D), _branch_spec(lambda i, p: 2 - 2 * p)),
            pl.BlockSpec((NSH, D // NSH, D), _branch_spec(1)),
            vec, vec,
            pl.BlockSpec((GRP, BLK, BLK), const3),
            pl.BlockSpec((BLK, GRP), lambda i, p: (0, 0)),
            pl.BlockSpec((NPG, PGW, PGW), const3),
            vec, vec,
            pl.BlockSpec((3, D), lambda i, p: (0, 0)), vec,
        ],
        out_specs=[
            pl.BlockSpec((6, tm, D), lambda i, p: (1 - p, nT - 1 - i, 0)),
            pl.BlockSpec((3, tm, D), lambda i, p: (0, nT - 1 - i, 0)),
            pl.BlockSpec((GRP, BLK, BLK), const3),
            pl.BlockSpec((BLK, GRP), lambda i, p: (0, 0)),
            pl.BlockSpec((8, D), lambda i, p: (0, 0)),
            pl.BlockSpec((NPG, PGW, PGW), const3),
        ],
        out_shape=[
            jax.ShapeDtypeStruct((12, T, D), BF), jax.ShapeDtypeStruct((3, T, D), BF),
            jax.ShapeDtypeStruct((GRP, BLK, BLK), F32), jax.ShapeDtypeStruct((BLK, GRP), F32),
            jax.ShapeDtypeStruct((8, D), F32), jax.ShapeDtypeStruct((NPG, PGW, PGW), F32),
        ],
        scratch_shapes=[
            pltpu.VMEM((tm, D), F32),
            pltpu.VMEM((tm, D), F32),
            pltpu.VMEM((tm, D), F32),
            pltpu.VMEM((tm, D), F32),
            pltpu.VMEM((tm + HALO, D), F32),
            pltpu.VMEM((tm + HALO, D), F32),
            pltpu.VMEM((2, HALO, D), F32),
            pltpu.VMEM((tm, D), F32),
            pltpu.VMEM((tm, D), F32),
            pltpu.VMEM((tm, BLK), F32),
            pltpu.VMEM((4, RC, D), F32),
            pltpu.VMEM((BLK, D), F32),
            pltpu.VMEM((GRP, BLK, BLK), BF), pltpu.VMEM((GRP, BLK, BLK), BF),
        ] + [pltpu.VMEM((BLK, D), dt) for dt in (BF, F32, BF) for _ in range(tm // BLK)],
        compiler_params=_params(("arbitrary", "arbitrary"), 56),
    )(dxo, p3, proj, proj, proj, proj, proj, wl, wl, wl, lng, lnb, sgw, sgbT, pwb, pb, psc, cw, cb)


def _bwd_in(dproj, wl, x, dxo, ng):
    T = x.shape[0]
    tm = _tile(T, 256)
    nT = T // tm

    def body(dpj_ref, w_hbm, x_ref, dxo_ref, ng_ref, dx_ref, dng_ref, w_s, even_s, odd_s, g_s, wsem):
        i = pl.program_id(0)
        bufs = (even_s, odd_s)

        def w_copy(j):
            return pltpu.make_async_copy(w_hbm.at[j, pl.ds(0, SHW), :], w_s.at[j], wsem.at[j])

        def d_h():
            return sum(_mm_nt(dpj_ref[3 * j + b], w_s[j, b * D:(b + 1) * D, :]) for j in range(NSH) for b in range(3))

        def finish(prev):
            for c in range(tm // RC):
                finish_rows(prev, c * RC)

        def finish_rows(prev, r0):
            rows = pl.ds(r0, RC)
            xs = [x_ref[rows, c0:c0 + CW] for c0 in COLS]
            r = lax.rsqrt(sum(_rowsum(v * v) for v in xs) * (1.0 / D) + RMS_EPS)
            xh = [v * r for v in xs]
            dhg, m = [], 0.0
            for k, c0 in enumerate(COLS):
                cs = slice(c0, c0 + CW)
                dh = prev[rows, cs]
                g_s[:, cs] += dh * xh[k]
                dhg.append(dh * ng_ref[:, cs])
                m = m + _rowsum(dhg[k] * xh[k])
            m = m * (1.0 / D)
            for k, c0 in enumerate(COLS):
                cs = slice(c0, c0 + CW)
                dx_ref[rows, cs] = dxo_ref[rows, cs] + r * (dhg[k] - xh[k] * m)

        @pl.when(i == 0)
        def _():
            for j in range(NSH):
                w_copy(j).start()
            g_s[...] = jnp.zeros_like(g_s)
            for j in range(NSH):
                w_copy(j).wait()
            bufs[0][...] = d_h()

        for par in (0, 1):
            @pl.when((i % 2 == par) & (i > 0) & (i < nT))
            def _():
                bufs[par][...] = d_h()
                finish(bufs[1 - par])

        @pl.when(i == nT)
        def _():
            finish(bufs[(nT - 1) % 2])
            dng_ref[...] = jnp.sum(g_s[...], axis=0, keepdims=True)

    vec = pl.BlockSpec((1, D), lambda i: (0, 0))
    tile = pl.BlockSpec((tm, D), lambda i: (jnp.maximum(i - 1, 0), 0))
    return _pcall(
        body, name="bwd_in", grid=(nT + 1,),
        in_specs=[pl.BlockSpec((3 * NSH, tm, D), lambda i: (0, jnp.minimum(i, nT - 1), 0)), ANY, tile, tile, vec],
        out_specs=[tile, vec],
        out_shape=[jax.ShapeDtypeStruct((T, D), F32), jax.ShapeDtypeStruct((1, D), F32)],
        scratch_shapes=[pltpu.VMEM((NSH, SHW, D), BF), pltpu.VMEM((tm, D), F32), pltpu.VMEM((tm, D), F32),
                        pltpu.VMEM((RC, D), F32), pltpu.SemaphoreType.DMA((NSH,))],
        compiler_params=_params(("arbitrary",), 52),
    )(dproj, wl, x, dxo, ng)


def _tn_grad(a3, b3, split, out_map, name, into=None, after=None):
    nb, T, _ = b3.shape
    tk = _tile(T, 2048)
    nk = T // tk
    rows = D // split
    a_batched = a3.shape[0] > 1

    def body(a_ref, b_ref, *rest):
        o_ref, acc_s = rest[-2:]
        k = pl.program_id(1)

        @pl.when(k == 0)
        def _():
            acc_s[...] = _mm_tn(a_ref[...], b_ref[...])

        @pl.when((k > 0) & (k < nk))
        def _():
            acc_s[...] += _mm_tn(a_ref[...], b_ref[...])

        @pl.when(k >= nk - 1)
        def _():
            r0 = pl.multiple_of((k - (nk - 1)) * rows, rows)
            o_ref[...] = acc_s[pl.ds(r0, rows), :].astype(o_ref.dtype)

    def tok(k):
        return jnp.minimum(k, nk - 1)

    extra = ([] if into is None else [into]) + ([] if after is None else [after])
    return _pcall(
        body, name=name, grid=(nb, nk + split - 1),
        in_specs=[
            pl.BlockSpec((None, tk, D), (lambda n, k: (n, tok(k), 0)) if a_batched else (lambda n, k: (0, tok(k), 0))),
            pl.BlockSpec((None, tk, D), lambda n, k: (n, tok(k), 0)),
        ] + [ANY] * len(extra),
        out_specs=pl.BlockSpec((None, rows, D), lambda n, k: out_map(n, jnp.maximum(k - (nk - 1), 0))),
        out_shape=jax.ShapeDtypeStruct((NSH, PK_ROWS, D), BF),
        input_output_aliases={} if into is None else {2: 0},
        scratch_shapes=[pltpu.VMEM((D, D), F32)],
        compiler_params=_params(("arbitrary", "arbitrary"), 56),
    )(a3, b3, *extra)


def _place():
    x, y, c = lax.axis_index("x"), lax.axis_index("y"), lax.axis_index("c")
    chips = [(1 - x, y), (x, 1 - y), (1 - x, 1 - y)]
    return x, y, c, chips


def _peers(reach):
    x, y, c, chips = _place()
    if reach == "chips":
        return 2 * x + y, [((px, py, c), 2 * px + py) for px, py in chips]
    others = [(x, y, 1 - c)] + [(px, py, pc) for px, py in chips for pc in (c, 1 - c)]
    return 4 * x + 2 * y + c, [(pr, 4 * pr[0] + 2 * pr[1] + pr[2]) for pr in others]


ANY = pl.BlockSpec(memory_space=pl.ANY)


HBM = pl.BlockSpec(memory_space=pltpu.HBM)
SEM = pl.BlockSpec(memory_space=pltpu.SEMAPHORE)
EFFECT = pltpu.SideEffectType.DATAFLOW_SIDE_EFFECTING


def _own_slot(src, from_slot, name):
    rows = src.shape[-2]
    rb = rows // 8
    me = (2 * lax.axis_index("x") + lax.axis_index("y")).astype(jnp.int32).reshape(1)

    def body(me_ref, src_ref, land_ref):
        land_ref[...] = src_ref[...]

    if from_slot:
        src_spec = pl.BlockSpec((None, rb, D), lambda i, me_ref: (me_ref[0], i, 0))
    else:
        src_spec = pl.BlockSpec((rb, D), lambda i, me_ref: (i, 0))
    return _pcall(
        body, name=name,
        grid_spec=pltpu.PrefetchScalarGridSpec(
            num_scalar_prefetch=1, grid=(rows // rb,), in_specs=[src_spec],
            out_specs=pl.BlockSpec((None, rb, D), lambda i, me_ref: (me_ref[0], i, 0))),
        out_shape=jax.ShapeDtypeStruct((NSH, rows, D), src.dtype),
        compiler_params=_params(("arbitrary",), 32),
    )(me, src)


def _my_half():
    return pl.ds(pl.multiple_of(lax.axis_index("c") * PK_HALF, PK_HALF), PK_HALF)


def _push_start(name, srcs, lands, per_peer, reach="chips", halved=()):
    n = len(srcs)
    npeer = 3 if reach == "chips" else 7
    ns = n * npeer

    def body(*refs):
        src, land = refs[:n], refs[n:2 * n]
        ssem, rsem = refs[2 * n:2 * n + ns], refs[2 * n + ns:2 * n + 2 * ns]
        token = refs[-1]
        me, peers = _peers(reach)
        for i in range(n):
            for j, (peer, slot) in enumerate(peers):
                s = src[i].at[slot] if per_peer else src[i]
                d = land[i].at[me]
                if i in halved:
                    s, d = s.at[_my_half()], d.at[_my_half()]
                pltpu.make_async_remote_copy(
                    src_ref=s, dst_ref=d, send_sem=ssem[npeer * i + j], recv_sem=rsem[npeer * i + j], device_id=peer,
                    device_id_type=MESH).start()
        token[...] = jnp.zeros_like(token)

    ops = list(srcs) + list(lands)
    out = _pcall(
        body, name=name,
        out_shape=tuple([pltpu.SemaphoreType.DMA(())] * (2 * ns) + [pltpu.HBM(a.shape, a.dtype) for a in ops]
                        + [jax.ShapeDtypeStruct((8, 128), F32)]),
        in_specs=[HBM] * (2 * n),
        out_specs=tuple([SEM] * (2 * ns) + [HBM] * (2 * n) + [pl.BlockSpec(memory_space=pltpu.VMEM)]),
        input_output_aliases={i: 2 * ns + i for i in range(2 * n)},
        compiler_params=pltpu.CompilerParams(has_side_effects=EFFECT),
    )(*[pltpu.with_memory_space_constraint(a, pltpu.HBM) for a in ops])
    return out[:ns], out[ns:2 * ns], out[2 * ns:2 * ns + n], out[2 * ns + n:2 * ns + 2 * n], out[-1]


def _push_wait(name, src, land, ssem, rsem, after, per_peer, reach="chips", halved=False):
    npeer = len(ssem)

    def body(src_ref, land_ref, *rest):
        sems = rest[:2 * npeer]
        _, peers = _peers(reach)
        for j, (peer, slot) in enumerate(peers):
            s = src_ref.at[slot] if per_peer else src_ref
            d = land_ref.at[slot]
            if halved:
                s, d = s.at[_my_half()], d.at[_my_half()]
            cp = pltpu.make_async_remote_copy(
                src_ref=s, dst_ref=d, send_sem=sems[j],
                recv_sem=sems[npeer + j], device_id=peer, device_id_type=MESH)
            cp.wait_send()
            cp.wait_recv()

    return _pcall(
        body, name=name,
        out_shape=(pltpu.HBM(src.shape, src.dtype), pltpu.HBM(land.shape, land.dtype)),
        in_specs=[HBM, HBM] + [SEM] * (2 * npeer) + [ANY], out_specs=(HBM, HBM),
        input_output_aliases={0: 0, 1: 1},
        compiler_params=pltpu.CompilerParams(has_side_effects=EFFECT),
    )(src, land, *ssem, *rsem, after)[1]


def _share_halves(land):
    def body(land_in, land_ref, ssem, rsem):
        x, y, c, chips = _place()
        other = pl.ds(pl.multiple_of((1 - c) * PK_HALF, PK_HALF), PK_HALF)
        sends = [pltpu.make_async_remote_copy(
            src_ref=land_ref.at[2 * px + py, _my_half()], dst_ref=land_ref.at[2 * px + py, _my_half()],
            send_sem=ssem.at[j], recv_sem=rsem.at[j], device_id=(x, y, 1 - c), device_id_type=MESH)
            for j, (px, py) in enumerate(chips)]
        for cp in sends:
            cp.start()
        for j, (px, py) in enumerate(chips):
            pltpu.make_async_remote_copy(
                src_ref=land_ref.at[2 * px + py, other], dst_ref=land_ref.at[2 * px + py, other], send_sem=ssem.at[j],
                recv_sem=rsem.at[j], device_id=(x, y, 1 - c), device_id_type=MESH).wait_recv()
        for cp in sends:
            cp.wait_send()

    return _pcall(
        body, name="share_halves", in_specs=[ANY], out_specs=ANY,
        out_shape=jax.ShapeDtypeStruct(land.shape, land.dtype), input_output_aliases={0: 0},
        scratch_shapes=[pltpu.SemaphoreType.DMA((3,)), pltpu.SemaphoreType.DMA((3,))],
    )(land)


def _swap_sibling(arrs):
    n = len(arrs)

    def body(*refs):
        src, dst = refs[:n], refs[n:2 * n]
        ssem, rsem = refs[2 * n:]
        x, y, c, _ = _place()
        cps = [pltpu.make_async_remote_copy(src_ref=src[a], dst_ref=dst[a], send_sem=ssem.at[a], recv_sem=rsem.at[a],
                                            device_id=(x, y, 1 - c), device_id_type=MESH) for a in range(n)]
        for cp in cps:
            cp.start()
        for cp in cps:
            cp.wait()

    return _pcall(
        body, name="swap_sibling",
        in_specs=[ANY] * n, out_specs=[ANY] * n,
        out_shape=[jax.ShapeDtypeStruct(a.shape, a.dtype) for a in arrs],
        scratch_shapes=[pltpu.SemaphoreType.DMA((n,)), pltpu.SemaphoreType.DMA((n,))],
    )(*arrs)


def _gather_all(v):
    def body(src, dst, ssem, rsem, lsem):
        x, y, c, _ = _place()
        me = 4 * x + 2 * y + c
        peers = [(x, y, 1 - c), (1 - x, y, c), (1 - x, y, 1 - c), (x, 1 - y, c), (x, 1 - y, 1 - c),
                 (1 - x, 1 - y, c), (1 - x, 1 - y, 1 - c)]
        local = pltpu.make_async_copy(src, dst.at[me], lsem)
        local.start()
        sends = [pltpu.make_async_remote_copy(src_ref=src, dst_ref=dst.at[me], send_sem=ssem.at[j], recv_sem=rsem.at[j],
                                              device_id=pr, device_id_type=MESH) for j, pr in enumerate(peers)]
        for cp in sends:
            cp.start()
        for j, (px, py, pc) in enumerate(peers):
            pltpu.make_async_remote_copy(src_ref=src, dst_ref=dst.at[4 * px + 2 * py + pc], send_sem=ssem.at[j],
                                         recv_sem=rsem.at[j], device_id=(px, py, pc), device_id_type=MESH).wait_recv()
        for cp in sends:
            cp.wait_send()
        local.wait()

    return _pcall(
        body, name="gather_all", in_specs=[ANY], out_specs=ANY,
        out_shape=jax.ShapeDtypeStruct((8,) + v.shape, v.dtype),
        scratch_shapes=[pltpu.SemaphoreType.DMA((7,)), pltpu.SemaphoreType.DMA((7,)), pltpu.SemaphoreType.DMA(())],
    )(v)


def _sum_slots(r, rb):
    S = r.shape[0]

    def body(r_ref, o_ref):
        acc = r_ref[0].astype(F32)
        for s in range(1, S):
            acc = acc + r_ref[s].astype(F32)
        o_ref[...] = acc

    if r.ndim == 3:
        _, R, C = r.shape
        grid, blk, imap = (R // rb,), (S, rb, C), (lambda i: (0, i, 0))
        oblk, omap = (rb, C), (lambda i: (i, 0))
    else:
        _, K, R, C = r.shape
        grid, blk, imap = (K,), (S, None, R, C), (lambda i: (0, i, 0, 0))
        oblk, omap = (None, R, C), (lambda i: (i, 0, 0))
    return _pcall(
        body, name="sum_slots", grid=grid, in_specs=[pl.BlockSpec(blk, imap)], out_specs=pl.BlockSpec(oblk, omap),
        out_shape=jax.ShapeDtypeStruct(r.shape[1:], F32), compiler_params=_params(("arbitrary",), 48),
    )(r)


def _adamw(gs, g_spec, w, m, v, p_spec, prev, grid):
    ng = len(gs)
    bc1 = 1.0 - ADAM_B1 ** ADAM_STEP
    bc2 = 1.0 - ADAM_B2 ** ADAM_STEP

    def body(*refs):
        g = refs[0][...]
        for a in range(1, ng):
            g = g + refs[a][...]
        w_ref, m_ref, v_ref = refs[ng:ng + 3]
        go, do, mo, vo = refs[ng + 3 + 4:]
        mn = ADAM_B1 * m_ref[...] + (1.0 - ADAM_B1) * g
        vn = ADAM_B2 * v_ref[...] + (1.0 - ADAM_B2) * (g * g)
        go[...] = g
        mo[...] = mn
        vo[...] = vn
        do[...] = -ADAM_LR * ((mn / bc1) / (jnp.sqrt(vn / bc2) + ADAM_EPS) + ADAM_WD * w_ref[...])

    out = jax.ShapeDtypeStruct(w.shape, F32)
    k0 = ng + 3
    return _pcall(
        body, name="adamw", grid=grid,
        in_specs=[g_spec] * ng + [p_spec] * 3 + [ANY] * 4,
        out_specs=[p_spec] * 4, out_shape=[out] * 4,
        input_output_aliases={k0: 0, k0 + 1: 1, k0 + 2: 2, k0 + 3: 3},
        compiler_params=_params(("arbitrary",) * len(grid), 48),
    )(*gs, w, m, v, *prev)


def _empty4(w):
    return tuple(lax.empty(w.shape, F32) for _ in range(4))


N_SGW = L * GRP * BLK * BLK // D
O_NG, O_VEC, O_SGB, O_FG, O_SGW = 0, 8, 32, 40, 48
O_CW = O_SGW + N_SGW
N_PACK = O_CW + 16
PACK_RB = N_PACK // 3


def _pad_to(a, rows):
    return jnp.pad(a, ((0, rows - a.shape[0]), (0, 0)))


def _pack_small(ng, vecs, sgb, fg, sgw, cw):
    parts = [_pad_to(ng, 8), _pad_to(vecs.reshape(L * 5, D), 24), _pad_to(sgb.reshape(L, D), 8),
             _pad_to(fg.reshape(1, D), 8), sgw.reshape(N_SGW, D), _pad_to(cw, 16)]
    return jnp.concatenate(parts, axis=0)


def kernel(x, norm_g, w_in, sgu_ln_g, sgu_ln_b, sgu_w, sgu_b, pool_w, pool_b, pool_scale, conv_w, conv_b, w_branch_a, w_branch_b, w_branch_c, w_out, final_g, loss_target, m_norm_g, m_w_in, m_sgu_ln_g, m_sgu_ln_b, m_sgu_w, m_sgu_b, m_pool_w, m_pool_b, m_pool_scale, m_conv_w, m_conv_b, m_w_branch_a, m_w_branch_b, m_w_branch_c, m_w_out, m_final_g, v_norm_g, v_w_in, v_sgu_ln_g, v_sgu_ln_b, v_sgu_w, v_sgu_b, v_pool_w, v_pool_b, v_pool_scale, v_conv_w, v_conv_b, v_w_branch_a, v_w_branch_b, v_w_branch_c, v_w_out, v_final_g):
    cx, cy = lax.axis_index("x"), lax.axis_index("y")
    me = 2 * cx + cy
    xl, tgt = x[0], loss_target[0]
    q = D // NSH

    wq = w_in.astype(BF).reshape(L, D, 3, D).transpose(0, 2, 1, 3).reshape(L, SHW, D)
    brq = jnp.stack([w_branch_a, w_branch_b, w_branch_c, w_out], axis=1).astype(BF).reshape(L, D, D)
    pwq = pool_w.astype(BF).reshape(L, PK_CW - PK_PW, D)
    cwq = lax.bitcast_convert_type(conv_w, BF).reshape(L, 3 * q * 2)
    cwq = jnp.pad(cwq, ((0, 0), (0, (PK_ROWS - PK_CW) * D - 3 * q * 2))).reshape(L, PK_ROWS - PK_CW, D)
    packs = [jnp.concatenate([wq[l], brq[l], pwq[l], cwq[l]], axis=0) for l in range(L)]
    lands = [_own_slot(packs[l], False, f"ag_own_{l}") for l in range(L)]
    ag_s, ag_r, packs, lands, tok = _push_start("ag_start", packs, lands, False, halved=(0,))
    sgbT = sgu_b.transpose(0, 2, 1)

    def layer_weights(l, after):
        wl = _push_wait(f"ag_wait_{l}", packs[l], lands[l], ag_s[3 * l:3 * l + 3], ag_r[3 * l:3 * l + 3], after, False,
                        halved=l == 0)
        if l == 0:
            wl = _share_halves(wl)
        pwb = wl[:, PK_PW:PK_CW].reshape(NSH, NPG, PGW // NSH, PGW).transpose(1, 0, 2, 3).reshape(NPG, PGW, PGW)
        cwb = wl[:, PK_CW:].reshape(NSH, (PK_ROWS - PK_CW) * D)[:, :3 * q * 2].reshape(NSH, 3, q, 2)
        cwf = lax.bitcast_convert_type(cwb, F32).transpose(1, 0, 2).reshape(3, D)
        small = (sgu_ln_g[l:l + 1], sgu_ln_b[l:l + 1], sgu_w[l], sgbT[l], pwb, pool_b[l:l + 1], pool_scale[l:l + 1],
                 cwf, conv_b[l:l + 1])
        return wl, small

    xs, saved, wts = [xl], [], []
    for l in range(L):
        wl, small = layer_weights(l, tok if l == 0 else xs[l])
        wts.append((wl, small))
        proj, h, y3 = _fwd_in(xs[l], norm_g[l:l + 1], wl, *small)
        p3, mg, xo = _fwd_out(y3, proj, xs[l], wl)
        saved.append((proj, h, y3, p3, mg))
        xs.append(xo)

    dx, sq, dfg = _loss_head(xs[L], final_g[None], tgt)
    loss = lax.psum(jnp.sum(sq) * (0.5 / D), ("x", "y", "c"))

    g_w_in = _empty4(w_in)
    g_br = [_empty4(w_out) for _ in range(4)]
    g_pw = _empty4(pool_w)
    dng, dvec, dsgw, dsgb = [None] * L, [None] * L, [None] * L, [None] * L
    branches = [(w_branch_a, m_w_branch_a, v_w_branch_a), (w_branch_b, m_w_branch_b, v_w_branch_b),
                (w_branch_c, m_w_branch_c, v_w_branch_c), (w_out, m_w_out, v_w_out)]
    nb = D // 128

    def finish(l, landed):
        nonlocal g_w_in, g_pw
        mine = _sum_slots(landed, PK_ROWS // 8)
        sums = [mine, _swap_sibling([mine])[0]]
        g_w_in = _adamw(sums, pl.BlockSpec((128, D), lambda b, i: (b * nb + i, 0)), w_in, m_w_in, v_w_in,
                        pl.BlockSpec((None, 128, D), lambda b, i: (l, i, b)), g_w_in, (3, nb))
        for k, (w, m, v) in enumerate(branches):
            g_br[k] = _adamw(sums, pl.BlockSpec((q, D), lambda i, k=k: (PK_BR // q + k, 0)), w, m, v,
                             pl.BlockSpec((None, q, D), lambda i: (l, 0, 0)), g_br[k], (1,))
        pools = [a[PK_PW:PK_CW].reshape(NPG, PGW // NSH, PGW) for a in sums]
        g_pw = _adamw(pools, pl.BlockSpec((None, PGW // NSH, PGW), lambda g: (g, 0, 0)), pool_w, m_pool_w, v_pool_w,
                      pl.BlockSpec((None, None, PGW // NSH, PGW), lambda g: (l, g, 0, 0)), g_pw, (NPG,))

    pend = None
    for l in reversed(range(L)):
        proj, h, y3, p3, mg = saved[l]
        wl, small = wts[l]
        dproj, dp3, dsgw[l], dsbT, dvec[l], dpw = _bwd_mix(dx, p3, proj, wl, *small)
        dsgb[l] = dsbT.T
        if l == 0:
            dv = jnp.stack(dvec)
            part = _pack_small(jnp.zeros((L, D), F32), dv[:, 0:5], jnp.stack(dsgb), dfg[0], jnp.stack(dsgw),
                               dv[:, 5:8].reshape(L * 3, D))
            zone = lax.dynamic_update_slice(lax.empty((8, N_PACK, D), F32), part[None],
                                            (2 * me + lax.axis_index("c"), 0, 0))
            sm_s, sm_r, (part,), (zone,), started = _push_start("small_start", [part], [zone], False, "all")
        grads = _tn_grad(h[None], dproj, 1, lambda n, s: (n // 3, n % 3, 0), "grad_w_in", after=started if l == 0 else None)
        grads = _tn_grad(y3, dp3, NSH, lambda n, s: (s, PK_BR // q + n, 0), "grad_w_branch", into=grads)
        grads = _tn_grad(mg[None], dx[None], NSH, lambda n, s: (s, PK_BR // q + 3, 0), "grad_w_out", into=grads)
        dpq = dpw.astype(BF).reshape(NPG, NSH, PGW // NSH, PGW).transpose(1, 0, 2, 3).reshape(NSH, PK_CW - PK_PW, D)
        grads = lax.dynamic_update_slice(grads, jnp.pad(dpq, ((0, 0), (0, PK_ROWS - PK_CW), (0, 0))), (0, PK_PW, 0))
        if pend is not None:
            landed = _push_wait(f"rs_wait_{pend[0]}", *pend[1:], dproj, True)
        land = _own_slot(grads, True, f"rs_own_{l}")
        ss, rs, (grads,), (land,), tok = _push_start(f"rs_start_{l}", [grads], [land], True)
        if pend is not None:
            finish(pend[0], landed)
        dx, dng[l] = _bwd_in(dproj, wl, xs[l], dx, norm_g[l:l + 1] + tok[0, 0])
        pend = (l, grads, land, ss, rs)
    finish(pend[0], _push_wait(f"rs_wait_{pend[0]}", *pend[1:], dx, True))

    zone = _push_wait("small_wait", part, zone, sm_s, sm_r, dx, False, "all")
    gng = _sum_slots(_gather_all(_pad_to(jnp.concatenate(dng), O_VEC)), O_VEC)
    gsmall = jnp.concatenate([gng, _sum_slots(zone, PACK_RB)[O_VEC:]])
    gcw =lax.dynamic_slice_in_dim(gsmall[O_CW:O_CW + L * 3], me * q, q, axis=1)
    gpack = jnp.concatenate([gsmall[:O_CW], _pad_to(gcw.reshape(L * 3 * q // D, D), 16)])

    def pack(ng, lg, lb, sw, sb, pb_, ps, cwv, cb_, fg):
        return _pack_small(ng, jnp.stack([lg, lb, pb_, ps, cb_], axis=1), sb, fg, sw, cwv.reshape(L * 3 * q // D, D))

    wp = pack(norm_g, sgu_ln_g, sgu_ln_b, sgu_w, sgu_b, pool_b, pool_scale, conv_w, conv_b, final_g)
    mp = pack(m_norm_g, m_sgu_ln_g, m_sgu_ln_b, m_sgu_w, m_sgu_b, m_pool_b, m_pool_scale, m_conv_w, m_conv_b, m_final_g)
    vp = pack(v_norm_g, v_sgu_ln_g, v_sgu_ln_b, v_sgu_w, v_sgu_b, v_pool_b, v_pool_scale, v_conv_w, v_conv_b, v_final_g)
    rows = pl.BlockSpec((PACK_RB, D), lambda i: (i, 0))
    sm = _adamw([gpack], rows, wp, mp, vp, rows, _empty4(wp), (N_PACK // PACK_RB,))

    def unpack(a):
        vv = a[O_VEC:O_VEC + L * 5].reshape(L, 5, D)
        sb = a[O_SGB:O_SGB + L].reshape(L, GRP, BLK)
        fg = a[O_FG]
        sw = a[O_SGW:O_SGW + N_SGW].reshape(L, GRP, BLK, BLK)
        cwv = a[O_CW:O_CW + L * 3 * q // D].reshape(L, 3, q)
        return dict(norm_g=a[O_NG:O_NG + L], w_in=None, sgu_ln_g=vv[:, 0], sgu_ln_b=vv[:, 1], sgu_w=sw, sgu_b=sb, pool_w=None,
                    pool_b=vv[:, 2], pool_scale=vv[:, 3], conv_w=cwv, conv_b=vv[:, 4], w_branch_a=None,
                    w_branch_b=None, w_branch_c=None, w_out=None, final_g=fg)

    outs = [loss, dx[None]]
    for kind in range(4):
        d = unpack(sm[kind])
        d.update(w_in=g_w_in[kind], pool_w=g_pw[kind], w_branch_a=g_br[0][kind], w_branch_b=g_br[1][kind],
                 w_branch_c=g_br[2][kind], w_out=g_br[3][kind])
        outs.extend(d[n] for n in ("norm_g", "w_in", "sgu_ln_g", "sgu_ln_b", "sgu_w", "sgu_b", "pool_w", "pool_b",
                                   "pool_scale", "conv_w", "conv_b", "w_branch_a", "w_branch_b", "w_branch_c", "w_out",
                                   "final_g"))
    return tuple(outs)
```

```python
import functools

import jax
import jax.numpy as jnp
from jax import lax
from jax.experimental import pallas as pl
from jax.experimental.pallas import tpu as pltpu

F32 = jnp.float32
BF = jnp.bfloat16
MESH = pl.DeviceIdType.MESH

D = 1024
L = 4
NSH = 4
SHW = 3 * D
NIN = NSH * SHW
GRP = 8
BLK = 128
NPG = 4
PGW = D // NPG
POOL_WINDOWS = (2, 4, 8, 16)
HALO = 16
RMS_EPS = 1e-6
LN_EPS = 1e-5
ADAM_LR, ADAM_B1, ADAM_B2, ADAM_EPS, ADAM_WD, ADAM_STEP = 0.001, 0.9, 0.999, 1e-8, 0.01, 10

RC = 16
CW = 512
COLS = tuple(range(0, D, CW))
MIB = 1 << 20

PK_BR = SHW
PK_PW = PK_BR + D
PK_CW = PK_PW + NPG * (PGW // NSH) * PGW // D
PK_ROWS = PK_CW + 64
PK_HALF = PK_ROWS // 2

_C0 = 0.7978845608028654
_C1 = 0.044715


def _pcall(body, **kw):
    return pl.pallas_call(body, **kw)


def _params(sem, vmem_mib):
    return pltpu.CompilerParams(dimension_semantics=sem, vmem_limit_bytes=vmem_mib * MIB)


def _mm(a, b):
    return jnp.dot(a.astype(BF), b.astype(BF), preferred_element_type=F32)


def _mm_nt(a, b):
    return lax.dot_general(a.astype(BF), b.astype(BF), (((1,), (1,)), ((), ())), preferred_element_type=F32)


def _mm_tn(a, b):
    return lax.dot_general(a.astype(BF), b.astype(BF), (((0,), (0,)), ((), ())), preferred_element_type=F32)


def _gelu(x):
    return 0.5 * x * (1.0 + jnp.tanh(_C0 * x * (1.0 + _C1 * x * x)))


def _gelu_fg(x):
    x2 = x * x
    t = jnp.tanh(_C0 * x * (1.0 + _C1 * x2))
    f = 0.5 * x * (1.0 + t)
    g = 0.5 * (1.0 + t) + 0.5 * x * (1.0 - t * t) * (_C0 * (1.0 + 3.0 * _C1 * x2))
    return f, g


def _sigmoid(x):
    return 0.5 * jnp.tanh(0.5 * x) + 0.5


def _silu(x):
    return x * _sigmoid(x)


def _silu_fg(x):
    s = _sigmoid(x)
    return x * s, s * (1.0 + x * (1.0 - s))


def _rowsum(v):
    return jnp.sum(v, axis=1, keepdims=True)


def _chunks(n_rows, fn, unroll=2):
    def body(c, carry):
        fn(pl.multiple_of(c * RC, RC))
        return carry
    lax.fori_loop(0, n_rows // RC, body, 0, unroll=unroll)


def _trail(e, win):
    s, sh = e, 1
    while sh < win:
        s = s + pltpu.roll(s, sh, 0)
        sh *= 2
    return s


def _lead(e, win):
    n = e.shape[0]
    s, sh = e, 1
    while sh < win:
        s = s + pltpu.roll(s, n - sh, 0)
        sh *= 2
    return s


def _inv_count(pos0, win):
    pos = pos0 + lax.broadcasted_iota(jnp.int32, (RC, 1), 0)
    return 1.0 / jnp.minimum(pos + 1, win).astype(F32)


def _masked_sgu(sgw_ref, g):
    r = lax.broadcasted_iota(jnp.int32, (BLK, BLK), 0) // 64
    c = lax.broadcasted_iota(jnp.int32, (BLK, BLK), 1) // 64
    return jnp.where(c <= r, sgw_ref[g], 0.0)


class _AsF32:
    def __init__(self, ref, col0=0):
        self.ref, self.col0 = ref, col0

    def __getitem__(self, idx):
        if self.col0:
            rows, cols = idx
            idx = (rows, slice(cols.start + self.col0, cols.stop + self.col0))
        return self.ref[idx].astype(F32)


class _From:
    def __init__(self, ref, lead0):
        self.ref, self.lead0 = ref, lead0

    def __setitem__(self, idx, val):
        self.ref[(self.lead0 + idx[0],) + tuple(idx[1:])] = val


def _tile(t, want):
    return min(t, want)


def _fwd_in(x, ng, wl, lng, lnb, sgw, sgbT, pwb, pb, psc, cw, cb):
    T = x.shape[0]
    tm = _tile(T, 256)
    nT = T // tm

    def body(x_ref, xn_ref, ng_ref, w_hbm, lng_ref, lnb_ref, sgw_ref, sgbT_ref, pw_ref, pb_ref, psc_ref, cw_ref, cb_ref,
             proj_hbm, h_ref, y_ref, w_s, pja, pjb, pjc, pjd, qa, qb, qc, qd, h_s, hn_s, vn_s, mix_s, d_s, wm_s, extp_s,
             extc_s, wsem, psem):
        i = pl.program_id(0)
        pj = (pja, pjb, pjc, pjd)
        kept = (qa, qb, qc, qd)

        def w_copy(p):
            return pltpu.make_async_copy(w_hbm.at[p, pl.ds(0, SHW), :], w_s.at[p], wsem.at[p])

        def p_copy(p, tile=None):
            t0 = (i if tile is None else tile) * tm
            dst = proj_hbm.at[pl.ds(pl.multiple_of(t0, tm), tm), pl.ds(p * SHW, SHW)]
            return pltpu.make_async_copy(kept[p], dst, psem.at[p])

        def keep(p):
            kept[p][...] = pj[p][...].astype(BF)
            p_copy(p).start()

        def all_rows(fn):
            for c in range(tm // RC):
                fn(c * RC)

        def buffer_free(p):
            @pl.when(i == 0)
            def _():
                w_copy(p).wait()

            @pl.when(i > 0)
            def _():
                p_copy(p).wait()

        def project(p, blocks, h=None):
            for b in blocks:
                pj[p][:, b * D:(b + 1) * D] = jnp.dot((h_s if h is None else h)[...], w_s[p, b * D:(b + 1) * D, :],
                                                      preferred_element_type=F32)

        @pl.when(i == 0)
        def _():
            for p in range(NSH):
                w_copy(p).start()
            extp_s[0:HALO, :] = jnp.zeros((HALO, D), F32)
            extc_s[0:HALO, :] = jnp.zeros((HALO, D), F32)
            for g in range(GRP):
                wm_s[g] = _masked_sgu(sgw_ref, g).astype(BF)

        def norm_rows(src, dst, r0):
            rows = pl.ds(r0, RC)
            xs = [src[rows, c0:c0 + CW] for c0 in COLS]
            ms = sum(_rowsum(v * v) for v in xs) * (1.0 / D)
            r = lax.rsqrt(ms + RMS_EPS)
            for k, c0 in enumerate(COLS):
                dst[rows, c0:c0 + CW] = (xs[k] * r * ng_ref[:, c0:c0 + CW]).astype(BF)

        def gating(first, rest):
            proj_ref = pj[0]
            first()

            def f1(r0):
                rows = pl.ds(r0, RC)
                gv = [_gelu(proj_ref[rows, D + c0:D + c0 + CW]) for c0 in COLS]
                mu = sum(_rowsum(v) for v in gv) * (1.0 / D)
                dv = [v - mu for v in gv]
                var = sum(_rowsum(v * v) for v in dv) * (1.0 / D)
                rstd = lax.rsqrt(var + LN_EPS)
                for k, c0 in enumerate(COLS):
                    vn_s[rows, c0:c0 + CW] = (dv[k] * rstd * lng_ref[:, c0:c0 + CW] + lnb_ref[:, c0:c0 + CW]).astype(BF)
            all_rows(f1)
            for g in range(GRP):
                w = wm_s[g]
                bcol = sgbT_ref[:, g:g + 1]
                gc = slice(g * BLK, (g + 1) * BLK)
                for n in range(tm // BLK):
                    rr = slice(n * BLK, (n + 1) * BLK)
                    mix_s[rr, gc] = jnp.dot(w, vn_s[rr, gc], preferred_element_type=F32) + bcol
            rest()

            def f2(r0):
                rows = pl.ds(r0, RC)
                for c0 in COLS:
                    au = proj_ref[rows, c0:c0 + CW]
                    az = proj_ref[rows, 2 * D + c0:2 * D + c0 + CW]
                    y_ref[0, rows, c0:c0 + CW] = (_gelu(au) * mix_s[rows, c0:c0 + CW] * _silu(az)).astype(BF)
            all_rows(f2)

        def pooling(first, rest):
            proj_ref = pj[1]
            first()
            extp_s[HALO:HALO + tm, :] = proj_ref[:, 0:D]

            def f1(r0):
                rows = pl.ds(r0, RC)
                e = extp_s[pl.ds(r0, 2 * RC), :]
                for k, win in enumerate(POOL_WINDOWS):
                    gc = slice(k * PGW, (k + 1) * PGW)
                    s = _trail(e[:, gc], win)[RC:2 * RC]
                    d_s[rows, gc] = (s * _inv_count(i * tm + r0, win) - e[RC:2 * RC, gc]).astype(BF)
            all_rows(f1)
            extp_s[0:HALO, :] = extp_s[tm:tm + HALO, :]
            for k in range(NPG):
                gc = slice(k * PGW, (k + 1) * PGW)
                mix_s[:, gc] = jnp.dot(d_s[:, gc], pw_ref[k], preferred_element_type=F32)
            rest()

            def f2(r0):
                rows = pl.ds(r0, RC)
                for c0 in COLS:
                    cs = slice(c0, c0 + CW)
                    bz = proj_ref[rows, D + c0:D + c0 + CW]
                    y_ref[1, rows, cs] = ((mix_s[rows, cs] + pb_ref[:, cs]) * psc_ref[:, cs] * _silu(bz)).astype(BF)
            all_rows(f2)

        def convolution(first, rest):
            proj_ref = pj[2]
            first()
            all_rows(functools.partial(norm_rows, xn_ref, hn_s))

            def f1(r0):
                rows = pl.ds(r0, RC)
                extc_s[pl.ds(HALO + r0, RC), :] = proj_ref[rows, D:2 * D] * pj[1][rows, 2 * D:3 * D]
            all_rows(f1)
            rest()
            project(0, (0, 1, 2), hn_s)

            def f2(r0):
                rows = pl.ds(r0, RC)
                for c0 in COLS:
                    cs = slice(c0, c0 + CW)
                    e = extc_s[pl.ds(r0, 2 * RC), cs]
                    conv = (cw_ref[2:3, cs] * e + cw_ref[1:2, cs] * pltpu.roll(e, 1, 0)
                            + cw_ref[0:1, cs] * pltpu.roll(e, 2, 0))[RC:2 * RC] + cb_ref[:, cs]
                    cbv = proj_ref[rows, cs]
                    cz = proj_ref[rows, 2 * D + c0:2 * D + c0 + CW]
                    y_ref[2, rows, cs] = (cbv * conv * _silu(cz)).astype(BF)
            all_rows(f2)
            extc_s[0:HALO, :] = extc_s[tm:tm + HALO, :]

        @pl.when(i == 0)
        def _():
            all_rows(functools.partial(norm_rows, x_ref, h_s))
            w_copy(0).wait()
            project(0, (0, 1, 2))
            keep(0)

        @pl.when(i > 0)
        def _():
            h_s[...] = hn_s[...]
        h_ref[...] = h_s[...]

        for p, mixer, early in ((1, gating, 2), (2, pooling, 1), (3, convolution, 1)):
            buffer_free(p)
            if p == 3:
                p_copy(0).wait()
            mixer(functools.partial(project, p, range(early)), functools.partial(project, p, range(early, 3)))
            keep(p)
        kept[0][...] = pj[0][...].astype(BF)

        @pl.when(i < nT - 1)
        def _():
            p_copy(0, i + 1).start()

        @pl.when(i == nT - 1)
        def _():
            for p in range(1, NSH):
                p_copy(p).wait()

    vec = pl.BlockSpec((1, D), lambda i: (0, 0))
    return _pcall(
        body, name="fwd_in", grid=(nT,),
        in_specs=[
            pl.BlockSpec((tm, D), lambda i: (i, 0)), pl.BlockSpec((tm, D), lambda i: (jnp.minimum(i + 1, nT - 1), 0)),
            vec, ANY, vec, vec,
            pl.BlockSpec((GRP, BLK, BLK), lambda i: (0, 0, 0)),
            pl.BlockSpec((BLK, GRP), lambda i: (0, 0)),
            pl.BlockSpec((NPG, PGW, PGW), lambda i: (0, 0, 0)),
            vec, vec,
            pl.BlockSpec((3, D), lambda i: (0, 0)), vec,
        ],
        out_specs=[ANY, pl.BlockSpec((tm, D), lambda i: (i, 0)), pl.BlockSpec((3, tm, D), lambda i: (0, i, 0))],
        out_shape=[jax.ShapeDtypeStruct((T, NIN), BF), jax.ShapeDtypeStruct((T, D), BF),
                   jax.ShapeDtypeStruct((3, T, D), BF)],
        scratch_shapes=[
            pltpu.VMEM((NSH, SHW, D), BF),
            pltpu.VMEM((tm, SHW), F32), pltpu.VMEM((tm, SHW), F32), pltpu.VMEM((tm, SHW), F32), pltpu.VMEM((tm, SHW), F32),
            pltpu.VMEM((tm, SHW), BF), pltpu.VMEM((tm, SHW), BF), pltpu.VMEM((tm, SHW), BF), pltpu.VMEM((tm, SHW), BF),
            pltpu.VMEM((tm, D), BF), pltpu.VMEM((tm, D), BF), pltpu.VMEM((tm, D), BF), pltpu.VMEM((tm, D), F32),
            pltpu.VMEM((tm, D), BF), pltpu.VMEM((GRP, BLK, BLK), BF),
            pltpu.VMEM((tm + HALO, D), F32), pltpu.VMEM((tm + HALO, D), F32),
            pltpu.SemaphoreType.DMA((NSH,)), pltpu.SemaphoreType.DMA((NSH,)),
        ],
        compiler_params=_params(("arbitrary",), 60),
    )(x, x, ng, wl, lng, lnb, sgw, sgbT, pwb, pb, psc, cw, cb)


def _branch_spec(which):
    q = D // NSH
    return lambda *g: (0, PK_BR // q + (which(*g) if callable(which) else which), 0)


def _fwd_out(y3, proj, x, wl):
    T = x.shape[0]
    tm = _tile(T, 256)
    q = D // NSH

    nT = T // tm

    def body(y_ref, gl_ref, x_ref, wa_ref, wb_ref, wc_ref, wo_ref, p_ref, mg_ref, xo_ref, even_s, odd_s):
        i = pl.program_id(0)
        gl_ref = _AsF32(gl_ref)

        def branches(buf):
            for k, w_ref in enumerate((wa_ref, wb_ref, wc_ref)):
                buf[k] = jnp.dot(y_ref[k], w_ref[...].reshape(D, D), preferred_element_type=F32)

        def merge(buf):
            for c in range(tm // RC):
                rows = pl.ds(c * RC, RC)
                for c0 in COLS:
                    cs = slice(c0, c0 + CW)
                    m = 0.0
                    for k in range(3):
                        pk = buf[k, rows, cs]
                        p_ref[k, rows, cs] = pk.astype(BF)
                        m = m + _sigmoid(gl_ref[rows, k * D + c0:k * D + c0 + CW]) * pk
                    mg_ref[rows, cs] = m.astype(BF)
            xo_ref[...] = x_ref[...] + jnp.dot(mg_ref[...], wo_ref[...].reshape(D, D), preferred_element_type=F32)

        bufs = (even_s, odd_s)

        @pl.when(i == 0)
        def _():
            branches(bufs[0])

        for par in (0, 1):
            @pl.when((i % 2 == par) & (i > 0) & (i < nT))
            def _():
                branches(bufs[par])
                merge(bufs[1 - par])

        @pl.when(i == nT)
        def _():
            merge(bufs[(nT - 1) % 2])

    prev = lambda i: jnp.maximum(i - 1, 0)
    return _pcall(
        body, name="fwd_out", grid=(nT + 1,),
        in_specs=[
            pl.BlockSpec((3, tm, D), lambda i: (0, jnp.minimum(i, nT - 1), 0)),
            pl.BlockSpec((tm, SHW), lambda i: (prev(i), 3)),
            pl.BlockSpec((tm, D), lambda i: (prev(i), 0)),
        ] + [pl.BlockSpec((NSH, q, D), _branch_spec(k)) for k in range(4)],
        out_specs=[
            pl.BlockSpec((3, tm, D), lambda i: (0, prev(i), 0)),
            pl.BlockSpec((tm, D), lambda i: (prev(i), 0)),
            pl.BlockSpec((tm, D), lambda i: (prev(i), 0)),
        ],
        out_shape=[jax.ShapeDtypeStruct((3, T, D), BF), jax.ShapeDtypeStruct((T, D), BF),
                   jax.ShapeDtypeStruct((T, D), F32)],
        scratch_shapes=[pltpu.VMEM((3, tm, D), F32), pltpu.VMEM((3, tm, D), F32)],
        compiler_params=_params(("arbitrary",), 52),
    )(y3, proj, x, wl, wl, wl, wl)


def _loss_head(x, fg, tgt):
    T = x.shape[0]
    tm = _tile(T, 512)

    def body(x_ref, g_ref, t_ref, dx_ref, sq_ref, dg_ref, acc_s):
        i = pl.program_id(0)

        @pl.when(i == 0)
        def _():
            acc_s[...] = jnp.zeros_like(acc_s)

        def f(r0):
            rows = pl.ds(r0, RC)
            xs = [x_ref[rows, c0:c0 + CW] for c0 in COLS]
            r = lax.rsqrt(sum(_rowsum(v * v) for v in xs) * (1.0 / D) + RMS_EPS)
            xh = [v * r for v in xs]
            dyg, m = [], 0.0
            for k, c0 in enumerate(COLS):
                cs = slice(c0, c0 + CW)
                err = xh[k] * g_ref[:, cs] - t_ref[rows, cs]
                acc_s[0, :, cs] += err * err
                dy = err * (1.0 / D)
                acc_s[1, :, cs] += dy * xh[k]
                dyg.append(dy * g_ref[:, cs])
                m = m + _rowsum(dyg[k] * xh[k])
            m = m * (1.0 / D)
            for k, c0 in enumerate(COLS):
                dx_ref[rows, c0:c0 + CW] = r * (dyg[k] - xh[k] * m)
        _chunks(tm, f, unroll=4)

        @pl.when(i == pl.num_programs(0) - 1)
        def _():
            sq_ref[...] = jnp.sum(acc_s[0], axis=0, keepdims=True)
            dg_ref[...] = jnp.sum(acc_s[1], axis=0, keepdims=True)

    vec = pl.BlockSpec((1, D), lambda i: (0, 0))
    tile = pl.BlockSpec((tm, D), lambda i: (i, 0))
    return _pcall(
        body, name="loss_head", grid=(T // tm,),
        in_specs=[tile, vec, tile], out_specs=[tile, vec, vec],
        out_shape=[jax.ShapeDtypeStruct((T, D), F32), jax.ShapeDtypeStruct((1, D), F32), jax.ShapeDtypeStruct((1, D), F32)],
        scratch_shapes=[pltpu.VMEM((2, RC, D), F32)],
        compiler_params=_params(("arbitrary",), 32),
    )(x, fg, tgt)


def _bwd_mix(dxo, p3, proj, wl, lng, lnb, sgw, sgbT, pwb, pb, psc, cw, cb):
    T = dxo.shape[0]
    tm = _tile(T, 256)
    nT = T // tm
    hb = tm // HALO

    def body(dxo_ref, p_ref, pj2_ref, ch_ref, bpp_ref, ccp_ref, chp_ref, w_ref, w2_ref, w3_ref, lng_ref, lnb_ref, sgw_ref,
             sgbT_ref, pw_ref, pb_ref, psc_ref, cw_ref, cb_ref,
             dpj2_ref, dp_ref, dsw_ref, dsbT_ref, vec_ref, dpw_ref,
             dy_s, dy2_s, dy3_s, dyc_s, ext_s, nxt_s, halo_s, dch_s, xh_s, rstd_s, acc_s, accb_s,
             wm_s, wmT_s, *blk):
        i = pl.program_id(0)
        step = pl.program_id(1)
        ti = nT - 1 - i
        p_ref, ch_ref, bpp_ref, ccp_ref, chp_ref = (_AsF32(r) for r in (p_ref, ch_ref, bpp_ref, ccp_ref, chp_ref))

        @pl.when((i == 0) & (step == 0))
        def _():
            dsw_ref[...] = jnp.zeros_like(dsw_ref)
            dsbT_ref[...] = jnp.zeros_like(dsbT_ref)
            vec_ref[...] = jnp.zeros_like(vec_ref)
            dpw_ref[...] = jnp.zeros_like(dpw_ref)
            halo_s[...] = jnp.zeros_like(halo_s)
            for g in range(GRP):
                wm = _masked_sgu(sgw_ref, g)
                wm_s[g] = wm.astype(BF)
                wmT_s[g] = wm.T.astype(BF)

        def flush(n_acc, rows_of):
            for a in range(n_acc):
                vec_ref[rows_of[a]:rows_of[a] + 1, :] += jnp.sum(acc_s[a], axis=0, keepdims=True)

        def all_rows(fn):
            for c in range(tm // RC):
                fn(c * RC)

        @pl.when(step == 0)
        def _():
            pj_ref, dpj_ref = _AsF32(pj2_ref, SHW), _From(dpj2_ref, 3)
            dy_s[...] = _mm_nt(dxo_ref[...], w_ref[...].reshape(D, D))

            def gate(k, r0):
                rows = pl.ds(r0, RC)
                for c0 in COLS:
                    cs = slice(c0, c0 + CW)
                    dm = dy_s[rows, cs]
                    s = _sigmoid(pj_ref[rows, k * D + c0:k * D + c0 + CW])
                    dp_ref[k, rows, cs] = (s * dm).astype(BF)
                    dpj_ref[k, rows, cs] = (dm * p_ref[k, rows, cs] * s * (1.0 - s)).astype(BF)

            all_rows(functools.partial(gate, 2))
            dyc_s[...] = _mm_nt(dp_ref[2], w2_ref[...].reshape(D, D))
            all_rows(functools.partial(gate, 0))
            all_rows(functools.partial(gate, 1))

        @pl.when(step == 0)
        def _():
            pj_ref, dpj_ref = _AsF32(pj2_ref), _From(dpj2_ref, 0)
            dy3_s[...] = _mm_nt(dp_ref[1], w3_ref[...].reshape(D, D))
            acc_s[...] = jnp.zeros_like(acc_s)
            ext_s[0:HALO, :] = jnp.where(ti > 0, ccp_ref[...] * chp_ref[...], 0.0)
            nxt_s[tm:tm + HALO, :] = halo_s[0]

            def f1(r0):
                rows = pl.ds(r0, RC)
                ext_s[pl.ds(HALO + r0, RC), :] = pj_ref[rows, D:2 * D] * ch_ref[rows, :]
            all_rows(f1)

            def f2(r0):
                rows = pl.ds(r0, RC)
                for c0 in COLS:
                    cs = slice(c0, c0 + CW)
                    e = ext_s[pl.ds(r0, 2 * RC), cs]
                    e0, e1, e2 = e[RC:2 * RC], pltpu.roll(e, 1, 0)[RC:2 * RC], pltpu.roll(e, 2, 0)[RC:2 * RC]
                    conv = cw_ref[2:3, cs] * e0 + cw_ref[1:2, cs] * e1 + cw_ref[0:1, cs] * e2 + cb_ref[:, cs]
                    cbv = pj_ref[rows, cs]
                    sz, sg = _silu_fg(pj_ref[rows, 2 * D + c0:2 * D + c0 + CW])
                    dyc = dyc_s[rows, cs]
                    dconv = dyc * cbv * sz
                    dpj_ref[0, rows, cs] = (dyc * conv * sz).astype(BF)
                    dpj_ref[2, rows, cs] = (dyc * cbv * conv * sg).astype(BF)
                    nxt_s[rows, cs] = dconv
                    acc_s[0, :, cs] += dconv
                    acc_s[1, :, cs] += dconv * e2
                    acc_s[2, :, cs] += dconv * e1
                    acc_s[3, :, cs] += dconv * e0
            all_rows(f2)

            def f3(r0):
                rows = pl.ds(r0, RC)
                for c0 in COLS:
                    cs = slice(c0, c0 + CW)
                    e = nxt_s[pl.ds(r0, 2 * RC), cs]
                    dcc = (cw_ref[2:3, cs] * e + cw_ref[1:2, cs] * pltpu.roll(e, 2 * RC - 1, 0)
                           + cw_ref[0:1, cs] * pltpu.roll(e, 2 * RC - 2, 0))[0:RC]
                    dpj_ref[1, rows, cs] = (dcc * ch_ref[rows, cs]).astype(BF)
                    dch_s[rows, cs] = dcc * pj_ref[rows, D + c0:D + c0 + CW]
            all_rows(f3)
            halo_s[0] = nxt_s[0:HALO, :]
            flush(4, (4, 5, 6, 7))

        @pl.when(step == 1)
        def _():
            pj_ref, dpj_ref = _AsF32(pj2_ref, SHW), _From(dpj2_ref, 3)
            dy2_s[...] = _mm_nt(dp_ref[0], w2_ref[...].reshape(D, D))
            acc_s[...] = jnp.zeros_like(acc_s)
            ext_s[0:HALO, :] = jnp.where(ti > 0, bpp_ref[...], 0.0)
            ext_s[HALO:HALO + tm, :] = pj_ref[:, 0:D]
            nxt_s[tm:tm + HALO, :] = halo_s[1]

            nblk = tm // BLK
            d_b, t_b, dy0_b = blk[:nblk], blk[nblk:2 * nblk], blk[2 * nblk:]

            def block_rows(fn, n):
                for c in range(BLK // RC):
                    fn(n, c * RC)

            def f1(n, b0):
                r0, brow = n * BLK + b0, pl.ds(b0, RC)
                e = ext_s[pl.ds(r0, 2 * RC), :]
                for k, win in enumerate(POOL_WINDOWS):
                    gc = slice(k * PGW, (k + 1) * PGW)
                    s = _trail(e[:, gc], win)[RC:2 * RC]
                    d_b[n][brow, gc] = (s * _inv_count(ti * tm + r0, win) - e[RC:2 * RC, gc]).astype(BF)

            def pool(n):
                for k in range(NPG):
                    gc = slice(k * PGW, (k + 1) * PGW)
                    t_b[n][:, gc] = jnp.dot(d_b[n][:, gc], pw_ref[k], preferred_element_type=F32)

            def f2(n, b0):
                rows, brow = pl.ds(n * BLK + b0, RC), pl.ds(b0, RC)
                for c0 in COLS:
                    cs = slice(c0, c0 + CW)
                    y0 = t_b[n][brow, cs] + pb_ref[:, cs]
                    sz, sg = _silu_fg(pj_ref[rows, D + c0:D + c0 + CW])
                    dyb = dy3_s[rows, cs]
                    dy0 = dyb * psc_ref[:, cs] * sz
                    acc_s[0, :, cs] += dy0
                    acc_s[1, :, cs] += dyb * y0 * sz
                    dy0_b[n][brow, cs] = dy0.astype(BF)
                    dpj_ref[1, rows, cs] = (dyb * y0 * psc_ref[:, cs] * sg).astype(BF)

            def pool_back(n):
                for k in range(NPG):
                    gc = slice(k * PGW, (k + 1) * PGW)
                    dpw_ref[k] += _mm_tn(d_b[n][:, gc], dy0_b[n][:, gc])
                    t_b[n][:, gc] = _mm_nt(dy0_b[n][:, gc], pw_ref[k])

            def f3(n, b0):
                r0, brow = n * BLK + b0, pl.ds(b0, RC)
                for k, win in enumerate(POOL_WINDOWS):
                    gc = slice(k * PGW, (k + 1) * PGW)
                    nxt_s[pl.ds(r0, RC), gc] = t_b[n][brow, gc] * _inv_count(ti * tm + r0, win)

            def f4(n, b0):
                r0, brow = n * BLK + b0, pl.ds(b0, RC)
                rows = pl.ds(r0, RC)
                e = nxt_s[pl.ds(r0, 2 * RC), :]
                for k, win in enumerate(POOL_WINDOWS):
                    gc = slice(k * PGW, (k + 1) * PGW)
                    dpj_ref[0, rows, gc] = (_lead(e[:, gc], win)[0:RC] - t_b[n][brow, gc]).astype(BF)
                dpj_ref[2, rows, :] = dch_s[rows, :].astype(BF)

            def spread(n):
                block_rows(f3, n)
                block_rows(f4, n)

            stages = (functools.partial(block_rows, f1), pool, functools.partial(block_rows, f2), pool_back, spread)
            for t in range(nblk + len(stages) - 1):
                for s in reversed(range(len(stages))):
                    if 0 <= t - s < nblk:
                        stages[s](nblk - 1 - (t - s))
            halo_s[1] = nxt_s[0:HALO, :]
            flush(2, (2, 3))

        @pl.when(step == 1)
        def _():
            pj_ref, dpj_ref = _AsF32(pj2_ref), _From(dpj2_ref, 0)
            acc_s[...] = jnp.zeros_like(acc_s)
            accb_s[...] = jnp.zeros_like(accb_s)
            nblk = tm // BLK
            vn_b, t_b, dmix_b = blk[:nblk], blk[nblk:2 * nblk], blk[2 * nblk:]

            def block_rows(fn, n):
                for c in range(BLK // RC):
                    fn(n, c * RC)

            def f1(n, b0):
                rows, brow = pl.ds(n * BLK + b0, RC), pl.ds(b0, RC)
                gv = [_gelu(pj_ref[rows, D + c0:D + c0 + CW]) for c0 in COLS]
                mu = sum(_rowsum(v) for v in gv) * (1.0 / D)
                dv = [v - mu for v in gv]
                var = sum(_rowsum(v * v) for v in dv) * (1.0 / D)
                rstd = lax.rsqrt(var + LN_EPS)
                rstd_s[rows, :] = jnp.broadcast_to(rstd, (RC, BLK))
                for k, c0 in enumerate(COLS):
                    cs = slice(c0, c0 + CW)
                    xh = dv[k] * rstd
                    xh_s[rows, cs] = xh
                    vn_b[n][brow, cs] = (xh * lng_ref[:, cs] + lnb_ref[:, cs]).astype(BF)

            def mix(n):
                for g in range(GRP):
                    gc = slice(g * BLK, (g + 1) * BLK)
                    t_b[n][:, gc] = jnp.dot(wm_s[g], vn_b[n][:, gc], preferred_element_type=F32) + sgbT_ref[:, g:g + 1]

            def f2(n, b0):
                rows, brow = pl.ds(n * BLK + b0, RC), pl.ds(b0, RC)
                for c0 in COLS:
                    cs = slice(c0, c0 + CW)
                    gu, ggu = _gelu_fg(pj_ref[rows, cs])
                    sz, sg = _silu_fg(pj_ref[rows, 2 * D + c0:2 * D + c0 + CW])
                    dya = dy2_s[rows, cs]
                    mx = t_b[n][brow, cs]
                    dmix = dya * gu * sz
                    dpj_ref[0, rows, cs] = (dya * mx * sz * ggu).astype(BF)
                    dpj_ref[2, rows, cs] = (dya * gu * mx * sg).astype(BF)
                    dmix_b[n][brow, cs] = dmix.astype(BF)
                    accb_s[brow, cs] += dmix

            def mix_back(n):
                for g in range(GRP):
                    gc = slice(g * BLK, (g + 1) * BLK)
                    t_b[n][:, gc] = jnp.dot(wmT_s[g], dmix_b[n][:, gc], preferred_element_type=F32)
                    dsw_ref[g] += _mm_nt(dmix_b[n][:, gc], vn_b[n][:, gc])

            def f3(n, b0):
                rows, brow = pl.ds(n * BLK + b0, RC), pl.ds(b0, RC)
                rstd = rstd_s[rows, 0:1]
                dxh, m1, m2 = [], 0.0, 0.0
                for k, c0 in enumerate(COLS):
                    cs = slice(c0, c0 + CW)
                    dvn = t_b[n][brow, cs]
                    xh = xh_s[rows, cs]
                    acc_s[0, :, cs] += dvn * xh
                    acc_s[1, :, cs] += dvn
                    dxh.append(dvn * lng_ref[:, cs])
                    m1 = m1 + _rowsum(dxh[k])
                    m2 = m2 + _rowsum(dxh[k] * xh)
                m1 = m1 * (1.0 / D)
                m2 = m2 * (1.0 / D)
                for k, c0 in enumerate(COLS):
                    cs = slice(c0, c0 + CW)
                    _, ggv = _gelu_fg(pj_ref[rows, D + c0:D + c0 + CW])
                    dpj_ref[1, rows, cs] = (rstd * (dxh[k] - m1 - xh_s[rows, cs] * m2) * ggv).astype(BF)

            stages = (functools.partial(block_rows, f1), mix, functools.partial(block_rows, f2), mix_back,
                      functools.partial(block_rows, f3))
            for t in range(nblk + len(stages) - 1):
                for s in reversed(range(len(stages))):
                    if 0 <= t - s < nblk:
                        stages[s](t - s)
            for g in range(GRP):
                dsbT_ref[:, g:g + 1] += _rowsum(accb_s[:, g * BLK:(g + 1) * BLK])
            flush(2, (0, 1))

            @pl.when(i == nT - 1)
            def _():
                for g in range(GRP):
                    r = lax.broadcasted_iota(jnp.int32, (BLK, BLK), 0) // 64
                    c = lax.broadcasted_iota(jnp.int32, (BLK, BLK), 1) // 64
                    dsw_ref[g] = jnp.where(c <= r, dsw_ref[g], 0.0)

    def prev(col):
        return pl.BlockSpec((HALO, D), lambda i, p: (jnp.maximum((nT - 1 - i) * hb - 1, 0), col))

    vec = pl.BlockSpec((1, D), lambda i, p: (0, 0))
    const3 = lambda i, p: (0, 0, 0)
    return _pcall(
        body, name="bwd_mix", grid=(nT, 2),
        in_specs=[
            pl.BlockSpec((tm, D), lambda i, p: (nT - 1 - i, 0)),
            pl.BlockSpec((3, tm, D), lambda i, p: (0, nT - 1 - i, 0)),
            pl.BlockSpec((tm, 2 * SHW), lambda i, p: (nT - 1 - i, 1 - p)),
            pl.BlockSpec((tm, D), lambda i, p: (nT - 1 - i, 5)),
            prev(3), prev(7), prev(5),
            pl.BlockSpec((NSH, D // NSH, D), _branch_spec(3)),
            pl.BlockSpec((NSH, D // NSH, D), _branch_spec(lambda i, p: 2 - 2 * p)),
            pl.BlockSpec((NSH, D // NSH, D), _branch_spec(1)),
            vec, vec,
            pl.BlockSpec((GRP, BLK, BLK), const3),
            pl.BlockSpec((BLK, GRP), lambda i, p: (0, 0)),
            pl.BlockSpec((NPG, PGW, PGW), const3),
            vec, vec,
            pl.BlockSpec((3, D), lambda i, p: (0, 0)), vec,
        ],
        out_specs=[
            pl.BlockSpec((6, tm, D), lambda i, p: (1 - p, nT - 1 - i, 0)),
            pl.BlockSpec((3, tm, D), lambda i, p: (0, nT - 1 - i, 0)),
            pl.BlockSpec((GRP, BLK, BLK), const3),
            pl.BlockSpec((BLK, GRP), lambda i, p: (0, 0)),
            pl.BlockSpec((8, D), lambda i, p: (0, 0)),
            pl.BlockSpec((NPG, PGW, PGW), const3),
        ],
        out_shape=[
            jax.ShapeDtypeStruct((12, T, D), BF), jax.ShapeDtypeStruct((3, T, D), BF),
            jax.ShapeDtypeStruct((GRP, BLK, BLK), F32), jax.ShapeDtypeStruct((BLK, GRP), F32),
            jax.ShapeDtypeStruct((8, D), F32), jax.ShapeDtypeStruct((NPG, PGW, PGW), F32),
        ],
        scratch_shapes=[
            pltpu.VMEM((tm, D), F32),
            pltpu.VMEM((tm, D), F32),
            pltpu.VMEM((tm, D), F32),
            pltpu.VMEM((tm, D), F32),
            pltpu.VMEM((tm + HALO, D), F32),
            pltpu.VMEM((tm + HALO, D), F32),
            pltpu.VMEM((2, HALO, D), F32),
            pltpu.VMEM((tm, D), F32),
            pltpu.VMEM((tm, D), F32),
            pltpu.VMEM((tm, BLK), F32),
            pltpu.VMEM((4, RC, D), F32),
            pltpu.VMEM((BLK, D), F32),
            pltpu.VMEM((GRP, BLK, BLK), BF), pltpu.VMEM((GRP, BLK, BLK), BF),
        ] + [pltpu.VMEM((BLK, D), dt) for dt in (BF, F32, BF) for _ in range(tm // BLK)],
        compiler_params=_params(("arbitrary", "arbitrary"), 56),
    )(dxo, p3, proj, proj, proj, proj, proj, wl, wl, wl, lng, lnb, sgw, sgbT, pwb, pb, psc, cw, cb)


def _bwd_in(dproj, wl, x, dxo, ng):
    T = x.shape[0]
    tm = _tile(T, 256)
    nT = T // tm

    def body(dpj_ref, w_hbm, x_ref, dxo_ref, ng_ref, dx_ref, dng_ref, w_s, even_s, odd_s, g_s, wsem):
        i = pl.program_id(0)
        bufs = (even_s, odd_s)

        def w_copy(j):
            return pltpu.make_async_copy(w_hbm.at[j, pl.ds(0, SHW), :], w_s.at[j], wsem.at[j])

        def d_h():
            return sum(_mm_nt(dpj_ref[3 * j + b], w_s[j, b * D:(b + 1) * D, :]) for j in range(NSH) for b in range(3))

        def finish(prev):
            for c in range(tm // RC):
                finish_rows(prev, c * RC)

        def finish_rows(prev, r0):
            rows = pl.ds(r0, RC)
            xs = [x_ref[rows, c0:c0 + CW] for c0 in COLS]
            r = lax.rsqrt(sum(_rowsum(v * v) for v in xs) * (1.0 / D) + RMS_EPS)
            xh = [v * r for v in xs]
            dhg, m = [], 0.0
            for k, c0 in enumerate(COLS):
                cs = slice(c0, c0 + CW)
                dh = prev[rows, cs]
                g_s[:, cs] += dh * xh[k]
                dhg.append(dh * ng_ref[:, cs])
                m = m + _rowsum(dhg[k] * xh[k])
            m = m * (1.0 / D)
            for k, c0 in enumerate(COLS):
                cs = slice(c0, c0 + CW)
                dx_ref[rows, cs] = dxo_ref[rows, cs] + r * (dhg[k] - xh[k] * m)

        @pl.when(i == 0)
        def _():
            for j in range(NSH):
                w_copy(j).start()
            g_s[...] = jnp.zeros_like(g_s)
            for j in range(NSH):
                w_copy(j).wait()
            bufs[0][...] = d_h()

        for par in (0, 1):
            @pl.when((i % 2 == par) & (i > 0) & (i < nT))
            def _():
                bufs[par][...] = d_h()
                finish(bufs[1 - par])

        @pl.when(i == nT)
        def _():
            finish(bufs[(nT - 1) % 2])
            dng_ref[...] = jnp.sum(g_s[...], axis=0, keepdims=True)

    vec = pl.BlockSpec((1, D), lambda i: (0, 0))
    tile = pl.BlockSpec((tm, D), lambda i: (jnp.maximum(i - 1, 0), 0))
    return _pcall(
        body, name="bwd_in", grid=(nT + 1,),
        in_specs=[pl.BlockSpec((3 * NSH, tm, D), lambda i: (0, jnp.minimum(i, nT - 1), 0)), ANY, tile, tile, vec],
        out_specs=[tile, vec],
        out_shape=[jax.ShapeDtypeStruct((T, D), F32), jax.ShapeDtypeStruct((1, D), F32)],
        scratch_shapes=[pltpu.VMEM((NSH, SHW, D), BF), pltpu.VMEM((tm, D), F32), pltpu.VMEM((tm, D), F32),
                        pltpu.VMEM((RC, D), F32), pltpu.SemaphoreType.DMA((NSH,))],
        compiler_params=_params(("arbitrary",), 52),
    )(dproj, wl, x, dxo, ng)


def _tn_grad(a3, b3, split, out_map, name, into=None, after=None):
    nb, T, _ = b3.shape
    tk = _tile(T, 2048)
    nk = T // tk
    rows = D // split
    a_batched = a3.shape[0] > 1

    def body(a_ref, b_ref, *rest):
        o_ref, acc_s = rest[-2:]
        k = pl.program_id(1)

        @pl.when(k == 0)
        def _():
            acc_s[...] = _mm_tn(a_ref[...], b_ref[...])

        @pl.when((k > 0) & (k < nk))
        def _():
            acc_s[...] += _mm_tn(a_ref[...], b_ref[...])

        @pl.when(k >= nk - 1)
        def _():
            r0 = pl.multiple_of((k - (nk - 1)) * rows, rows)
            o_ref[...] = acc_s[pl.ds(r0, rows), :].astype(o_ref.dtype)

    def tok(k):
        return jnp.minimum(k, nk - 1)

    extra = ([] if into is None else [into]) + ([] if after is None else [after])
    return _pcall(
        body, name=name, grid=(nb, nk + split - 1),
        in_specs=[
            pl.BlockSpec((None, tk, D), (lambda n, k: (n, tok(k), 0)) if a_batched else (lambda n, k: (0, tok(k), 0))),
            pl.BlockSpec((None, tk, D), lambda n, k: (n, tok(k), 0)),
        ] + [ANY] * len(extra),
        out_specs=pl.BlockSpec((None, rows, D), lambda n, k: out_map(n, jnp.maximum(k - (nk - 1), 0))),
        out_shape=jax.ShapeDtypeStruct((NSH, PK_ROWS, D), BF),
        input_output_aliases={} if into is None else {2: 0},
        scratch_shapes=[pltpu.VMEM((D, D), F32)],
        compiler_params=_params(("arbitrary", "arbitrary"), 56),
    )(a3, b3, *extra)


def _place():
    x, y, c = lax.axis_index("x"), lax.axis_index("y"), lax.axis_index("c")
    chips = [(1 - x, y), (x, 1 - y), (1 - x, 1 - y)]
    return x, y, c, chips


def _peers(reach):
    x, y, c, chips = _place()
    if reach == "chips":
        return 2 * x + y, [((px, py, c), 2 * px + py) for px, py in chips]
    others = [(x, y, 1 - c)] + [(px, py, pc) for px, py in chips for pc in (c, 1 - c)]
    return 4 * x + 2 * y + c, [(pr, 4 * pr[0] + 2 * pr[1] + pr[2]) for pr in others]


ANY = pl.BlockSpec(memory_space=pl.ANY)


HBM = pl.BlockSpec(memory_space=pltpu.HBM)
SEM = pl.BlockSpec(memory_space=pltpu.SEMAPHORE)
EFFECT = pltpu.SideEffectType.DATAFLOW_SIDE_EFFECTING


def _own_slot(src, from_slot, name):
    rows = src.shape[-2]
    rb = rows // 8
    me = (2 * lax.axis_index("x") + lax.axis_index("y")).astype(jnp.int32).reshape(1)

    def body(me_ref, src_ref, land_ref):
        land_ref[...] = src_ref[...]

    if from_slot:
        src_spec = pl.BlockSpec((None, rb, D), lambda i, me_ref: (me_ref[0], i, 0))
    else:
        src_spec = pl.BlockSpec((rb, D), lambda i, me_ref: (i, 0))
    return _pcall(
        body, name=name,
        grid_spec=pltpu.PrefetchScalarGridSpec(
            num_scalar_prefetch=1, grid=(rows // rb,), in_specs=[src_spec],
            out_specs=pl.BlockSpec((None, rb, D), lambda i, me_ref: (me_ref[0], i, 0))),
        out_shape=jax.ShapeDtypeStruct((NSH, rows, D), src.dtype),
        compiler_params=_params(("arbitrary",), 32),
    )(me, src)


def _my_half():
    return pl.ds(pl.multiple_of(lax.axis_index("c") * PK_HALF, PK_HALF), PK_HALF)


def _push_start(name, srcs, lands, per_peer, reach="chips", halved=()):
    n = len(srcs)
    npeer = 3 if reach == "chips" else 7
    ns = n * npeer

    def body(*refs):
        src, land = refs[:n], refs[n:2 * n]
        ssem, rsem = refs[2 * n:2 * n + ns], refs[2 * n + ns:2 * n + 2 * ns]
        token = refs[-1]
        me, peers = _peers(reach)
        for i in range(n):
            for j, (peer, slot) in enumerate(peers):
                s = src[i].at[slot] if per_peer else src[i]
                d = land[i].at[me]
                if i in halved:
                    s, d = s.at[_my_half()], d.at[_my_half()]
                pltpu.make_async_remote_copy(
                    src_ref=s, dst_ref=d, send_sem=ssem[npeer * i + j], recv_sem=rsem[npeer * i + j], device_id=peer,
                    device_id_type=MESH).start()
        token[...] = jnp.zeros_like(token)

    ops = list(srcs) + list(lands)
    out = _pcall(
        body, name=name,
        out_shape=tuple([pltpu.SemaphoreType.DMA(())] * (2 * ns) + [pltpu.HBM(a.shape, a.dtype) for a in ops]
                        + [jax.ShapeDtypeStruct((8, 128), F32)]),
        in_specs=[HBM] * (2 * n),
        out_specs=tuple([SEM] * (2 * ns) + [HBM] * (2 * n) + [pl.BlockSpec(memory_space=pltpu.VMEM)]),
        input_output_aliases={i: 2 * ns + i for i in range(2 * n)},
        compiler_params=pltpu.CompilerParams(has_side_effects=EFFECT),
    )(*[pltpu.with_memory_space_constraint(a, pltpu.HBM) for a in ops])
    return out[:ns], out[ns:2 * ns], out[2 * ns:2 * ns + n], out[2 * ns + n:2 * ns + 2 * n], out[-1]


def _push_wait(name, src, land, ssem, rsem, after, per_peer, reach="chips", halved=False):
    npeer = len(ssem)

    def body(src_ref, land_ref, *rest):
        sems = rest[:2 * npeer]
        _, peers = _peers(reach)
        for j, (peer, slot) in enumerate(peers):
            s = src_ref.at[slot] if per_peer else src_ref
            d = land_ref.at[slot]
            if halved:
                s, d = s.at[_my_half()], d.at[_my_half()]
            cp = pltpu.make_async_remote_copy(
                src_ref=s, dst_ref=d, send_sem=sems[j],
                recv_sem=sems[npeer + j], device_id=peer, device_id_type=MESH)
            cp.wait_send()
            cp.wait_recv()

    return _pcall(
        body, name=name,
        out_shape=(pltpu.HBM(src.shape, src.dtype), pltpu.HBM(land.shape, land.dtype)),
        in_specs=[HBM, HBM] + [SEM] * (2 * npeer) + [ANY], out_specs=(HBM, HBM),
        input_output_aliases={0: 0, 1: 1},
        compiler_params=pltpu.CompilerParams(has_side_effects=EFFECT),
    )(src, land, *ssem, *rsem, after)[1]


def _share_halves(land):
    def body(land_in, land_ref, ssem, rsem):
        x, y, c, chips = _place()
        other = pl.ds(pl.multiple_of((1 - c) * PK_HALF, PK_HALF), PK_HALF)
        sends = [pltpu.make_async_remote_copy(
            src_ref=land_ref.at[2 * px + py, _my_half()], dst_ref=land_ref.at[2 * px + py, _my_half()],
            send_sem=ssem.at[j], recv_sem=rsem.at[j], device_id=(x, y, 1 - c), device_id_type=MESH)
            for j, (px, py) in enumerate(chips)]
        for cp in sends:
            cp.start()
        for j, (px, py) in enumerate(chips):
            pltpu.make_async_remote_copy(
                src_ref=land_ref.at[2 * px + py, other], dst_ref=land_ref.at[2 * px + py, other], send_sem=ssem.at[j],
                recv_sem=rsem.at[j], device_id=(x, y, 1 - c), device_id_type=MESH).wait_recv()
        for cp in sends:
            cp.wait_send()

    return _pcall(
        body, name="share_halves", in_specs=[ANY], out_specs=ANY,
        out_shape=jax.ShapeDtypeStruct(land.shape, land.dtype), input_output_aliases={0: 0},
        scratch_shapes=[pltpu.SemaphoreType.DMA((3,)), pltpu.SemaphoreType.DMA((3,))],
    )(land)


def _swap_sibling(arrs):
    n = len(arrs)

    def body(*refs):
        src, dst = refs[:n], refs[n:2 * n]
        ssem, rsem = refs[2 * n:]
        x, y, c, _ = _place()
        cps = [pltpu.make_async_remote_copy(src_ref=src[a], dst_ref=dst[a], send_sem=ssem.at[a], recv_sem=rsem.at[a],
                                            device_id=(x, y, 1 - c), device_id_type=MESH) for a in range(n)]
        for cp in cps:
            cp.start()
        for cp in cps:
            cp.wait()

    return _pcall(
        body, name="swap_sibling",
        in_specs=[ANY] * n, out_specs=[ANY] * n,
        out_shape=[jax.ShapeDtypeStruct(a.shape, a.dtype) for a in arrs],
        scratch_shapes=[pltpu.SemaphoreType.DMA((n,)), pltpu.SemaphoreType.DMA((n,))],
    )(*arrs)


def _gather_all(v):
    def body(src, dst, ssem, rsem, lsem):
        x, y, c, _ = _place()
        me = 4 * x + 2 * y + c
        peers = [(x, y, 1 - c), (1 - x, y, c), (1 - x, y, 1 - c), (x, 1 - y, c), (x, 1 - y, 1 - c),
                 (1 - x, 1 - y, c), (1 - x, 1 - y, 1 - c)]
        local = pltpu.make_async_copy(src, dst.at[me], lsem)
        local.start()
        sends = [pltpu.make_async_remote_copy(src_ref=src, dst_ref=dst.at[me], send_sem=ssem.at[j], recv_sem=rsem.at[j],
                                              device_id=pr, device_id_type=MESH) for j, pr in enumerate(peers)]
        for cp in sends:
            cp.start()
        for j, (px, py, pc) in enumerate(peers):
            pltpu.make_async_remote_copy(src_ref=src, dst_ref=dst.at[4 * px + 2 * py + pc], send_sem=ssem.at[j],
                                         recv_sem=rsem.at[j], device_id=(px, py, pc), device_id_type=MESH).wait_recv()
        for cp in sends:
            cp.wait_send()
        local.wait()

    return _pcall(
        body, name="gather_all", in_specs=[ANY], out_specs=ANY,
        out_shape=jax.ShapeDtypeStruct((8,) + v.shape, v.dtype),
        scratch_shapes=[pltpu.SemaphoreType.DMA((7,)), pltpu.SemaphoreType.DMA((7,)), pltpu.SemaphoreType.DMA(())],
    )(v)


def _sum_slots(r, rb):
    S = r.shape[0]

    def body(r_ref, o_ref):
        acc = r_ref[0].astype(F32)
        for s in range(1, S):
            acc = acc + r_ref[s].astype(F32)
        o_ref[...] = acc

    if r.ndim == 3:
        _, R, C = r.shape
        grid, blk, imap = (R // rb,), (S, rb, C), (lambda i: (0, i, 0))
        oblk, omap = (rb, C), (lambda i: (i, 0))
    else:
        _, K, R, C = r.shape
        grid, blk, imap = (K,), (S, None, R, C), (lambda i: (0, i, 0, 0))
        oblk, omap = (None, R, C), (lambda i: (i, 0, 0))
    return _pcall(
        body, name="sum_slots", grid=grid, in_specs=[pl.BlockSpec(blk, imap)], out_specs=pl.BlockSpec(oblk, omap),
        out_shape=jax.ShapeDtypeStruct(r.shape[1:], F32), compiler_params=_params(("arbitrary",), 48),
    )(r)


def _adamw(gs, g_spec, w, m, v, p_spec, prev, grid):
    ng = len(gs)
    bc1 = 1.0 - ADAM_B1 ** ADAM_STEP
    bc2 = 1.0 - ADAM_B2 ** ADAM_STEP

    def body(*refs):
        g = refs[0][...]
        for a in range(1, ng):
            g = g + refs[a][...]
        w_ref, m_ref, v_ref = refs[ng:ng + 3]
        go, do, mo, vo = refs[ng + 3 + 4:]
        mn = ADAM_B1 * m_ref[...] + (1.0 - ADAM_B1) * g
        vn = ADAM_B2 * v_ref[...] + (1.0 - ADAM_B2) * (g * g)
        go[...] = g
        mo[...] = mn
        vo[...] = vn
        do[...] = -ADAM_LR * ((mn / bc1) / (jnp.sqrt(vn / bc2) + ADAM_EPS) + ADAM_WD * w_ref[...])

    out = jax.ShapeDtypeStruct(w.shape, F32)
    k0 = ng + 3
    return _pcall(
        body, name="adamw", grid=grid,
        in_specs=[g_spec] * ng + [p_spec] * 3 + [ANY] * 4,
        out_specs=[p_spec] * 4, out_shape=[out] * 4,
        input_output_aliases={k0: 0, k0 + 1: 1, k0 + 2: 2, k0 + 3: 3},
        compiler_params=_params(("arbitrary",) * len(grid), 48),
    )(*gs, w, m, v, *prev)


def _empty4(w):
    return tuple(lax.empty(w.shape, F32) for _ in range(4))


N_SGW = L * GRP * BLK * BLK // D
O_NG, O_VEC, O_SGB, O_FG, O_SGW = 0, 8, 32, 40, 48
O_CW = O_SGW + N_SGW
N_PACK = O_CW + 16
PACK_RB = N_PACK // 3


def _pad_to(a, rows):
    return jnp.pad(a, ((0, rows - a.shape[0]), (0, 0)))


def _pack_small(ng, vecs, sgb, fg, sgw, cw):
    parts = [_pad_to(ng, 8), _pad_to(vecs.reshape(L * 5, D), 24), _pad_to(sgb.reshape(L, D), 8),
             _pad_to(fg.reshape(1, D), 8), sgw.reshape(N_SGW, D), _pad_to(cw, 16)]
    return jnp.concatenate(parts, axis=0)


def kernel(x, norm_g, w_in, sgu_ln_g, sgu_ln_b, sgu_w, sgu_b, pool_w, pool_b, pool_scale, conv_w, conv_b, w_branch_a, w_branch_b, w_branch_c, w_out, final_g, loss_target, m_norm_g, m_w_in, m_sgu_ln_g, m_sgu_ln_b, m_sgu_w, m_sgu_b, m_pool_w, m_pool_b, m_pool_scale, m_conv_w, m_conv_b, m_w_branch_a, m_w_branch_b, m_w_branch_c, m_w_out, m_final_g, v_norm_g, v_w_in, v_sgu_ln_g, v_sgu_ln_b, v_sgu_w, v_sgu_b, v_pool_w, v_pool_b, v_pool_scale, v_conv_w, v_conv_b, v_w_branch_a, v_w_branch_b, v_w_branch_c, v_w_out, v_final_g):
    cx, cy = lax.axis_index("x"), lax.axis_index("y")
    me = 2 * cx + cy
    xl, tgt = x[0], loss_target[0]
    q = D // NSH

    wq = w_in.astype(BF).reshape(L, D, 3, D).transpose(0, 2, 1, 3).reshape(L, SHW, D)
    brq = jnp.stack([w_branch_a, w_branch_b, w_branch_c, w_out], axis=1).astype(BF).reshape(L, D, D)
    pwq = pool_w.astype(BF).reshape(L, PK_CW - PK_PW, D)
    cwq = lax.bitcast_convert_type(conv_w, BF).reshape(L, 3 * q * 2)
    cwq = jnp.pad(cwq, ((0, 0), (0, (PK_ROWS - PK_CW) * D - 3 * q * 2))).reshape(L, PK_ROWS - PK_CW, D)
    packs = [jnp.concatenate([wq[l], brq[l], pwq[l], cwq[l]], axis=0) for l in range(L)]
    lands = [_own_slot(packs[l], False, f"ag_own_{l}") for l in range(L)]
    ag_s, ag_r, packs, lands, tok = _push_start("ag_start", packs, lands, False, halved=(0,))
    sgbT = sgu_b.transpose(0, 2, 1)

    def layer_weights(l, after):
        wl = _push_wait(f"ag_wait_{l}", packs[l], lands[l], ag_s[3 * l:3 * l + 3], ag_r[3 * l:3 * l + 3], after, False,
                        halved=l == 0)
        if l == 0:
            wl = _share_halves(wl)
        pwb = wl[:, PK_PW:PK_CW].reshape(NSH, NPG, PGW // NSH, PGW).transpose(1, 0, 2, 3).reshape(NPG, PGW, PGW)
        cwb = wl[:, PK_CW:].reshape(NSH, (PK_ROWS - PK_CW) * D)[:, :3 * q * 2].reshape(NSH, 3, q, 2)
        cwf = lax.bitcast_convert_type(cwb, F32).transpose(1, 0, 2).reshape(3, D)
        small = (sgu_ln_g[l:l + 1], sgu_ln_b[l:l + 1], sgu_w[l], sgbT[l], pwb, pool_b[l:l + 1], pool_scale[l:l + 1],
                 cwf, conv_b[l:l + 1])
        return wl, small

    xs, saved, wts = [xl], [], []
    for l in range(L):
        wl, small = layer_weights(l, tok if l == 0 else xs[l])
        wts.append((wl, small))
        proj, h, y3 = _fwd_in(xs[l], norm_g[l:l + 1], wl, *small)
        p3, mg, xo = _fwd_out(y3, proj, xs[l], wl)
        saved.append((proj, h, y3, p3, mg))
        xs.append(xo)

    dx, sq, dfg = _loss_head(xs[L], final_g[None], tgt)
    loss = lax.psum(jnp.sum(sq) * (0.5 / D), ("x", "y", "c"))

    g_w_in = _empty4(w_in)
    g_br = [_empty4(w_out) for _ in range(4)]
    g_pw = _empty4(pool_w)
    dng, dvec, dsgw, dsgb = [None] * L, [None] * L, [None] * L, [None] * L
    branches = [(w_branch_a, m_w_branch_a, v_w_branch_a), (w_branch_b, m_w_branch_b, v_w_branch_b),
                (w_branch_c, m_w_branch_c, v_w_branch_c), (w_out, m_w_out, v_w_out)]
    nb = D // 128

    def finish(l, landed):
        nonlocal g_w_in, g_pw
        mine = _sum_slots(landed, PK_ROWS // 8)
        sums = [mine, _swap_sibling([mine])[0]]
        g_w_in = _adamw(sums, pl.BlockSpec((128, D), lambda b, i: (b * nb + i, 0)), w_in, m_w_in, v_w_in,
                        pl.BlockSpec((None, 128, D), lambda b, i: (l, i, b)), g_w_in, (3, nb))
        for k, (w, m, v) in enumerate(branches):
            g_br[k] = _adamw(sums, pl.BlockSpec((q, D), lambda i, k=k: (PK_BR // q + k, 0)), w, m, v,
                             pl.BlockSpec((None, q, D), lambda i: (l, 0, 0)), g_br[k], (1,))
        pools = [a[PK_PW:PK_CW].reshape(NPG, PGW // NSH, PGW) for a in sums]
        g_pw = _adamw(pools, pl.BlockSpec((None, PGW // NSH, PGW), lambda g: (g, 0, 0)), pool_w, m_pool_w, v_pool_w,
                      pl.BlockSpec((None, None, PGW // NSH, PGW), lambda g: (l, g, 0, 0)), g_pw, (NPG,))

    pend = None
    for l in reversed(range(L)):
        proj, h, y3, p3, mg = saved[l]
        wl, small = wts[l]
        dproj, dp3, dsgw[l], dsbT, dvec[l], dpw = _bwd_mix(dx, p3, proj, wl, *small)
        dsgb[l] = dsbT.T
        if l == 0:
            dv = jnp.stack(dvec)
            part = _pack_small(jnp.zeros((L, D), F32), dv[:, 0:5], jnp.stack(dsgb), dfg[0], jnp.stack(dsgw),
                               dv[:, 5:8].reshape(L * 3, D))
            zone = lax.dynamic_update_slice(lax.empty((8, N_PACK, D), F32), part[None],
                                            (2 * me + lax.axis_index("c"), 0, 0))
            sm_s, sm_r, (part,), (zone,), started = _push_start("small_start", [part], [zone], False, "all")
        grads = _tn_grad(h[None], dproj, 1, lambda n, s: (n // 3, n % 3, 0), "grad_w_in", after=started if l == 0 else None)
        grads = _tn_grad(y3, dp3, NSH, lambda n, s: (s, PK_BR // q + n, 0), "grad_w_branch", into=grads)
        grads = _tn_grad(mg[None], dx[None], NSH, lambda n, s: (s, PK_BR // q + 3, 0), "grad_w_out", into=grads)
        dpq = dpw.astype(BF).reshape(NPG, NSH, PGW // NSH, PGW).transpose(1, 0, 2, 3).reshape(NSH, PK_CW - PK_PW, D)
        grads = lax.dynamic_update_slice(grads, jnp.pad(dpq, ((0, 0), (0, PK_ROWS - PK_CW), (0, 0))), (0, PK_PW, 0))
        if pend is not None:
            landed = _push_wait(f"rs_wait_{pend[0]}", *pend[1:], dproj, True)
        land = _own_slot(grads, True, f"rs_own_{l}")
        ss, rs, (grads,), (land,), tok = _push_start(f"rs_start_{l}", [grads], [land], True)
        if pend is not None:
            finish(pend[0], landed)
        dx, dng[l] = _bwd_in(dproj, wl, xs[l], dx, norm_g[l:l + 1] + tok[0, 0])
        pend = (l, grads, land, ss, rs)
    finish(pend[0], _push_wait(f"rs_wait_{pend[0]}", *pend[1:], dx, True))

    zone = _push_wait("small_wait", part, zone, sm_s, sm_r, dx, False, "all")
    gng = _sum_slots(_gather_all(_pad_to(jnp.concatenate(dng), O_VEC)), O_VEC)
    gsmall = jnp.concatenate([gng, _sum_slots(zone, PACK_RB)[O_VEC:]])
    gcw =lax.dynamic_slice_in_dim(gsmall[O_CW:O_CW + L * 3], me * q, q, axis=1)
    gpack = jnp.concatenate([gsmall[:O_CW], _pad_to(gcw.reshape(L * 3 * q // D, D), 16)])

    def pack(ng, lg, lb, sw, sb, pb_, ps, cwv, cb_, fg):
        return _pack_small(ng, jnp.stack([lg, lb, pb_, ps, cb_], axis=1), sb, fg, sw, cwv.reshape(L * 3 * q // D, D))

    wp = pack(norm_g, sgu_ln_g, sgu_ln_b, sgu_w, sgu_b, pool_b, pool_scale, conv_w, conv_b, final_g)
    mp = pack(m_norm_g, m_sgu_ln_g, m_sgu_ln_b, m_sgu_w, m_sgu_b, m_pool_b, m_pool_scale, m_conv_w, m_conv_b, m_final_g)
    vp = pack(v_norm_g, v_sgu_ln_g, v_sgu_ln_b, v_sgu_w, v_sgu_b, v_pool_b, v_pool_scale, v_conv_w, v_conv_b, v_final_g)
    rows = pl.BlockSpec((PACK_RB, D), lambda i: (i, 0))
    sm = _adamw([gpack], rows, wp, mp, vp, rows, _empty4(wp), (N_PACK // PACK_RB,))

    def unpack(a):
        vv = a[O_VEC:O_VEC + L * 5].reshape(L, 5, D)
        sb = a[O_SGB:O_SGB + L].reshape(L, GRP, BLK)
        fg = a[O_FG]
        sw = a[O_SGW:O_SGW + N_SGW].reshape(L, GRP, BLK, BLK)
        cwv = a[O_CW:O_CW + L * 3 * q // D].reshape(L, 3, q)
        return dict(norm_g=a[O_NG:O_NG + L], w_in=None, sgu_ln_g=vv[:, 0], sgu_ln_b=vv[:, 1], sgu_w=sw, sgu_b=sb, pool_w=None,
                    pool_b=vv[:, 2], pool_scale=vv[:, 3], conv_w=cwv, conv_b=vv[:, 4], w_branch_a=None,
                    w_branch_b=None, w_branch_c=None, w_out=None, final_g=fg)

    outs = [loss, dx[None]]
    for kind in range(4):
        d = unpack(sm[kind])
        d.update(w_in=g_w_in[kind], pool_w=g_pw[kind], w_branch_a=g_br[0][kind], w_branch_b=g_br[1][kind],
                 w_branch_c=g_br[2][kind], w_out=g_br[3][kind])
        outs.extend(d[n] for n in ("norm_g", "w_in", "sgu_ln_g", "sgu_ln_b", "sgu_w", "sgu_b", "pool_w", "pool_b",
                                   "pool_scale", "conv_w", "conv_b", "w_branch_a", "w_branch_b", "w_branch_c", "w_out",
                                   "final_g"))
    return tuple(outs)
```

```python
import functools

import jax
import jax.numpy as jnp
from jax import lax
from jax.experimental import pallas as pl
from jax.experimental.pallas import tpu as pltpu

F32 = jnp.float32
BF = jnp.bfloat16
MESH = pl.DeviceIdType.MESH

D = 1024
L = 4
NSH = 4
SHW = 3 * D
NIN = NSH * SHW
GRP = 8
BLK = 128
NPG = 4
PGW = D // NPG
POOL_WINDOWS = (2, 4, 8, 16)
HALO = 16
RMS_EPS = 1e-6
LN_EPS = 1e-5
ADAM_LR, ADAM_B1, ADAM_B2, ADAM_EPS, ADAM_WD, ADAM_STEP = 0.001, 0.9, 0.999, 1e-8, 0.01, 10

RC = 16
CW = 512
COLS = tuple(range(0, D, CW))
MIB = 1 << 20

PK_BR = SHW
PK_PW = PK_BR + D
PK_CW = PK_PW + NPG * (PGW // NSH) * PGW // D
PK_ROWS = PK_CW + 64
PK_HALF = PK_ROWS // 2

_C0 = 0.7978845608028654
_C1 = 0.044715


def _pcall(body, **kw):
    return pl.pallas_call(body, **kw)


def _params(sem, vmem_mib):
    return pltpu.CompilerParams(dimension_semantics=sem, vmem_limit_bytes=vmem_mib * MIB)


def _mm(a, b):
    return jnp.dot(a.astype(BF), b.astype(BF), preferred_element_type=F32)


def _mm_nt(a, b):
    return lax.dot_general(a.astype(BF), b.astype(BF), (((1,), (1,)), ((), ())), preferred_element_type=F32)


def _mm_tn(a, b):
    return lax.dot_general(a.astype(BF), b.astype(BF), (((0,), (0,)), ((), ())), preferred_element_type=F32)


def _gelu(x):
    return 0.5 * x * (1.0 + jnp.tanh(_C0 * x * (1.0 + _C1 * x * x)))


def _gelu_fg(x):
    x2 = x * x
    t = jnp.tanh(_C0 * x * (1.0 + _C1 * x2))
    f = 0.5 * x * (1.0 + t)
    g = 0.5 * (1.0 + t) + 0.5 * x * (1.0 - t * t) * (_C0 * (1.0 + 3.0 * _C1 * x2))
    return f, g


def _sigmoid(x):
    return 0.5 * jnp.tanh(0.5 * x) + 0.5


def _silu(x):
    return x * _sigmoid(x)


def _silu_fg(x):
    s = _sigmoid(x)
    f = x * s
    return f, s + (f - f * s)


def _rowsum(v):
    return jnp.sum(v, axis=1, keepdims=True)


def _chunks(n_rows, fn, unroll=2):
    def body(c, carry):
        fn(pl.multiple_of(c * RC, RC))
        return carry
    lax.fori_loop(0, n_rows // RC, body, 0, unroll=unroll)


def _trail(e, win):
    s, sh = e, 1
    while sh < win:
        s = s + pltpu.roll(s, sh, 0)
        sh *= 2
    return s


def _lead(e, win):
    n = e.shape[0]
    s, sh = e, 1
    while sh < win:
        s = s + pltpu.roll(s, n - sh, 0)
        sh *= 2
    return s


def _inv_count(pos0, win):
    pos = pos0 + lax.broadcasted_iota(jnp.int32, (RC, 1), 0)
    return 1.0 / jnp.minimum(pos + 1, win).astype(F32)


def _masked_sgu(sgw_ref, g):
    r = lax.broadcasted_iota(jnp.int32, (BLK, BLK), 0) // 64
    c = lax.broadcasted_iota(jnp.int32, (BLK, BLK), 1) // 64
    return jnp.where(c <= r, sgw_ref[g], 0.0)


class _AsF32:
    def __init__(self, ref, col0=0):
        self.ref, self.col0 = ref, col0

    def __getitem__(self, idx):
        if self.col0:
            rows, cols = idx
            idx = (rows, slice(cols.start + self.col0, cols.stop + self.col0))
        return self.ref[idx].astype(F32)


class _From:
    def __init__(self, ref, lead0):
        self.ref, self.lead0 = ref, lead0

    def __setitem__(self, idx, val):
        self.ref[(self.lead0 + idx[0],) + tuple(idx[1:])] = val


def _tile(t, want):
    return min(t, want)


def _fwd_in(x, ng, wl, lng, lnb, sgw, sgbT, pwb, pb, psc, cw, cb):
    T = x.shape[0]
    tm = _tile(T, 256)
    nT = T // tm

    def body(x_ref, xn_ref, ng_ref, w_hbm, lng_ref, lnb_ref, sgw_ref, sgbT_ref, pw_ref, pb_ref, psc_ref, cw_ref, cb_ref,
             proj_hbm, h_ref, y_ref, w_s, pja, pjb, pjc, pjd, qa, qb, qc, qd, h_s, hn_s, vn_s, mix_s, d_s, wm_s, extp_s,
             extc_s, wsem, psem):
        i = pl.program_id(0)
        pj = (pja, pjb, pjc, pjd)
        kept = (qa, qb, qc, qd)

        def w_copy(p):
            return pltpu.make_async_copy(w_hbm.at[p, pl.ds(0, SHW), :], w_s.at[p], wsem.at[p])

        def p_copy(p, tile=None):
            t0 = (i if tile is None else tile) * tm
            dst = proj_hbm.at[pl.ds(pl.multiple_of(t0, tm), tm), pl.ds(p * SHW, SHW)]
            return pltpu.make_async_copy(kept[p], dst, psem.at[p])

        def keep(p):
            kept[p][...] = pj[p][...].astype(BF)
            p_copy(p).start()

        def all_rows(fn):
            for c in range(tm // RC):
                fn(c * RC)

        def buffer_free(p):
            @pl.when(i == 0)
            def _():
                w_copy(p).wait()

            @pl.when(i > 0)
            def _():
                p_copy(p).wait()

        def project(p, blocks, h=None):
            for b in blocks:
                pj[p][:, b * D:(b + 1) * D] = jnp.dot((h_s if h is None else h)[...], w_s[p, b * D:(b + 1) * D, :],
                                                      preferred_element_type=F32)

        @pl.when(i == 0)
        def _():
            for p in range(NSH):
                w_copy(p).start()
            extp_s[0:HALO, :] = jnp.zeros((HALO, D), F32)
            extc_s[0:HALO, :] = jnp.zeros((HALO, D), F32)
            for g in range(GRP):
                wm_s[g] = _masked_sgu(sgw_ref, g).astype(BF)

        def norm_rows(src, dst, r0):
            rows = pl.ds(r0, RC)
            xs = [src[rows, c0:c0 + CW] for c0 in COLS]
            ms = sum(_rowsum(v * v) for v in xs) * (1.0 / D)
            r = lax.rsqrt(ms + RMS_EPS)
            for k, c0 in enumerate(COLS):
                dst[rows, c0:c0 + CW] = (xs[k] * r * ng_ref[:, c0:c0 + CW]).astype(BF)

        def gating(first, rest):
            proj_ref = pj[0]
            first()

            def f1(r0):
                rows = pl.ds(r0, RC)
                gv = [_gelu(proj_ref[rows, D + c0:D + c0 + CW]) for c0 in COLS]
                mu = sum(_rowsum(v) for v in gv) * (1.0 / D)
                dv = [v - mu for v in gv]
                var = sum(_rowsum(v * v) for v in dv) * (1.0 / D)
                rstd = lax.rsqrt(var + LN_EPS)
                for k, c0 in enumerate(COLS):
                    vn_s[rows, c0:c0 + CW] = (dv[k] * rstd * lng_ref[:, c0:c0 + CW] + lnb_ref[:, c0:c0 + CW]).astype(BF)
            all_rows(f1)
            for g in range(GRP):
                w = wm_s[g]
                bcol = sgbT_ref[:, g:g + 1]
                gc = slice(g * BLK, (g + 1) * BLK)
                for n in range(tm // BLK):
                    rr = slice(n * BLK, (n + 1) * BLK)
                    mix_s[rr, gc] = jnp.dot(w, vn_s[rr, gc], preferred_element_type=F32) + bcol
            rest()

            def f2(r0):
                rows = pl.ds(r0, RC)
                for c0 in COLS:
                    au = proj_ref[rows, c0:c0 + CW]
                    az = proj_ref[rows, 2 * D + c0:2 * D + c0 + CW]
                    y_ref[0, rows, c0:c0 + CW] = (_gelu(au) * mix_s[rows, c0:c0 + CW] * _silu(az)).astype(BF)
            all_rows(f2)

        def pooling(first, rest):
            proj_ref = pj[1]
            first()
            extp_s[HALO:HALO + tm, :] = proj_ref[:, 0:D]

            def f1(r0):
                rows = pl.ds(r0, RC)
                e = extp_s[pl.ds(r0, 2 * RC), :]
                for k, win in enumerate(POOL_WINDOWS):
                    gc = slice(k * PGW, (k + 1) * PGW)
                    s = _trail(e[:, gc], win)[RC:2 * RC]
                    d_s[rows, gc] = (s * _inv_count(i * tm + r0, win) - e[RC:2 * RC, gc]).astype(BF)
            all_rows(f1)
            extp_s[0:HALO, :] = extp_s[tm:tm + HALO, :]
            for k in range(NPG):
                gc = slice(k * PGW, (k + 1) * PGW)
                mix_s[:, gc] = jnp.dot(d_s[:, gc], pw_ref[k], preferred_element_type=F32)
            rest()

            def f2(r0):
                rows = pl.ds(r0, RC)
                for c0 in COLS:
                    cs = slice(c0, c0 + CW)
                    bz = proj_ref[rows, D + c0:D + c0 + CW]
                    y_ref[1, rows, cs] = ((mix_s[rows, cs] + pb_ref[:, cs]) * psc_ref[:, cs] * _silu(bz)).astype(BF)
            all_rows(f2)

        def convolution(first, rest):
            proj_ref = pj[2]
            first()
            all_rows(functools.partial(norm_rows, xn_ref, hn_s))

            def f1(r0):
                rows = pl.ds(r0, RC)
                extc_s[pl.ds(HALO + r0, RC), :] = proj_ref[rows, D:2 * D] * pj[1][rows, 2 * D:3 * D]
            all_rows(f1)
            rest()
            project(0, (0, 1, 2), hn_s)

            def f2(r0):
                rows = pl.ds(r0, RC)
                for c0 in COLS:
                    cs = slice(c0, c0 + CW)
                    e = extc_s[pl.ds(r0, 2 * RC), cs]
                    conv = (cw_ref[2:3, cs] * e + cw_ref[1:2, cs] * pltpu.roll(e, 1, 0)
                            + cw_ref[0:1, cs] * pltpu.roll(e, 2, 0))[RC:2 * RC] + cb_ref[:, cs]
                    cbv = proj_ref[rows, cs]
                    cz = proj_ref[rows, 2 * D + c0:2 * D + c0 + CW]
                    y_ref[2, rows, cs] = (cbv * conv * _silu(cz)).astype(BF)
            all_rows(f2)
            extc_s[0:HALO, :] = extc_s[tm:tm + HALO, :]

        @pl.when(i == 0)
        def _():
            all_rows(functools.partial(norm_rows, x_ref, h_s))
            w_copy(0).wait()
            project(0, (0, 1, 2))
            keep(0)

        @pl.when(i > 0)
        def _():
            h_s[...] = hn_s[...]
        h_ref[...] = h_s[...]

        for p, mixer, early in ((1, gating, 2), (2, pooling, 1), (3, convolution, 1)):
            buffer_free(p)
            if p == 3:
                p_copy(0).wait()
            mixer(functools.partial(project, p, range(early)), functools.partial(project, p, range(early, 3)))
            keep(p)
        kept[0][...] = pj[0][...].astype(BF)

        @pl.when(i < nT - 1)
        def _():
            p_copy(0, i + 1).start()

        @pl.when(i == nT - 1)
        def _():
            for p in range(1, NSH):
                p_copy(p).wait()

    vec = pl.BlockSpec((1, D), lambda i: (0, 0))
    return _pcall(
        body, name="fwd_in", grid=(nT,),
        in_specs=[
            pl.BlockSpec((tm, D), lambda i: (i, 0)), pl.BlockSpec((tm, D), lambda i: (jnp.minimum(i + 1, nT - 1), 0)),
            vec, ANY, vec, vec,
            pl.BlockSpec((GRP, BLK, BLK), lambda i: (0, 0, 0)),
            pl.BlockSpec((BLK, GRP), lambda i: (0, 0)),
            pl.BlockSpec((NPG, PGW, PGW), lambda i: (0, 0, 0)),
            vec, vec,
            pl.BlockSpec((3, D), lambda i: (0, 0)), vec,
        ],
        out_specs=[ANY, pl.BlockSpec((tm, D), lambda i: (i, 0)), pl.BlockSpec((3, tm, D), lambda i: (0, i, 0))],
        out_shape=[jax.ShapeDtypeStruct((T, NIN), BF), jax.ShapeDtypeStruct((T, D), BF),
                   jax.ShapeDtypeStruct((3, T, D), BF)],
        scratch_shapes=[
            pltpu.VMEM((NSH, SHW, D), BF),
            pltpu.VMEM((tm, SHW), F32), pltpu.VMEM((tm, SHW), F32), pltpu.VMEM((tm, SHW), F32), pltpu.VMEM((tm, SHW), F32),
            pltpu.VMEM((tm, SHW), BF), pltpu.VMEM((tm, SHW), BF), pltpu.VMEM((tm, SHW), BF), pltpu.VMEM((tm, SHW), BF),
            pltpu.VMEM((tm, D), BF), pltpu.VMEM((tm, D), BF), pltpu.VMEM((tm, D), BF), pltpu.VMEM((tm, D), F32),
            pltpu.VMEM((tm, D), BF), pltpu.VMEM((GRP, BLK, BLK), BF),
            pltpu.VMEM((tm + HALO, D), F32), pltpu.VMEM((tm + HALO, D), F32),
            pltpu.SemaphoreType.DMA((NSH,)), pltpu.SemaphoreType.DMA((NSH,)),
        ],
        compiler_params=_params(("arbitrary",), 60),
    )(x, x, ng, wl, lng, lnb, sgw, sgbT, pwb, pb, psc, cw, cb)


def _branch_spec(which):
    q = D // NSH
    return lambda *g: (0, PK_BR // q + (which(*g) if callable(which) else which), 0)


def _fwd_out(y3, proj, x, wl):
    T = x.shape[0]
    tm = _tile(T, 256)
    q = D // NSH

    nT = T // tm

    def body(y_ref, gl_ref, x_ref, wa_ref, wb_ref, wc_ref, wo_ref, p_ref, mg_ref, xo_ref, even_s, odd_s):
        i = pl.program_id(0)
        gl_ref = _AsF32(gl_ref)

        def branches(buf):
            for k, w_ref in enumerate((wa_ref, wb_ref, wc_ref)):
                buf[k] = jnp.dot(y_ref[k], w_ref[...].reshape(D, D), preferred_element_type=F32)

        def merge(buf):
            for c in range(tm // RC):
                rows = pl.ds(c * RC, RC)
                for c0 in COLS:
                    cs = slice(c0, c0 + CW)
                    m = 0.0
                    for k in range(3):
                        pk = buf[k, rows, cs]
                        p_ref[k, rows, cs] = pk.astype(BF)
                        m = m + _sigmoid(gl_ref[rows, k * D + c0:k * D + c0 + CW]) * pk
                    mg_ref[rows, cs] = m.astype(BF)
            xo_ref[...] = x_ref[...] + jnp.dot(mg_ref[...], wo_ref[...].reshape(D, D), preferred_element_type=F32)

        bufs = (even_s, odd_s)

        @pl.when(i == 0)
        def _():
            branches(bufs[0])

        for par in (0, 1):
            @pl.when((i % 2 == par) & (i > 0) & (i < nT))
            def _():
                branches(bufs[par])
                merge(bufs[1 - par])

        @pl.when(i == nT)
        def _():
            merge(bufs[(nT - 1) % 2])

    prev = lambda i: jnp.maximum(i - 1, 0)
    return _pcall(
        body, name="fwd_out", grid=(nT + 1,),
        in_specs=[
            pl.BlockSpec((3, tm, D), lambda i: (0, jnp.minimum(i, nT - 1), 0)),
            pl.BlockSpec((tm, SHW), lambda i: (prev(i), 3)),
            pl.BlockSpec((tm, D), lambda i: (prev(i), 0)),
        ] + [pl.BlockSpec((NSH, q, D), _branch_spec(k)) for k in range(4)],
        out_specs=[
            pl.BlockSpec((3, tm, D), lambda i: (0, prev(i), 0)),
            pl.BlockSpec((tm, D), lambda i: (prev(i), 0)),
            pl.BlockSpec((tm, D), lambda i: (prev(i), 0)),
        ],
        out_shape=[jax.ShapeDtypeStruct((3, T, D), BF), jax.ShapeDtypeStruct((T, D), BF),
                   jax.ShapeDtypeStruct((T, D), F32)],
        scratch_shapes=[pltpu.VMEM((3, tm, D), F32), pltpu.VMEM((3, tm, D), F32)],
        compiler_params=_params(("arbitrary",), 52),
    )(y3, proj, x, wl, wl, wl, wl)


def _loss_head(x, fg, tgt):
    T = x.shape[0]
    tm = _tile(T, 512)

    def body(x_ref, g_ref, t_ref, dx_ref, sq_ref, dg_ref, acc_s):
        i = pl.program_id(0)

        @pl.when(i == 0)
        def _():
            acc_s[...] = jnp.zeros_like(acc_s)

        def f(r0):
            rows = pl.ds(r0, RC)
            xs = [x_ref[rows, c0:c0 + CW] for c0 in COLS]
            r = lax.rsqrt(sum(_rowsum(v * v) for v in xs) * (1.0 / D) + RMS_EPS)
            xh = [v * r for v in xs]
            dyg, m = [], 0.0
            for k, c0 in enumerate(COLS):
                cs = slice(c0, c0 + CW)
                err = xh[k] * g_ref[:, cs] - t_ref[rows, cs]
                acc_s[0, :, cs] += err * err
                dy = err * (1.0 / D)
                acc_s[1, :, cs] += dy * xh[k]
                dyg.append(dy * g_ref[:, cs])
                m = m + _rowsum(dyg[k] * xh[k])
            m = m * (1.0 / D)
            for k, c0 in enumerate(COLS):
                dx_ref[rows, c0:c0 + CW] = r * (dyg[k] - xh[k] * m)
        _chunks(tm, f, unroll=4)

        @pl.when(i == pl.num_programs(0) - 1)
        def _():
            sq_ref[...] = jnp.sum(acc_s[0], axis=0, keepdims=True)
            dg_ref[...] = jnp.sum(acc_s[1], axis=0, keepdims=True)

    vec = pl.BlockSpec((1, D), lambda i: (0, 0))
    tile = pl.BlockSpec((tm, D), lambda i: (i, 0))
    return _pcall(
        body, name="loss_head", grid=(T // tm,),
        in_specs=[tile, vec, tile], out_specs=[tile, vec, vec],
        out_shape=[jax.ShapeDtypeStruct((T, D), F32), jax.ShapeDtypeStruct((1, D), F32), jax.ShapeDtypeStruct((1, D), F32)],
        scratch_shapes=[pltpu.VMEM((2, RC, D), F32)],
        compiler_params=_params(("arbitrary",), 32),
    )(x, fg, tgt)


def _bwd_mix(dxo, p3, proj, wl, lng, lnb, sgw, sgbT, pwb, pb, psc, cw, cb):
    T = dxo.shape[0]
    tm = _tile(T, 256)
    nT = T // tm
    hb = tm // HALO

    def body(dxo_ref, p_ref, pj2_ref, ch_ref, bpp_ref, ccp_ref, chp_ref, w_ref, w2_ref, w3_ref, lng_ref, lnb_ref, sgw_ref,
             sgbT_ref, pw_ref, pb_ref, psc_ref, cw_ref, cb_ref,
             dpj2_ref, dp_ref, dsw_ref, dsbT_ref, vec_ref, dpw_ref,
             dy_s, dy2_s, dy3_s, dyc_s, ext_s, nxt_s, halo_s, dch_s, xh_s, rstd_s, acc_s, accb_s,
             wm_s, wmT_s, *blk):
        i = pl.program_id(0)
        step = pl.program_id(1)
        ti = nT - 1 - i
        p_ref, ch_ref, bpp_ref, ccp_ref, chp_ref = (_AsF32(r) for r in (p_ref, ch_ref, bpp_ref, ccp_ref, chp_ref))

        @pl.when((i == 0) & (step == 0))
        def _():
            dsw_ref[...] = jnp.zeros_like(dsw_ref)
            dsbT_ref[...] = jnp.zeros_like(dsbT_ref)
            vec_ref[...] = jnp.zeros_like(vec_ref)
            dpw_ref[...] = jnp.zeros_like(dpw_ref)
            halo_s[...] = jnp.zeros_like(halo_s)
            for g in range(GRP):
                wm = _masked_sgu(sgw_ref, g)
                wm_s[g] = wm.astype(BF)
                wmT_s[g] = wm.T.astype(BF)

        def flush(n_acc, rows_of):
            for a in range(n_acc):
                vec_ref[rows_of[a]:rows_of[a] + 1, :] += jnp.sum(acc_s[a], axis=0, keepdims=True)

        def all_rows(fn):
            for c in range(tm // RC):
                fn(c * RC)

        @pl.when(step == 0)
        def _():
            pj_ref, dpj_ref = _AsF32(pj2_ref, SHW), _From(dpj2_ref, 3)
            dy_s[...] = _mm_nt(dxo_ref[...], w_ref[...].reshape(D, D))

            def gate(k, r0):
                rows = pl.ds(r0, RC)
                for c0 in COLS:
                    cs = slice(c0, c0 + CW)
                    dm = dy_s[rows, cs]
                    s = _sigmoid(pj_ref[rows, k * D + c0:k * D + c0 + CW])
                    dp_ref[k, rows, cs] = (s * dm).astype(BF)
                    dpj_ref[k, rows, cs] = (dm * p_ref[k, rows, cs] * s * (1.0 - s)).astype(BF)

            all_rows(functools.partial(gate, 2))
            dyc_s[...] = _mm_nt(dp_ref[2], w2_ref[...].reshape(D, D))
            all_rows(functools.partial(gate, 0))
            all_rows(functools.partial(gate, 1))

        @pl.when(step == 0)
        def _():
            pj_ref, dpj_ref = _AsF32(pj2_ref), _From(dpj2_ref, 0)
            dy3_s[...] = _mm_nt(dp_ref[1], w3_ref[...].reshape(D, D))
            acc_s[...] = jnp.zeros_like(acc_s)
            ext_s[0:HALO, :] = jnp.where(ti > 0, ccp_ref[...] * chp_ref[...], 0.0)
            nxt_s[tm:tm + HALO, :] = halo_s[0]

            def f1(r0):
                rows = pl.ds(r0, RC)
                ext_s[pl.ds(HALO + r0, RC), :] = pj_ref[rows, D:2 * D] * ch_ref[rows, :]
            all_rows(f1)

            def f2(r0):
                rows = pl.ds(r0, RC)
                for c0 in COLS:
                    cs = slice(c0, c0 + CW)
                    e = ext_s[pl.ds(r0, 2 * RC), cs]
                    e0, e1, e2 = e[RC:2 * RC], pltpu.roll(e, 1, 0)[RC:2 * RC], pltpu.roll(e, 2, 0)[RC:2 * RC]
                    conv = cw_ref[2:3, cs] * e0 + cw_ref[1:2, cs] * e1 + cw_ref[0:1, cs] * e2 + cb_ref[:, cs]
                    cbv = pj_ref[rows, cs]
                    sz, sg = _silu_fg(pj_ref[rows, 2 * D + c0:2 * D + c0 + CW])
                    dyc = dyc_s[rows, cs]
                    dconv = dyc * cbv * sz
                    dpj_ref[0, rows, cs] = (dyc * conv * sz).astype(BF)
                    dpj_ref[2, rows, cs] = (dyc * cbv * conv * sg).astype(BF)
                    nxt_s[rows, cs] = dconv
                    acc_s[0, :, cs] += dconv
                    acc_s[1, :, cs] += dconv * e2
                    acc_s[2, :, cs] += dconv * e1
                    acc_s[3, :, cs] += dconv * e0
            all_rows(f2)

            def f3(r0):
                rows = pl.ds(r0, RC)
                for c0 in COLS:
                    cs = slice(c0, c0 + CW)
                    e = nxt_s[pl.ds(r0, 2 * RC), cs]
                    dcc = (cw_ref[2:3, cs] * e + cw_ref[1:2, cs] * pltpu.roll(e, 2 * RC - 1, 0)
                           + cw_ref[0:1, cs] * pltpu.roll(e, 2 * RC - 2, 0))[0:RC]
                    dpj_ref[1, rows, cs] = (dcc * ch_ref[rows, cs]).astype(BF)
                    dch_s[rows, cs] = dcc * pj_ref[rows, D + c0:D + c0 + CW]
            all_rows(f3)
            halo_s[0] = nxt_s[0:HALO, :]
            flush(4, (4, 5, 6, 7))

        @pl.when(step == 1)
        def _():
            pj_ref, dpj_ref = _AsF32(pj2_ref, SHW), _From(dpj2_ref, 3)
            dy2_s[...] = _mm_nt(dp_ref[0], w2_ref[...].reshape(D, D))
            acc_s[...] = jnp.zeros_like(acc_s)
            ext_s[0:HALO, :] = jnp.where(ti > 0, bpp_ref[...], 0.0)
            ext_s[HALO:HALO + tm, :] = pj_ref[:, 0:D]
            nxt_s[tm:tm + HALO, :] = halo_s[1]

            nblk = tm // BLK
            d_b, t_b, dy0_b = blk[:nblk], blk[nblk:2 * nblk], blk[2 * nblk:]

            def block_rows(fn, n):
                for c in range(BLK // RC):
                    fn(n, c * RC)

            def f1(n, b0):
                r0, brow = n * BLK + b0, pl.ds(b0, RC)
                e = ext_s[pl.ds(r0, 2 * RC), :]
                for k, win in enumerate(POOL_WINDOWS):
                    gc = slice(k * PGW, (k + 1) * PGW)
                    s = _trail(e[:, gc], win)[RC:2 * RC]
                    d_b[n][brow, gc] = (s * _inv_count(ti * tm + r0, win) - e[RC:2 * RC, gc]).astype(BF)

            def pool(n):
                for k in range(NPG):
                    gc = slice(k * PGW, (k + 1) * PGW)
                    t_b[n][:, gc] = jnp.dot(d_b[n][:, gc], pw_ref[k], preferred_element_type=F32)

            def f2(n, b0):
                rows, brow = pl.ds(n * BLK + b0, RC), pl.ds(b0, RC)
                for c0 in COLS:
                    cs = slice(c0, c0 + CW)
                    y0 = t_b[n][brow, cs] + pb_ref[:, cs]
                    sz, sg = _silu_fg(pj_ref[rows, D + c0:D + c0 + CW])
                    dyb = dy3_s[rows, cs]
                    dy0 = dyb * psc_ref[:, cs] * sz
                    acc_s[0, :, cs] += dy0
                    acc_s[1, :, cs] += dyb * y0 * sz
                    dy0_b[n][brow, cs] = dy0.astype(BF)
                    dpj_ref[1, rows, cs] = (dyb * y0 * psc_ref[:, cs] * sg).astype(BF)

            def pool_back(n):
                for k in range(NPG):
                    gc = slice(k * PGW, (k + 1) * PGW)
                    dpw_ref[k] += _mm_tn(d_b[n][:, gc], dy0_b[n][:, gc])
                    t_b[n][:, gc] = _mm_nt(dy0_b[n][:, gc], pw_ref[k])

            def f3(n, b0):
                r0, brow = n * BLK + b0, pl.ds(b0, RC)
                for k, win in enumerate(POOL_WINDOWS):
                    gc = slice(k * PGW, (k + 1) * PGW)
                    nxt_s[pl.ds(r0, RC), gc] = t_b[n][brow, gc] * _inv_count(ti * tm + r0, win)

            def f4(n, b0):
                r0, brow = n * BLK + b0, pl.ds(b0, RC)
                rows = pl.ds(r0, RC)
                e = nxt_s[pl.ds(r0, 2 * RC), :]
                for k, win in enumerate(POOL_WINDOWS):
                    gc = slice(k * PGW, (k + 1) * PGW)
                    dpj_ref[0, rows, gc] = (_lead(e[:, gc], win)[0:RC] - t_b[n][brow, gc]).astype(BF)
                dpj_ref[2, rows, :] = dch_s[rows, :].astype(BF)

            def spread(n):
                block_rows(f3, n)
                block_rows(f4, n)

            stages = (functools.partial(block_rows, f1), pool, functools.partial(block_rows, f2), pool_back, spread)
            for t in range(nblk + len(stages) - 1):
                for s in reversed(range(len(stages))):
                    if 0 <= t - s < nblk:
                        stages[s](nblk - 1 - (t - s))
            halo_s[1] = nxt_s[0:HALO, :]
            flush(2, (2, 3))

        @pl.when(step == 1)
        def _():
            pj_ref, dpj_ref = _AsF32(pj2_ref), _From(dpj2_ref, 0)
            acc_s[...] = jnp.zeros_like(acc_s)
            accb_s[...] = jnp.zeros_like(accb_s)
            nblk = tm // BLK
            vn_b, t_b, dmix_b = blk[:nblk], blk[nblk:2 * nblk], blk[2 * nblk:]

            def block_rows(fn, n):
                for c in range(BLK // RC):
                    fn(n, c * RC)

            def f1(n, b0):
                rows, brow = pl.ds(n * BLK + b0, RC), pl.ds(b0, RC)
                gv = [_gelu(pj_ref[rows, D + c0:D + c0 + CW]) for c0 in COLS]
                mu = sum(_rowsum(v) for v in gv) * (1.0 / D)
                dv = [v - mu for v in gv]
                var = sum(_rowsum(v * v) for v in dv) * (1.0 / D)
                rstd = lax.rsqrt(var + LN_EPS)
                rstd_s[rows, :] = jnp.broadcast_to(rstd, (RC, BLK))
                for k, c0 in enumerate(COLS):
                    cs = slice(c0, c0 + CW)
                    xh = dv[k] * rstd
                    xh_s[rows, cs] = xh
                    vn_b[n][brow, cs] = (xh * lng_ref[:, cs] + lnb_ref[:, cs]).astype(BF)

            def mix(n):
                for g in range(GRP):
                    gc = slice(g * BLK, (g + 1) * BLK)
                    t_b[n][:, gc] = jnp.dot(wm_s[g], vn_b[n][:, gc], preferred_element_type=F32) + sgbT_ref[:, g:g + 1]

            def f2(n, b0):
                rows, brow = pl.ds(n * BLK + b0, RC), pl.ds(b0, RC)
                for c0 in COLS:
                    cs = slice(c0, c0 + CW)
                    gu, ggu = _gelu_fg(pj_ref[rows, cs])
                    sz, sg = _silu_fg(pj_ref[rows, 2 * D + c0:2 * D + c0 + CW])
                    dya = dy2_s[rows, cs]
                    mx = t_b[n][brow, cs]
                    dmix = dya * gu * sz
                    dpj_ref[0, rows, cs] = (dya * mx * sz * ggu).astype(BF)
                    dpj_ref[2, rows, cs] = (dya * gu * mx * sg).astype(BF)
                    dmix_b[n][brow, cs] = dmix.astype(BF)
                    accb_s[brow, cs] += dmix

            def mix_back(n):
                for g in range(GRP):
                    gc = slice(g * BLK, (g + 1) * BLK)
                    t_b[n][:, gc] = jnp.dot(wmT_s[g], dmix_b[n][:, gc], preferred_element_type=F32)
                    dsw_ref[g] += _mm_nt(dmix_b[n][:, gc], vn_b[n][:, gc])

            def f3(n, b0):
                rows, brow = pl.ds(n * BLK + b0, RC), pl.ds(b0, RC)
                rstd = rstd_s[rows, 0:1]
                dxh, m1, m2 = [], 0.0, 0.0
                for k, c0 in enumerate(COLS):
                    cs = slice(c0, c0 + CW)
                    dvn = t_b[n][brow, cs]
                    xh = xh_s[rows, cs]
                    acc_s[0, :, cs] += dvn * xh
                    acc_s[1, :, cs] += dvn
                    dxh.append(dvn * lng_ref[:, cs])
                    m1 = m1 + _rowsum(dxh[k])
                    m2 = m2 + _rowsum(dxh[k] * xh)
                m1 = m1 * (1.0 / D)
                m2 = m2 * (1.0 / D)
                for k, c0 in enumerate(COLS):
                    cs = slice(c0, c0 + CW)
                    _, ggv = _gelu_fg(pj_ref[rows, D + c0:D + c0 + CW])
                    dpj_ref[1, rows, cs] = (rstd * (dxh[k] - m1 - xh_s[rows, cs] * m2) * ggv).astype(BF)

            stages = (functools.partial(block_rows, f1), mix, functools.partial(block_rows, f2), mix_back,
                      functools.partial(block_rows, f3))
            for t in range(nblk + len(stages) - 1):
                for s in reversed(range(len(stages))):
                    if 0 <= t - s < nblk:
                        stages[s](t - s)
            for g in range(GRP):
                dsbT_ref[:, g:g + 1] += _rowsum(accb_s[:, g * BLK:(g + 1) * BLK])
            flush(2, (0, 1))

            @pl.when(i == nT - 1)
            def _():
                for g in range(GRP):
                    r = lax.broadcasted_iota(jnp.int32, (BLK, BLK), 0) // 64
                    c = lax.broadcasted_iota(jnp.int32, (BLK, BLK), 1) // 64
                    dsw_ref[g] = jnp.where(c <= r, dsw_ref[g], 0.0)

    def prev(col):
        return pl.BlockSpec((HALO, D), lambda i, p: (jnp.maximum((nT - 1 - i) * hb - 1, 0), col))

    vec = pl.BlockSpec((1, D), lambda i, p: (0, 0))
    const3 = lambda i, p: (0, 0, 0)
    return _pcall(
        body, name="bwd_mix", grid=(nT, 2),
        in_specs=[
            pl.BlockSpec((tm, D), lambda i, p: (nT - 1 - i, 0)),
            pl.BlockSpec((3, tm, D), lambda i, p: (0, nT - 1 - i, 0)),
            pl.BlockSpec((tm, 2 * SHW), lambda i, p: (nT - 1 - i, 1 - p)),
            pl.BlockSpec((tm, D), lambda i, p: (nT - 1 - i, 5)),
            prev(3), prev(7), prev(5),
            pl.BlockSpec((NSH, D // NSH, D), _branch_spec(3)),
            pl.BlockSpec((NSH, D // NSH, D), _branch_spec(lambda i, p: 2 - 2 * p)),
            pl.BlockSpec((NSH, D // NSH, D), _branch_spec(1)),
            vec, vec,
            pl.BlockSpec((GRP, BLK, BLK), const3),
            pl.BlockSpec((BLK, GRP), lambda i, p: (0, 0)),
            pl.BlockSpec((NPG, PGW, PGW), const3),
            vec, vec,
            pl.BlockSpec((3, D), lambda i, p: (0, 0)), vec,
        ],
        out_specs=[
            pl.BlockSpec((6, tm, D), lambda i, p: (1 - p, nT - 1 - i, 0)),
            pl.BlockSpec((3, tm, D), lambda i, p: (0, nT - 1 - i, 0)),
            pl.BlockSpec((GRP, BLK, BLK), const3),
            pl.BlockSpec((BLK, GRP), lambda i, p: (0, 0)),
            pl.BlockSpec((8, D), lambda i, p: (0, 0)),
            pl.BlockSpec((NPG, PGW, PGW), const3),
        ],
        out_shape=[
            jax.ShapeDtypeStruct((12, T, D), BF), jax.ShapeDtypeStruct((3, T, D), BF),
            jax.ShapeDtypeStruct((GRP, BLK, BLK), F32), jax.ShapeDtypeStruct((BLK, GRP), F32),
            jax.ShapeDtypeStruct((8, D), F32), jax.ShapeDtypeStruct((NPG, PGW, PGW), F32),
        ],
        scratch_shapes=[
            pltpu.VMEM((tm, D), F32),
            pltpu.VMEM((tm, D), F32),
            pltpu.VMEM((tm, D), F32),
            pltpu.VMEM((tm, D), F32),
            pltpu.VMEM((tm + HALO, D), F32),
            pltpu.VMEM((tm + HALO, D), F32),
            pltpu.VMEM((2, HALO, D), F32),
            pltpu.VMEM((tm, D), F32),
            pltpu.VMEM((tm, D), F32),
            pltpu.VMEM((tm, BLK), F32),
            pltpu.VMEM((4, RC, D), F32),
            pltpu.VMEM((BLK, D), F32),
            pltpu.VMEM((GRP, BLK, BLK), BF), pltpu.VMEM((GRP, BLK, BLK), BF),
        ] + [pltpu.VMEM((BLK, D), dt) for dt in (BF, F32, BF) for _ in range(tm // BLK)],
        compiler_params=_params(("arbitrary", "arbitrary"), 56),
    )(dxo, p3, proj, proj, proj, proj, proj, wl, wl, wl, lng, lnb, sgw, sgbT, pwb, pb, psc, cw, cb)


def _bwd_in(dproj, wl, x, dxo, ng):
    T = x.shape[0]
    tm = _tile(T, 256)
    nT = T // tm

    def body(dpj_ref, w_hbm, x_ref, dxo_ref, ng_ref, dx_ref, dng_ref, w_s, even_s, odd_s, g_s, wsem):
        i = pl.program_id(0)
        bufs = (even_s, odd_s)

        def w_copy(j):
            return pltpu.make_async_copy(w_hbm.at[j, pl.ds(0, SHW), :], w_s.at[j], wsem.at[j])

        def d_h():
            return sum(_mm_nt(dpj_ref[3 * j + b], w_s[j, b * D:(b + 1) * D, :]) for j in range(NSH) for b in range(3))

        def finish(prev):
            for c in range(tm // RC):
                finish_rows(prev, c * RC)

        def finish_rows(prev, r0):
            rows = pl.ds(r0, RC)
            xs = [x_ref[rows, c0:c0 + CW] for c0 in COLS]
            r = lax.rsqrt(sum(_rowsum(v * v) for v in xs) * (1.0 / D) + RMS_EPS)
            xh = [v * r for v in xs]
            dhg, m = [], 0.0
            for k, c0 in enumerate(COLS):
                cs = slice(c0, c0 + CW)
                dh = prev[rows, cs]
                g_s[:, cs] += dh * xh[k]
                dhg.append(dh * ng_ref[:, cs])
                m = m + _rowsum(dhg[k] * xh[k])
            m = m * (1.0 / D)
            for k, c0 in enumerate(COLS):
                cs = slice(c0, c0 + CW)
                dx_ref[rows, cs] = dxo_ref[rows, cs] + r * (dhg[k] - xh[k] * m)

        @pl.when(i == 0)
        def _():
            for j in range(NSH):
                w_copy(j).start()
            g_s[...] = jnp.zeros_like(g_s)
            for j in range(NSH):
                w_copy(j).wait()
            bufs[0][...] = d_h()

        for par in (0, 1):
            @pl.when((i % 2 == par) & (i > 0) & (i < nT))
            def _():
                bufs[par][...] = d_h()
                finish(bufs[1 - par])

        @pl.when(i == nT)
        def _():
            finish(bufs[(nT - 1) % 2])
            dng_ref[...] = jnp.sum(g_s[...], axis=0, keepdims=True)

    vec = pl.BlockSpec((1, D), lambda i: (0, 0))
    tile = pl.BlockSpec((tm, D), lambda i: (jnp.maximum(i - 1, 0), 0))
    return _pcall(
        body, name="bwd_in", grid=(nT + 1,),
        in_specs=[pl.BlockSpec((3 * NSH, tm, D), lambda i: (0, jnp.minimum(i, nT - 1), 0)), ANY, tile, tile, vec],
        out_specs=[tile, vec],
        out_shape=[jax.ShapeDtypeStruct((T, D), F32), jax.ShapeDtypeStruct((1, D), F32)],
        scratch_shapes=[pltpu.VMEM((NSH, SHW, D), BF), pltpu.VMEM((tm, D), F32), pltpu.VMEM((tm, D), F32),
                        pltpu.VMEM((RC, D), F32), pltpu.SemaphoreType.DMA((NSH,))],
        compiler_params=_params(("arbitrary",), 52),
    )(dproj, wl, x, dxo, ng)


def _tn_grad(a3, b3, split, out_map, name, into=None, after=None):
    nb, T, _ = b3.shape
    tk = _tile(T, 4096 if b3.dtype == BF else 2048)
    nk = T // tk
    rows = D // split
    a_batched = a3.shape[0] > 1

    def body(a_ref, b_ref, *rest):
        o_ref, acc_s = rest[-2:]
        k = pl.program_id(1)

        @pl.when(k == 0)
        def _():
            acc_s[...] = _mm_tn(a_ref[...], b_ref[...])

        @pl.when((k > 0) & (k < nk))
        def _():
            acc_s[...] += _mm_tn(a_ref[...], b_ref[...])

        @pl.when(k >= nk - 1)
        def _():
            r0 = pl.multiple_of((k - (nk - 1)) * rows, rows)
            o_ref[...] = acc_s[pl.ds(r0, rows), :].astype(o_ref.dtype)

    def tok(k):
        return jnp.minimum(k, nk - 1)

    extra = ([] if into is None else [into]) + ([] if after is None else [after])
    return _pcall(
        body, name=name, grid=(nb, nk + split - 1),
        in_specs=[
            pl.BlockSpec((None, tk, D), (lambda n, k: (n, tok(k), 0)) if a_batched else (lambda n, k: (0, tok(k), 0))),
            pl.BlockSpec((None, tk, D), lambda n, k: (n, tok(k), 0)),
        ] + [ANY] * len(extra),
        out_specs=pl.BlockSpec((None, rows, D), lambda n, k: out_map(n, jnp.maximum(k - (nk - 1), 0))),
        out_shape=jax.ShapeDtypeStruct((NSH, PK_ROWS, D), BF),
        input_output_aliases={} if into is None else {2: 0},
        scratch_shapes=[pltpu.VMEM((D, D), F32)],
        compiler_params=_params(("arbitrary", "arbitrary"), 56),
    )(a3, b3, *extra)


def _place():
    x, y, c = lax.axis_index("x"), lax.axis_index("y"), lax.axis_index("c")
    chips = [(1 - x, y), (x, 1 - y), (1 - x, 1 - y)]
    return x, y, c, chips


def _peers(reach):
    x, y, c, chips = _place()
    if reach == "chips":
        return 2 * x + y, [((px, py, c), 2 * px + py) for px, py in chips]
    others = [(x, y, 1 - c)] + [(px, py, pc) for px, py in chips for pc in (c, 1 - c)]
    return 4 * x + 2 * y + c, [(pr, 4 * pr[0] + 2 * pr[1] + pr[2]) for pr in others]


ANY = pl.BlockSpec(memory_space=pl.ANY)


HBM = pl.BlockSpec(memory_space=pltpu.HBM)
SEM = pl.BlockSpec(memory_space=pltpu.SEMAPHORE)
EFFECT = pltpu.SideEffectType.DATAFLOW_SIDE_EFFECTING


def _own_slot(src, from_slot, name):
    rows = src.shape[-2]
    rb = rows // 8
    me = (2 * lax.axis_index("x") + lax.axis_index("y")).astype(jnp.int32).reshape(1)

    def body(me_ref, src_ref, land_ref):
        land_ref[...] = src_ref[...]

    if from_slot:
        src_spec = pl.BlockSpec((None, rb, D), lambda i, me_ref: (me_ref[0], i, 0))
    else:
        src_spec = pl.BlockSpec((rb, D), lambda i, me_ref: (i, 0))
    return _pcall(
        body, name=name,
        grid_spec=pltpu.PrefetchScalarGridSpec(
            num_scalar_prefetch=1, grid=(rows // rb,), in_specs=[src_spec],
            out_specs=pl.BlockSpec((None, rb, D), lambda i, me_ref: (me_ref[0], i, 0))),
        out_shape=jax.ShapeDtypeStruct((NSH, rows, D), src.dtype),
        compiler_params=_params(("arbitrary",), 32),
    )(me, src)


def _my_half():
    return pl.ds(pl.multiple_of(lax.axis_index("c") * PK_HALF, PK_HALF), PK_HALF)


def _push_start(name, srcs, lands, per_peer, reach="chips", halved=()):
    n = len(srcs)
    npeer = 3 if reach == "chips" else 7
    ns = n * npeer

    def body(*refs):
        src, land = refs[:n], refs[n:2 * n]
        ssem, rsem = refs[2 * n:2 * n + ns], refs[2 * n + ns:2 * n + 2 * ns]
        token = refs[-1]
        me, peers = _peers(reach)
        for i in range(n):
            for j, (peer, slot) in enumerate(peers):
                s = src[i].at[slot] if per_peer else src[i]
                d = land[i].at[me]
                if i in halved:
                    s, d = s.at[_my_half()], d.at[_my_half()]
                pltpu.make_async_remote_copy(
                    src_ref=s, dst_ref=d, send_sem=ssem[npeer * i + j], recv_sem=rsem[npeer * i + j], device_id=peer,
                    device_id_type=MESH).start()
        token[...] = jnp.zeros_like(token)

    ops = list(srcs) + list(lands)
    out = _pcall(
        body, name=name,
        out_shape=tuple([pltpu.SemaphoreType.DMA(())] * (2 * ns) + [pltpu.HBM(a.shape, a.dtype) for a in ops]
                        + [jax.ShapeDtypeStruct((8, 128), F32)]),
        in_specs=[HBM] * (2 * n),
        out_specs=tuple([SEM] * (2 * ns) + [HBM] * (2 * n) + [pl.BlockSpec(memory_space=pltpu.VMEM)]),
        input_output_aliases={i: 2 * ns + i for i in range(2 * n)},
        compiler_params=pltpu.CompilerParams(has_side_effects=EFFECT),
    )(*[pltpu.with_memory_space_constraint(a, pltpu.HBM) for a in ops])
    return out[:ns], out[ns:2 * ns], out[2 * ns:2 * ns + n], out[2 * ns + n:2 * ns + 2 * n], out[-1]


def _push_wait(name, src, land, ssem, rsem, after, per_peer, reach="chips", halved=False):
    npeer = len(ssem)

    def body(src_ref, land_ref, *rest):
        sems = rest[:2 * npeer]
        _, peers = _peers(reach)
        for j, (peer, slot) in enumerate(peers):
            s = src_ref.at[slot] if per_peer else src_ref
            d = land_ref.at[slot]
            if halved:
                s, d = s.at[_my_half()], d.at[_my_half()]
            cp = pltpu.make_async_remote_copy(
                src_ref=s, dst_ref=d, send_sem=sems[j],
                recv_sem=sems[npeer + j], device_id=peer, device_id_type=MESH)
            cp.wait_send()
            cp.wait_recv()

    return _pcall(
        body, name=name,
        out_shape=(pltpu.HBM(src.shape, src.dtype), pltpu.HBM(land.shape, land.dtype)),
        in_specs=[HBM, HBM] + [SEM] * (2 * npeer) + [ANY], out_specs=(HBM, HBM),
        input_output_aliases={0: 0, 1: 1},
        compiler_params=pltpu.CompilerParams(has_side_effects=EFFECT),
    )(src, land, *ssem, *rsem, after)[1]


def _share_halves(land):
    def body(land_in, land_ref, ssem, rsem):
        x, y, c, chips = _place()
        other = pl.ds(pl.multiple_of((1 - c) * PK_HALF, PK_HALF), PK_HALF)
        sends = [pltpu.make_async_remote_copy(
            src_ref=land_ref.at[2 * px + py, _my_half()], dst_ref=land_ref.at[2 * px + py, _my_half()],
            send_sem=ssem.at[j], recv_sem=rsem.at[j], device_id=(x, y, 1 - c), device_id_type=MESH)
            for j, (px, py) in enumerate(chips)]
        for cp in sends:
            cp.start()
        for j, (px, py) in enumerate(chips):
            pltpu.make_async_remote_copy(
                src_ref=land_ref.at[2 * px + py, other], dst_ref=land_ref.at[2 * px + py, other], send_sem=ssem.at[j],
                recv_sem=rsem.at[j], device_id=(x, y, 1 - c), device_id_type=MESH).wait_recv()
        for cp in sends:
            cp.wait_send()

    return _pcall(
        body, name="share_halves", in_specs=[ANY], out_specs=ANY,
        out_shape=jax.ShapeDtypeStruct(land.shape, land.dtype), input_output_aliases={0: 0},
        scratch_shapes=[pltpu.SemaphoreType.DMA((3,)), pltpu.SemaphoreType.DMA((3,))],
    )(land)


def _swap_sibling(arrs):
    n = len(arrs)

    def body(*refs):
        src, dst = refs[:n], refs[n:2 * n]
        ssem, rsem = refs[2 * n:]
        x, y, c, _ = _place()
        cps = [pltpu.make_async_remote_copy(src_ref=src[a], dst_ref=dst[a], send_sem=ssem.at[a], recv_sem=rsem.at[a],
                                            device_id=(x, y, 1 - c), device_id_type=MESH) for a in range(n)]
        for cp in cps:
            cp.start()
        for cp in cps:
            cp.wait()

    return _pcall(
        body, name="swap_sibling",
        in_specs=[ANY] * n, out_specs=[ANY] * n,
        out_shape=[jax.ShapeDtypeStruct(a.shape, a.dtype) for a in arrs],
        scratch_shapes=[pltpu.SemaphoreType.DMA((n,)), pltpu.SemaphoreType.DMA((n,))],
    )(*arrs)


def _gather_all(v):
    def body(src, dst, ssem, rsem, lsem):
        x, y, c, _ = _place()
        me = 4 * x + 2 * y + c
        peers = [(x, y, 1 - c), (1 - x, y, c), (1 - x, y, 1 - c), (x, 1 - y, c), (x, 1 - y, 1 - c),
                 (1 - x, 1 - y, c), (1 - x, 1 - y, 1 - c)]
        local = pltpu.make_async_copy(src, dst.at[me], lsem)
        local.start()
        sends = [pltpu.make_async_remote_copy(src_ref=src, dst_ref=dst.at[me], send_sem=ssem.at[j], recv_sem=rsem.at[j],
                                              device_id=pr, device_id_type=MESH) for j, pr in enumerate(peers)]
        for cp in sends:
            cp.start()
        for j, (px, py, pc) in enumerate(peers):
            pltpu.make_async_remote_copy(src_ref=src, dst_ref=dst.at[4 * px + 2 * py + pc], send_sem=ssem.at[j],
                                         recv_sem=rsem.at[j], device_id=(px, py, pc), device_id_type=MESH).wait_recv()
        for cp in sends:
            cp.wait_send()
        local.wait()

    return _pcall(
        body, name="gather_all", in_specs=[ANY], out_specs=ANY,
        out_shape=jax.ShapeDtypeStruct((8,) + v.shape, v.dtype),
        scratch_shapes=[pltpu.SemaphoreType.DMA((7,)), pltpu.SemaphoreType.DMA((7,)), pltpu.SemaphoreType.DMA(())],
    )(v)


def _sum_slots(r, rb):
    S = r.shape[0]

    def body(r_ref, o_ref):
        acc = r_ref[0].astype(F32)
        for s in range(1, S):
            acc = acc + r_ref[s].astype(F32)
        o_ref[...] = acc

    if r.ndim == 3:
        _, R, C = r.shape
        grid, blk, imap = (R // rb,), (S, rb, C), (lambda i: (0, i, 0))
        oblk, omap = (rb, C), (lambda i: (i, 0))
    else:
        _, K, R, C = r.shape
        grid, blk, imap = (K,), (S, None, R, C), (lambda i: (0, i, 0, 0))
        oblk, omap = (None, R, C), (lambda i: (i, 0, 0))
    return _pcall(
        body, name="sum_slots", grid=grid, in_specs=[pl.BlockSpec(blk, imap)], out_specs=pl.BlockSpec(oblk, omap),
        out_shape=jax.ShapeDtypeStruct(r.shape[1:], F32), compiler_params=_params(("arbitrary",), 48),
    )(r)


def _adamw(gs, g_spec, w, m, v, p_spec, prev, grid):
    ng = len(gs)
    bc1 = 1.0 - ADAM_B1 ** ADAM_STEP
    bc2 = 1.0 - ADAM_B2 ** ADAM_STEP

    def body(*refs):
        g = refs[0][...]
        for a in range(1, ng):
            g = g + refs[a][...]
        w_ref, m_ref, v_ref = refs[ng:ng + 3]
        go, do, mo, vo = refs[ng + 3 + 4:]
        mn = ADAM_B1 * m_ref[...] + (1.0 - ADAM_B1) * g
        vn = ADAM_B2 * v_ref[...] + (1.0 - ADAM_B2) * (g * g)
        go[...] = g
        mo[...] = mn
        vo[...] = vn
        do[...] = -ADAM_LR * ((mn / bc1) / (jnp.sqrt(vn / bc2) + ADAM_EPS) + ADAM_WD * w_ref[...])

    out = jax.ShapeDtypeStruct(w.shape, F32)
    k0 = ng + 3
    return _pcall(
        body, name="adamw", grid=grid,
        in_specs=[g_spec] * ng + [p_spec] * 3 + [ANY] * 4,
        out_specs=[p_spec] * 4, out_shape=[out] * 4,
        input_output_aliases={k0: 0, k0 + 1: 1, k0 + 2: 2, k0 + 3: 3},
        compiler_params=_params(("arbitrary",) * len(grid), 48),
    )(*gs, w, m, v, *prev)


def _empty4(w):
    return tuple(lax.empty(w.shape, F32) for _ in range(4))


N_SGW = L * GRP * BLK * BLK // D
O_NG, O_VEC, O_SGB, O_FG, O_SGW = 0, 8, 32, 40, 48
O_CW = O_SGW + N_SGW
N_PACK = O_CW + 16
PACK_RB = N_PACK // 3


def _pad_to(a, rows):
    return jnp.pad(a, ((0, rows - a.shape[0]), (0, 0)))


def _pack_small(ng, vecs, sgb, fg, sgw, cw):
    parts = [_pad_to(ng, 8), _pad_to(vecs.reshape(L * 5, D), 24), _pad_to(sgb.reshape(L, D), 8),
             _pad_to(fg.reshape(1, D), 8), sgw.reshape(N_SGW, D), _pad_to(cw, 16)]
    return jnp.concatenate(parts, axis=0)


def kernel(x, norm_g, w_in, sgu_ln_g, sgu_ln_b, sgu_w, sgu_b, pool_w, pool_b, pool_scale, conv_w, conv_b, w_branch_a, w_branch_b, w_branch_c, w_out, final_g, loss_target, m_norm_g, m_w_in, m_sgu_ln_g, m_sgu_ln_b, m_sgu_w, m_sgu_b, m_pool_w, m_pool_b, m_pool_scale, m_conv_w, m_conv_b, m_w_branch_a, m_w_branch_b, m_w_branch_c, m_w_out, m_final_g, v_norm_g, v_w_in, v_sgu_ln_g, v_sgu_ln_b, v_sgu_w, v_sgu_b, v_pool_w, v_pool_b, v_pool_scale, v_conv_w, v_conv_b, v_w_branch_a, v_w_branch_b, v_w_branch_c, v_w_out, v_final_g):
    cx, cy = lax.axis_index("x"), lax.axis_index("y")
    me = 2 * cx + cy
    xl, tgt = x[0], loss_target[0]
    q = D // NSH

    wq = w_in.astype(BF).reshape(L, D, 3, D).transpose(0, 2, 1, 3).reshape(L, SHW, D)
    brq = jnp.stack([w_branch_a, w_branch_b, w_branch_c, w_out], axis=1).astype(BF).reshape(L, D, D)
    pwq = pool_w.astype(BF).reshape(L, PK_CW - PK_PW, D)
    cwq = lax.bitcast_convert_type(conv_w, BF).reshape(L, 3 * q * 2)
    cwq = jnp.pad(cwq, ((0, 0), (0, (PK_ROWS - PK_CW) * D - 3 * q * 2))).reshape(L, PK_ROWS - PK_CW, D)
    packs = [jnp.concatenate([wq[l], brq[l], pwq[l], cwq[l]], axis=0) for l in range(L)]
    lands = [_own_slot(packs[l], False, f"ag_own_{l}") for l in range(L)]
    ag_s, ag_r, packs, lands, tok = _push_start("ag_start", packs, lands, False, halved=(0,))
    sgbT = sgu_b.transpose(0, 2, 1)

    def layer_weights(l, after):
        wl = _push_wait(f"ag_wait_{l}", packs[l], lands[l], ag_s[3 * l:3 * l + 3], ag_r[3 * l:3 * l + 3], after, False,
                        halved=l == 0)
        if l == 0:
            wl = _share_halves(wl)
        pwb = wl[:, PK_PW:PK_CW].reshape(NSH, NPG, PGW // NSH, PGW).transpose(1, 0, 2, 3).reshape(NPG, PGW, PGW)
        cwb = wl[:, PK_CW:].reshape(NSH, (PK_ROWS - PK_CW) * D)[:, :3 * q * 2].reshape(NSH, 3, q, 2)
        cwf = lax.bitcast_convert_type(cwb, F32).transpose(1, 0, 2).reshape(3, D)
        small = (sgu_ln_g[l:l + 1], sgu_ln_b[l:l + 1], sgu_w[l], sgbT[l], pwb, pool_b[l:l + 1], pool_scale[l:l + 1],
                 cwf, conv_b[l:l + 1])
        return wl, small

    xs, saved, wts = [xl], [], []
    for l in range(L):
        wl, small = layer_weights(l, tok if l == 0 else xs[l])
        wts.append((wl, small))
        proj, h, y3 = _fwd_in(xs[l], norm_g[l:l + 1], wl, *small)
        p3, mg, xo = _fwd_out(y3, proj, xs[l], wl)
        saved.append((proj, h, y3, p3, mg))
        xs.append(xo)

    dx, sq, dfg = _loss_head(xs[L], final_g[None], tgt)
    loss = lax.psum(jnp.sum(sq) * (0.5 / D), ("x", "y", "c"))

    g_w_in = _empty4(w_in)
    g_br = [_empty4(w_out) for _ in range(4)]
    g_pw = _empty4(pool_w)
    dng, dvec, dsgw, dsgb = [None] * L, [None] * L, [None] * L, [None] * L
    branches = [(w_branch_a, m_w_branch_a, v_w_branch_a), (w_branch_b, m_w_branch_b, v_w_branch_b),
                (w_branch_c, m_w_branch_c, v_w_branch_c), (w_out, m_w_out, v_w_out)]
    nb = D // 128

    def finish(l, landed):
        nonlocal g_w_in, g_pw
        mine = _sum_slots(landed, PK_ROWS // 8)
        sums = [mine, _swap_sibling([mine])[0]]
        g_w_in = _adamw(sums, pl.BlockSpec((128, D), lambda b, i: (b * nb + i, 0)), w_in, m_w_in, v_w_in,
                        pl.BlockSpec((None, 128, D), lambda b, i: (l, i, b)), g_w_in, (3, nb))
        for k, (w, m, v) in enumerate(branches):
            g_br[k] = _adamw(sums, pl.BlockSpec((q, D), lambda i, k=k: (PK_BR // q + k, 0)), w, m, v,
                             pl.BlockSpec((None, q, D), lambda i: (l, 0, 0)), g_br[k], (1,))
        pools = [a[PK_PW:PK_CW].reshape(NPG, PGW // NSH, PGW) for a in sums]
        g_pw = _adamw(pools, pl.BlockSpec((None, PGW // NSH, PGW), lambda g: (g, 0, 0)), pool_w, m_pool_w, v_pool_w,
                      pl.BlockSpec((None, None, PGW // NSH, PGW), lambda g: (l, g, 0, 0)), g_pw, (NPG,))

    pend = None
    for l in reversed(range(L)):
        proj, h, y3, p3, mg = saved[l]
        wl, small = wts[l]
        dproj, dp3, dsgw[l], dsbT, dvec[l], dpw = _bwd_mix(dx, p3, proj, wl, *small)
        dsgb[l] = dsbT.T
        if l == 0:
            dv = jnp.stack(dvec)
            part = _pack_small(jnp.zeros((L, D), F32), dv[:, 0:5], jnp.stack(dsgb), dfg[0], jnp.stack(dsgw),
                               dv[:, 5:8].reshape(L * 3, D))
            zone = lax.dynamic_update_slice(lax.empty((8, N_PACK, D), F32), part[None],
                                            (2 * me + lax.axis_index("c"), 0, 0))
            sm_s, sm_r, (part,), (zone,), started = _push_start("small_start", [part], [zone], False, "all")
        grads = _tn_grad(h[None], dproj, 1, lambda n, s: (n // 3, n % 3, 0), "grad_w_in", after=started if l == 0 else None)
        grads = _tn_grad(y3, dp3, NSH, lambda n, s: (s, PK_BR // q + n, 0), "grad_w_branch", into=grads)
        grads = _tn_grad(mg[None], dx[None], NSH, lambda n, s: (s, PK_BR // q + 3, 0), "grad_w_out", into=grads)
        dpq = dpw.astype(BF).reshape(NPG, NSH, PGW // NSH, PGW).transpose(1, 0, 2, 3).reshape(NSH, PK_CW - PK_PW, D)
        grads = lax.dynamic_update_slice(grads, jnp.pad(dpq, ((0, 0), (0, PK_ROWS - PK_CW), (0, 0))), (0, PK_PW, 0))
        if pend is not None:
            landed = _push_wait(f"rs_wait_{pend[0]}", *pend[1:], dproj, True)
        land = _own_slot(grads, True, f"rs_own_{l}")
        ss, rs, (grads,), (land,), tok = _push_start(f"rs_start_{l}", [grads], [land], True)
        if pend is not None:
            finish(pend[0], landed)
        dx, dng[l] = _bwd_in(dproj, wl, xs[l], dx, norm_g[l:l + 1] + tok[0, 0])
        pend = (l, grads, land, ss, rs)
    finish(pend[0], _push_wait(f"rs_wait_{pend[0]}", *pend[1:], dx, True))

    zone = _push_wait("small_wait", part, zone, sm_s, sm_r, dx, False, "all")
    gng = _sum_slots(_gather_all(_pad_to(jnp.concatenate(dng), O_VEC)), O_VEC)
    gsmall = jnp.concatenate([gng, _sum_slots(zone, PACK_RB)[O_VEC:]])
    gcw =lax.dynamic_slice_in_dim(gsmall[O_CW:O_CW + L * 3], me * q, q, axis=1)
    gpack = jnp.concatenate([gsmall[:O_CW], _pad_to(gcw.reshape(L * 3 * q // D, D), 16)])

    def pack(ng, lg, lb, sw, sb, pb_, ps, cwv, cb_, fg):
        return _pack_small(ng, jnp.stack([lg, lb, pb_, ps, cb_], axis=1), sb, fg, sw, cwv.reshape(L * 3 * q // D, D))

    wp = pack(norm_g, sgu_ln_g, sgu_ln_b, sgu_w, sgu_b, pool_b, pool_scale, conv_w, conv_b, final_g)
    mp = pack(m_norm_g, m_sgu_ln_g, m_sgu_ln_b, m_sgu_w, m_sgu_b, m_pool_b, m_pool_scale, m_conv_w, m_conv_b, m_final_g)
    vp = pack(v_norm_g, v_sgu_ln_g, v_sgu_ln_b, v_sgu_w, v_sgu_b, v_pool_b, v_pool_scale, v_conv_w, v_conv_b, v_final_g)
    rows = pl.BlockSpec((PACK_RB, D), lambda i: (i, 0))
    sm = _adamw([gpack], rows, wp, mp, vp, rows, _empty4(wp), (N_PACK // PACK_RB,))

    def unpack(a):
        vv = a[O_VEC:O_VEC + L * 5].reshape(L, 5, D)
        sb = a[O_SGB:O_SGB + L].reshape(L, GRP, BLK)
        fg = a[O_FG]
        sw = a[O_SGW:O_SGW + N_SGW].reshape(L, GRP, BLK, BLK)
        cwv = a[O_CW:O_CW + L * 3 * q // D].reshape(L, 3, q)
        return dict(norm_g=a[O_NG:O_NG + L], w_in=None, sgu_ln_g=vv[:, 0], sgu_ln_b=vv[:, 1], sgu_w=sw, sgu_b=sb, pool_w=None,
                    pool_b=vv[:, 2], pool_scale=vv[:, 3], conv_w=cwv, conv_b=vv[:, 4], w_branch_a=None,
                    w_branch_b=None, w_branch_c=None, w_out=None, final_g=fg)

    outs = [loss, dx[None]]
    for kind in range(4):
        d = unpack(sm[kind])
        d.update(w_in=g_w_in[kind], pool_w=g_pw[kind], w_branch_a=g_br[0][kind], w_branch_b=g_br[1][kind],
                 w_branch_c=g_br[2][kind], w_out=g_br[3][kind])
        outs.extend(d[n] for n in ("norm_g", "w_in", "sgu_ln_g", "sgu_ln_b", "sgu_w", "sgu_b", "pool_w", "pool_b",
                                   "pool_scale", "conv_w", "conv_b", "w_branch_a", "w_branch_b", "w_branch_c", "w_out",
                                   "final_g"))
    return tuple(outs)
```

```python
import functools

import jax
import jax.numpy as jnp
from jax import lax
from jax.experimental import pallas as pl
from jax.experimental.pallas import tpu as pltpu

F32 = jnp.float32
BF = jnp.bfloat16
MESH = pl.DeviceIdType.MESH

D = 1024
L = 4
NSH = 4
SHW = 3 * D
NIN = NSH * SHW
GRP = 8
BLK = 128
NPG = 4
PGW = D // NPG
POOL_WINDOWS = (2, 4, 8, 16)
HALO = 16
RMS_EPS = 1e-6
LN_EPS = 1e-5
ADAM_LR, ADAM_B1, ADAM_B2, ADAM_EPS, ADAM_WD, ADAM_STEP = 0.001, 0.9, 0.999, 1e-8, 0.01, 10

RC = 16
CW = 512
COLS = tuple(range(0, D, CW))
MIB = 1 << 20

PK_BR = SHW
PK_PW = PK_BR + D
PK_CW = PK_PW + NPG * (PGW // NSH) * PGW // D
PK_ROWS = PK_CW + 64
PK_HALF = PK_ROWS // 2

_C0 = 0.7978845608028654
_C1 = 0.044715


def _pcall(body, **kw):
    return pl.pallas_call(body, **kw)


def _params(sem, vmem_mib):
    return pltpu.CompilerParams(dimension_semantics=sem, vmem_limit_bytes=vmem_mib * MIB)


def _mm(a, b):
    return jnp.dot(a.astype(BF), b.astype(BF), preferred_element_type=F32)


def _mm_nt(a, b):
    return lax.dot_general(a.astype(BF), b.astype(BF), (((1,), (1,)), ((), ())), preferred_element_type=F32)


def _mm_tn(a, b):
    return lax.dot_general(a.astype(BF), b.astype(BF), (((0,), (0,)), ((), ())), preferred_element_type=F32)


def _gelu(x):
    return 0.5 * x * (1.0 + jnp.tanh(_C0 * x * (1.0 + _C1 * x * x)))


def _gelu_fg(x):
    x2 = x * x
    t = jnp.tanh(_C0 * x * (1.0 + _C1 * x2))
    f = 0.5 * x * (1.0 + t)
    g = 0.5 * (1.0 + t) + 0.5 * x * (1.0 - t * t) * (_C0 * (1.0 + 3.0 * _C1 * x2))
    return f, g


def _sigmoid(x):
    return 0.5 * jnp.tanh(0.5 * x) + 0.5


def _silu(x):
    return x * _sigmoid(x)


def _silu_fg(x):
    s = _sigmoid(x)
    f = x * s
    return f, s + (f - f * s)


def _rowsum(v):
    return jnp.sum(v, axis=1, keepdims=True)


def _chunks(n_rows, fn, unroll=2):
    def body(c, carry):
        fn(pl.multiple_of(c * RC, RC))
        return carry
    lax.fori_loop(0, n_rows // RC, body, 0, unroll=unroll)


def _trail(e, win):
    s, sh = e, 1
    while sh < win:
        s = s + pltpu.roll(s, sh, 0)
        sh *= 2
    return s


def _lead(e, win):
    n = e.shape[0]
    s, sh = e, 1
    while sh < win:
        s = s + pltpu.roll(s, n - sh, 0)
        sh *= 2
    return s


def _inv_count(pos0, win):
    pos = pos0 + lax.broadcasted_iota(jnp.int32, (RC, 1), 0)
    return 1.0 / jnp.minimum(pos + 1, win).astype(F32)


def _masked_sgu(sgw_ref, g):
    r = lax.broadcasted_iota(jnp.int32, (BLK, BLK), 0) // 64
    c = lax.broadcasted_iota(jnp.int32, (BLK, BLK), 1) // 64
    return jnp.where(c <= r, sgw_ref[g], 0.0)


class _AsF32:
    def __init__(self, ref, col0=0):
        self.ref, self.col0 = ref, col0

    def __getitem__(self, idx):
        if self.col0:
            rows, cols = idx
            idx = (rows, slice(cols.start + self.col0, cols.stop + self.col0))
        return self.ref[idx].astype(F32)


class _From:
    def __init__(self, ref, lead0):
        self.ref, self.lead0 = ref, lead0

    def __setitem__(self, idx, val):
        self.ref[(self.lead0 + idx[0],) + tuple(idx[1:])] = val


def _tile(t, want):
    return min(t, want)


def _fwd_in(x, ng, wl, lng, lnb, sgw, sgbT, pwb, pb, psc, cw, cb):
    T = x.shape[0]
    tm = _tile(T, 256)
    nT = T // tm

    def body(x_ref, xn_ref, ng_ref, w_hbm, lng_ref, lnb_ref, sgw_ref, sgbT_ref, pw_ref, pb_ref, psc_ref, cw_ref, cb_ref,
             proj_hbm, h_ref, y_ref, w_s, pja, pjb, pjc, pjd, qa, qb, qc, qd, h_s, hn_s, vn_s, mix_s, d_s, wm_s, extp_s,
             extc_s, wsem, psem):
        i = pl.program_id(0)
        pj = (pja, pjb, pjc, pjd)
        kept = (qa, qb, qc, qd)

        def w_copy(p):
            return pltpu.make_async_copy(w_hbm.at[p, pl.ds(0, SHW), :], w_s.at[p], wsem.at[p])

        def p_copy(p, tile=None):
            t0 = (i if tile is None else tile) * tm
            dst = proj_hbm.at[pl.ds(pl.multiple_of(t0, tm), tm), pl.ds(p * SHW, SHW)]
            return pltpu.make_async_copy(kept[p], dst, psem.at[p])

        def keep(p):
            kept[p][...] = pj[p][...].astype(BF)
            p_copy(p).start()

        def all_rows(fn):
            for c in range(tm // RC):
                fn(c * RC)

        def buffer_free(p):
            @pl.when(i == 0)
            def _():
                w_copy(p).wait()

            @pl.when(i > 0)
            def _():
                p_copy(p).wait()

        def project(p, blocks, h=None):
            for b in blocks:
                pj[p][:, b * D:(b + 1) * D] = jnp.dot((h_s if h is None else h)[...], w_s[p, b * D:(b + 1) * D, :],
                                                      preferred_element_type=F32)

        @pl.when(i == 0)
        def _():
            for p in range(NSH):
                w_copy(p).start()
            extp_s[0:HALO, :] = jnp.zeros((HALO, D), F32)
            extc_s[0:HALO, :] = jnp.zeros((HALO, D), F32)
            for g in range(GRP):
                wm_s[g] = _masked_sgu(sgw_ref, g).astype(BF)

        def norm_rows(src, dst, r0):
            rows = pl.ds(r0, RC)
            xs = [src[rows, c0:c0 + CW] for c0 in COLS]
            ms = sum(_rowsum(v * v) for v in xs) * (1.0 / D)
            r = lax.rsqrt(ms + RMS_EPS)
            for k, c0 in enumerate(COLS):
                dst[rows, c0:c0 + CW] = (xs[k] * r * ng_ref[:, c0:c0 + CW]).astype(BF)

        def gating(first, rest):
            proj_ref = pj[0]
            first()

            def f1(r0):
                rows = pl.ds(r0, RC)
                gv = [_gelu(proj_ref[rows, D + c0:D + c0 + CW]) for c0 in COLS]
                mu = sum(_rowsum(v) for v in gv) * (1.0 / D)
                dv = [v - mu for v in gv]
                var = sum(_rowsum(v * v) for v in dv) * (1.0 / D)
                rstd = lax.rsqrt(var + LN_EPS)
                for k, c0 in enumerate(COLS):
                    vn_s[rows, c0:c0 + CW] = (dv[k] * rstd * lng_ref[:, c0:c0 + CW] + lnb_ref[:, c0:c0 + CW]).astype(BF)
            all_rows(f1)
            for g in range(GRP):
                w = wm_s[g]
                bcol = sgbT_ref[:, g:g + 1]
                gc = slice(g * BLK, (g + 1) * BLK)
                for n in range(tm // BLK):
                    rr = slice(n * BLK, (n + 1) * BLK)
                    mix_s[rr, gc] = jnp.dot(w, vn_s[rr, gc], preferred_element_type=F32) + bcol
            rest()

            def f2(r0):
                rows = pl.ds(r0, RC)
                for c0 in COLS:
                    au = proj_ref[rows, c0:c0 + CW]
                    az = proj_ref[rows, 2 * D + c0:2 * D + c0 + CW]
                    y_ref[0, rows, c0:c0 + CW] = (_gelu(au) * mix_s[rows, c0:c0 + CW] * _silu(az)).astype(BF)
            all_rows(f2)

        def pooling(first, rest):
            proj_ref = pj[1]
            first()
            extp_s[HALO:HALO + tm, :] = proj_ref[:, 0:D]

            def f1(r0):
                rows = pl.ds(r0, RC)
                e = extp_s[pl.ds(r0, 2 * RC), :]
                for k, win in enumerate(POOL_WINDOWS):
                    gc = slice(k * PGW, (k + 1) * PGW)
                    s = _trail(e[:, gc], win)[RC:2 * RC]
                    d_s[rows, gc] = (s * _inv_count(i * tm + r0, win) - e[RC:2 * RC, gc]).astype(BF)
            all_rows(f1)
            extp_s[0:HALO, :] = extp_s[tm:tm + HALO, :]
            for k in range(NPG):
                gc = slice(k * PGW, (k + 1) * PGW)
                mix_s[:, gc] = jnp.dot(d_s[:, gc], pw_ref[k], preferred_element_type=F32)
            rest()

            def f2(r0):
                rows = pl.ds(r0, RC)
                for c0 in COLS:
                    cs = slice(c0, c0 + CW)
                    bz = proj_ref[rows, D + c0:D + c0 + CW]
                    y_ref[1, rows, cs] = ((mix_s[rows, cs] + pb_ref[:, cs]) * psc_ref[:, cs] * _silu(bz)).astype(BF)
            all_rows(f2)

        def convolution(first, rest):
            proj_ref = pj[2]
            first()
            all_rows(functools.partial(norm_rows, xn_ref, hn_s))

            def f1(r0):
                rows = pl.ds(r0, RC)
                extc_s[pl.ds(HALO + r0, RC), :] = proj_ref[rows, D:2 * D] * pj[1][rows, 2 * D:3 * D]
            all_rows(f1)
            rest()
            project(0, (0, 1, 2), hn_s)

            def f2(r0):
                rows = pl.ds(r0, RC)
                for c0 in COLS:
                    cs = slice(c0, c0 + CW)
                    e = extc_s[pl.ds(r0, 2 * RC), cs]
                    conv = (cw_ref[2:3, cs] * e + cw_ref[1:2, cs] * pltpu.roll(e, 1, 0)
                            + cw_ref[0:1, cs] * pltpu.roll(e, 2, 0))[RC:2 * RC] + cb_ref[:, cs]
                    cbv = proj_ref[rows, cs]
                    cz = proj_ref[rows, 2 * D + c0:2 * D + c0 + CW]
                    y_ref[2, rows, cs] = (cbv * conv * _silu(cz)).astype(BF)
            all_rows(f2)
            extc_s[0:HALO, :] = extc_s[tm:tm + HALO, :]

        @pl.when(i == 0)
        def _():
            all_rows(functools.partial(norm_rows, x_ref, h_s))
            w_copy(0).wait()
            project(0, (0, 1, 2))
            keep(0)

        @pl.when(i > 0)
        def _():
            h_s[...] = hn_s[...]
        h_ref[...] = h_s[...]

        for p, mixer, early in ((1, gating, 2), (2, pooling, 1), (3, convolution, 1)):
            buffer_free(p)
            if p == 3:
                p_copy(0).wait()
            mixer(functools.partial(project, p, range(early)), functools.partial(project, p, range(early, 3)))
            keep(p)
        kept[0][...] = pj[0][...].astype(BF)

        @pl.when(i < nT - 1)
        def _():
            p_copy(0, i + 1).start()

        @pl.when(i == nT - 1)
        def _():
            for p in range(1, NSH):
                p_copy(p).wait()

    vec = pl.BlockSpec((1, D), lambda i: (0, 0))
    return _pcall(
        body, name="fwd_in", grid=(nT,),
        in_specs=[
            pl.BlockSpec((tm, D), lambda i: (i, 0)), pl.BlockSpec((tm, D), lambda i: (jnp.minimum(i + 1, nT - 1), 0)),
            vec, ANY, vec, vec,
            pl.BlockSpec((GRP, BLK, BLK), lambda i: (0, 0, 0)),
            pl.BlockSpec((BLK, GRP), lambda i: (0, 0)),
            pl.BlockSpec((NPG, PGW, PGW), lambda i: (0, 0, 0)),
            vec, vec,
            pl.BlockSpec((3, D), lambda i: (0, 0)), vec,
        ],
        out_specs=[ANY, pl.BlockSpec((tm, D), lambda i: (i, 0)), pl.BlockSpec((3, tm, D), lambda i: (0, i, 0))],
        out_shape=[jax.ShapeDtypeStruct((T, NIN), BF), jax.ShapeDtypeStruct((T, D), BF),
                   jax.ShapeDtypeStruct((3, T, D), BF)],
        scratch_shapes=[
            pltpu.VMEM((NSH, SHW, D), BF),
            pltpu.VMEM((tm, SHW), F32), pltpu.VMEM((tm, SHW), F32), pltpu.VMEM((tm, SHW), F32), pltpu.VMEM((tm, SHW), F32),
            pltpu.VMEM((tm, SHW), BF), pltpu.VMEM((tm, SHW), BF), pltpu.VMEM((tm, SHW), BF), pltpu.VMEM((tm, SHW), BF),
            pltpu.VMEM((tm, D), BF), pltpu.VMEM((tm, D), BF), pltpu.VMEM((tm, D), BF), pltpu.VMEM((tm, D), F32),
            pltpu.VMEM((tm, D), BF), pltpu.VMEM((GRP, BLK, BLK), BF),
            pltpu.VMEM((tm + HALO, D), F32), pltpu.VMEM((tm + HALO, D), F32),
            pltpu.SemaphoreType.DMA((NSH,)), pltpu.SemaphoreType.DMA((NSH,)),
        ],
        compiler_params=_params(("arbitrary",), 60),
    )(x, x, ng, wl, lng, lnb, sgw, sgbT, pwb, pb, psc, cw, cb)


def _branch_spec(which):
    q = D // NSH
    return lambda *g: (0, PK_BR // q + (which(*g) if callable(which) else which), 0)


def _fwd_out(y3, proj, x, wl):
    T = x.shape[0]
    tm = _tile(T, 256)
    q = D // NSH

    nT = T // tm

    def body(y_ref, gl_ref, x_ref, wa_ref, wb_ref, wc_ref, wo_ref, p_ref, mg_ref, xo_ref, even_s, odd_s):
        i = pl.program_id(0)
        gl_ref = _AsF32(gl_ref)

        def branches(buf):
            for k, w_ref in enumerate((wa_ref, wb_ref, wc_ref)):
                buf[k] = jnp.dot(y_ref[k], w_ref[...].reshape(D, D), preferred_element_type=F32)

        def merge(buf):
            for c in range(tm // RC):
                rows = pl.ds(c * RC, RC)
                for c0 in COLS:
                    cs = slice(c0, c0 + CW)
                    m = 0.0
                    for k in range(3):
                        pk = buf[k, rows, cs]
                        p_ref[k, rows, cs] = pk.astype(BF)
                        m = m + _sigmoid(gl_ref[rows, k * D + c0:k * D + c0 + CW]) * pk
                    mg_ref[rows, cs] = m.astype(BF)
            xo_ref[...] = x_ref[...] + jnp.dot(mg_ref[...], wo_ref[...].reshape(D, D), preferred_element_type=F32)

        bufs = (even_s, odd_s)

        @pl.when(i == 0)
        def _():
            branches(bufs[0])

        for par in (0, 1):
            @pl.when((i % 2 == par) & (i > 0) & (i < nT))
            def _():
                branches(bufs[par])
                merge(bufs[1 - par])

        @pl.when(i == nT)
        def _():
            merge(bufs[(nT - 1) % 2])

    prev = lambda i: jnp.maximum(i - 1, 0)
    return _pcall(
        body, name="fwd_out", grid=(nT + 1,),
        in_specs=[
            pl.BlockSpec((3, tm, D), lambda i: (0, jnp.minimum(i, nT - 1), 0)),
            pl.BlockSpec((tm, SHW), lambda i: (prev(i), 3)),
            pl.BlockSpec((tm, D), lambda i: (prev(i), 0)),
        ] + [pl.BlockSpec((NSH, q, D), _branch_spec(k)) for k in range(4)],
        out_specs=[
            pl.BlockSpec((3, tm, D), lambda i: (0, prev(i), 0)),
            pl.BlockSpec((tm, D), lambda i: (prev(i), 0)),
            pl.BlockSpec((tm, D), lambda i: (prev(i), 0)),
        ],
        out_shape=[jax.ShapeDtypeStruct((3, T, D), BF), jax.ShapeDtypeStruct((T, D), BF),
                   jax.ShapeDtypeStruct((T, D), F32)],
        scratch_shapes=[pltpu.VMEM((3, tm, D), F32), pltpu.VMEM((3, tm, D), F32)],
        compiler_params=_params(("arbitrary",), 52),
    )(y3, proj, x, wl, wl, wl, wl)


def _loss_head(x, fg, tgt):
    T = x.shape[0]
    tm = _tile(T, 512)

    def body(x_ref, g_ref, t_ref, dx_ref, sq_ref, dg_ref, acc_s):
        i = pl.program_id(0)

        @pl.when(i == 0)
        def _():
            acc_s[...] = jnp.zeros_like(acc_s)

        def f(r0):
            rows = pl.ds(r0, RC)
            xs = [x_ref[rows, c0:c0 + CW] for c0 in COLS]
            r = lax.rsqrt(sum(_rowsum(v * v) for v in xs) * (1.0 / D) + RMS_EPS)
            xh = [v * r for v in xs]
            dyg, m = [], 0.0
            for k, c0 in enumerate(COLS):
                cs = slice(c0, c0 + CW)
                err = xh[k] * g_ref[:, cs] - t_ref[rows, cs]
                acc_s[0, :, cs] += err * err
                dy = err * (1.0 / D)
                acc_s[1, :, cs] += dy * xh[k]
                dyg.append(dy * g_ref[:, cs])
                m = m + _rowsum(dyg[k] * xh[k])
            m = m * (1.0 / D)
            for k, c0 in enumerate(COLS):
                dx_ref[rows, c0:c0 + CW] = r * (dyg[k] - xh[k] * m)
        _chunks(tm, f, unroll=4)

        @pl.when(i == pl.num_programs(0) - 1)
        def _():
            sq_ref[...] = jnp.sum(acc_s[0], axis=0, keepdims=True)
            dg_ref[...] = jnp.sum(acc_s[1], axis=0, keepdims=True)

    vec = pl.BlockSpec((1, D), lambda i: (0, 0))
    tile = pl.BlockSpec((tm, D), lambda i: (i, 0))
    return _pcall(
        body, name="loss_head", grid=(T // tm,),
        in_specs=[tile, vec, tile], out_specs=[tile, vec, vec],
        out_shape=[jax.ShapeDtypeStruct((T, D), F32), jax.ShapeDtypeStruct((1, D), F32), jax.ShapeDtypeStruct((1, D), F32)],
        scratch_shapes=[pltpu.VMEM((2, RC, D), F32)],
        compiler_params=_params(("arbitrary",), 32),
    )(x, fg, tgt)


def _bwd_mix(dxo, p3, proj, wl, lng, lnb, sgw, sgbT, pwb, pb, psc, cw, cb):
    T = dxo.shape[0]
    tm = _tile(T, 256)
    nT = T // tm
    hb = tm // HALO

    def body(dxo_ref, p_ref, pj2_ref, ch_ref, bpp_ref, ccp_ref, chp_ref, w_ref, w2_ref, w3_ref, lng_ref, lnb_ref, sgw_ref,
             sgbT_ref, pw_ref, pb_ref, psc_ref, cw_ref, cb_ref,
             dpj2_ref, dp_ref, dsw_ref, dsbT_ref, vec_ref, dpw_ref,
             dy_s, dy2_s, dy3_s, dyc_s, ext_s, nxt_s, halo_s, dch_s, xh_s, rstd_s, acc_s, accb_s,
             wm_s, wmT_s, *blk):
        i = pl.program_id(0)
        step = pl.program_id(1)
        ti = nT - 1 - i
        p_ref, ch_ref, bpp_ref, ccp_ref, chp_ref = (_AsF32(r) for r in (p_ref, ch_ref, bpp_ref, ccp_ref, chp_ref))

        @pl.when((i == 0) & (step == 0))
        def _():
            dsw_ref[...] = jnp.zeros_like(dsw_ref)
            dsbT_ref[...] = jnp.zeros_like(dsbT_ref)
            vec_ref[...] = jnp.zeros_like(vec_ref)
            dpw_ref[...] = jnp.zeros_like(dpw_ref)
            halo_s[...] = jnp.zeros_like(halo_s)
            for g in range(GRP):
                wm = _masked_sgu(sgw_ref, g)
                wm_s[g] = wm.astype(BF)
                wmT_s[g] = wm.T.astype(BF)

        def flush(n_acc, rows_of):
            for a in range(n_acc):
                vec_ref[rows_of[a]:rows_of[a] + 1, :] += jnp.sum(acc_s[a], axis=0, keepdims=True)

        def all_rows(fn):
            for c in range(tm // RC):
                fn(c * RC)

        @pl.when(step == 0)
        def _():
            pj_ref, dpj_ref = _AsF32(pj2_ref, SHW), _From(dpj2_ref, 3)
            dy_s[...] = _mm_nt(dxo_ref[...], w_ref[...].reshape(D, D))

            def gate(k, r0):
                rows = pl.ds(r0, RC)
                for c0 in COLS:
                    cs = slice(c0, c0 + CW)
                    dm = dy_s[rows, cs]
                    s = _sigmoid(pj_ref[rows, k * D + c0:k * D + c0 + CW])
                    dp_ref[k, rows, cs] = (s * dm).astype(BF)
                    dpj_ref[k, rows, cs] = (dm * p_ref[k, rows, cs] * s * (1.0 - s)).astype(BF)

            all_rows(functools.partial(gate, 2))
            dyc_s[...] = _mm_nt(dp_ref[2], w2_ref[...].reshape(D, D))
            all_rows(functools.partial(gate, 0))
            all_rows(functools.partial(gate, 1))

        @pl.when(step == 0)
        def _():
            pj_ref, dpj_ref = _AsF32(pj2_ref), _From(dpj2_ref, 0)
            dy3_s[...] = _mm_nt(dp_ref[1], w3_ref[...].reshape(D, D))
            acc_s[...] = jnp.zeros_like(acc_s)
            ext_s[0:HALO, :] = jnp.where(ti > 0, ccp_ref[...] * chp_ref[...], 0.0)
            nxt_s[tm:tm + HALO, :] = halo_s[0]

            def f1(r0):
                rows = pl.ds(r0, RC)
                ext_s[pl.ds(HALO + r0, RC), :] = pj_ref[rows, D:2 * D] * ch_ref[rows, :]
            all_rows(f1)

            def f2(r0):
                rows = pl.ds(r0, RC)
                for c0 in COLS:
                    cs = slice(c0, c0 + CW)
                    e = ext_s[pl.ds(r0, 2 * RC), cs]
                    e0, e1, e2 = e[RC:2 * RC], pltpu.roll(e, 1, 0)[RC:2 * RC], pltpu.roll(e, 2, 0)[RC:2 * RC]
                    conv = cw_ref[2:3, cs] * e0 + cw_ref[1:2, cs] * e1 + cw_ref[0:1, cs] * e2 + cb_ref[:, cs]
                    cbv = pj_ref[rows, cs]
                    sz, sg = _silu_fg(pj_ref[rows, 2 * D + c0:2 * D + c0 + CW])
                    dyc = dyc_s[rows, cs]
                    dconv = dyc * cbv * sz
                    dpj_ref[0, rows, cs] = (dyc * conv * sz).astype(BF)
                    dpj_ref[2, rows, cs] = (dyc * cbv * conv * sg).astype(BF)
                    nxt_s[rows, cs] = dconv
                    acc_s[0, :, cs] += dconv
                    acc_s[1, :, cs] += dconv * e2
                    acc_s[2, :, cs] += dconv * e1
                    acc_s[3, :, cs] += dconv * e0
            all_rows(f2)

            def f3(r0):
                rows = pl.ds(r0, RC)
                for c0 in COLS:
                    cs = slice(c0, c0 + CW)
                    e = nxt_s[pl.ds(r0, 2 * RC), cs]
                    dcc = (cw_ref[2:3, cs] * e + cw_ref[1:2, cs] * pltpu.roll(e, 2 * RC - 1, 0)
                           + cw_ref[0:1, cs] * pltpu.roll(e, 2 * RC - 2, 0))[0:RC]
                    dpj_ref[1, rows, cs] = (dcc * ch_ref[rows, cs]).astype(BF)
                    dch_s[rows, cs] = dcc * pj_ref[rows, D + c0:D + c0 + CW]
            all_rows(f3)
            halo_s[0] = nxt_s[0:HALO, :]
            flush(4, (4, 5, 6, 7))

        @pl.when(step == 1)
        def _():
            pj_ref, dpj_ref = _AsF32(pj2_ref, SHW), _From(dpj2_ref, 3)
            dy2_s[...] = _mm_nt(dp_ref[0], w2_ref[...].reshape(D, D))
            acc_s[...] = jnp.zeros_like(acc_s)
            ext_s[0:HALO, :] = jnp.where(ti > 0, bpp_ref[...], 0.0)
            ext_s[HALO:HALO + tm, :] = pj_ref[:, 0:D]
            nxt_s[tm:tm + HALO, :] = halo_s[1]

            nblk = tm // BLK
            d_b, t_b, dy0_b = blk[:nblk], blk[nblk:2 * nblk], blk[2 * nblk:]

            def block_rows(fn, n):
                for c in range(BLK // RC):
                    fn(n, c * RC)

            def f1(n, b0):
                r0, brow = n * BLK + b0, pl.ds(b0, RC)
                e = ext_s[pl.ds(r0, 2 * RC), :]
                for k, win in enumerate(POOL_WINDOWS):
                    gc = slice(k * PGW, (k + 1) * PGW)
                    s = _trail(e[:, gc], win)[RC:2 * RC]
                    d_b[n][brow, gc] = (s * _inv_count(ti * tm + r0, win) - e[RC:2 * RC, gc]).astype(BF)

            def pool(n):
                for k in range(NPG):
                    gc = slice(k * PGW, (k + 1) * PGW)
                    t_b[n][:, gc] = jnp.dot(d_b[n][:, gc], pw_ref[k], preferred_element_type=F32)

            def f2(n, b0):
                rows, brow = pl.ds(n * BLK + b0, RC), pl.ds(b0, RC)
                for c0 in COLS:
                    cs = slice(c0, c0 + CW)
                    y0 = t_b[n][brow, cs] + pb_ref[:, cs]
                    sz, sg = _silu_fg(pj_ref[rows, D + c0:D + c0 + CW])
                    dyb = dy3_s[rows, cs]
                    dy0 = dyb * psc_ref[:, cs] * sz
                    acc_s[0, :, cs] += dy0
                    acc_s[1, :, cs] += dyb * y0 * sz
                    dy0_b[n][brow, cs] = dy0.astype(BF)
                    dpj_ref[1, rows, cs] = (dyb * y0 * psc_ref[:, cs] * sg).astype(BF)

            def pool_back(n):
                for k in range(NPG):
                    gc = slice(k * PGW, (k + 1) * PGW)
                    dpw_ref[k] += _mm_tn(d_b[n][:, gc], dy0_b[n][:, gc])
                    t_b[n][:, gc] = _mm_nt(dy0_b[n][:, gc], pw_ref[k])

            def f3(n, b0):
                r0, brow = n * BLK + b0, pl.ds(b0, RC)
                for k, win in enumerate(POOL_WINDOWS):
                    gc = slice(k * PGW, (k + 1) * PGW)
                    nxt_s[pl.ds(r0, RC), gc] = t_b[n][brow, gc] * _inv_count(ti * tm + r0, win)

            def f4(n, b0):
                r0, brow = n * BLK + b0, pl.ds(b0, RC)
                rows = pl.ds(r0, RC)
                e = nxt_s[pl.ds(r0, 2 * RC), :]
                for k, win in enumerate(POOL_WINDOWS):
                    gc = slice(k * PGW, (k + 1) * PGW)
                    dpj_ref[0, rows, gc] = (_lead(e[:, gc], win)[0:RC] - t_b[n][brow, gc]).astype(BF)
                dpj_ref[2, rows, :] = dch_s[rows, :].astype(BF)

            def spread(n):
                block_rows(f3, n)
                block_rows(f4, n)

            stages = (functools.partial(block_rows, f1), pool, functools.partial(block_rows, f2), pool_back, spread)
            for t in range(nblk + len(stages) - 1):
                for s in reversed(range(len(stages))):
                    if 0 <= t - s < nblk:
                        stages[s](nblk - 1 - (t - s))
            halo_s[1] = nxt_s[0:HALO, :]
            flush(2, (2, 3))

        @pl.when(step == 1)
        def _():
            pj_ref, dpj_ref = _AsF32(pj2_ref), _From(dpj2_ref, 0)
            acc_s[...] = jnp.zeros_like(acc_s)
            accb_s[...] = jnp.zeros_like(accb_s)
            nblk = tm // BLK
            vn_b, t_b, dmix_b = blk[:nblk], blk[nblk:2 * nblk], blk[2 * nblk:]

            def block_rows(fn, n):
                for c in range(BLK // RC):
                    fn(n, c * RC)

            def f1(n, b0):
                rows, brow = pl.ds(n * BLK + b0, RC), pl.ds(b0, RC)
                gv = [_gelu(pj_ref[rows, D + c0:D + c0 + CW]) for c0 in COLS]
                mu = sum(_rowsum(v) for v in gv) * (1.0 / D)
                dv = [v - mu for v in gv]
                var = sum(_rowsum(v * v) for v in dv) * (1.0 / D)
                rstd = lax.rsqrt(var + LN_EPS)
                rstd_s[rows, :] = jnp.broadcast_to(rstd, (RC, BLK))
                for k, c0 in enumerate(COLS):
                    cs = slice(c0, c0 + CW)
                    xh = dv[k] * rstd
                    xh_s[rows, cs] = xh
                    vn_b[n][brow, cs] = (xh * lng_ref[:, cs] + lnb_ref[:, cs]).astype(BF)

            def mix(n):
                for g in range(GRP):
                    gc = slice(g * BLK, (g + 1) * BLK)
                    t_b[n][:, gc] = jnp.dot(wm_s[g], vn_b[n][:, gc], preferred_element_type=F32) + sgbT_ref[:, g:g + 1]

            def f2(n, b0):
                rows, brow = pl.ds(n * BLK + b0, RC), pl.ds(b0, RC)
                for c0 in COLS:
                    cs = slice(c0, c0 + CW)
                    gu, ggu = _gelu_fg(pj_ref[rows, cs])
                    sz, sg = _silu_fg(pj_ref[rows, 2 * D + c0:2 * D + c0 + CW])
                    dya = dy2_s[rows, cs]
                    mx = t_b[n][brow, cs]
                    dmix = dya * gu * sz
                    dpj_ref[0, rows, cs] = (dya * mx * sz * ggu).astype(BF)
                    dpj_ref[2, rows, cs] = (dya * gu * mx * sg).astype(BF)
                    dmix_b[n][brow, cs] = dmix.astype(BF)
                    accb_s[brow, cs] += dmix

            def mix_back(n):
                for g in range(GRP):
                    gc = slice(g * BLK, (g + 1) * BLK)
                    t_b[n][:, gc] = jnp.dot(wmT_s[g], dmix_b[n][:, gc], preferred_element_type=F32)
                    dsw_ref[g] += _mm_nt(dmix_b[n][:, gc], vn_b[n][:, gc])

            def f3(n, b0):
                rows, brow = pl.ds(n * BLK + b0, RC), pl.ds(b0, RC)
                rstd = rstd_s[rows, 0:1]
                dxh, m1, m2 = [], 0.0, 0.0
                for k, c0 in enumerate(COLS):
                    cs = slice(c0, c0 + CW)
                    dvn = t_b[n][brow, cs]
                    xh = xh_s[rows, cs]
                    acc_s[0, :, cs] += dvn * xh
                    acc_s[1, :, cs] += dvn
                    dxh.append(dvn * lng_ref[:, cs])
                    m1 = m1 + _rowsum(dxh[k])
                    m2 = m2 + _rowsum(dxh[k] * xh)
                m1 = m1 * (1.0 / D)
                m2 = m2 * (1.0 / D)
                for k, c0 in enumerate(COLS):
                    cs = slice(c0, c0 + CW)
                    _, ggv = _gelu_fg(pj_ref[rows, D + c0:D + c0 + CW])
                    dpj_ref[1, rows, cs] = (rstd * (dxh[k] - m1 - xh_s[rows, cs] * m2) * ggv).astype(BF)

            stages = (functools.partial(block_rows, f1), mix, functools.partial(block_rows, f2), mix_back,
                      functools.partial(block_rows, f3))
            for t in range(nblk + len(stages) - 1):
                for s in reversed(range(len(stages))):
                    if 0 <= t - s < nblk:
                        stages[s](t - s)
            for g in range(GRP):
                dsbT_ref[:, g:g + 1] += _rowsum(accb_s[:, g * BLK:(g + 1) * BLK])
            flush(2, (0, 1))

            @pl.when(i == nT - 1)
            def _():
                for g in range(GRP):
                    r = lax.broadcasted_iota(jnp.int32, (BLK, BLK), 0) // 64
                    c = lax.broadcasted_iota(jnp.int32, (BLK, BLK), 1) // 64
                    dsw_ref[g] = jnp.where(c <= r, dsw_ref[g], 0.0)

    def prev(col):
        return pl.BlockSpec((HALO, D), lambda i, p: (jnp.maximum((nT - 1 - i) * hb - 1, 0), col))

    vec = pl.BlockSpec((1, D), lambda i, p: (0, 0))
    const3 = lambda i, p: (0, 0, 0)
    return _pcall(
        body, name="bwd_mix", grid=(nT, 2),
        in_specs=[
            pl.BlockSpec((tm, D), lambda i, p: (nT - 1 - i, 0)),
            pl.BlockSpec((3, tm, D), lambda i, p: (0, nT - 1 - i, 0)),
            pl.BlockSpec((tm, 2 * SHW), lambda i, p: (nT - 1 - i, 1 - p)),
            pl.BlockSpec((tm, D), lambda i, p: (nT - 1 - i, 5)),
            prev(3), prev(7), prev(5),
            pl.BlockSpec((NSH, D // NSH, D), _branch_spec(3)),
            pl.BlockSpec((NSH, D // NSH, D), _branch_spec(lambda i, p: 2 - 2 * p)),
            pl.BlockSpec((NSH, D // NSH, D), _branch_spec(1)),
            vec, vec,
            pl.BlockSpec((GRP, BLK, BLK), const3),
            pl.BlockSpec((BLK, GRP), lambda i, p: (0, 0)),
            pl.BlockSpec((NPG, PGW, PGW), const3),
            vec, vec,
            pl.BlockSpec((3, D), lambda i, p: (0, 0)), vec,
        ],
        out_specs=[
            pl.BlockSpec((6, tm, D), lambda i, p: (1 - p, nT - 1 - i, 0)),
            pl.BlockSpec((3, tm, D), lambda i, p: (0, nT - 1 - i, 0)),
            pl.BlockSpec((GRP, BLK, BLK), const3),
            pl.BlockSpec((BLK, GRP), lambda i, p: (0, 0)),
            pl.BlockSpec((8, D), lambda i, p: (0, 0)),
            pl.BlockSpec((NPG, PGW, PGW), const3),
        ],
        out_shape=[
            jax.ShapeDtypeStruct((12, T, D), BF), jax.ShapeDtypeStruct((3, T, D), BF),
            jax.ShapeDtypeStruct((GRP, BLK, BLK), F32), jax.ShapeDtypeStruct((BLK, GRP), F32),
            jax.ShapeDtypeStruct((8, D), F32), jax.ShapeDtypeStruct((NPG, PGW, PGW), F32),
        ],
        scratch_shapes=[
            pltpu.VMEM((tm, D), F32),
            pltpu.VMEM((tm, D), F32),
            pltpu.VMEM((tm, D), F32),
            pltpu.VMEM((tm, D), F32),
            pltpu.VMEM((tm + HALO, D), F32),
            pltpu.VMEM((tm + HALO, D), F32),
            pltpu.VMEM((2, HALO, D), F32),
            pltpu.VMEM((tm, D), F32),
            pltpu.VMEM((tm, D), F32),
            pltpu.VMEM((tm, BLK), F32),
            pltpu.VMEM((4, RC, D), F32),
            pltpu.VMEM((BLK, D), F32),
            pltpu.VMEM((GRP, BLK, BLK), BF), pltpu.VMEM((GRP, BLK, BLK), BF),
        ] + [pltpu.VMEM((BLK, D), dt) for dt in (BF, F32, BF) for _ in range(tm // BLK)],
        compiler_params=_params(("arbitrary", "arbitrary"), 56),
    )(dxo, p3, proj, proj, proj, proj, proj, wl, wl, wl, lng, lnb, sgw, sgbT, pwb, pb, psc, cw, cb)


def _bwd_in(dproj, wl, x, dxo, ng):
    T = x.shape[0]
    tm = _tile(T, 256)
    nT = T // tm

    def body(dpj_ref, w_hbm, x_ref, dxo_ref, ng_ref, dx_ref, dng_ref, w_s, even_s, odd_s, g_s, wsem):
        i = pl.program_id(0)
        bufs = (even_s, odd_s)

        def w_copy(j):
            return pltpu.make_async_copy(w_hbm.at[j, pl.ds(0, SHW), :], w_s.at[j], wsem.at[j])

        def d_h():
            return sum(_mm_nt(dpj_ref[3 * j + b], w_s[j, b * D:(b + 1) * D, :]) for j in range(NSH) for b in range(3))

        def finish(prev):
            for c in range(tm // RC):
                finish_rows(prev, c * RC)

        def finish_rows(prev, r0):
            rows = pl.ds(r0, RC)
            xs = [x_ref[rows, c0:c0 + CW] for c0 in COLS]
            r = lax.rsqrt(sum(_rowsum(v * v) for v in xs) * (1.0 / D) + RMS_EPS)
            xh = [v * r for v in xs]
            dhg, m = [], 0.0
            for k, c0 in enumerate(COLS):
                cs = slice(c0, c0 + CW)
                dh = prev[rows, cs]
                g_s[:, cs] += dh * xh[k]
                dhg.append(dh * ng_ref[:, cs])
                m = m + _rowsum(dhg[k] * xh[k])
            m = m * (1.0 / D)
            for k, c0 in enumerate(COLS):
                cs = slice(c0, c0 + CW)
                dx_ref[rows, cs] = dxo_ref[rows, cs] + r * (dhg[k] - xh[k] * m)

        @pl.when(i == 0)
        def _():
            for j in range(NSH):
                w_copy(j).start()
            g_s[...] = jnp.zeros_like(g_s)
            for j in range(NSH):
                w_copy(j).wait()
            bufs[0][...] = d_h()

        for par in (0, 1):
            @pl.when((i % 2 == par) & (i > 0) & (i < nT))
            def _():
                bufs[par][...] = d_h()
                finish(bufs[1 - par])

        @pl.when(i == nT)
        def _():
            finish(bufs[(nT - 1) % 2])
            dng_ref[...] = jnp.sum(g_s[...], axis=0, keepdims=True)

    vec = pl.BlockSpec((1, D), lambda i: (0, 0))
    tile = pl.BlockSpec((tm, D), lambda i: (jnp.maximum(i - 1, 0), 0))
    return _pcall(
        body, name="bwd_in", grid=(nT + 1,),
        in_specs=[pl.BlockSpec((3 * NSH, tm, D), lambda i: (0, jnp.minimum(i, nT - 1), 0)), ANY, tile, tile, vec],
        out_specs=[tile, vec],
        out_shape=[jax.ShapeDtypeStruct((T, D), F32), jax.ShapeDtypeStruct((1, D), F32)],
        scratch_shapes=[pltpu.VMEM((NSH, SHW, D), BF), pltpu.VMEM((tm, D), F32), pltpu.VMEM((tm, D), F32),
                        pltpu.VMEM((RC, D), F32), pltpu.SemaphoreType.DMA((NSH,))],
        compiler_params=_params(("arbitrary",), 52),
    )(dproj, wl, x, dxo, ng)


def _tn_grad(a3, b3, split, out_map, name, into=None, after=None):
    nb, T, _ = b3.shape
    tk = _tile(T, 4096 if b3.dtype == BF and split == 1 else 2048)
    nk = T // tk
    rows = D // split
    a_batched = a3.shape[0] > 1

    def body(a_ref, b_ref, *rest):
        o_ref, acc_s = rest[-2:]
        k = pl.program_id(1)

        @pl.when(k == 0)
        def _():
            acc_s[...] = _mm_tn(a_ref[...], b_ref[...])

        @pl.when((k > 0) & (k < nk))
        def _():
            acc_s[...] += _mm_tn(a_ref[...], b_ref[...])

        @pl.when(k >= nk - 1)
        def _():
            r0 = pl.multiple_of((k - (nk - 1)) * rows, rows)
            o_ref[...] = acc_s[pl.ds(r0, rows), :].astype(o_ref.dtype)

    def tok(k):
        return jnp.minimum(k, nk - 1)

    extra = ([] if into is None else [into]) + ([] if after is None else [after])
    return _pcall(
        body, name=name, grid=(nb, nk + split - 1),
        in_specs=[
            pl.BlockSpec((None, tk, D), (lambda n, k: (n, tok(k), 0)) if a_batched else (lambda n, k: (0, tok(k), 0))),
            pl.BlockSpec((None, tk, D), lambda n, k: (n, tok(k), 0)),
        ] + [ANY] * len(extra),
        out_specs=pl.BlockSpec((None, rows, D), lambda n, k: out_map(n, jnp.maximum(k - (nk - 1), 0))),
        out_shape=jax.ShapeDtypeStruct((NSH, PK_ROWS, D), BF),
        input_output_aliases={} if into is None else {2: 0},
        scratch_shapes=[pltpu.VMEM((D, D), F32)],
        compiler_params=_params(("arbitrary", "arbitrary"), 56),
    )(a3, b3, *extra)


def _place():
    x, y, c = lax.axis_index("x"), lax.axis_index("y"), lax.axis_index("c")
    chips = [(1 - x, y), (x, 1 - y), (1 - x, 1 - y)]
    return x, y, c, chips


def _peers(reach):
    x, y, c, chips = _place()
    if reach == "chips":
        return 2 * x + y, [((px, py, c), 2 * px + py) for px, py in chips]
    others = [(x, y, 1 - c)] + [(px, py, pc) for px, py in chips for pc in (c, 1 - c)]
    return 4 * x + 2 * y + c, [(pr, 4 * pr[0] + 2 * pr[1] + pr[2]) for pr in others]


ANY = pl.BlockSpec(memory_space=pl.ANY)


HBM = pl.BlockSpec(memory_space=pltpu.HBM)
SEM = pl.BlockSpec(memory_space=pltpu.SEMAPHORE)
EFFECT = pltpu.SideEffectType.DATAFLOW_SIDE_EFFECTING


def _own_slot(src, from_slot, name):
    rows = src.shape[-2]
    rb = rows // 8
    me = (2 * lax.axis_index("x") + lax.axis_index("y")).astype(jnp.int32).reshape(1)

    def body(me_ref, src_ref, land_ref):
        land_ref[...] = src_ref[...]

    if from_slot:
        src_spec = pl.BlockSpec((None, rb, D), lambda i, me_ref: (me_ref[0], i, 0))
    else:
        src_spec = pl.BlockSpec((rb, D), lambda i, me_ref: (i, 0))
    return _pcall(
        body, name=name,
        grid_spec=pltpu.PrefetchScalarGridSpec(
            num_scalar_prefetch=1, grid=(rows // rb,), in_specs=[src_spec],
            out_specs=pl.BlockSpec((None, rb, D), lambda i, me_ref: (me_ref[0], i, 0))),
        out_shape=jax.ShapeDtypeStruct((NSH, rows, D), src.dtype),
        compiler_params=_params(("arbitrary",), 32),
    )(me, src)


def _my_half():
    return pl.ds(pl.multiple_of(lax.axis_index("c") * PK_HALF, PK_HALF), PK_HALF)


def _push_start(name, srcs, lands, per_peer, reach="chips", halved=()):
    n = len(srcs)
    npeer = 3 if reach == "chips" else 7
    ns = n * npeer

    def body(*refs):
        src, land = refs[:n], refs[n:2 * n]
        ssem, rsem = refs[2 * n:2 * n + ns], refs[2 * n + ns:2 * n + 2 * ns]
        token = refs[-1]
        me, peers = _peers(reach)
        for i in range(n):
            for j, (peer, slot) in enumerate(peers):
                s = src[i].at[slot] if per_peer else src[i]
                d = land[i].at[me]
                if i in halved:
                    s, d = s.at[_my_half()], d.at[_my_half()]
                pltpu.make_async_remote_copy(
                    src_ref=s, dst_ref=d, send_sem=ssem[npeer * i + j], recv_sem=rsem[npeer * i + j], device_id=peer,
                    device_id_type=MESH).start()
        token[...] = jnp.zeros_like(token)

    ops = list(srcs) + list(lands)
    out = _pcall(
        body, name=name,
        out_shape=tuple([pltpu.SemaphoreType.DMA(())] * (2 * ns) + [pltpu.HBM(a.shape, a.dtype) for a in ops]
                        + [jax.ShapeDtypeStruct((8, 128), F32)]),
        in_specs=[HBM] * (2 * n),
        out_specs=tuple([SEM] * (2 * ns) + [HBM] * (2 * n) + [pl.BlockSpec(memory_space=pltpu.VMEM)]),
        input_output_aliases={i: 2 * ns + i for i in range(2 * n)},
        compiler_params=pltpu.CompilerParams(has_side_effects=EFFECT),
    )(*[pltpu.with_memory_space_constraint(a, pltpu.HBM) for a in ops])
    return out[:ns], out[ns:2 * ns], out[2 * ns:2 * ns + n], out[2 * ns + n:2 * ns + 2 * n], out[-1]


def _push_wait(name, src, land, ssem, rsem, after, per_peer, reach="chips", halved=False):
    npeer = len(ssem)

    def body(src_ref, land_ref, *rest):
        sems = rest[:2 * npeer]
        _, peers = _peers(reach)
        for j, (peer, slot) in enumerate(peers):
            s = src_ref.at[slot] if per_peer else src_ref
            d = land_ref.at[slot]
            if halved:
                s, d = s.at[_my_half()], d.at[_my_half()]
            cp = pltpu.make_async_remote_copy(
                src_ref=s, dst_ref=d, send_sem=sems[j],
                recv_sem=sems[npeer + j], device_id=peer, device_id_type=MESH)
            cp.wait_send()
            cp.wait_recv()

    return _pcall(
        body, name=name,
        out_shape=(pltpu.HBM(src.shape, src.dtype), pltpu.HBM(land.shape, land.dtype)),
        in_specs=[HBM, HBM] + [SEM] * (2 * npeer) + [ANY], out_specs=(HBM, HBM),
        input_output_aliases={0: 0, 1: 1},
        compiler_params=pltpu.CompilerParams(has_side_effects=EFFECT),
    )(src, land, *ssem, *rsem, after)[1]


def _share_halves(land):
    def body(land_in, land_ref, ssem, rsem):
        x, y, c, chips = _place()
        other = pl.ds(pl.multiple_of((1 - c) * PK_HALF, PK_HALF), PK_HALF)
        sends = [pltpu.make_async_remote_copy(
            src_ref=land_ref.at[2 * px + py, _my_half()], dst_ref=land_ref.at[2 * px + py, _my_half()],
            send_sem=ssem.at[j], recv_sem=rsem.at[j], device_id=(x, y, 1 - c), device_id_type=MESH)
            for j, (px, py) in enumerate(chips)]
        for cp in sends:
            cp.start()
        for j, (px, py) in enumerate(chips):
            pltpu.make_async_remote_copy(
                src_ref=land_ref.at[2 * px + py, other], dst_ref=land_ref.at[2 * px + py, other], send_sem=ssem.at[j],
                recv_sem=rsem.at[j], device_id=(x, y, 1 - c), device_id_type=MESH).wait_recv()
        for cp in sends:
            cp.wait_send()

    return _pcall(
        body, name="share_halves", in_specs=[ANY], out_specs=ANY,
        out_shape=jax.ShapeDtypeStruct(land.shape, land.dtype), input_output_aliases={0: 0},
        scratch_shapes=[pltpu.SemaphoreType.DMA((3,)), pltpu.SemaphoreType.DMA((3,))],
    )(land)


def _swap_sibling(arrs):
    n = len(arrs)

    def body(*refs):
        src, dst = refs[:n], refs[n:2 * n]
        ssem, rsem = refs[2 * n:]
        x, y, c, _ = _place()
        cps = [pltpu.make_async_remote_copy(src_ref=src[a], dst_ref=dst[a], send_sem=ssem.at[a], recv_sem=rsem.at[a],
                                            device_id=(x, y, 1 - c), device_id_type=MESH) for a in range(n)]
        for cp in cps:
            cp.start()
        for cp in cps:
            cp.wait()

    return _pcall(
        body, name="swap_sibling",
        in_specs=[ANY] * n, out_specs=[ANY] * n,
        out_shape=[jax.ShapeDtypeStruct(a.shape, a.dtype) for a in arrs],
        scratch_shapes=[pltpu.SemaphoreType.DMA((n,)), pltpu.SemaphoreType.DMA((n,))],
    )(*arrs)


def _gather_all(v):
    def body(src, dst, ssem, rsem, lsem):
        x, y, c, _ = _place()
        me = 4 * x + 2 * y + c
        peers = [(x, y, 1 - c), (1 - x, y, c), (1 - x, y, 1 - c), (x, 1 - y, c), (x, 1 - y, 1 - c),
                 (1 - x, 1 - y, c), (1 - x, 1 - y, 1 - c)]
        local = pltpu.make_async_copy(src, dst.at[me], lsem)
        local.start()
        sends = [pltpu.make_async_remote_copy(src_ref=src, dst_ref=dst.at[me], send_sem=ssem.at[j], recv_sem=rsem.at[j],
                                              device_id=pr, device_id_type=MESH) for j, pr in enumerate(peers)]
        for cp in sends:
            cp.start()
        for j, (px, py, pc) in enumerate(peers):
            pltpu.make_async_remote_copy(src_ref=src, dst_ref=dst.at[4 * px + 2 * py + pc], send_sem=ssem.at[j],
                                         recv_sem=rsem.at[j], device_id=(px, py, pc), device_id_type=MESH).wait_recv()
        for cp in sends:
            cp.wait_send()
        local.wait()

    return _pcall(
        body, name="gather_all", in_specs=[ANY], out_specs=ANY,
        out_shape=jax.ShapeDtypeStruct((8,) + v.shape, v.dtype),
        scratch_shapes=[pltpu.SemaphoreType.DMA((7,)), pltpu.SemaphoreType.DMA((7,)), pltpu.SemaphoreType.DMA(())],
    )(v)


def _sum_slots(r, rb, out_dtype=F32):
    S = r.shape[0]

    def body(r_ref, o_ref):
        acc = r_ref[0].astype(F32)
        for s in range(1, S):
            acc = acc + r_ref[s].astype(F32)
        o_ref[...] = acc.astype(out_dtype)

    if r.ndim == 3:
        _, R, C = r.shape
        grid, blk, imap = (R // rb,), (S, rb, C), (lambda i: (0, i, 0))
        oblk, omap = (rb, C), (lambda i: (i, 0))
    else:
        _, K, R, C = r.shape
        grid, blk, imap = (K,), (S, None, R, C), (lambda i: (0, i, 0, 0))
        oblk, omap = (None, R, C), (lambda i: (i, 0, 0))
    return _pcall(
        body, name="sum_slots", grid=grid, in_specs=[pl.BlockSpec(blk, imap)], out_specs=pl.BlockSpec(oblk, omap),
        out_shape=jax.ShapeDtypeStruct(r.shape[1:], out_dtype), compiler_params=_params(("arbitrary",), 48),
    )(r)


def _adamw(gs, g_spec, w, m, v, p_spec, prev, grid):
    ng = len(gs)
    bc1 = 1.0 - ADAM_B1 ** ADAM_STEP
    bc2 = 1.0 - ADAM_B2 ** ADAM_STEP

    def body(*refs):
        g = refs[0][...].astype(F32)
        for a in range(1, ng):
            g = g + refs[a][...].astype(F32)
        w_ref, m_ref, v_ref = refs[ng:ng + 3]
        go, do, mo, vo = refs[ng + 3 + 4:]
        mn = ADAM_B1 * m_ref[...] + (1.0 - ADAM_B1) * g
        vn = ADAM_B2 * v_ref[...] + (1.0 - ADAM_B2) * (g * g)
        go[...] = g
        mo[...] = mn
        vo[...] = vn
        do[...] = -ADAM_LR * ((mn / bc1) / (jnp.sqrt(vn / bc2) + ADAM_EPS) + ADAM_WD * w_ref[...])

    out = jax.ShapeDtypeStruct(w.shape, F32)
    k0 = ng + 3
    return _pcall(
        body, name="adamw", grid=grid,
        in_specs=[g_spec] * ng + [p_spec] * 3 + [ANY] * 4,
        out_specs=[p_spec] * 4, out_shape=[out] * 4,
        input_output_aliases={k0: 0, k0 + 1: 1, k0 + 2: 2, k0 + 3: 3},
        compiler_params=_params(("arbitrary",) * len(grid), 48),
    )(*gs, w, m, v, *prev)


def _empty4(w):
    return tuple(lax.empty(w.shape, F32) for _ in range(4))


N_SGW = L * GRP * BLK * BLK // D
O_NG, O_VEC, O_SGB, O_FG, O_SGW = 0, 8, 32, 40, 48
O_CW = O_SGW + N_SGW
N_PACK = O_CW + 16
PACK_RB = N_PACK // 3


def _pad_to(a, rows):
    return jnp.pad(a, ((0, rows - a.shape[0]), (0, 0)))


def _pack_small(ng, vecs, sgb, fg, sgw, cw):
    parts = [_pad_to(ng, 8), _pad_to(vecs.reshape(L * 5, D), 24), _pad_to(sgb.reshape(L, D), 8),
             _pad_to(fg.reshape(1, D), 8), sgw.reshape(N_SGW, D), _pad_to(cw, 16)]
    return jnp.concatenate(parts, axis=0)


def kernel(x, norm_g, w_in, sgu_ln_g, sgu_ln_b, sgu_w, sgu_b, pool_w, pool_b, pool_scale, conv_w, conv_b, w_branch_a, w_branch_b, w_branch_c, w_out, final_g, loss_target, m_norm_g, m_w_in, m_sgu_ln_g, m_sgu_ln_b, m_sgu_w, m_sgu_b, m_pool_w, m_pool_b, m_pool_scale, m_conv_w, m_conv_b, m_w_branch_a, m_w_branch_b, m_w_branch_c, m_w_out, m_final_g, v_norm_g, v_w_in, v_sgu_ln_g, v_sgu_ln_b, v_sgu_w, v_sgu_b, v_pool_w, v_pool_b, v_pool_scale, v_conv_w, v_conv_b, v_w_branch_a, v_w_branch_b, v_w_branch_c, v_w_out, v_final_g):
    cx, cy = lax.axis_index("x"), lax.axis_index("y")
    me = 2 * cx + cy
    xl, tgt = x[0], loss_target[0]
    q = D // NSH

    wq = w_in.astype(BF).reshape(L, D, 3, D).transpose(0, 2, 1, 3).reshape(L, SHW, D)
    brq = jnp.stack([w_branch_a, w_branch_b, w_branch_c, w_out], axis=1).astype(BF).reshape(L, D, D)
    pwq = pool_w.astype(BF).reshape(L, PK_CW - PK_PW, D)
    cwq = lax.bitcast_convert_type(conv_w, BF).reshape(L, 3 * q * 2)
    cwq = jnp.pad(cwq, ((0, 0), (0, (PK_ROWS - PK_CW) * D - 3 * q * 2))).reshape(L, PK_ROWS - PK_CW, D)
    packs = [jnp.concatenate([wq[l], brq[l], pwq[l], cwq[l]], axis=0) for l in range(L)]
    lands = [_own_slot(packs[l], False, f"ag_own_{l}") for l in range(L)]
    ag_s, ag_r, packs, lands, tok = _push_start("ag_start", packs, lands, False, halved=(0,))
    sgbT = sgu_b.transpose(0, 2, 1)

    def layer_weights(l, after):
        wl = _push_wait(f"ag_wait_{l}", packs[l], lands[l], ag_s[3 * l:3 * l + 3], ag_r[3 * l:3 * l + 3], after, False,
                        halved=l == 0)
        if l == 0:
            wl = _share_halves(wl)
        pwb = wl[:, PK_PW:PK_CW].reshape(NSH, NPG, PGW // NSH, PGW).transpose(1, 0, 2, 3).reshape(NPG, PGW, PGW)
        cwb = wl[:, PK_CW:].reshape(NSH, (PK_ROWS - PK_CW) * D)[:, :3 * q * 2].reshape(NSH, 3, q, 2)
        cwf = lax.bitcast_convert_type(cwb, F32).transpose(1, 0, 2).reshape(3, D)
        small = (sgu_ln_g[l:l + 1], sgu_ln_b[l:l + 1], sgu_w[l], sgbT[l], pwb, pool_b[l:l + 1], pool_scale[l:l + 1],
                 cwf, conv_b[l:l + 1])
        return wl, small

    xs, saved, wts = [xl], [], []
    for l in range(L):
        wl, small = layer_weights(l, tok if l == 0 else xs[l])
        wts.append((wl, small))
        proj, h, y3 = _fwd_in(xs[l], norm_g[l:l + 1], wl, *small)
        p3, mg, xo = _fwd_out(y3, proj, xs[l], wl)
        saved.append((proj, h, y3, p3, mg))
        xs.append(xo)

    dx, sq, dfg = _loss_head(xs[L], final_g[None], tgt)
    loss = lax.psum(jnp.sum(sq) * (0.5 / D), ("x", "y", "c"))

    g_w_in = _empty4(w_in)
    g_br = [_empty4(w_out) for _ in range(4)]
    g_pw = _empty4(pool_w)
    dng, dvec, dsgw, dsgb = [None] * L, [None] * L, [None] * L, [None] * L
    branches = [(w_branch_a, m_w_branch_a, v_w_branch_a), (w_branch_b, m_w_branch_b, v_w_branch_b),
                (w_branch_c, m_w_branch_c, v_w_branch_c), (w_out, m_w_out, v_w_out)]
    nb = D // 128

    def finish(l, landed):
        nonlocal g_w_in, g_pw
        mine = _sum_slots(landed, PK_ROWS // 8, BF)
        sums = [mine, _swap_sibling([mine])[0]]
        g_w_in = _adamw(sums, pl.BlockSpec((128, D), lambda b, i: (b * nb + i, 0)), w_in, m_w_in, v_w_in,
                        pl.BlockSpec((None, 128, D), lambda b, i: (l, i, b)), g_w_in, (3, nb))
        for k, (w, m, v) in enumerate(branches):
            g_br[k] = _adamw(sums, pl.BlockSpec((q, D), lambda i, k=k: (PK_BR // q + k, 0)), w, m, v,
                             pl.BlockSpec((None, q, D), lambda i: (l, 0, 0)), g_br[k], (1,))
        pools = [a[PK_PW:PK_CW].reshape(NPG, PGW // NSH, PGW) for a in sums]
        g_pw = _adamw(pools, pl.BlockSpec((None, PGW // NSH, PGW), lambda g: (g, 0, 0)), pool_w, m_pool_w, v_pool_w,
                      pl.BlockSpec((None, None, PGW // NSH, PGW), lambda g: (l, g, 0, 0)), g_pw, (NPG,))

    pend = None
    for l in reversed(range(L)):
        proj, h, y3, p3, mg = saved[l]
        wl, small = wts[l]
        dproj, dp3, dsgw[l], dsbT, dvec[l], dpw = _bwd_mix(dx, p3, proj, wl, *small)
        dsgb[l] = dsbT.T
        if l == 0:
            dv = jnp.stack(dvec)
            part = _pack_small(jnp.zeros((L, D), F32), dv[:, 0:5], jnp.stack(dsgb), dfg[0], jnp.stack(dsgw),
                               dv[:, 5:8].reshape(L * 3, D))
            zone = lax.dynamic_update_slice(lax.empty((8, N_PACK, D), F32), part[None],
                                            (2 * me + lax.axis_index("c"), 0, 0))
            sm_s, sm_r, (part,), (zone,), started = _push_start("small_start", [part], [zone], False, "all")
        grads = _tn_grad(h[None], dproj, 1, lambda n, s: (n // 3, n % 3, 0), "grad_w_in", after=started if l == 0 else None)
        grads = _tn_grad(y3, dp3, NSH, lambda n, s: (s, PK_BR // q + n, 0), "grad_w_branch", into=grads)
        grads = _tn_grad(mg[None], dx[None], NSH, lambda n, s: (s, PK_BR // q + 3, 0), "grad_w_out", into=grads)
        dpq = dpw.astype(BF).reshape(NPG, NSH, PGW // NSH, PGW).transpose(1, 0, 2, 3).reshape(NSH, PK_CW - PK_PW, D)
        grads = lax.dynamic_update_slice(grads, jnp.pad(dpq, ((0, 0), (0, PK_ROWS - PK_CW), (0, 0))), (0, PK_PW, 0))
        if pend is not None:
            landed = _push_wait(f"rs_wait_{pend[0]}", *pend[1:], dproj, True)
        land = _own_slot(grads, True, f"rs_own_{l}")
        ss, rs, (grads,), (land,), tok = _push_start(f"rs_start_{l}", [grads], [land], True)
        if pend is not None:
            finish(pend[0], landed)
        dx, dng[l] = _bwd_in(dproj, wl, xs[l], dx, norm_g[l:l + 1] + tok[0, 0])
        pend = (l, grads, land, ss, rs)
    finish(pend[0], _push_wait(f"rs_wait_{pend[0]}", *pend[1:], dx, True))

    zone = _push_wait("small_wait", part, zone, sm_s, sm_r, dx, False, "all")
    gng = _sum_slots(_gather_all(_pad_to(jnp.concatenate(dng), O_VEC)), O_VEC)
    gsmall = jnp.concatenate([gng, _sum_slots(zone, PACK_RB)[O_VEC:]])
    gcw =lax.dynamic_slice_in_dim(gsmall[O_CW:O_CW + L * 3], me * q, q, axis=1)
    gpack = jnp.concatenate([gsmall[:O_CW], _pad_to(gcw.reshape(L * 3 * q // D, D), 16)])

    def pack(ng, lg, lb, sw, sb, pb_, ps, cwv, cb_, fg):
        return _pack_small(ng, jnp.stack([lg, lb, pb_, ps, cb_], axis=1), sb, fg, sw, cwv.reshape(L * 3 * q // D, D))

    wp = pack(norm_g, sgu_ln_g, sgu_ln_b, sgu_w, sgu_b, pool_b, pool_scale, conv_w, conv_b, final_g)
    mp = pack(m_norm_g, m_sgu_ln_g, m_sgu_ln_b, m_sgu_w, m_sgu_b, m_pool_b, m_pool_scale, m_conv_w, m_conv_b, m_final_g)
    vp = pack(v_norm_g, v_sgu_ln_g, v_sgu_ln_b, v_sgu_w, v_sgu_b, v_pool_b, v_pool_scale, v_conv_w, v_conv_b, v_final_g)
    rows = pl.BlockSpec((PACK_RB, D), lambda i: (i, 0))
    sm = _adamw([gpack], rows, wp, mp, vp, rows, _empty4(wp), (N_PACK // PACK_RB,))

    def unpack(a):
        vv = a[O_VEC:O_VEC + L * 5].reshape(L, 5, D)
        sb = a[O_SGB:O_SGB + L].reshape(L, GRP, BLK)
        fg = a[O_FG]
        sw = a[O_SGW:O_SGW + N_SGW].reshape(L, GRP, BLK, BLK)
        cwv = a[O_CW:O_CW + L * 3 * q // D].reshape(L, 3, q)
        return dict(norm_g=a[O_NG:O_NG + L], w_in=None, sgu_ln_g=vv[:, 0], sgu_ln_b=vv[:, 1], sgu_w=sw, sgu_b=sb, pool_w=None,
                    pool_b=vv[:, 2], pool_scale=vv[:, 3], conv_w=cwv, conv_b=vv[:, 4], w_branch_a=None,
                    w_branch_b=None, w_branch_c=None, w_out=None, final_g=fg)

    outs = [loss, dx[None]]
    for kind in range(4):
        d = unpack(sm[kind])
        d.update(w_in=g_w_in[kind], pool_w=g_pw[kind], w_branch_a=g_br[0][kind], w_branch_b=g_br[1][kind],
                 w_branch_c=g_br[2][kind], w_out=g_br[3][kind])
        outs.extend(d[n] for n in ("norm_g", "w_in", "sgu_ln_g", "sgu_ln_b", "sgu_w", "sgu_b", "pool_w", "pool_b",
                                   "pool_scale", "conv_w", "conv_b", "w_branch_a", "w_branch_b", "w_branch_c", "w_out",
                                   "final_g"))
    return tuple(outs)
```

```python
import functools

import jax
import jax.numpy as jnp
from jax import lax
from jax.experimental import pallas as pl
from jax.experimental.pallas import tpu as pltpu

F32 = jnp.float32
BF = jnp.bfloat16
MESH = pl.DeviceIdType.MESH

D = 1024
L = 4
NSH = 4
SHW = 3 * D
NIN = NSH * SHW
GRP = 8
BLK = 128
NPG = 4
PGW = D // NPG
POOL_WINDOWS = (2, 4, 8, 16)
HALO = 16
RMS_EPS = 1e-6
LN_EPS = 1e-5
ADAM_LR, ADAM_B1, ADAM_B2, ADAM_EPS, ADAM_WD, ADAM_STEP = 0.001, 0.9, 0.999, 1e-8, 0.01, 10

RC = 16
CW = 512
COLS = tuple(range(0, D, CW))
MIB = 1 << 20

PK_BR = SHW
PK_PW = PK_BR + D
PK_CW = PK_PW + NPG * (PGW // NSH) * PGW // D
PK_ROWS = PK_CW + 64
PK_HALF = PK_ROWS // 2

_C0 = 0.7978845608028654
_C1 = 0.044715


def _pcall(body, **kw):
    return pl.pallas_call(body, **kw)


def _params(sem, vmem_mib):
    return pltpu.CompilerParams(dimension_semantics=sem, vmem_limit_bytes=vmem_mib * MIB)


def _mm_nt(a, b):
    return lax.dot_general(a.astype(BF), b.astype(BF), (((1,), (1,)), ((), ())), preferred_element_type=F32)


def _mm_tn(a, b):
    return lax.dot_general(a.astype(BF), b.astype(BF), (((0,), (0,)), ((), ())), preferred_element_type=F32)


def _gelu(x):
    return 0.5 * x * (1.0 + jnp.tanh(_C0 * x * (1.0 + _C1 * x * x)))


def _gelu_fg(x):
    x2 = x * x
    t = jnp.tanh(_C0 * x * (1.0 + _C1 * x2))
    f = 0.5 * x * (1.0 + t)
    g = 0.5 * (1.0 + t) + 0.5 * x * (1.0 - t * t) * (_C0 * (1.0 + 3.0 * _C1 * x2))
    return f, g


def _sigmoid(x):
    return 0.5 * jnp.tanh(0.5 * x) + 0.5


def _silu(x):
    return x * _sigmoid(x)


def _silu_fg(x):
    s = _sigmoid(x)
    f = x * s
    return f, s + (f - f * s)


def _rowsum(v):
    return jnp.sum(v, axis=1, keepdims=True)


def _chunks(n_rows, fn, unroll=2):
    def body(c, carry):
        fn(pl.multiple_of(c * RC, RC))
        return carry
    lax.fori_loop(0, n_rows // RC, body, 0, unroll=unroll)


def _trail(e, win):
    s, sh = e, 1
    while sh < win:
        s = s + pltpu.roll(s, sh, 0)
        sh *= 2
    return s


def _lead(e, win):
    n = e.shape[0]
    s, sh = e, 1
    while sh < win:
        s = s + pltpu.roll(s, n - sh, 0)
        sh *= 2
    return s


def _inv_count(pos0, win):
    pos = pos0 + lax.broadcasted_iota(jnp.int32, (RC, 1), 0)
    return 1.0 / jnp.minimum(pos + 1, win).astype(F32)


def _masked_sgu(sgw_ref, g):
    r = lax.broadcasted_iota(jnp.int32, (BLK, BLK), 0) // 64
    c = lax.broadcasted_iota(jnp.int32, (BLK, BLK), 1) // 64
    return jnp.where(c <= r, sgw_ref[g], 0.0)


class _AsF32:
    def __init__(self, ref, col0=0):
        self.ref, self.col0 = ref, col0

    def __getitem__(self, idx):
        if self.col0:
            rows, cols = idx
            idx = (rows, slice(cols.start + self.col0, cols.stop + self.col0))
        return self.ref[idx].astype(F32)


class _From:
    def __init__(self, ref, lead0):
        self.ref, self.lead0 = ref, lead0

    def __setitem__(self, idx, val):
        self.ref[(self.lead0 + idx[0],) + tuple(idx[1:])] = val


def _tile(t, want):
    return min(t, want)


def _fwd_in(x, ng, wl, lng, lnb, sgw, sgbT, pwb, pb, psc, cw, cb):
    T = x.shape[0]
    tm = _tile(T, 256)
    nT = T // tm

    def body(x_ref, xn_ref, ng_ref, w_hbm, lng_ref, lnb_ref, sgw_ref, sgbT_ref, pw_ref, pb_ref, psc_ref, cw_ref, cb_ref,
             proj_hbm, h_ref, y_ref, w_s, pja, pjb, pjc, pjd, qa, qb, qc, qd, h_s, hn_s, vn_s, mix_s, d_s, wm_s, extp_s,
             extc_s, wsem, psem):
        i = pl.program_id(0)
        pj = (pja, pjb, pjc, pjd)
        kept = (qa, qb, qc, qd)

        def w_copy(p):
            return pltpu.make_async_copy(w_hbm.at[p, pl.ds(0, SHW), :], w_s.at[p], wsem.at[p])

        def p_copy(p, tile=None):
            t0 = (i if tile is None else tile) * tm
            dst = proj_hbm.at[pl.ds(pl.multiple_of(t0, tm), tm), pl.ds(p * SHW, SHW)]
            return pltpu.make_async_copy(kept[p], dst, psem.at[p])

        def keep(p):
            kept[p][...] = pj[p][...].astype(BF)
            p_copy(p).start()

        def all_rows(fn):
            for c in range(tm // RC):
                fn(c * RC)

        def buffer_free(p):
            @pl.when(i == 0)
            def _():
                w_copy(p).wait()

            @pl.when(i > 0)
            def _():
                p_copy(p).wait()

        def project(p, blocks, h=None):
            for b in blocks:
                pj[p][:, b * D:(b + 1) * D] = jnp.dot((h_s if h is None else h)[...], w_s[p, b * D:(b + 1) * D, :],
                                                      preferred_element_type=F32)

        @pl.when(i == 0)
        def _():
            for p in range(NSH):
                w_copy(p).start()
            extp_s[0:HALO, :] = jnp.zeros((HALO, D), F32)
            extc_s[0:HALO, :] = jnp.zeros((HALO, D), F32)
            for g in range(GRP):
                wm_s[g] = _masked_sgu(sgw_ref, g).astype(BF)

        def norm_rows(src, dst, r0):
            rows = pl.ds(r0, RC)
            xs = [src[rows, c0:c0 + CW] for c0 in COLS]
            ms = sum(_rowsum(v * v) for v in xs) * (1.0 / D)
            r = lax.rsqrt(ms + RMS_EPS)
            for k, c0 in enumerate(COLS):
                dst[rows, c0:c0 + CW] = (xs[k] * r * ng_ref[:, c0:c0 + CW]).astype(BF)

        def gating(first, rest):
            proj_ref = pj[0]
            first()

            def f1(r0):
                rows = pl.ds(r0, RC)
                gv = [_gelu(proj_ref[rows, D + c0:D + c0 + CW]) for c0 in COLS]
                mu = sum(_rowsum(v) for v in gv) * (1.0 / D)
                dv = [v - mu for v in gv]
                var = sum(_rowsum(v * v) for v in dv) * (1.0 / D)
                rstd = lax.rsqrt(var + LN_EPS)
                for k, c0 in enumerate(COLS):
                    vn_s[rows, c0:c0 + CW] = (dv[k] * rstd * lng_ref[:, c0:c0 + CW] + lnb_ref[:, c0:c0 + CW]).astype(BF)
            all_rows(f1)
            for g in range(GRP):
                w = wm_s[g]
                bcol = sgbT_ref[:, g:g + 1]
                gc = slice(g * BLK, (g + 1) * BLK)
                for n in range(tm // BLK):
                    rr = slice(n * BLK, (n + 1) * BLK)
                    mix_s[rr, gc] = jnp.dot(w, vn_s[rr, gc], preferred_element_type=F32) + bcol
            rest()

            def f2(r0):
                rows = pl.ds(r0, RC)
                for c0 in COLS:
                    au = proj_ref[rows, c0:c0 + CW]
                    az = proj_ref[rows, 2 * D + c0:2 * D + c0 + CW]
                    y_ref[0, rows, c0:c0 + CW] = (_gelu(au) * mix_s[rows, c0:c0 + CW] * _silu(az)).astype(BF)
            all_rows(f2)

        def pooling(first, rest):
            proj_ref = pj[1]
            first()
            extp_s[HALO:HALO + tm, :] = proj_ref[:, 0:D]

            def f1(r0):
                rows = pl.ds(r0, RC)
                e = extp_s[pl.ds(r0, 2 * RC), :]
                for k, win in enumerate(POOL_WINDOWS):
                    gc = slice(k * PGW, (k + 1) * PGW)
                    s = _trail(e[:, gc], win)[RC:2 * RC]
                    d_s[rows, gc] = (s * _inv_count(i * tm + r0, win) - e[RC:2 * RC, gc]).astype(BF)
            all_rows(f1)
            extp_s[0:HALO, :] = extp_s[tm:tm + HALO, :]
            for k in range(NPG):
                gc = slice(k * PGW, (k + 1) * PGW)
                mix_s[:, gc] = jnp.dot(d_s[:, gc], pw_ref[k], preferred_element_type=F32)
            rest()

            def f2(r0):
                rows = pl.ds(r0, RC)
                for c0 in COLS:
                    cs = slice(c0, c0 + CW)
                    bz = proj_ref[rows, D + c0:D + c0 + CW]
                    y_ref[1, rows, cs] = ((mix_s[rows, cs] + pb_ref[:, cs]) * psc_ref[:, cs] * _silu(bz)).astype(BF)
            all_rows(f2)

        def convolution(first, rest):
            proj_ref = pj[2]
            first()
            all_rows(functools.partial(norm_rows, xn_ref, hn_s))

            def f1(r0):
                rows = pl.ds(r0, RC)
                extc_s[pl.ds(HALO + r0, RC), :] = proj_ref[rows, D:2 * D] * pj[1][rows, 2 * D:3 * D]
            all_rows(f1)
            rest()
            project(0, (0, 1, 2), hn_s)

            def f2(r0):
                rows = pl.ds(r0, RC)
                for c0 in COLS:
                    cs = slice(c0, c0 + CW)
                    e = extc_s[pl.ds(r0, 2 * RC), cs]
                    conv = (cw_ref[2:3, cs] * e + cw_ref[1:2, cs] * pltpu.roll(e, 1, 0)
                            + cw_ref[0:1, cs] * pltpu.roll(e, 2, 0))[RC:2 * RC] + cb_ref[:, cs]
                    cbv = proj_ref[rows, cs]
                    cz = proj_ref[rows, 2 * D + c0:2 * D + c0 + CW]
                    y_ref[2, rows, cs] = (cbv * conv * _silu(cz)).astype(BF)
            all_rows(f2)
            extc_s[0:HALO, :] = extc_s[tm:tm + HALO, :]

        @pl.when(i == 0)
        def _():
            all_rows(functools.partial(norm_rows, x_ref, h_s))
            w_copy(0).wait()
            project(0, (0, 1, 2))
            keep(0)

        @pl.when(i > 0)
        def _():
            h_s[...] = hn_s[...]
        h_ref[...] = h_s[...]

        for p, mixer, early in ((1, gating, 2), (2, pooling, 1), (3, convolution, 1)):
            buffer_free(p)
            if p == 3:
                p_copy(0).wait()
            mixer(functools.partial(project, p, range(early)), functools.partial(project, p, range(early, 3)))
            keep(p)
        kept[0][...] = pj[0][...].astype(BF)

        @pl.when(i < nT - 1)
        def _():
            p_copy(0, i + 1).start()

        @pl.when(i == nT - 1)
        def _():
            for p in range(1, NSH):
                p_copy(p).wait()

    vec = pl.BlockSpec((1, D), lambda i: (0, 0))
    return _pcall(
        body, name="fwd_in", grid=(nT,),
        in_specs=[
            pl.BlockSpec((tm, D), lambda i: (i, 0)), pl.BlockSpec((tm, D), lambda i: (jnp.minimum(i + 1, nT - 1), 0)),
            vec, ANY, vec, vec,
            pl.BlockSpec((GRP, BLK, BLK), lambda i: (0, 0, 0)),
            pl.BlockSpec((BLK, GRP), lambda i: (0, 0)),
            pl.BlockSpec((NPG, PGW, PGW), lambda i: (0, 0, 0)),
            vec, vec,
            pl.BlockSpec((3, D), lambda i: (0, 0)), vec,
        ],
        out_specs=[ANY, pl.BlockSpec((tm, D), lambda i: (i, 0)), pl.BlockSpec((3, tm, D), lambda i: (0, i, 0))],
        out_shape=[jax.ShapeDtypeStruct((T, NIN), BF), jax.ShapeDtypeStruct((T, D), BF),
                   jax.ShapeDtypeStruct((3, T, D), BF)],
        scratch_shapes=[
            pltpu.VMEM((NSH, SHW, D), BF),
            pltpu.VMEM((tm, SHW), F32), pltpu.VMEM((tm, SHW), F32), pltpu.VMEM((tm, SHW), F32), pltpu.VMEM((tm, SHW), F32),
            pltpu.VMEM((tm, SHW), BF), pltpu.VMEM((tm, SHW), BF), pltpu.VMEM((tm, SHW), BF), pltpu.VMEM((tm, SHW), BF),
            pltpu.VMEM((tm, D), BF), pltpu.VMEM((tm, D), BF), pltpu.VMEM((tm, D), BF), pltpu.VMEM((tm, D), F32),
            pltpu.VMEM((tm, D), BF), pltpu.VMEM((GRP, BLK, BLK), BF),
            pltpu.VMEM((tm + HALO, D), F32), pltpu.VMEM((tm + HALO, D), F32),
            pltpu.SemaphoreType.DMA((NSH,)), pltpu.SemaphoreType.DMA((NSH,)),
        ],
        compiler_params=_params(("arbitrary",), 60),
    )(x, x, ng, wl, lng, lnb, sgw, sgbT, pwb, pb, psc, cw, cb)


def _branch_spec(which):
    q = D // NSH
    return lambda *g: (0, PK_BR // q + (which(*g) if callable(which) else which), 0)


def _fwd_out(y3, proj, x, wl):
    T = x.shape[0]
    tm = _tile(T, 256)
    q = D // NSH

    nT = T // tm

    def body(y_ref, gl_ref, x_ref, wa_ref, wb_ref, wc_ref, wo_ref, p_ref, mg_ref, xo_ref, even_s, odd_s):
        i = pl.program_id(0)
        gl_ref = _AsF32(gl_ref)

        def branches(buf):
            for k, w_ref in enumerate((wa_ref, wb_ref, wc_ref)):
                buf[k] = jnp.dot(y_ref[k], w_ref[...].reshape(D, D), preferred_element_type=F32)

        def merge(buf):
            for c in range(tm // RC):
                rows = pl.ds(c * RC, RC)
                for c0 in COLS:
                    cs = slice(c0, c0 + CW)
                    m = 0.0
                    for k in range(3):
                        pk = buf[k, rows, cs]
                        p_ref[k, rows, cs] = pk.astype(BF)
                        m = m + _sigmoid(gl_ref[rows, k * D + c0:k * D + c0 + CW]) * pk
                    mg_ref[rows, cs] = m.astype(BF)
            xo_ref[...] = x_ref[...] + jnp.dot(mg_ref[...], wo_ref[...].reshape(D, D), preferred_element_type=F32)

        bufs = (even_s, odd_s)

        @pl.when(i == 0)
        def _():
            branches(bufs[0])

        for par in (0, 1):
            @pl.when((i % 2 == par) & (i > 0) & (i < nT))
            def _():
                branches(bufs[par])
                merge(bufs[1 - par])

        @pl.when(i == nT)
        def _():
            merge(bufs[(nT - 1) % 2])

    prev = lambda i: jnp.maximum(i - 1, 0)
    return _pcall(
        body, name="fwd_out", grid=(nT + 1,),
        in_specs=[
            pl.BlockSpec((3, tm, D), lambda i: (0, jnp.minimum(i, nT - 1), 0)),
            pl.BlockSpec((tm, SHW), lambda i: (prev(i), 3)),
            pl.BlockSpec((tm, D), lambda i: (prev(i), 0)),
        ] + [pl.BlockSpec((NSH, q, D), _branch_spec(k)) for k in range(4)],
        out_specs=[
            pl.BlockSpec((3, tm, D), lambda i: (0, prev(i), 0)),
            pl.BlockSpec((tm, D), lambda i: (prev(i), 0)),
            pl.BlockSpec((tm, D), lambda i: (prev(i), 0)),
        ],
        out_shape=[jax.ShapeDtypeStruct((3, T, D), BF), jax.ShapeDtypeStruct((T, D), BF),
                   jax.ShapeDtypeStruct((T, D), F32)],
        scratch_shapes=[pltpu.VMEM((3, tm, D), F32), pltpu.VMEM((3, tm, D), F32)],
        compiler_params=_params(("arbitrary",), 52),
    )(y3, proj, x, wl, wl, wl, wl)


def _loss_head(x, fg, tgt):
    T = x.shape[0]
    tm = _tile(T, 512)

    def body(x_ref, g_ref, t_ref, dx_ref, sq_ref, dg_ref, acc_s):
        i = pl.program_id(0)

        @pl.when(i == 0)
        def _():
            acc_s[...] = jnp.zeros_like(acc_s)

        def f(r0):
            rows = pl.ds(r0, RC)
            xs = [x_ref[rows, c0:c0 + CW] for c0 in COLS]
            r = lax.rsqrt(sum(_rowsum(v * v) for v in xs) * (1.0 / D) + RMS_EPS)
            xh = [v * r for v in xs]
            dyg, m = [], 0.0
            for k, c0 in enumerate(COLS):
                cs = slice(c0, c0 + CW)
                err = xh[k] * g_ref[:, cs] - t_ref[rows, cs]
                acc_s[0, :, cs] += err * err
                dy = err * (1.0 / D)
                acc_s[1, :, cs] += dy * xh[k]
                dyg.append(dy * g_ref[:, cs])
                m = m + _rowsum(dyg[k] * xh[k])
            m = m * (1.0 / D)
            for k, c0 in enumerate(COLS):
                dx_ref[rows, c0:c0 + CW] = r * (dyg[k] - xh[k] * m)
        _chunks(tm, f, unroll=8)

        @pl.when(i == pl.num_programs(0) - 1)
        def _():
            sq_ref[...] = jnp.sum(acc_s[0], axis=0, keepdims=True)
            dg_ref[...] = jnp.sum(acc_s[1], axis=0, keepdims=True)

    vec = pl.BlockSpec((1, D), lambda i: (0, 0))
    tile = pl.BlockSpec((tm, D), lambda i: (i, 0))
    return _pcall(
        body, name="loss_head", grid=(T // tm,),
        in_specs=[tile, vec, tile], out_specs=[tile, vec, vec],
        out_shape=[jax.ShapeDtypeStruct((T, D), F32), jax.ShapeDtypeStruct((1, D), F32), jax.ShapeDtypeStruct((1, D), F32)],
        scratch_shapes=[pltpu.VMEM((2, RC, D), F32)],
        compiler_params=_params(("arbitrary",), 32),
    )(x, fg, tgt)


def _bwd_mix(dxo, p3, proj, wl, lng, lnb, sgw, sgbT, pwb, pb, psc, cw, cb):
    T = dxo.shape[0]
    tm = _tile(T, 256)
    nT = T // tm
    hb = tm // HALO

    def body(dxo_ref, p_ref, pj2_ref, ch_ref, bpp_ref, ccp_ref, chp_ref, w_ref, w2_ref, w3_ref, lng_ref, lnb_ref, sgw_ref,
             sgbT_ref, pw_ref, pb_ref, psc_ref, cw_ref, cb_ref,
             dpj2_ref, dp_ref, dsw_ref, dsbT_ref, vec_ref, dpw_ref,
             dy_s, dy2_s, dy3_s, dyc_s, ext_s, nxt_s, halo_s, dch_s, xh_s, rstd_s, acc_s, accb_s,
             wm_s, wmT_s, *blk):
        i = pl.program_id(0)
        step = pl.program_id(1)
        ti = nT - 1 - i
        p_ref, ch_ref, bpp_ref, ccp_ref, chp_ref = (_AsF32(r) for r in (p_ref, ch_ref, bpp_ref, ccp_ref, chp_ref))

        @pl.when((i == 0) & (step == 0))
        def _():
            dsw_ref[...] = jnp.zeros_like(dsw_ref)
            dsbT_ref[...] = jnp.zeros_like(dsbT_ref)
            vec_ref[...] = jnp.zeros_like(vec_ref)
            dpw_ref[...] = jnp.zeros_like(dpw_ref)
            halo_s[...] = jnp.zeros_like(halo_s)
            for g in range(GRP):
                wm = _masked_sgu(sgw_ref, g)
                wm_s[g] = wm.astype(BF)
                wmT_s[g] = wm.T.astype(BF)

        def flush(n_acc, rows_of):
            for a in range(n_acc):
                vec_ref[rows_of[a]:rows_of[a] + 1, :] += jnp.sum(acc_s[a], axis=0, keepdims=True)

        def all_rows(fn):
            for c in range(tm // RC):
                fn(c * RC)

        @pl.when(step == 0)
        def _():
            pj_ref, dpj_ref = _AsF32(pj2_ref, SHW), _From(dpj2_ref, 3)
            dy_s[...] = _mm_nt(dxo_ref[...], w_ref[...].reshape(D, D))

            def gate(k, r0):
                rows = pl.ds(r0, RC)
                for c0 in COLS:
                    cs = slice(c0, c0 + CW)
                    dm = dy_s[rows, cs]
                    s = _sigmoid(pj_ref[rows, k * D + c0:k * D + c0 + CW])
                    dp_ref[k, rows, cs] = (s * dm).astype(BF)
                    dpj_ref[k, rows, cs] = (dm * p_ref[k, rows, cs] * s * (1.0 - s)).astype(BF)

            all_rows(functools.partial(gate, 2))
            dyc_s[...] = _mm_nt(dp_ref[2], w2_ref[...].reshape(D, D))
            all_rows(functools.partial(gate, 0))
            all_rows(functools.partial(gate, 1))

        @pl.when(step == 0)
        def _():
            pj_ref, dpj_ref = _AsF32(pj2_ref), _From(dpj2_ref, 0)
            dy3_s[...] = _mm_nt(dp_ref[1], w3_ref[...].reshape(D, D))
            acc_s[...] = jnp.zeros_like(acc_s)
            ext_s[0:HALO, :] = jnp.where(ti > 0, ccp_ref[...] * chp_ref[...], 0.0)
            nxt_s[tm:tm + HALO, :] = halo_s[0]

            def f1(r0):
                rows = pl.ds(r0, RC)
                ext_s[pl.ds(HALO + r0, RC), :] = pj_ref[rows, D:2 * D] * ch_ref[rows, :]
            all_rows(f1)

            def f2(r0):
                rows = pl.ds(r0, RC)
                for c0 in COLS:
                    cs = slice(c0, c0 + CW)
                    e = ext_s[pl.ds(r0, 2 * RC), cs]
                    e0, e1, e2 = e[RC:2 * RC], pltpu.roll(e, 1, 0)[RC:2 * RC], pltpu.roll(e, 2, 0)[RC:2 * RC]
                    conv = cw_ref[2:3, cs] * e0 + cw_ref[1:2, cs] * e1 + cw_ref[0:1, cs] * e2 + cb_ref[:, cs]
                    cbv = pj_ref[rows, cs]
                    sz, sg = _silu_fg(pj_ref[rows, 2 * D + c0:2 * D + c0 + CW])
                    dyc = dyc_s[rows, cs]
                    dconv = dyc * cbv * sz
                    dpj_ref[0, rows, cs] = (dyc * conv * sz).astype(BF)
                    dpj_ref[2, rows, cs] = (dyc * cbv * conv * sg).astype(BF)
                    nxt_s[rows, cs] = dconv
                    acc_s[0, :, cs] += dconv
                    acc_s[1, :, cs] += dconv * e2
                    acc_s[2, :, cs] += dconv * e1
                    acc_s[3, :, cs] += dconv * e0
            all_rows(f2)

            def f3(r0):
                rows = pl.ds(r0, RC)
                for c0 in COLS:
                    cs = slice(c0, c0 + CW)
                    e = nxt_s[pl.ds(r0, 2 * RC), cs]
                    dcc = (cw_ref[2:3, cs] * e + cw_ref[1:2, cs] * pltpu.roll(e, 2 * RC - 1, 0)
                           + cw_ref[0:1, cs] * pltpu.roll(e, 2 * RC - 2, 0))[0:RC]
                    dpj_ref[1, rows, cs] = (dcc * ch_ref[rows, cs]).astype(BF)
                    dch_s[rows, cs] = dcc * pj_ref[rows, D + c0:D + c0 + CW]
            all_rows(f3)
            halo_s[0] = nxt_s[0:HALO, :]
            flush(4, (4, 5, 6, 7))

        @pl.when(step == 1)
        def _():
            pj_ref, dpj_ref = _AsF32(pj2_ref, SHW), _From(dpj2_ref, 3)
            dy2_s[...] = _mm_nt(dp_ref[0], w2_ref[...].reshape(D, D))
            acc_s[...] = jnp.zeros_like(acc_s)
            ext_s[0:HALO, :] = jnp.where(ti > 0, bpp_ref[...], 0.0)
            ext_s[HALO:HALO + tm, :] = pj_ref[:, 0:D]
            nxt_s[tm:tm + HALO, :] = halo_s[1]

            nblk = tm // BLK
            d_b, t_b, dy0_b = blk[:nblk], blk[nblk:2 * nblk], blk[2 * nblk:]

            def block_rows(fn, n):
                for c in range(BLK // RC):
                    fn(n, c * RC)

            def f1(n, b0):
                r0, brow = n * BLK + b0, pl.ds(b0, RC)
                e = ext_s[pl.ds(r0, 2 * RC), :]
                for k, win in enumerate(POOL_WINDOWS):
                    gc = slice(k * PGW, (k + 1) * PGW)
                    s = _trail(e[:, gc], win)[RC:2 * RC]
                    d_b[n][brow, gc] = (s * _inv_count(ti * tm + r0, win) - e[RC:2 * RC, gc]).astype(BF)

            def pool(n):
                for k in range(NPG):
                    gc = slice(k * PGW, (k + 1) * PGW)
                    t_b[n][:, gc] = jnp.dot(d_b[n][:, gc], pw_ref[k], preferred_element_type=F32)

            def f2(n, b0):
                rows, brow = pl.ds(n * BLK + b0, RC), pl.ds(b0, RC)
                for c0 in COLS:
                    cs = slice(c0, c0 + CW)
                    y0 = t_b[n][brow, cs] + pb_ref[:, cs]
                    sz, sg = _silu_fg(pj_ref[rows, D + c0:D + c0 + CW])
                    dyb = dy3_s[rows, cs]
                    dy0 = dyb * psc_ref[:, cs] * sz
                    acc_s[0, :, cs] += dy0
                    acc_s[1, :, cs] += dyb * y0 * sz
                    dy0_b[n][brow, cs] = dy0.astype(BF)
                    dpj_ref[1, rows, cs] = (dyb * y0 * psc_ref[:, cs] * sg).astype(BF)

            def pool_back(n):
                for k in range(NPG):
                    gc = slice(k * PGW, (k + 1) * PGW)
                    dpw_ref[k] += _mm_tn(d_b[n][:, gc], dy0_b[n][:, gc])
                    t_b[n][:, gc] = _mm_nt(dy0_b[n][:, gc], pw_ref[k])

            def f3(n, b0):
                r0, brow = n * BLK + b0, pl.ds(b0, RC)
                for k, win in enumerate(POOL_WINDOWS):
                    gc = slice(k * PGW, (k + 1) * PGW)
                    nxt_s[pl.ds(r0, RC), gc] = t_b[n][brow, gc] * _inv_count(ti * tm + r0, win)

            def f4(n, b0):
                r0, brow = n * BLK + b0, pl.ds(b0, RC)
                rows = pl.ds(r0, RC)
                e = nxt_s[pl.ds(r0, 2 * RC), :]
                for k, win in enumerate(POOL_WINDOWS):
                    gc = slice(k * PGW, (k + 1) * PGW)
                    dpj_ref[0, rows, gc] = (_lead(e[:, gc], win)[0:RC] - t_b[n][brow, gc]).astype(BF)
                dpj_ref[2, rows, :] = dch_s[rows, :].astype(BF)

            def spread(n):
                block_rows(f3, n)
                block_rows(f4, n)

            stages = (functools.partial(block_rows, f1), pool, functools.partial(block_rows, f2), pool_back, spread)
            for t in range(nblk + len(stages) - 1):
                for s in reversed(range(len(stages))):
                    if 0 <= t - s < nblk:
                        stages[s](nblk - 1 - (t - s))
            halo_s[1] = nxt_s[0:HALO, :]
            flush(2, (2, 3))

        @pl.when(step == 1)
        def _():
            pj_ref, dpj_ref = _AsF32(pj2_ref), _From(dpj2_ref, 0)
            acc_s[...] = jnp.zeros_like(acc_s)
            accb_s[...] = jnp.zeros_like(accb_s)
            nblk = tm // BLK
            vn_b, t_b, dmix_b = blk[:nblk], blk[nblk:2 * nblk], blk[2 * nblk:]

            def block_rows(fn, n):
                for c in range(BLK // RC):
                    fn(n, c * RC)

            def f1(n, b0):
                rows, brow = pl.ds(n * BLK + b0, RC), pl.ds(b0, RC)
                gv = [_gelu(pj_ref[rows, D + c0:D + c0 + CW]) for c0 in COLS]
                mu = sum(_rowsum(v) for v in gv) * (1.0 / D)
                dv = [v - mu for v in gv]
                var = sum(_rowsum(v * v) for v in dv) * (1.0 / D)
                rstd = lax.rsqrt(var + LN_EPS)
                rstd_s[rows, :] = jnp.broadcast_to(rstd, (RC, BLK))
                for k, c0 in enumerate(COLS):
                    cs = slice(c0, c0 + CW)
                    xh = dv[k] * rstd
                    xh_s[rows, cs] = xh
                    vn_b[n][brow, cs] = (xh * lng_ref[:, cs] + lnb_ref[:, cs]).astype(BF)

            def mix(n):
                for g in range(GRP):
                    gc = slice(g * BLK, (g + 1) * BLK)
                    t_b[n][:, gc] = jnp.dot(wm_s[g], vn_b[n][:, gc], preferred_element_type=F32) + sgbT_ref[:, g:g + 1]

            def f2(n, b0):
                rows, brow = pl.ds(n * BLK + b0, RC), pl.ds(b0, RC)
                for c0 in COLS:
                    cs = slice(c0, c0 + CW)
                    gu, ggu = _gelu_fg(pj_ref[rows, cs])
                    sz, sg = _silu_fg(pj_ref[rows, 2 * D + c0:2 * D + c0 + CW])
                    dya = dy2_s[rows, cs]
                    mx = t_b[n][brow, cs]
                    dmix = dya * gu * sz
                    dpj_ref[0, rows, cs] = (dya * mx * sz * ggu).astype(BF)
                    dpj_ref[2, rows, cs] = (dya * gu * mx * sg).astype(BF)
                    dmix_b[n][brow, cs] = dmix.astype(BF)
                    accb_s[brow, cs] += dmix

            def mix_back(n):
                for g in range(GRP):
                    gc = slice(g * BLK, (g + 1) * BLK)
                    t_b[n][:, gc] = jnp.dot(wmT_s[g], dmix_b[n][:, gc], preferred_element_type=F32)
                    dsw_ref[g] += _mm_nt(dmix_b[n][:, gc], vn_b[n][:, gc])

            def f3(n, b0):
                rows, brow = pl.ds(n * BLK + b0, RC), pl.ds(b0, RC)
                rstd = rstd_s[rows, 0:1]
                dxh, m1, m2 = [], 0.0, 0.0
                for k, c0 in enumerate(COLS):
                    cs = slice(c0, c0 + CW)
                    dvn = t_b[n][brow, cs]
                    xh = xh_s[rows, cs]
                    acc_s[0, :, cs] += dvn * xh
                    acc_s[1, :, cs] += dvn
                    dxh.append(dvn * lng_ref[:, cs])
                    m1 = m1 + _rowsum(dxh[k])
                    m2 = m2 + _rowsum(dxh[k] * xh)
                m1 = m1 * (1.0 / D)
                m2 = m2 * (1.0 / D)
                for k, c0 in enumerate(COLS):
                    cs = slice(c0, c0 + CW)
                    _, ggv = _gelu_fg(pj_ref[rows, D + c0:D + c0 + CW])
                    dpj_ref[1, rows, cs] = (rstd * (dxh[k] - m1 - xh_s[rows, cs] * m2) * ggv).astype(BF)

            stages = (functools.partial(block_rows, f1), mix, functools.partial(block_rows, f2), mix_back,
                      functools.partial(block_rows, f3))
            for t in range(nblk + len(stages) - 1):
                for s in reversed(range(len(stages))):
                    if 0 <= t - s < nblk:
                        stages[s](t - s)
            for g in range(GRP):
                dsbT_ref[:, g:g + 1] += _rowsum(accb_s[:, g * BLK:(g + 1) * BLK])
            flush(2, (0, 1))

            @pl.when(i == nT - 1)
            def _():
                for g in range(GRP):
                    r = lax.broadcasted_iota(jnp.int32, (BLK, BLK), 0) // 64
                    c = lax.broadcasted_iota(jnp.int32, (BLK, BLK), 1) // 64
                    dsw_ref[g] = jnp.where(c <= r, dsw_ref[g], 0.0)

    def prev(col):
        return pl.BlockSpec((HALO, D), lambda i, p: (jnp.maximum((nT - 1 - i) * hb - 1, 0), col))

    vec = pl.BlockSpec((1, D), lambda i, p: (0, 0))
    const3 = lambda i, p: (0, 0, 0)
    return _pcall(
        body, name="bwd_mix", grid=(nT, 2),
        in_specs=[
            pl.BlockSpec((tm, D), lambda i, p: (nT - 1 - i, 0)),
            pl.BlockSpec((3, tm, D), lambda i, p: (0, nT - 1 - i, 0)),
            pl.BlockSpec((tm, 2 * SHW), lambda i, p: (nT - 1 - i, 1 - p)),
            pl.BlockSpec((tm, D), lambda i, p: (nT - 1 - i, 5)),
            prev(3), prev(7), prev(5),
            pl.BlockSpec((NSH, D // NSH, D), _branch_spec(3)),
            pl.BlockSpec((NSH, D // NSH, D), _branch_spec(lambda i, p: 2 - 2 * p)),
            pl.BlockSpec((NSH, D // NSH, D), _branch_spec(1)),
            vec, vec,
            pl.BlockSpec((GRP, BLK, BLK), const3),
            pl.BlockSpec((BLK, GRP), lambda i, p: (0, 0)),
            pl.BlockSpec((NPG, PGW, PGW), const3),
            vec, vec,
            pl.BlockSpec((3, D), lambda i, p: (0, 0)), vec,
        ],
        out_specs=[
            pl.BlockSpec((6, tm, D), lambda i, p: (1 - p, nT - 1 - i, 0)),
            pl.BlockSpec((3, tm, D), lambda i, p: (0, nT - 1 - i, 0)),
            pl.BlockSpec((GRP, BLK, BLK), const3),
            pl.BlockSpec((BLK, GRP), lambda i, p: (0, 0)),
            pl.BlockSpec((8, D), lambda i, p: (0, 0)),
            pl.BlockSpec((NPG, PGW, PGW), const3),
        ],
        out_shape=[
            jax.ShapeDtypeStruct((12, T, D), BF), jax.ShapeDtypeStruct((3, T, D), BF),
            jax.ShapeDtypeStruct((GRP, BLK, BLK), F32), jax.ShapeDtypeStruct((BLK, GRP), F32),
            jax.ShapeDtypeStruct((8, D), F32), jax.ShapeDtypeStruct((NPG, PGW, PGW), F32),
        ],
        scratch_shapes=[
            pltpu.VMEM((tm, D), F32),
            pltpu.VMEM((tm, D), F32),
            pltpu.VMEM((tm, D), F32),
            pltpu.VMEM((tm, D), F32),
            pltpu.VMEM((tm + HALO, D), F32),
            pltpu.VMEM((tm + HALO, D), F32),
            pltpu.VMEM((2, HALO, D), F32),
            pltpu.VMEM((tm, D), F32),
            pltpu.VMEM((tm, D), F32),
            pltpu.VMEM((tm, BLK), F32),
            pltpu.VMEM((4, RC, D), F32),
            pltpu.VMEM((BLK, D), F32),
            pltpu.VMEM((GRP, BLK, BLK), BF), pltpu.VMEM((GRP, BLK, BLK), BF),
        ] + [pltpu.VMEM((BLK, D), dt) for dt in (BF, F32, BF) for _ in range(tm // BLK)],
        compiler_params=_params(("arbitrary", "arbitrary"), 56),
    )(dxo, p3, proj, proj, proj, proj, proj, wl, wl, wl, lng, lnb, sgw, sgbT, pwb, pb, psc, cw, cb)


def _bwd_in(dproj, wl, x, dxo, ng):
    T = x.shape[0]
    tm = _tile(T, 256)
    nT = T // tm

    def body(dpj_ref, w_hbm, x_ref, dxo_ref, ng_ref, dx_ref, dng_ref, w_s, even_s, odd_s, g_s, wsem):
        i = pl.program_id(0)
        bufs = (even_s, odd_s)

        def w_copy(j):
            return pltpu.make_async_copy(w_hbm.at[j, pl.ds(0, SHW), :], w_s.at[j], wsem.at[j])

        def d_h():
            return sum(_mm_nt(dpj_ref[3 * j + b], w_s[j, b * D:(b + 1) * D, :]) for j in range(NSH) for b in range(3))

        def finish(prev):
            for c in range(tm // RC):
                finish_rows(prev, c * RC)

        def finish_rows(prev, r0):
            rows = pl.ds(r0, RC)
            xs = [x_ref[rows, c0:c0 + CW] for c0 in COLS]
            r = lax.rsqrt(sum(_rowsum(v * v) for v in xs) * (1.0 / D) + RMS_EPS)
            xh = [v * r for v in xs]
            dhg, m = [], 0.0
            for k, c0 in enumerate(COLS):
                cs = slice(c0, c0 + CW)
                dh = prev[rows, cs]
                g_s[:, cs] += dh * xh[k]
                dhg.append(dh * ng_ref[:, cs])
                m = m + _rowsum(dhg[k] * xh[k])
            m = m * (1.0 / D)
            for k, c0 in enumerate(COLS):
                cs = slice(c0, c0 + CW)
                dx_ref[rows, cs] = dxo_ref[rows, cs] + r * (dhg[k] - xh[k] * m)

        @pl.when(i == 0)
        def _():
            for j in range(NSH):
                w_copy(j).start()
            g_s[...] = jnp.zeros_like(g_s)
            for j in range(NSH):
                w_copy(j).wait()
            bufs[0][...] = d_h()

        for par in (0, 1):
            @pl.when((i % 2 == par) & (i > 0) & (i < nT))
            def _():
                bufs[par][...] = d_h()
                finish(bufs[1 - par])

        @pl.when(i == nT)
        def _():
            finish(bufs[(nT - 1) % 2])
            dng_ref[...] = jnp.sum(g_s[...], axis=0, keepdims=True)

    vec = pl.BlockSpec((1, D), lambda i: (0, 0))
    tile = pl.BlockSpec((tm, D), lambda i: (jnp.maximum(i - 1, 0), 0))
    return _pcall(
        body, name="bwd_in", grid=(nT + 1,),
        in_specs=[pl.BlockSpec((3 * NSH, tm, D), lambda i: (0, jnp.minimum(i, nT - 1), 0)), ANY, tile, tile, vec],
        out_specs=[tile, vec],
        out_shape=[jax.ShapeDtypeStruct((T, D), F32), jax.ShapeDtypeStruct((1, D), F32)],
        scratch_shapes=[pltpu.VMEM((NSH, SHW, D), BF), pltpu.VMEM((tm, D), F32), pltpu.VMEM((tm, D), F32),
                        pltpu.VMEM((RC, D), F32), pltpu.SemaphoreType.DMA((NSH,))],
        compiler_params=_params(("arbitrary",), 52),
    )(dproj, wl, x, dxo, ng)


def _tn_grad(a3, b3, split, out_map, name, into=None, after=None):
    nb, T, _ = b3.shape
    tk = _tile(T, 4096 if b3.dtype == BF and split == 1 else 2048)
    nk = T // tk
    rows = D // split
    a_batched = a3.shape[0] > 1

    def body(a_ref, b_ref, *rest):
        o_ref, acc_s = rest[-2:]
        k = pl.program_id(1)

        @pl.when(k == 0)
        def _():
            acc_s[...] = _mm_tn(a_ref[...], b_ref[...])

        @pl.when((k > 0) & (k < nk))
        def _():
            acc_s[...] += _mm_tn(a_ref[...], b_ref[...])

        @pl.when(k >= nk - 1)
        def _():
            r0 = pl.multiple_of((k - (nk - 1)) * rows, rows)
            o_ref[...] = acc_s[pl.ds(r0, rows), :].astype(o_ref.dtype)

    def tok(k):
        return jnp.minimum(k, nk - 1)

    extra = ([] if into is None else [into]) + ([] if after is None else [after])
    return _pcall(
        body, name=name, grid=(nb, nk + split - 1),
        in_specs=[
            pl.BlockSpec((None, tk, D), (lambda n, k: (n, tok(k), 0)) if a_batched else (lambda n, k: (0, tok(k), 0))),
            pl.BlockSpec((None, tk, D), lambda n, k: (n, tok(k), 0)),
        ] + [ANY] * len(extra),
        out_specs=pl.BlockSpec((None, rows, D), lambda n, k: out_map(n, jnp.maximum(k - (nk - 1), 0))),
        out_shape=jax.ShapeDtypeStruct((NSH, PK_ROWS, D), BF),
        input_output_aliases={} if into is None else {2: 0},
        scratch_shapes=[pltpu.VMEM((D, D), F32)],
        compiler_params=_params(("arbitrary", "arbitrary"), 56),
    )(a3, b3, *extra)


def _place():
    x, y, c = lax.axis_index("x"), lax.axis_index("y"), lax.axis_index("c")
    chips = [(1 - x, y), (x, 1 - y), (1 - x, 1 - y)]
    return x, y, c, chips


def _peers(reach):
    x, y, c, chips = _place()
    if reach == "chips":
        return 2 * x + y, [((px, py, c), 2 * px + py) for px, py in chips]
    others = [(x, y, 1 - c)] + [(px, py, pc) for px, py in chips for pc in (c, 1 - c)]
    return 4 * x + 2 * y + c, [(pr, 4 * pr[0] + 2 * pr[1] + pr[2]) for pr in others]


ANY = pl.BlockSpec(memory_space=pl.ANY)


HBM = pl.BlockSpec(memory_space=pltpu.HBM)
SEM = pl.BlockSpec(memory_space=pltpu.SEMAPHORE)
EFFECT = pltpu.SideEffectType.DATAFLOW_SIDE_EFFECTING


def _own_slot(src, from_slot, name):
    rows = src.shape[-2]
    rb = rows // 8
    me = (2 * lax.axis_index("x") + lax.axis_index("y")).astype(jnp.int32).reshape(1)

    def body(me_ref, src_ref, land_ref):
        land_ref[...] = src_ref[...]

    if from_slot:
        src_spec = pl.BlockSpec((None, rb, D), lambda i, me_ref: (me_ref[0], i, 0))
    else:
        src_spec = pl.BlockSpec((rb, D), lambda i, me_ref: (i, 0))
    return _pcall(
        body, name=name,
        grid_spec=pltpu.PrefetchScalarGridSpec(
            num_scalar_prefetch=1, grid=(rows // rb,), in_specs=[src_spec],
            out_specs=pl.BlockSpec((None, rb, D), lambda i, me_ref: (me_ref[0], i, 0))),
        out_shape=jax.ShapeDtypeStruct((NSH, rows, D), src.dtype),
        compiler_params=_params(("arbitrary",), 32),
    )(me, src)


def _my_half():
    return pl.ds(pl.multiple_of(lax.axis_index("c") * PK_HALF, PK_HALF), PK_HALF)


def _push_start(name, srcs, lands, per_peer, reach="chips", halved=()):
    n = len(srcs)
    npeer = 3 if reach == "chips" else 7
    ns = n * npeer

    def body(*refs):
        src, land = refs[:n], refs[n:2 * n]
        ssem, rsem = refs[2 * n:2 * n + ns], refs[2 * n + ns:2 * n + 2 * ns]
        token = refs[-1]
        me, peers = _peers(reach)
        for i in range(n):
            for j, (peer, slot) in enumerate(peers):
                s = src[i].at[slot] if per_peer else src[i]
                d = land[i].at[me]
                if i in halved:
                    s, d = s.at[_my_half()], d.at[_my_half()]
                pltpu.make_async_remote_copy(
                    src_ref=s, dst_ref=d, send_sem=ssem[npeer * i + j], recv_sem=rsem[npeer * i + j], device_id=peer,
                    device_id_type=MESH).start()
        token[...] = jnp.zeros_like(token)

    ops = list(srcs) + list(lands)
    out = _pcall(
        body, name=name,
        out_shape=tuple([pltpu.SemaphoreType.DMA(())] * (2 * ns) + [pltpu.HBM(a.shape, a.dtype) for a in ops]
                        + [jax.ShapeDtypeStruct((8, 128), F32)]),
        in_specs=[HBM] * (2 * n),
        out_specs=tuple([SEM] * (2 * ns) + [HBM] * (2 * n) + [pl.BlockSpec(memory_space=pltpu.VMEM)]),
        input_output_aliases={i: 2 * ns + i for i in range(2 * n)},
        compiler_params=pltpu.CompilerParams(has_side_effects=EFFECT),
    )(*[pltpu.with_memory_space_constraint(a, pltpu.HBM) for a in ops])
    return out[:ns], out[ns:2 * ns], out[2 * ns:2 * ns + n], out[2 * ns + n:2 * ns + 2 * n], out[-1]


def _push_wait(name, src, land, ssem, rsem, after, per_peer, reach="chips", halved=False):
    npeer = len(ssem)

    def body(src_ref, land_ref, *rest):
        sems = rest[:2 * npeer]
        _, peers = _peers(reach)
        for j, (peer, slot) in enumerate(peers):
            s = src_ref.at[slot] if per_peer else src_ref
            d = land_ref.at[slot]
            if halved:
                s, d = s.at[_my_half()], d.at[_my_half()]
            cp = pltpu.make_async_remote_copy(
                src_ref=s, dst_ref=d, send_sem=sems[j],
                recv_sem=sems[npeer + j], device_id=peer, device_id_type=MESH)
            cp.wait_send()
            cp.wait_recv()

    return _pcall(
        body, name=name,
        out_shape=(pltpu.HBM(src.shape, src.dtype), pltpu.HBM(land.shape, land.dtype)),
        in_specs=[HBM, HBM] + [SEM] * (2 * npeer) + [ANY], out_specs=(HBM, HBM),
        input_output_aliases={0: 0, 1: 1},
        compiler_params=pltpu.CompilerParams(has_side_effects=EFFECT),
    )(src, land, *ssem, *rsem, after)[1]


def _share_halves(land):
    def body(land_in, land_ref, ssem, rsem):
        x, y, c, chips = _place()
        other = pl.ds(pl.multiple_of((1 - c) * PK_HALF, PK_HALF), PK_HALF)
        sends = [pltpu.make_async_remote_copy(
            src_ref=land_ref.at[2 * px + py, _my_half()], dst_ref=land_ref.at[2 * px + py, _my_half()],
            send_sem=ssem.at[j], recv_sem=rsem.at[j], device_id=(x, y, 1 - c), device_id_type=MESH)
            for j, (px, py) in enumerate(chips)]
        for cp in sends:
            cp.start()
        for j, (px, py) in enumerate(chips):
            pltpu.make_async_remote_copy(
                src_ref=land_ref.at[2 * px + py, other], dst_ref=land_ref.at[2 * px + py, other], send_sem=ssem.at[j],
                recv_sem=rsem.at[j], device_id=(x, y, 1 - c), device_id_type=MESH).wait_recv()
        for cp in sends:
            cp.wait_send()

    return _pcall(
        body, name="share_halves", in_specs=[ANY], out_specs=ANY,
        out_shape=jax.ShapeDtypeStruct(land.shape, land.dtype), input_output_aliases={0: 0},
        scratch_shapes=[pltpu.SemaphoreType.DMA((3,)), pltpu.SemaphoreType.DMA((3,))],
    )(land)


def _swap_sibling(arrs):
    n = len(arrs)

    def body(*refs):
        src, dst = refs[:n], refs[n:2 * n]
        ssem, rsem = refs[2 * n:]
        x, y, c, _ = _place()
        cps = [pltpu.make_async_remote_copy(src_ref=src[a], dst_ref=dst[a], send_sem=ssem.at[a], recv_sem=rsem.at[a],
                                            device_id=(x, y, 1 - c), device_id_type=MESH) for a in range(n)]
        for cp in cps:
            cp.start()
        for cp in cps:
            cp.wait()

    return _pcall(
        body, name="swap_sibling",
        in_specs=[ANY] * n, out_specs=[ANY] * n,
        out_shape=[jax.ShapeDtypeStruct(a.shape, a.dtype) for a in arrs],
        scratch_shapes=[pltpu.SemaphoreType.DMA((n,)), pltpu.SemaphoreType.DMA((n,))],
    )(*arrs)


def _gather_all(v):
    def body(src, dst, ssem, rsem, lsem):
        x, y, c, _ = _place()
        me = 4 * x + 2 * y + c
        peers = [(x, y, 1 - c), (1 - x, y, c), (1 - x, y, 1 - c), (x, 1 - y, c), (x, 1 - y, 1 - c),
                 (1 - x, 1 - y, c), (1 - x, 1 - y, 1 - c)]
        local = pltpu.make_async_copy(src, dst.at[me], lsem)
        local.start()
        sends = [pltpu.make_async_remote_copy(src_ref=src, dst_ref=dst.at[me], send_sem=ssem.at[j], recv_sem=rsem.at[j],
                                              device_id=pr, device_id_type=MESH) for j, pr in enumerate(peers)]
        for cp in sends:
            cp.start()
        for j, (px, py, pc) in enumerate(peers):
            pltpu.make_async_remote_copy(src_ref=src, dst_ref=dst.at[4 * px + 2 * py + pc], send_sem=ssem.at[j],
                                         recv_sem=rsem.at[j], device_id=(px, py, pc), device_id_type=MESH).wait_recv()
        for cp in sends:
            cp.wait_send()
        local.wait()

    return _pcall(
        body, name="gather_all", in_specs=[ANY], out_specs=ANY,
        out_shape=jax.ShapeDtypeStruct((8,) + v.shape, v.dtype),
        scratch_shapes=[pltpu.SemaphoreType.DMA((7,)), pltpu.SemaphoreType.DMA((7,)), pltpu.SemaphoreType.DMA(())],
    )(v)


def _sum_slots(r, rb, out_dtype=F32):
    S = r.shape[0]

    def body(r_ref, o_ref):
        acc = r_ref[0].astype(F32)
        for s in range(1, S):
            acc = acc + r_ref[s].astype(F32)
        o_ref[...] = acc.astype(out_dtype)

    if r.ndim == 3:
        _, R, C = r.shape
        grid, blk, imap = (R // rb,), (S, rb, C), (lambda i: (0, i, 0))
        oblk, omap = (rb, C), (lambda i: (i, 0))
    else:
        _, K, R, C = r.shape
        grid, blk, imap = (K,), (S, None, R, C), (lambda i: (0, i, 0, 0))
        oblk, omap = (None, R, C), (lambda i: (i, 0, 0))
    return _pcall(
        body, name="sum_slots", grid=grid, in_specs=[pl.BlockSpec(blk, imap)], out_specs=pl.BlockSpec(oblk, omap),
        out_shape=jax.ShapeDtypeStruct(r.shape[1:], out_dtype), compiler_params=_params(("arbitrary",), 48),
    )(r)


def _adamw(gs, g_spec, w, m, v, p_spec, prev, grid):
    ng = len(gs)
    bc1 = 1.0 - ADAM_B1 ** ADAM_STEP
    bc2 = 1.0 - ADAM_B2 ** ADAM_STEP

    def body(*refs):
        g = refs[0][...].astype(F32)
        for a in range(1, ng):
            g = g + refs[a][...].astype(F32)
        w_ref, m_ref, v_ref = refs[ng:ng + 3]
        go, do, mo, vo = refs[ng + 3 + 4:]
        mn = ADAM_B1 * m_ref[...] + (1.0 - ADAM_B1) * g
        vn = ADAM_B2 * v_ref[...] + (1.0 - ADAM_B2) * (g * g)
        go[...] = g
        mo[...] = mn
        vo[...] = vn
        do[...] = -ADAM_LR * ((mn / bc1) / (jnp.sqrt(vn / bc2) + ADAM_EPS) + ADAM_WD * w_ref[...])

    out = jax.ShapeDtypeStruct(w.shape, F32)
    k0 = ng + 3
    return _pcall(
        body, name="adamw", grid=grid,
        in_specs=[g_spec] * ng + [p_spec] * 3 + [ANY] * 4,
        out_specs=[p_spec] * 4, out_shape=[out] * 4,
        input_output_aliases={k0: 0, k0 + 1: 1, k0 + 2: 2, k0 + 3: 3},
        compiler_params=_params(("arbitrary",) * len(grid), 48),
    )(*gs, w, m, v, *prev)


def _empty4(w):
    return tuple(lax.empty(w.shape, F32) for _ in range(4))


N_SGW = L * GRP * BLK * BLK // D
O_NG, O_VEC, O_SGB, O_FG, O_SGW = 0, 8, 32, 40, 48
O_CW = O_SGW + N_SGW
N_PACK = O_CW + 16
PACK_RB = N_PACK // 3


def _pad_to(a, rows):
    return jnp.pad(a, ((0, rows - a.shape[0]), (0, 0)))


def _pack_small(ng, vecs, sgb, fg, sgw, cw):
    parts = [_pad_to(ng, 8), _pad_to(vecs.reshape(L * 5, D), 24), _pad_to(sgb.reshape(L, D), 8),
             _pad_to(fg.reshape(1, D), 8), sgw.reshape(N_SGW, D), _pad_to(cw, 16)]
    return jnp.concatenate(parts, axis=0)


def kernel(x, norm_g, w_in, sgu_ln_g, sgu_ln_b, sgu_w, sgu_b, pool_w, pool_b, pool_scale, conv_w, conv_b, w_branch_a, w_branch_b, w_branch_c, w_out, final_g, loss_target, m_norm_g, m_w_in, m_sgu_ln_g, m_sgu_ln_b, m_sgu_w, m_sgu_b, m_pool_w, m_pool_b, m_pool_scale, m_conv_w, m_conv_b, m_w_branch_a, m_w_branch_b, m_w_branch_c, m_w_out, m_final_g, v_norm_g, v_w_in, v_sgu_ln_g, v_sgu_ln_b, v_sgu_w, v_sgu_b, v_pool_w, v_pool_b, v_pool_scale, v_conv_w, v_conv_b, v_w_branch_a, v_w_branch_b, v_w_branch_c, v_w_out, v_final_g):
    cx, cy = lax.axis_index("x"), lax.axis_index("y")
    me = 2 * cx + cy
    xl, tgt = x[0], loss_target[0]
    q = D // NSH

    wq = w_in.astype(BF).reshape(L, D, 3, D).transpose(0, 2, 1, 3).reshape(L, SHW, D)
    brq = jnp.stack([w_branch_a, w_branch_b, w_branch_c, w_out], axis=1).astype(BF).reshape(L, D, D)
    pwq = pool_w.astype(BF).reshape(L, PK_CW - PK_PW, D)
    cwq = lax.bitcast_convert_type(conv_w, BF).reshape(L, 3 * q * 2)
    cwq = jnp.pad(cwq, ((0, 0), (0, (PK_ROWS - PK_CW) * D - 3 * q * 2))).reshape(L, PK_ROWS - PK_CW, D)
    packs = [jnp.concatenate([wq[l], brq[l], pwq[l], cwq[l]], axis=0) for l in range(L)]
    lands = [_own_slot(packs[l], False, f"ag_own_{l}") for l in range(L)]
    ag_s, ag_r, packs, lands, tok = _push_start("ag_start", packs, lands, False, halved=(0,))
    sgbT = sgu_b.transpose(0, 2, 1)

    def layer_weights(l, after):
        wl = _push_wait(f"ag_wait_{l}", packs[l], lands[l], ag_s[3 * l:3 * l + 3], ag_r[3 * l:3 * l + 3], after, False,
                        halved=l == 0)
        if l == 0:
            wl = _share_halves(wl)
        pwb = wl[:, PK_PW:PK_CW].reshape(NSH, NPG, PGW // NSH, PGW).transpose(1, 0, 2, 3).reshape(NPG, PGW, PGW)
        cwb = wl[:, PK_CW:].reshape(NSH, (PK_ROWS - PK_CW) * D)[:, :3 * q * 2].reshape(NSH, 3, q, 2)
        cwf = lax.bitcast_convert_type(cwb, F32).transpose(1, 0, 2).reshape(3, D)
        small = (sgu_ln_g[l:l + 1], sgu_ln_b[l:l + 1], sgu_w[l], sgbT[l], pwb, pool_b[l:l + 1], pool_scale[l:l + 1],
                 cwf, conv_b[l:l + 1])
        return wl, small

    xs, saved, wts = [xl], [], []
    for l in range(L):
        wl, small = layer_weights(l, tok if l == 0 else xs[l])
        wts.append((wl, small))
        proj, h, y3 = _fwd_in(xs[l], norm_g[l:l + 1], wl, *small)
        p3, mg, xo = _fwd_out(y3, proj, xs[l], wl)
        saved.append((proj, h, y3, p3, mg))
        xs.append(xo)

    dx, sq, dfg = _loss_head(xs[L], final_g[None], tgt)
    loss = lax.psum(jnp.sum(sq) * (0.5 / D), ("x", "y", "c"))

    g_w_in = _empty4(w_in)
    g_br = [_empty4(w_out) for _ in range(4)]
    g_pw = _empty4(pool_w)
    dng, dvec, dsgw, dsgb = [None] * L, [None] * L, [None] * L, [None] * L
    branches = [(w_branch_a, m_w_branch_a, v_w_branch_a), (w_branch_b, m_w_branch_b, v_w_branch_b),
                (w_branch_c, m_w_branch_c, v_w_branch_c), (w_out, m_w_out, v_w_out)]
    nb = D // 128

    def finish(l, landed):
        nonlocal g_w_in, g_pw
        mine = _sum_slots(landed, PK_ROWS // 8, BF)
        sums = [mine, _swap_sibling([mine])[0]]
        g_w_in = _adamw(sums, pl.BlockSpec((128, D), lambda b, i: (b * nb + i, 0)), w_in, m_w_in, v_w_in,
                        pl.BlockSpec((None, 128, D), lambda b, i: (l, i, b)), g_w_in, (3, nb))
        for k, (w, m, v) in enumerate(branches):
            g_br[k] = _adamw(sums, pl.BlockSpec((q, D), lambda i, k=k: (PK_BR // q + k, 0)), w, m, v,
                             pl.BlockSpec((None, q, D), lambda i: (l, 0, 0)), g_br[k], (1,))
        pools = [a[PK_PW:PK_CW].reshape(NPG, PGW // NSH, PGW) for a in sums]
        g_pw = _adamw(pools, pl.BlockSpec((None, PGW // NSH, PGW), lambda g: (g, 0, 0)), pool_w, m_pool_w, v_pool_w,
                      pl.BlockSpec((None, None, PGW // NSH, PGW), lambda g: (l, g, 0, 0)), g_pw, (NPG,))

    pend = None
    for l in reversed(range(L)):
        proj, h, y3, p3, mg = saved[l]
        wl, small = wts[l]
        dproj, dp3, dsgw[l], dsbT, dvec[l], dpw = _bwd_mix(dx, p3, proj, wl, *small)
        dsgb[l] = dsbT.T
        if l == 0:
            dv = jnp.stack(dvec)
            part = _pack_small(jnp.zeros((L, D), F32), dv[:, 0:5], jnp.stack(dsgb), dfg[0], jnp.stack(dsgw),
                               dv[:, 5:8].reshape(L * 3, D))
            zone = lax.dynamic_update_slice(lax.empty((8, N_PACK, D), F32), part[None],
                                            (2 * me + lax.axis_index("c"), 0, 0))
            sm_s, sm_r, (part,), (zone,), started = _push_start("small_start", [part], [zone], False, "all")
        grads = _tn_grad(h[None], dproj, 1, lambda n, s: (n // 3, n % 3, 0), "grad_w_in", after=started if l == 0 else None)
        grads = _tn_grad(y3, dp3, NSH, lambda n, s: (s, PK_BR // q + n, 0), "grad_w_branch", into=grads)
        grads = _tn_grad(mg[None], dx[None], NSH, lambda n, s: (s, PK_BR // q + 3, 0), "grad_w_out", into=grads)
        dpq = dpw.astype(BF).reshape(NPG, NSH, PGW // NSH, PGW).transpose(1, 0, 2, 3).reshape(NSH, PK_CW - PK_PW, D)
        grads = lax.dynamic_update_slice(grads, jnp.pad(dpq, ((0, 0), (0, PK_ROWS - PK_CW), (0, 0))), (0, PK_PW, 0))
        if pend is not None:
            landed = _push_wait(f"rs_wait_{pend[0]}", *pend[1:], dproj, True)
        land = _own_slot(grads, True, f"rs_own_{l}")
        ss, rs, (grads,), (land,), tok = _push_start(f"rs_start_{l}", [grads], [land], True)
        if pend is not None:
            finish(pend[0], landed)
        dx, dng[l] = _bwd_in(dproj, wl, xs[l], dx, norm_g[l:l + 1] + tok[0, 0])
        pend = (l, grads, land, ss, rs)
    finish(pend[0], _push_wait(f"rs_wait_{pend[0]}", *pend[1:], dx, True))

    zone = _push_wait("small_wait", part, zone, sm_s, sm_r, dx, False, "all")
    gng = _sum_slots(_gather_all(_pad_to(jnp.concatenate(dng), O_VEC)), O_VEC)
    gsmall = jnp.concatenate([gng, _sum_slots(zone, PACK_RB)[O_VEC:]])
    gcw =lax.dynamic_slice_in_dim(gsmall[O_CW:O_CW + L * 3], me * q, q, axis=1)
    gpack = jnp.concatenate([gsmall[:O_CW], _pad_to(gcw.reshape(L * 3 * q // D, D), 16)])

    def pack(ng, lg, lb, sw, sb, pb_, ps, cwv, cb_, fg):
        return _pack_small(ng, jnp.stack([lg, lb, pb_, ps, cb_], axis=1), sb, fg, sw, cwv.reshape(L * 3 * q // D, D))

    wp = pack(norm_g, sgu_ln_g, sgu_ln_b, sgu_w, sgu_b, pool_b, pool_scale, conv_w, conv_b, final_g)
    mp = pack(m_norm_g, m_sgu_ln_g, m_sgu_ln_b, m_sgu_w, m_sgu_b, m_pool_b, m_pool_scale, m_conv_w, m_conv_b, m_final_g)
    vp = pack(v_norm_g, v_sgu_ln_g, v_sgu_ln_b, v_sgu_w, v_sgu_b, v_pool_b, v_pool_scale, v_conv_w, v_conv_b, v_final_g)
    rows = pl.BlockSpec((PACK_RB, D), lambda i: (i, 0))
    sm = _adamw([gpack], rows, wp, mp, vp, rows, _empty4(wp), (N_PACK // PACK_RB,))

    def unpack(a):
        vv = a[O_VEC:O_VEC + L * 5].reshape(L, 5, D)
        sb = a[O_SGB:O_SGB + L].reshape(L, GRP, BLK)
        fg = a[O_FG]
        sw = a[O_SGW:O_SGW + N_SGW].reshape(L, GRP, BLK, BLK)
        cwv = a[O_CW:O_CW + L * 3 * q // D].reshape(L, 3, q)
        return dict(norm_g=a[O_NG:O_NG + L], w_in=None, sgu_ln_g=vv[:, 0], sgu_ln_b=vv[:, 1], sgu_w=sw, sgu_b=sb, pool_w=None,
                    pool_b=vv[:, 2], pool_scale=vv[:, 3], conv_w=cwv, conv_b=vv[:, 4], w_branch_a=None,
                    w_branch_b=None, w_branch_c=None, w_out=None, final_g=fg)

    outs = [loss, dx[None]]
    for kind in range(4):
        d = unpack(sm[kind])
        d.update(w_in=g_w_in[kind], pool_w=g_pw[kind], w_branch_a=g_br[0][kind], w_branch_b=g_br[1][kind],
                 w_branch_c=g_br[2][kind], w_out=g_br[3][kind])
        outs.extend(d[n] for n in ("norm_g", "w_in", "sgu_ln_g", "sgu_ln_b", "sgu_w", "sgu_b", "pool_w", "pool_b",
                                   "pool_scale", "conv_w", "conv_b", "w_branch_a", "w_branch_b", "w_branch_c", "w_out",
                                   "final_g"))
    return tuple(outs)
```
